```python
import math
import jax
import jax.numpy as jnp
from jax import lax
import numpy as np

D_MODEL = 1024
BATCH = 8
SEQ = 4096
DEPTH = 2

HEAD_DIM = 64
D_MIX = D_MODEL
N_HEADS_TOTAL = D_MIX // HEAD_DIM
N_HEADS_SB = N_HEADS_TOTAL // 4
N_HEADS_FOX = N_HEADS_TOTAL // 4
N_HEADS_NSA = N_HEADS_TOTAL - N_HEADS_SB - N_HEADS_FOX
NSA_KV_GROUPS = 2
NSA_HPG = N_HEADS_NSA // NSA_KV_GROUPS
D_SB = N_HEADS_SB * HEAD_DIM
D_FOX = N_HEADS_FOX * HEAD_DIM
D_NSA = N_HEADS_NSA * HEAD_DIM
D_NSA_KV = NSA_KV_GROUPS * HEAD_DIM
CMP_BLOCK = 32
CMP_STRIDE = 16
CMP_HIDDEN = 256
SLC_BLOCK = 64
SLC_TOP = 16
WINDOW = 512
NSA_QBLOCK = 32
ATTN_QBLOCK = 128
REL_BUCKETS = 32
REL_MAX_DIST = 128
FORCE_SCORE = 1e4
RMS_EPS = 1e-6
IN_WIDTH = 4 * D_SB + 2 * D_NSA + 6 * D_NSA_KV + 3 * N_HEADS_NSA + 4 * D_FOX + N_HEADS_FOX

kernel_name = 'hybrid_sb_nsa_fox_parallel_heads'


def _rmsnorm(x, g):
    xf = x.astype(jnp.float32)
    y = xf * lax.rsqrt(jnp.mean(xf * xf, axis=-1, keepdims=True) + RMS_EPS)
    return (y * g.astype(jnp.float32)).astype(x.dtype)


def _heads(x, n):
    b, t, _ = x.shape
    return x.reshape(b, t, n, HEAD_DIM).transpose(0, 2, 1, 3)


def _merge(o):
    b, n, t, d = o.shape
    return o.transpose(0, 2, 1, 3).reshape(b, t, n * d)


def _split_proj(proj):
    widths = [D_SB, D_SB, D_SB, D_SB,
              D_NSA, D_NSA_KV, D_NSA_KV, D_NSA_KV, D_NSA_KV, D_NSA_KV, D_NSA_KV, 3 * N_HEADS_NSA, D_NSA,
              D_FOX, D_FOX, D_FOX, N_HEADS_FOX, D_FOX]
    idx, acc = [], 0
    for w in widths[:-1]:
        acc += w
        idx.append(acc)
    return jnp.split(proj, idx, axis=-1)


def _rel_bucket(dist):
    n = jnp.maximum(dist, 0)
    max_exact = REL_BUCKETS // 2
    nf = jnp.maximum(n, 1).astype(jnp.float32)
    large = max_exact + (jnp.log(nf / max_exact) / math.log(REL_MAX_DIST / max_exact)
                         * (REL_BUCKETS - max_exact)).astype(jnp.int32)
    large = jnp.minimum(large, REL_BUCKETS - 1)
    return jnp.where(n < max_exact, n, large)


def _masked_softmax(s, mask):
    p = jax.nn.softmax(jnp.where(mask, s, -1e30), axis=-1)
    return jnp.where(mask, p, 0.0)


def _stick_breaking(q, k, v):
    b, h, t, d = q.shape
    scale = d ** -0.5
    kpos = jnp.arange(t)

    def block(i):
        t0 = i * ATTN_QBLOCK
        qb = lax.dynamic_slice_in_dim(q, t0, ATTN_QBLOCK, axis=2)
        z = jnp.einsum('bhqd,bhkd->bhqk', qb, k).astype(jnp.float32) * scale
        qpos = t0 + jnp.arange(ATTN_QBLOCK)
        mask = kpos[None, :] < qpos[:, None]
        log1m = jnp.where(mask, -jax.nn.softplus(z), 0.0)
        cs = lax.cumsum(log1m, axis=3)
        log_w = jax.nn.log_sigmoid(z) + cs[..., -1:] - cs
        w = jnp.where(mask, jnp.exp(log_w), 0.0)
        return jnp.einsum('bhqk,bhkd->bhqd', w.astype(v.dtype), v)

    o = lax.map(block, jnp.arange(t // ATTN_QBLOCK))
    return o.transpose(1, 2, 0, 3, 4).reshape(b, h, t, d)


def _forgetting_attention(q, k, v, log_f):
    b, h, t, d = q.shape
    scale = d ** -0.5
    c = lax.cumsum(log_f, axis=2)
    kpos = jnp.arange(t)

    def block(i):
        t0 = i * ATTN_QBLOCK
        qb = lax.dynamic_slice_in_dim(q, t0, ATTN_QBLOCK, axis=2)
        cq = lax.dynamic_slice_in_dim(c, t0, ATTN_QBLOCK, axis=2)
        s = (jnp.einsum('bhqd,bhkd->bhqk', qb, k).astype(jnp.float32) * scale
             + cq[..., :, None] - c[..., None, :])
        qpos = t0 + jnp.arange(ATTN_QBLOCK)
        s = jnp.where(kpos[None, :] <= qpos[:, None], s, -jnp.inf)
        p = jax.nn.softmax(s, axis=-1)
        return jnp.einsum('bhqk,bhkd->bhqd', p.astype(v.dtype), v)

    o = lax.map(block, jnp.arange(t // ATTN_QBLOCK))
    return o.transpose(1, 2, 0, 3, 4).reshape(b, h, t, d)


def _compress(x, w1, b1, w2, pe):
    b, g, t, d = x.shape
    halves = x.reshape(b, g, t // CMP_STRIDE, CMP_STRIDE * d)
    w1a, w1b = w1[:CMP_STRIDE * d], w1[CMP_STRIDE * d:]
    h = halves[:, :, :-1] @ w1a + halves[:, :, 1:] @ w1b + (pe.reshape(-1) @ w1 + b1)
    return jax.nn.silu(h) @ w2


def _nsa(q, kc, vc, ks, vs, kw, vw, gate_logits, rel_table, w1, b1, w2, pe):
    b, hb, t, d = q.shape
    G, HPG = NSA_KV_GROUPS, NSA_HPG
    dt = q.dtype
    scale = d ** -0.5
    qg = q.reshape(b, G, HPG, t, d)
    k_cmp = _compress(kc, w1[0], b1[0], w2[0], pe[0])
    v_cmp = _compress(vc, w1[1], b1[1], w2[1], pe[1])
    n_cmp = k_cmp.shape[2]
    cmp_end = jnp.arange(n_cmp) * CMP_STRIDE + CMP_BLOCK - 1
    cmp_start = cmp_end - (CMP_BLOCK - 1)
    n_slc = t // SLC_BLOCK
    n_top = min(SLC_TOP, n_slc)
    slc_start = jnp.arange(n_slc) * SLC_BLOCK
    overlap = jnp.clip(jnp.minimum(cmp_end[:, None], slc_start[None, :] + SLC_BLOCK - 1)
                       - jnp.maximum(cmp_start[:, None], slc_start[None, :]) + 1, 0, None
                       ).astype(jnp.float32) / CMP_BLOCK
    ks_blk = ks.reshape(b, G, n_slc, SLC_BLOCK, d)
    vs_blk = vs.reshape(b, G, n_slc, SLC_BLOCK, d)
    kw_pad = jnp.pad(kw, ((0, 0), (0, 0), (WINDOW, 0), (0, 0)))
    vw_pad = jnp.pad(vw, ((0, 0), (0, 0), (WINDOW, 0), (0, 0)))
    table = rel_table.reshape(REL_BUCKETS, G, HPG)
    bi = jnp.arange(b)[:, None, None, None]
    gi = jnp.arange(G)[None, :, None, None]
    win_off = jnp.arange(WINDOW + NSA_QBLOCK) - WINDOW
    tok_in_blk = jnp.arange(SLC_BLOCK)
    blk_ids = jnp.arange(n_slc)

    def bias_2d(dist):
        return table[_rel_bucket(dist)].transpose(2, 3, 0, 1)

    def block(i):
        t0 = i * NSA_QBLOCK
        qpos = t0 + jnp.arange(NSA_QBLOCK)
        qb = lax.dynamic_slice_in_dim(qg, t0, NSA_QBLOCK, axis=3)
        s_c = (jnp.einsum('bghqd,bgcd->bghqc', qb, k_cmp).astype(jnp.float32) * scale
               + bias_2d(qpos[:, None] - cmp_end[None, :]))
        p_c = _masked_softmax(s_c, cmp_end[None, :] <= qpos[:, None])
        o_c = jnp.einsum('bghqc,bgcd->bghqd', p_c.astype(dt), v_cmp)
        imp = jnp.einsum('bghqc,cs->bgqs', p_c, overlap)
        cur = qpos // SLC_BLOCK
        forced = ((blk_ids[None, :] == 0) | (blk_ids[None, :] == cur[:, None])
                  | (blk_ids[None, :] == cur[:, None] - 1))
        future = slc_start[None, :] > qpos[:, None]
        imp = jnp.where(forced, FORCE_SCORE, imp)
        imp = jnp.where(future, -FORCE_SCORE, imp)
        _, idx = lax.top_k(imp, n_top)
        k_sel = ks_blk[bi, gi, idx].reshape(b, G, NSA_QBLOCK, n_top * SLC_BLOCK, d)
        v_sel = vs_blk[bi, gi, idx].reshape(b, G, NSA_QBLOCK, n_top * SLC_BLOCK, d)
        pos_sel = (idx[..., None] * SLC_BLOCK + tok_in_blk).reshape(b, G, NSA_QBLOCK, n_top * SLC_BLOCK)
        bias_sel = table[_rel_bucket(qpos[:, None] - pos_sel), gi].transpose(0, 1, 4, 2, 3)
        s_s = jnp.einsum('bghqd,bgqkd->bghqk', qb, k_sel).astype(jnp.float32) * scale + bias_sel
        p_s = _masked_softmax(s_s, (pos_sel <= qpos[:, None])[:, :, None])
        o_s = jnp.einsum('bghqk,bgqkd->bghqd', p_s.astype(dt), v_sel)
        kwb = lax.dynamic_slice_in_dim(kw_pad, t0, WINDOW + NSA_QBLOCK, axis=2)
        vwb = lax.dynamic_slice_in_dim(vw_pad, t0, WINDOW + NSA_QBLOCK, axis=2)
        pos_w = t0 + win_off
        dist_w = qpos[:, None] - pos_w[None, :]
        m_w = (pos_w[None, :] >= 0) & (dist_w >= 0) & (dist_w < WINDOW)
        s_w = jnp.einsum('bghqd,bgkd->bghqk', qb, kwb).astype(jnp.float32) * scale + bias_2d(dist_w)
        p_w = _masked_softmax(s_w, m_w)
        o_w = jnp.einsum('bghqk,bgkd->bghqd', p_w.astype(dt), vwb)
        return o_c, o_s, o_w

    o_c, o_s, o_w = lax.map(block, jnp.arange(t // NSA_QBLOCK))

    def to_btHd(o):
        return o.transpose(1, 0, 4, 2, 3, 5).reshape(b, t, hb, d)

    g = jax.nn.sigmoid(gate_logits).reshape(b, t, hb, 3)
    o = (g[..., 0:1] * to_btHd(o_c) + g[..., 1:2] * to_btHd(o_s) + g[..., 2:3] * to_btHd(o_w))
    return o.reshape(b, t, hb * d)


def setup_inputs(seed: int = 0) -> dict:
    key = jax.random.key(seed)
    ks = jax.random.split(key, 11)
    f32 = jnp.float32
    x = jax.random.normal(ks[0], (BATCH, SEQ, D_MODEL), f32)
    norm_g = 1.0 + 0.02 * jax.random.normal(ks[1], (DEPTH, D_MODEL), f32)
    w_in = jax.random.normal(ks[2], (DEPTH, D_MODEL, IN_WIDTH), f32) * D_MODEL ** -0.5
    w_out = jax.random.normal(ks[3], (DEPTH, D_MIX, D_MODEL), f32) * D_MIX ** -0.5
    forget_b = 2.0 + 0.5 * jax.random.normal(ks[4], (DEPTH, N_HEADS_FOX), f32)
    cmp_w1 = jax.random.normal(ks[5], (DEPTH, 2, CMP_BLOCK * HEAD_DIM, CMP_HIDDEN), f32) * (CMP_BLOCK * HEAD_DIM) ** -0.5
    cmp_b1 = 0.02 * jax.random.normal(ks[6], (DEPTH, 2, CMP_HIDDEN), f32)
    cmp_w2 = jax.random.normal(ks[7], (DEPTH, 2, CMP_HIDDEN, HEAD_DIM), f32) * CMP_HIDDEN ** -0.5
    cmp_pe = 0.1 * jax.random.normal(ks[8], (DEPTH, 2, CMP_BLOCK, HEAD_DIM), f32)
    rel_bias = 0.5 * jax.random.normal(ks[9], (REL_BUCKETS, N_HEADS_NSA), f32)
    final_g = 1.0 + 0.02 * jax.random.normal(ks[10], (D_MODEL,), f32)
    return {'x': x, 'norm_g': norm_g, 'w_in': w_in, 'w_out': w_out, 'forget_b': forget_b,
            'cmp_w1': cmp_w1, 'cmp_b1': cmp_b1, 'cmp_w2': cmp_w2, 'cmp_pe': cmp_pe,
            'rel_bias': rel_bias, 'final_g': final_g}


def reference(x, norm_g, w_in, w_out, forget_b, cmp_w1, cmp_b1, cmp_w2, cmp_pe, rel_bias, final_g):
    for l in range(DEPTH):
        h = _rmsnorm(x, norm_g[l])
        proj = h @ w_in[l]
        (qa, ka, va, za,
         qb, kc, vc, ksl, vsl, kwn, vwn, gb, zb,
         qc, kcf, vcf, fc, zc) = _split_proj(proj)
        o_a = _merge(_stick_breaking(_heads(qa, N_HEADS_SB), _heads(ka, N_HEADS_SB), _heads(va, N_HEADS_SB)))
        o_a = o_a * jax.nn.silu(za)
        o_b = _nsa(_heads(qb, N_HEADS_NSA), _heads(kc, NSA_KV_GROUPS), _heads(vc, NSA_KV_GROUPS),
                   _heads(ksl, NSA_KV_GROUPS), _heads(vsl, NSA_KV_GROUPS),
                   _heads(kwn, NSA_KV_GROUPS), _heads(vwn, NSA_KV_GROUPS),
                   gb, rel_bias, cmp_w1[l], cmp_b1[l], cmp_w2[l], cmp_pe[l])
        o_b = o_b * jax.nn.silu(zb)
        log_f = jax.nn.log_sigmoid((fc + forget_b[l]).astype(jnp.float32)).transpose(0, 2, 1)
        o_c = _merge(_forgetting_attention(_heads(qc, N_HEADS_FOX), _heads(kcf, N_HEADS_FOX),
                                           _heads(vcf, N_HEADS_FOX), log_f))
        o_c = o_c * jax.nn.silu(zc)
        mix = jnp.concatenate([o_a, o_b, o_c], axis=-1) @ w_out[l]
        x = x + mix
    return _rmsnorm(x, final_g)
```

```python
import functools
import math

import jax
import jax.numpy as jnp
import numpy as np
from jax import lax
from jax.experimental import pallas as pl
from jax.experimental.pallas import tpu as pltpu

D_MODEL = 1024
HEAD_DIM = 64
N_HEADS_SB = 4
N_HEADS_FOX = 4
N_HEADS_NSA = 8
NSA_KV_GROUPS = 2
NSA_HPG = N_HEADS_NSA // NSA_KV_GROUPS
CMP_BLOCK = 32
CMP_STRIDE = 16
CMP_HIDDEN = 256
SLC_BLOCK = 64
SLC_TOP = 16
WINDOW = 512
REL_BUCKETS = 32
REL_MAX_DIST = 128
FORCE_SCORE = 1e4
RMS_EPS = 1e-6
NEG = -1e30

LANES = 128
SUBLANES = 8
VMEM_LIMIT = 56 * 1024 * 1024

TM_PROJ = 512
TQ_SB = 256
TQ_FOX = 256
TQ_NSA = 128
TK = 128

PB_QB, PB_QA, PB_KA, PB_VA = 0, 4, 6, 8
PB_KC, PB_VC, PB_KS, PB_VS, PB_KW, PB_VW = 10, 11, 12, 13, 14, 15
PB_QC, PB_KF, PB_VF = 16, 18, 20
PB_BLOCKS = 22
PF_ZB, PF_ZA, PF_ZC, PF_GB, PF_FC = 0, 4, 6, 8, 9
PF_BLOCKS = 10

NSA_HEAD_ORDER = [0, 4, 1, 5, 2, 6, 3, 7]

_F32 = jnp.float32
_BF16 = jnp.bfloat16


def _cparams(sem):
    return pltpu.CompilerParams(dimension_semantics=sem, vmem_limit_bytes=VMEM_LIMIT)


def _dot(a, b):
    return jnp.dot(a, b, preferred_element_type=_F32)


def _dot_nt(a, b):
    return lax.dot_general(a, b, (((1,), (1,)), ((), ())), preferred_element_type=_F32)


def _split2(x):
    hi = x.astype(_BF16)
    lo = (x - hi.astype(_F32)).astype(_BF16)
    return jnp.concatenate([hi, lo], axis=1)


def _split3(x):
    h1 = x.astype(_BF16)
    r1 = x - h1.astype(_F32)
    h2 = r1.astype(_BF16)
    h3 = (r1 - h2.astype(_F32)).astype(_BF16)
    return jnp.concatenate([h1, h2, h3], axis=1)


def _sigmoid(x):
    return 1.0 / (1.0 + jnp.exp(-x))


def _silu(x):
    return x * _sigmoid(x)


def _rel_bucket_np(n):
    n = np.maximum(n, 0)
    max_exact = REL_BUCKETS // 2
    nf = np.maximum(n, 1).astype(np.float64)
    large = max_exact + (np.log(nf / max_exact) / math.log(REL_MAX_DIST / max_exact)
                         * (REL_BUCKETS - max_exact)).astype(np.int64)
    large = np.minimum(large, REL_BUCKETS - 1)
    return np.where(n < max_exact, n, large)


def _bucket_thresholds():
    n = np.arange(0, 4 * REL_MAX_DIST)
    bk = _rel_bucket_np(n)
    assert np.all(np.diff(bk) >= 0) and bk[-1] == REL_BUCKETS - 1
    return [int(np.argmax(bk >= b)) for b in range(REL_BUCKETS)]


_BUCKET_THR = _bucket_thresholds()


def _bias_kernel(tab_ref, dist_ref, o_ref):
    n = dist_ref[...]
    acc = [jnp.full(n.shape, tab_ref[0, h], _F32) for h in range(N_HEADS_NSA)]
    for b in range(1, REL_BUCKETS):
        ge = n >= _BUCKET_THR[b]
        for h in range(N_HEADS_NSA):
            acc[h] = jnp.where(ge, tab_ref[b, h], acc[h])
    for h in range(N_HEADS_NSA):
        o_ref[h] = acc[h]


def _bias_table(rel_bias, dist, rows):
    n_rows, n_cols = dist.shape
    return pl.pallas_call(
        _bias_kernel,
        grid=(n_rows // rows,),
        in_specs=[pl.BlockSpec(memory_space=pltpu.SMEM),
                  pl.BlockSpec((rows, n_cols), lambda i: (i, 0))],
        out_specs=pl.BlockSpec((N_HEADS_NSA, rows, n_cols), lambda i: (0, i, 0)),
        out_shape=jax.ShapeDtypeStruct((N_HEADS_NSA, n_rows, n_cols), _F32),
        compiler_params=_cparams(("arbitrary",)),
        name="rel_bias_table",
    )(rel_bias, dist)


def _proj_kernel(x_ref, g_ref, wb_ref, wf_ref, pb_ref, pf_ref):
    x = x_ref[...]
    y = x * lax.rsqrt(jnp.mean(x * x, axis=-1, keepdims=True) + RMS_EPS)
    h = (y * g_ref[...]).astype(_BF16)
    chunk = 4 * LANES
    for c in range(0, PB_BLOCKS * LANES, chunk):
        w = min(chunk, PB_BLOCKS * LANES - c)
        pb_ref[:, c:c + w] = _dot(h, wb_ref[:, c:c + w]).astype(_BF16)
    for c in range(0, PF_BLOCKS * LANES, chunk):
        w = min(chunk, PF_BLOCKS * LANES - c)
        pf_ref[:, c:c + w] = _dot(h, wf_ref[:, c:c + w])


def _proj(x2, g, wb, wf):
    n = x2.shape[0]
    return pl.pallas_call(
        _proj_kernel,
        grid=(n // TM_PROJ,),
        in_specs=[pl.BlockSpec((TM_PROJ, D_MODEL), lambda i: (i, 0)),
                  pl.BlockSpec((1, D_MODEL), lambda i: (0, 0)),
                  pl.BlockSpec((D_MODEL, PB_BLOCKS * LANES), lambda i: (0, 0)),
                  pl.BlockSpec((D_MODEL, PF_BLOCKS * LANES), lambda i: (0, 0))],
        out_specs=[pl.BlockSpec((TM_PROJ, PB_BLOCKS * LANES), lambda i: (i, 0)),
                   pl.BlockSpec((TM_PROJ, PF_BLOCKS * LANES), lambda i: (i, 0))],
        out_shape=[jax.ShapeDtypeStruct((n, PB_BLOCKS * LANES), _BF16),
                   jax.ShapeDtypeStruct((n, PF_BLOCKS * LANES), _F32)],
        compiler_params=_cparams(("arbitrary",)),
        name="rmsnorm_in_proj",
    )(x2, g, wb, wf)


def _fgate_kernel(fc_ref, fb_ref, ccol_ref, crow_ref):
    t = fc_ref.shape[1]
    z = fc_ref[0] + fb_ref[...]
    logf = jnp.minimum(z, 0.0) - jnp.log1p(jnp.exp(-jnp.abs(z)))
    row = lax.broadcasted_iota(jnp.int32, (t, LANES), 0)
    c = logf
    shift = 1
    while shift < t:
        c = c + jnp.where(row >= shift, pltpu.roll(c, shift, axis=0), 0.0)
        shift *= 2
    ct = c.T
    for h in range(N_HEADS_FOX):
        ccol_ref[0, h] = jnp.broadcast_to(c[:, h:h + 1], (t, LANES))
        for j in range(t // LANES):
            crow_ref[0, h, j] = jnp.broadcast_to(ct[h:h + 1, j * LANES:(j + 1) * LANES],
                                                 (SUBLANES, LANES))


def _fgate(pf3, fb_row):
    b, t, _ = pf3.shape
    return pl.pallas_call(
        _fgate_kernel,
        grid=(b,),
        in_specs=[pl.BlockSpec((1, t, LANES), lambda i: (i, 0, PF_FC)),
                  pl.BlockSpec((1, LANES), lambda i: (0, 0))],
        out_specs=[pl.BlockSpec((1, N_HEADS_FOX, t, LANES), lambda i: (i, 0, 0, 0)),
                   pl.BlockSpec((1, N_HEADS_FOX, t // LANES, SUBLANES, LANES),
                                lambda i: (i, 0, 0, 0, 0))],
        out_shape=[jax.ShapeDtypeStruct((b, N_HEADS_FOX, t, LANES), _F32),
                   jax.ShapeDtypeStruct((b, N_HEADS_FOX, t // LANES, SUBLANES, LANES), _F32)],
        compiler_params=_cparams(("arbitrary",)),
        name="forget_gate_cumsum",
    )(pf3, fb_row)


def _compress_kernel(x_ref, w1_ref, pe_ref, b1_ref, w2_ref, o_ref):
    nc = x_ref.shape[3]
    half = CMP_STRIDE * HEAD_DIM
    w1 = w1_ref[0]
    c1 = _dot(jnp.broadcast_to(pe_ref[0], (SUBLANES, 2 * half)), w1)[0:1] + b1_ref[0]
    out = jnp.zeros((nc, LANES), _F32)
    for g in range(NSA_KV_GROUPS):
        xg = x_ref[0, 0, g]
        a = _dot(xg, w1[:half])
        bb = _dot(xg, w1[half:])
        h = a + pltpu.roll(bb, nc - 1, axis=0) + c1
        out = out + _dot(_silu(h).astype(_BF16), w2_ref[0, g])
    o_ref[0, 0] = out.astype(_BF16)


def _compress(halves, w1, pe, b1, w2p):
    b, _, g, nc, width = halves.shape
    return pl.pallas_call(
        _compress_kernel,
        grid=(b, 2),
        in_specs=[pl.BlockSpec((1, 1, g, nc, width), lambda i, k: (i, k, 0, 0, 0)),
                  pl.BlockSpec((1, 2 * width, CMP_HIDDEN), lambda i, k: (k, 0, 0)),
                  pl.BlockSpec((1, 1, 2 * width), lambda i, k: (k, 0, 0)),
                  pl.BlockSpec((1, 1, CMP_HIDDEN), lambda i, k: (k, 0, 0)),
                  pl.BlockSpec((1, g, CMP_HIDDEN, LANES), lambda i, k: (k, 0, 0, 0))],
        out_specs=pl.BlockSpec((1, 1, nc, LANES), lambda i, k: (i, k, 0, 0)),
        out_shape=jax.ShapeDtypeStruct((b, 2, nc, LANES), _BF16),
        compiler_params=_cparams(("arbitrary", "arbitrary")),
        name="nsa_compress",
    )(halves, w1, pe, b1, w2p)


def _sb_kernel(q_ref, k_ref, v_ref, z_ref, o_ref, c_ref, acc_ref):
    qi = pl.program_id(2)
    tq = q_ref.shape[1]
    lane = lax.broadcasted_iota(jnp.int32, (tq, LANES), 1)
    q2 = q_ref[0]
    qh = [jnp.where(lane < HEAD_DIM, q2, jnp.zeros_like(q2)),
          jnp.where(lane >= HEAD_DIM, q2, jnp.zeros_like(q2))]
    r_i = lax.broadcasted_iota(jnp.int32, (2 * TK, 2 * TK), 0) & (TK - 1)
    c_i = lax.broadcasted_iota(jnp.int32, (2 * TK, 2 * TK), 1)
    uu = jnp.where((c_i >= TK) | (r_i >= c_i), 1.0, 0.0).astype(_BF16)
    c_ref[...] = jnp.zeros_like(c_ref)
    acc_ref[...] = jnp.zeros_like(acc_ref)
    qpos = qi * tq + lax.broadcasted_iota(jnp.int32, (tq, TK), 0)
    nk = (qi + 1) * (tq // TK)

    def body(it, carry):
        j = nk - 1 - it
        k2 = k_ref[0, pl.ds(j * TK, TK), :]
        v2 = v_ref[0, pl.ds(j * TK, TK), :]
        mask = (j * TK + lax.broadcasted_iota(jnp.int32, (tq, TK), 1)) < qpos
        for h in range(2):
            z = _dot_nt(qh[h], k2)
            sp = jnp.maximum(z, 0.0) + jnp.log1p(jnp.exp(-jnp.abs(z)))
            l1m = jnp.where(mask, -sp, 0.0)
            rc = _dot(_split2(l1m), uu)
            logw = z + rc[:, :TK] + c_ref[h]
            w = jnp.where(mask, jnp.exp(logw), 0.0)
            acc_ref[h] += _dot(w.astype(_BF16), v2)
            c_ref[h] += rc[:, TK:]
        return carry

    lax.fori_loop(0, nk, body, 0)
    o = jnp.where(lane < HEAD_DIM, acc_ref[0], acc_ref[1])
    o_ref[0] = (o * _silu(z_ref[0])).astype(o_ref.dtype)


def _sb_attention(pb3, pf3):
    b, t, _ = pb3.shape
    n_pairs = N_HEADS_SB // 2
    return pl.pallas_call(
        _sb_kernel,
        grid=(b, n_pairs, t // TQ_SB),
        in_specs=[pl.BlockSpec((1, TQ_SB, LANES), lambda i, p, q: (i, q, PB_QA + p)),
                  pl.BlockSpec((1, t, LANES), lambda i, p, q: (i, 0, PB_KA + p)),
                  pl.BlockSpec((1, t, LANES), lambda i, p, q: (i, 0, PB_VA + p)),
                  pl.BlockSpec((1, TQ_SB, LANES), lambda i, p, q: (i, q, PF_ZA + p))],
        out_specs=pl.BlockSpec((1, TQ_SB, LANES), lambda i, p, q: (i, q, p)),
        out_shape=jax.ShapeDtypeStruct((b, t, n_pairs * LANES), _BF16),
        scratch_shapes=[pltpu.VMEM((2, TQ_SB, LANES), _F32),
                        pltpu.VMEM((2, TQ_SB, LANES), _F32)],
        compiler_params=_cparams(("arbitrary", "arbitrary", "arbitrary")),
        name="stick_breaking_attention",
    )(pb3, pb3, pb3, pf3)


def _softmax_step(s, v2, m_ref, l_ref, acc_ref, h):
    m_old = m_ref[h]
    m_new = jnp.maximum(m_old, jnp.max(s, axis=-1, keepdims=True))
    p = jnp.exp(s - m_new[:, :s.shape[1]])
    alpha = jnp.exp(m_old - m_new)
    l_ref[h] = alpha * l_ref[h] + jnp.sum(p, axis=-1, keepdims=True)
    acc_ref[h] = alpha * acc_ref[h] + _dot(p.astype(_BF16), v2)
    m_ref[h] = m_new


def _normalized(l_ref, acc_ref, h):
    l = l_ref[h]
    return acc_ref[h] * jnp.where(l > 0.0, 1.0 / l, 0.0)


def _fox_kernel(q_ref, k_ref, v_ref, cq_ref, ck_ref, z_ref, o_ref, m_ref, l_ref, acc_ref):
    qi = pl.program_id(2)
    tq = q_ref.shape[1]
    lane = lax.broadcasted_iota(jnp.int32, (tq, LANES), 1)
    q2 = q_ref[0]
    qh = [jnp.where(lane < HEAD_DIM, q2, jnp.zeros_like(q2)),
          jnp.where(lane >= HEAD_DIM, q2, jnp.zeros_like(q2))]
    cq = [cq_ref[0, 0], cq_ref[0, 1]]
    m_ref[...] = jnp.full_like(m_ref, NEG)
    l_ref[...] = jnp.zeros_like(l_ref)
    acc_ref[...] = jnp.zeros_like(acc_ref)
    qpos = qi * tq + lax.broadcasted_iota(jnp.int32, (tq, TK), 0)
    nk = (qi + 1) * (tq // TK)

    def body(j, carry):
        k2 = k_ref[0, pl.ds(j * TK, TK), :]
        v2 = v_ref[0, pl.ds(j * TK, TK), :]
        mask = (j * TK + lax.broadcasted_iota(jnp.int32, (tq, TK), 1)) <= qpos
        for h in range(2):
            ck = jnp.broadcast_to(ck_ref[0, h, j][0:1], (tq, TK))
            s = _dot_nt(qh[h], k2) + cq[h] - ck
            s = jnp.where(mask, s, NEG)
            _softmax_step(s, v2, m_ref, l_ref, acc_ref, h)
        return carry

    lax.fori_loop(0, nk, body, 0)
    o = jnp.where(lane < HEAD_DIM, _normalized(l_ref, acc_ref, 0), _normalized(l_ref, acc_ref, 1))
    o_ref[0] = (o * _silu(z_ref[0])).astype(o_ref.dtype)


def _fox_attention(pb3, pf3, ccol, crow):
    b, t, _ = pb3.shape
    n_pairs = N_HEADS_FOX // 2
    return pl.pallas_call(
        _fox_kernel,
        grid=(b, n_pairs, t // TQ_FOX),
        in_specs=[pl.BlockSpec((1, TQ_FOX, LANES), lambda i, p, q: (i, q, PB_QC + p)),
                  pl.BlockSpec((1, t, LANES), lambda i, p, q: (i, 0, PB_KF + p)),
                  pl.BlockSpec((1, t, LANES), lambda i, p, q: (i, 0, PB_VF + p)),
                  pl.BlockSpec((1, 2, TQ_FOX, LANES), lambda i, p, q: (i, p, q, 0)),
                  pl.BlockSpec((1, 2, t // LANES, SUBLANES, LANES), lambda i, p, q: (i, p, 0, 0, 0)),
                  pl.BlockSpec((1, TQ_FOX, LANES), lambda i, p, q: (i, q, PF_ZC + p))],
        out_specs=pl.BlockSpec((1, TQ_FOX, LANES), lambda i, p, q: (i, q, p)),
        out_shape=jax.ShapeDtypeStruct((b, t, n_pairs * LANES), _BF16),
        scratch_shapes=[pltpu.VMEM((2, TQ_FOX, LANES), _F32),
                        pltpu.VMEM((2, TQ_FOX, LANES), _F32),
                        pltpu.VMEM((2, TQ_FOX, LANES), _F32)],
        compiler_params=_cparams(("arbitrary", "arbitrary", "arbitrary")),
        name="forgetting_attention",
    )(pb3, pb3, pb3, ccol, crow, pf3)


def _nsa_kernel(q_ref, kc_ref, vc_ref, ks_ref, vs_ref, kw_ref, vw_ref, gl_ref, z_ref,
                bc_ref, bd_ref, ov_ref, gx_ref, o_ref,
                m_ref, l_ref, acc_ref, osum_ref, sel_ref):
    qi = pl.program_id(1)
    tq = q_ref.shape[1]
    n_blk = N_HEADS_NSA // 2
    n_cmp = kc_ref.shape[2]
    lane = lax.broadcasted_iota(jnp.int32, (tq, LANES), 1)
    row = lax.broadcasted_iota(jnp.int32, (tq, LANES), 0)
    qrow = qi * tq + row

    def q_head(p, s):
        q2 = q_ref[0, :, p * LANES:(p + 1) * LANES]
        keep = (lane < HEAD_DIM) if s == 0 else (lane >= HEAD_DIM)
        return jnp.where(keep, q2, jnp.zeros_like(q2))

    gates = _dot(_split2(_sigmoid(gl_ref[0])), gx_ref[...])

    def gate(c, p):
        return gates[:, (c * n_blk + p) * LANES:(c * n_blk + p + 1) * LANES]

    kc = kc_ref[0, 0]
    vc = vc_ref[0, 0]
    jc = lax.broadcasted_iota(jnp.int32, (tq, n_cmp), 1)
    qc_pos = qi * tq + lax.broadcasted_iota(jnp.int32, (tq, n_cmp), 0)
    cmask = (jc * CMP_STRIDE + (CMP_BLOCK - 1) <= qc_pos) & (jc < n_cmp - 1)
    psum = [jnp.zeros((tq, n_cmp), _F32) for _ in range(NSA_KV_GROUPS)]
    for p in range(n_blk):
        o_pair = []
        for s in range(2):
            sc = _dot_nt(q_head(p, s), kc) + bc_ref[2 * p + s]
            sc = jnp.where(cmask, sc, NEG)
            mx = jnp.max(sc, axis=-1, keepdims=True)
            e = jnp.where(cmask, jnp.exp(sc - mx), 0.0)
            den = jnp.sum(e, axis=-1, keepdims=True)
            inv = jnp.where(den > 0.0, 1.0 / den, 0.0)
            pc = e * inv
            psum[s] = psum[s] + pc
            o_pair.append(_dot(pc.astype(_BF16), vc))
        o_c = jnp.where(lane < HEAD_DIM, o_pair[0], o_pair[1])
        osum_ref[p] = gate(0, p) * o_c

    imp = _dot(_split3(jnp.concatenate(psum, axis=1)), ov_ref[...])
    blk = lane & (SLC_BLOCK - 1)
    cur = qrow >> int(math.log2(SLC_BLOCK))
    forced = (blk == 0) | (blk == cur) | (blk == cur - 1)
    imp = jnp.where(forced, FORCE_SCORE, imp)
    imp = jnp.where(blk > cur, -FORCE_SCORE, imp)
    imp_t = imp.T
    n_sel = LANES // NSA_KV_GROUPS
    sub = lax.broadcasted_iota(jnp.int32, (n_sel, tq), 0)
    sel_t = []
    for g in range(NSA_KV_GROUPS):
        a = imp_t[g * n_sel:(g + 1) * n_sel]
        cnt = jnp.zeros((n_sel, tq), _F32)
        for j in range(n_sel):
            rj = jnp.broadcast_to(a[j:j + 1], (n_sel, tq))
            tie = jnp.where(sub > j, jnp.where(rj == a, 1.0, 0.0), 0.0)
            cnt = cnt + jnp.where(rj > a, 1.0, tie)
        sel_t.append(jnp.where(cnt < float(SLC_TOP), 1.0, 0.0))
    sel_ref[...] = jnp.concatenate(sel_t, axis=0).T.astype(_BF16)

    def reset():
        m_ref[...] = jnp.full_like(m_ref, NEG)
        l_ref[...] = jnp.zeros_like(l_ref)
        acc_ref[...] = jnp.zeros_like(acc_ref)

    kcol = lax.broadcasted_iota(jnp.int32, (tq, TK), 1)
    e_row = lax.broadcasted_iota(jnp.int32, (LANES, TK), 0)
    e_col = lax.broadcasted_iota(jnp.int32, (LANES, TK), 1) >> int(math.log2(SLC_BLOCK))

    def tile_step(kt, k_ref, v_ref, neg_g):
        k2 = k_ref[0, pl.ds(kt * TK, TK), :]
        v2 = v_ref[0, pl.ds(kt * TK, TK), :]
        didx = jnp.minimum(qi - kt, 2)
        for p in range(n_blk):
            for s in range(2):
                h = 2 * p + s
                sc = _dot_nt(q_head(p, s), k2) + bd_ref[h, didx] + neg_g[s]
                _softmax_step(sc, v2, m_ref, l_ref, acc_ref, h)

    def finish(c):
        for p in range(n_blk):
            o = jnp.where(lane < HEAD_DIM, _normalized(l_ref, acc_ref, 2 * p),
                          _normalized(l_ref, acc_ref, 2 * p + 1))
            osum_ref[p] += gate(c, p) * o

    reset()

    def sel_body(kt, carry):
        kpos = kt * TK + kcol
        neg_g = []
        for g in range(NSA_KV_GROUPS):
            expand = jnp.where(e_row == g * n_sel + kt * (TK // SLC_BLOCK) + e_col, 1.0, 0.0)
            mf = _dot(sel_ref[...], expand.astype(_BF16))
            neg_g.append(jnp.where(kpos <= qrow, (mf - 1.0) * (-NEG), NEG))
        tile_step(kt, ks_ref, vs_ref, neg_g)
        return carry

    lax.fori_loop(0, qi + 1, sel_body, 0)
    finish(1)

    reset()

    def win_body(kt, carry):
        dist = qrow - (kt * TK + kcol)
        neg = jnp.where((dist >= 0) & (dist < WINDOW), 0.0, NEG)
        tile_step(kt, kw_ref, vw_ref, [neg, neg])
        return carry

    lax.fori_loop(jnp.maximum(qi - WINDOW // TK, 0), qi + 1, win_body, 0)
    finish(2)

    for p in range(n_blk):
        zp = z_ref[0, :, p * LANES:(p + 1) * LANES]
        o_ref[0, :, p * LANES:(p + 1) * LANES] = (osum_ref[p] * _silu(zp)).astype(o_ref.dtype)


def _nsa_attention(pb3, pf3, kvc, bias_c, bias_d, ov, gx):
    b, t, _ = pb3.shape
    n_cmp = kvc.shape[2]
    n_blk = N_HEADS_NSA // 2
    full = lambda shape: pl.BlockSpec(shape, lambda i, q: (0,) * len(shape))
    kv_spec = lambda col: pl.BlockSpec((1, t, LANES), lambda i, q: (i, 0, col))
    return pl.pallas_call(
        _nsa_kernel,
        grid=(b, t // TQ_NSA),
        in_specs=[pl.BlockSpec((1, TQ_NSA, n_blk * LANES), lambda i, q: (i, q, PB_QB // n_blk)),
                  pl.BlockSpec((1, 1, n_cmp, LANES), lambda i, q: (i, 0, 0, 0)),
                  pl.BlockSpec((1, 1, n_cmp, LANES), lambda i, q: (i, 1, 0, 0)),
                  kv_spec(PB_KS), kv_spec(PB_VS), kv_spec(PB_KW), kv_spec(PB_VW),
                  pl.BlockSpec((1, TQ_NSA, LANES), lambda i, q: (i, q, PF_GB)),
                  pl.BlockSpec((1, TQ_NSA, n_blk * LANES), lambda i, q: (i, q, PF_ZB // n_blk)),
                  pl.BlockSpec((N_HEADS_NSA, TQ_NSA, n_cmp), lambda i, q: (0, q, 0)),
                  full(bias_d.shape), full(ov.shape), full(gx.shape)],
        out_specs=pl.BlockSpec((1, TQ_NSA, n_blk * LANES), lambda i, q: (i, q, 0)),
        out_shape=jax.ShapeDtypeStruct((b, t, n_blk * LANES), _BF16),
        scratch_shapes=[pltpu.VMEM((N_HEADS_NSA, TQ_NSA, LANES), _F32),
                        pltpu.VMEM((N_HEADS_NSA, TQ_NSA, LANES), _F32),
                        pltpu.VMEM((N_HEADS_NSA, TQ_NSA, LANES), _F32),
                        pltpu.VMEM((n_blk, TQ_NSA, LANES), _F32),
                        pltpu.VMEM((TQ_NSA, LANES), _BF16)],
        compiler_params=_cparams(("arbitrary", "arbitrary")),
        name="native_sparse_attention",
    )(pb3, kvc, kvc, pb3, pb3, pb3, pb3, pf3, pf3, bias_c, bias_d, ov, gx)


def _out_kernel(x_ref, oa_ref, ob_ref, oc_ref, w_ref, g_ref, o_ref, *, final_norm):
    na, nb = oa_ref.shape[1], ob_ref.shape[1]
    mix = (_dot(oa_ref[...], w_ref[0:na])
           + _dot(ob_ref[...], w_ref[na:na + nb])
           + _dot(oc_ref[...], w_ref[na + nb:]))
    x = x_ref[...] + mix
    if final_norm:
        x = x * lax.rsqrt(jnp.mean(x * x, axis=-1, keepdims=True) + RMS_EPS) * g_ref[...]
    o_ref[...] = x


def _out_proj(x2, oa, ob, oc, w, g, final_norm):
    n = x2.shape[0]
    row = lambda width: pl.BlockSpec((TM_PROJ, width), lambda i: (i, 0))
    return pl.pallas_call(
        functools.partial(_out_kernel, final_norm=final_norm),
        grid=(n // TM_PROJ,),
        in_specs=[row(D_MODEL), row(oa.shape[1]), row(ob.shape[1]), row(oc.shape[1]),
                  pl.BlockSpec((D_MODEL, D_MODEL), lambda i: (0, 0)),
                  pl.BlockSpec((1, D_MODEL), lambda i: (0, 0))],
        out_specs=row(D_MODEL),
        out_shape=jax.ShapeDtypeStruct((n, D_MODEL), _F32),
        compiler_params=_cparams(("arbitrary",)),
        name="out_proj_residual",
    )(x2, oa, ob, oc, w, g)


def _head_perm_cols(width_per_head, order):
    return np.concatenate([np.arange(h * width_per_head, (h + 1) * width_per_head) for h in order])


def _layout_w_in(w):
    widths = [256, 256, 256, 256, 512, 128, 128, 128, 128, 128, 128, 24, 512, 256, 256, 256, 4, 256]
    offs = np.concatenate([[0], np.cumsum(widths)])
    (qa, ka, va, za, qb, kc, vc, ks, vs, kw, vw, gb, zb, qc, kf, vf, fc, zc) = [
        w[:, offs[i]:offs[i + 1]] for i in range(len(widths))]
    scale = HEAD_DIM ** -0.5
    perm = _head_perm_cols(HEAD_DIM, NSA_HEAD_ORDER)
    pad = lambda a: jnp.pad(a, ((0, 0), (0, LANES - a.shape[1])))
    wb = jnp.concatenate([qb[:, perm] * scale, qa * scale, ka, va, kc, vc, ks, vs, kw, vw,
                          qc * scale, kf, vf], axis=1)
    wf = jnp.concatenate([zb[:, perm], za, zc, pad(gb), pad(fc)], axis=1)
    return wb.astype(_BF16), wf.astype(_BF16)


def _static_tables(t):
    tq = TQ_NSA
    n_cmp_pad = t // CMP_STRIDE
    j = np.arange(n_cmp_pad)
    dist_c = np.arange(t)[:, None] - (j[None, :] * CMP_STRIDE + CMP_BLOCK - 1)
    i_, j_ = np.arange(tq)[:, None], np.arange(TK)[None, :]
    dist_d = np.concatenate([d * TK + i_ - j_ for d in range(3)], axis=0)
    n_slc = LANES // NSA_KV_GROUPS
    cmp_start = j * CMP_STRIDE
    cmp_end = cmp_start + CMP_BLOCK - 1
    slc_start = np.arange(n_slc) * SLC_BLOCK
    ov1 = np.clip(np.minimum(cmp_end[:, None], slc_start[None, :] + SLC_BLOCK - 1)
                  - np.maximum(cmp_start[:, None], slc_start[None, :]) + 1, 0, None) / CMP_BLOCK
    ov1[n_cmp_pad - 1:] = 0.0
    ov1 = ov1[:, :min(n_slc, t // SLC_BLOCK)]
    ov = np.zeros((NSA_KV_GROUPS * n_cmp_pad, LANES), np.float32)
    for g in range(NSA_KV_GROUPS):
        ov[g * n_cmp_pad:(g + 1) * n_cmp_pad, g * n_slc:g * n_slc + ov1.shape[1]] = ov1
    ov3 = np.concatenate([ov, ov, ov], axis=0)
    n_blk = N_HEADS_NSA // 2
    gx = np.zeros((LANES, 3 * n_blk * LANES), np.float32)
    for c in range(3):
        for p in range(n_blk):
            for s in range(2):
                head = NSA_HEAD_ORDER[2 * p + s]
                col0 = (c * n_blk + p) * LANES + s * HEAD_DIM
                gx[3 * head + c, col0:col0 + HEAD_DIM] = 1.0
    gx2 = np.concatenate([gx, gx], axis=0)
    return (jnp.asarray(dist_c, jnp.int32), jnp.asarray(dist_d, jnp.int32),
            jnp.asarray(ov3, _BF16), jnp.asarray(gx2, _BF16))


def kernel(x, norm_g, w_in, w_out, forget_b, cmp_w1, cmp_b1, cmp_w2, cmp_pe, rel_bias, final_g):
    b, t, d = x.shape
    depth = norm_g.shape[0]
    assert d == D_MODEL and t % TM_PROJ == 0 and t % (CMP_STRIDE * LANES) == 0
    assert t // SLC_BLOCK <= LANES // NSA_KV_GROUPS and TQ_NSA == TK
    n_cmp_pad = t // CMP_STRIDE

    dist_c, dist_d, ov3, gx2 = _static_tables(t)
    perm_bias = rel_bias[:, np.asarray(NSA_HEAD_ORDER)]
    bias_c = _bias_table(perm_bias, dist_c, 32)
    bias_d = _bias_table(perm_bias, dist_d, 32).reshape(N_HEADS_NSA, 3, TQ_NSA, TK)

    perm_rows = _head_perm_cols(HEAD_DIM, NSA_HEAD_ORDER)
    x2 = x.reshape(b * t, d)
    out = None
    for l in range(depth):
        wb, wf = _layout_w_in(w_in[l])
        pb, pf = _proj(x2, norm_g[l].reshape(1, d), wb, wf)
        pb3 = pb.reshape(b, t, PB_BLOCKS * LANES)
        pf3 = pf.reshape(b, t, PF_BLOCKS * LANES)

        fb_row = jnp.pad(forget_b[l], (0, LANES - N_HEADS_FOX)).reshape(1, LANES)
        ccol, crow = _fgate(pf3, fb_row)

        kcvc = pb3[:, :, PB_KC * LANES:(PB_VC + 1) * LANES]
        halves = kcvc.reshape(b, n_cmp_pad, CMP_STRIDE, 2, NSA_KV_GROUPS, HEAD_DIM)
        halves = halves.transpose(0, 3, 4, 1, 2, 5).reshape(
            b, 2, NSA_KV_GROUPS, n_cmp_pad, CMP_STRIDE * HEAD_DIM)
        w2 = cmp_w2[l]
        zeros = jnp.zeros_like(w2)
        w2p = jnp.stack([jnp.concatenate([w2, zeros], axis=-1),
                         jnp.concatenate([zeros, w2], axis=-1)], axis=1).astype(_BF16)
        kvc = _compress(halves, cmp_w1[l].astype(_BF16),
                        cmp_pe[l].reshape(2, 1, CMP_BLOCK * HEAD_DIM).astype(_BF16),
                        cmp_b1[l].reshape(2, 1, CMP_HIDDEN), w2p)

        o_a = _sb_attention(pb3, pf3)
        o_b = _nsa_attention(pb3, pf3, kvc, bias_c, bias_d, ov3, gx2)
        o_c = _fox_attention(pb3, pf3, ccol, crow)

        wo = w_out[l]
        wo = jnp.concatenate([wo[:N_HEADS_SB * HEAD_DIM],
                              wo[N_HEADS_SB * HEAD_DIM:][:N_HEADS_NSA * HEAD_DIM][perm_rows],
                              wo[(N_HEADS_SB + N_HEADS_NSA) * HEAD_DIM:]], axis=0).astype(_BF16)
        last = l == depth - 1
        x2 = _out_proj(x2, o_a.reshape(b * t, -1), o_b.reshape(b * t, -1), o_c.reshape(b * t, -1),
                       wo, final_g.reshape(1, d), last)
    return x2.reshape(b, t, d)
```

```python
import functools
import math

import jax
import jax.numpy as jnp
import numpy as np
from jax import lax
from jax.experimental import pallas as pl
from jax.experimental.pallas import tpu as pltpu

D_MODEL = 1024
HEAD_DIM = 64
N_HEADS_SB = 4
N_HEADS_FOX = 4
N_HEADS_NSA = 8
NSA_KV_GROUPS = 2
NSA_HPG = N_HEADS_NSA // NSA_KV_GROUPS
CMP_BLOCK = 32
CMP_STRIDE = 16
CMP_HIDDEN = 256
SLC_BLOCK = 64
SLC_TOP = 16
WINDOW = 512
REL_BUCKETS = 32
REL_MAX_DIST = 128
FORCE_SCORE = 1e4
RMS_EPS = 1e-6
NEG = -1e30

LANES = 128
SUBLANES = 8
VMEM_LIMIT = 56 * 1024 * 1024

TM_PROJ = 512
TQ_SB = 512
TQ_FOX = 512
TQ_NSA = 128
TK = 128

PB_QB, PB_QA, PB_KA, PB_VA = 0, 4, 6, 8
PB_KC, PB_VC, PB_KS, PB_VS, PB_KW, PB_VW = 10, 11, 12, 13, 14, 15
PB_QC, PB_KF, PB_VF = 16, 18, 20
PB_BLOCKS = 22
PF_ZB, PF_ZA, PF_ZC, PF_GB, PF_FC = 0, 4, 6, 8, 9
PF_BLOCKS = 10

NSA_HEAD_ORDER = [0, 4, 1, 5, 2, 6, 3, 7]

_F32 = jnp.float32
_BF16 = jnp.bfloat16


def _cparams(sem):
    return pltpu.CompilerParams(dimension_semantics=sem, vmem_limit_bytes=VMEM_LIMIT)


def _dot(a, b):
    return jnp.dot(a, b, preferred_element_type=_F32)


def _dot_nt(a, b):
    return lax.dot_general(a, b, (((1,), (1,)), ((), ())), preferred_element_type=_F32)


def _split2(x):
    hi = x.astype(_BF16)
    lo = (x - hi.astype(_F32)).astype(_BF16)
    return jnp.concatenate([hi, lo], axis=1)


def _split3(x):
    h1 = x.astype(_BF16)
    r1 = x - h1.astype(_F32)
    h2 = r1.astype(_BF16)
    h3 = (r1 - h2.astype(_F32)).astype(_BF16)
    return jnp.concatenate([h1, h2, h3], axis=1)


def _sigmoid(x):
    return 1.0 / (1.0 + jnp.exp(-x))


def _silu(x):
    return x * _sigmoid(x)


def _rel_bucket_np(n):
    n = np.maximum(n, 0)
    max_exact = REL_BUCKETS // 2
    nf = np.maximum(n, 1).astype(np.float64)
    large = max_exact + (np.log(nf / max_exact) / math.log(REL_MAX_DIST / max_exact)
                         * (REL_BUCKETS - max_exact)).astype(np.int64)
    large = np.minimum(large, REL_BUCKETS - 1)
    return np.where(n < max_exact, n, large)


def _bucket_thresholds():
    n = np.arange(0, 4 * REL_MAX_DIST)
    bk = _rel_bucket_np(n)
    assert np.all(np.diff(bk) >= 0) and bk[-1] == REL_BUCKETS - 1
    return [int(np.argmax(bk >= b)) for b in range(REL_BUCKETS)]


_BUCKET_THR = _bucket_thresholds()


def _bias_kernel(tab_ref, dist_ref, o_ref):
    n = dist_ref[...]
    acc = [jnp.full(n.shape, tab_ref[0, h], _F32) for h in range(N_HEADS_NSA)]
    for b in range(1, REL_BUCKETS):
        ge = n >= _BUCKET_THR[b]
        for h in range(N_HEADS_NSA):
            acc[h] = jnp.where(ge, tab_ref[b, h], acc[h])
    for h in range(N_HEADS_NSA):
        o_ref[h] = acc[h]


def _bias_table(rel_bias, dist, rows):
    n_rows, n_cols = dist.shape
    return pl.pallas_call(
        _bias_kernel,
        grid=(n_rows // rows,),
        in_specs=[pl.BlockSpec(memory_space=pltpu.SMEM),
                  pl.BlockSpec((rows, n_cols), lambda i: (i, 0))],
        out_specs=pl.BlockSpec((N_HEADS_NSA, rows, n_cols), lambda i: (0, i, 0)),
        out_shape=jax.ShapeDtypeStruct((N_HEADS_NSA, n_rows, n_cols), _F32),
        compiler_params=_cparams(("arbitrary",)),
        name="rel_bias_table",
    )(rel_bias, dist)


def _proj_kernel(x_ref, g_ref, wb_ref, wf_ref, pb_ref, pf_ref):
    x = x_ref[...]
    y = x * lax.rsqrt(jnp.mean(x * x, axis=-1, keepdims=True) + RMS_EPS)
    h = (y * g_ref[...]).astype(_BF16)
    chunk = 4 * LANES
    for c in range(0, PB_BLOCKS * LANES, chunk):
        w = min(chunk, PB_BLOCKS * LANES - c)
        pb_ref[:, c:c + w] = _dot(h, wb_ref[:, c:c + w]).astype(_BF16)
    for c in range(0, PF_BLOCKS * LANES, chunk):
        w = min(chunk, PF_BLOCKS * LANES - c)
        pf_ref[:, c:c + w] = _dot(h, wf_ref[:, c:c + w])


def _proj(x2, g, wb, wf):
    n = x2.shape[0]
    return pl.pallas_call(
        _proj_kernel,
        grid=(n // TM_PROJ,),
        in_specs=[pl.BlockSpec((TM_PROJ, D_MODEL), lambda i: (i, 0)),
                  pl.BlockSpec((1, D_MODEL), lambda i: (0, 0)),
                  pl.BlockSpec((D_MODEL, PB_BLOCKS * LANES), lambda i: (0, 0)),
                  pl.BlockSpec((D_MODEL, PF_BLOCKS * LANES), lambda i: (0, 0))],
        out_specs=[pl.BlockSpec((TM_PROJ, PB_BLOCKS * LANES), lambda i: (i, 0)),
                   pl.BlockSpec((TM_PROJ, PF_BLOCKS * LANES), lambda i: (i, 0))],
        out_shape=[jax.ShapeDtypeStruct((n, PB_BLOCKS * LANES), _BF16),
                   jax.ShapeDtypeStruct((n, PF_BLOCKS * LANES), _F32)],
        compiler_params=_cparams(("arbitrary",)),
        name="rmsnorm_in_proj",
    )(x2, g, wb, wf)


N_SPLIT = 3


def _fgate_tables():
    n_pairs = N_HEADS_FOX // 2
    pq = np.zeros((N_SPLIT * LANES, n_pairs * LANES), np.float32)
    pk = np.zeros_like(pq)
    ones_q = np.zeros((1, n_pairs * LANES), np.float32)
    ones_k = np.zeros_like(ones_q)
    for head in range(N_HEADS_FOX):
        pair, slot = divmod(head, 2)
        base = pair * LANES + (HEAD_DIM if slot == 0 else 0)
        for j in range(N_SPLIT):
            pq[j * LANES + head, base + j] = 1.0
            pk[j * LANES + head, base + N_SPLIT + j] = 1.0
        ones_q[0, base + N_SPLIT:base + 2 * N_SPLIT] = 1.0
        ones_k[0, base:base + N_SPLIT] = 1.0
    return (jnp.asarray(pq, _BF16), jnp.asarray(pk, _BF16),
            jnp.asarray(ones_q), jnp.asarray(ones_k))


def _fgate_kernel(fc_ref, fb_ref, pq_ref, pk_ref, oq_ref, ok_ref, augq_ref, augk_ref):
    t = fc_ref.shape[1]
    z = fc_ref[0] + fb_ref[...]
    logf = jnp.minimum(z, 0.0) - jnp.log1p(jnp.exp(-jnp.abs(z)))
    row = lax.broadcasted_iota(jnp.int32, (t, LANES), 0)
    c = logf
    shift = 1
    while shift < t:
        c = c + jnp.where(row >= shift, pltpu.roll(c, shift, axis=0), 0.0)
        shift *= 2
    c3 = _split3(c)
    aq = _dot(c3, pq_ref[...]) + oq_ref[...]
    ak = ok_ref[...] - _dot(c3, pk_ref[...])
    for p in range(N_HEADS_FOX // 2):
        augq_ref[0, p] = aq[:, p * LANES:(p + 1) * LANES].astype(_BF16)
        augk_ref[0, p] = ak[:, p * LANES:(p + 1) * LANES].astype(_BF16)


def _fgate(pf3, fb_row):
    b, t, _ = pf3.shape
    n_pairs = N_HEADS_FOX // 2
    tables = _fgate_tables()
    full = lambda a: pl.BlockSpec(a.shape, lambda i: (0,) * a.ndim)
    aug = lambda: pl.BlockSpec((1, n_pairs, t, LANES), lambda i: (i, 0, 0, 0))
    return pl.pallas_call(
        _fgate_kernel,
        grid=(b,),
        in_specs=[pl.BlockSpec((1, t, LANES), lambda i: (i, 0, PF_FC)),
                  pl.BlockSpec((1, LANES), lambda i: (0, 0))] + [full(a) for a in tables],
        out_specs=[aug(), aug()],
        out_shape=[jax.ShapeDtypeStruct((b, n_pairs, t, LANES), _BF16),
                   jax.ShapeDtypeStruct((b, n_pairs, t, LANES), _BF16)],
        compiler_params=_cparams(("arbitrary",)),
        name="forget_gate_cumsum",
    )(pf3, fb_row, *tables)


def _compress_kernel(x_ref, w1_ref, pe_ref, b1_ref, w2_ref, o_ref):
    nc = x_ref.shape[3]
    half = CMP_STRIDE * HEAD_DIM
    w1 = w1_ref[0]
    c1 = _dot(jnp.broadcast_to(pe_ref[0], (SUBLANES, 2 * half)), w1)[0:1] + b1_ref[0]
    out = jnp.zeros((nc, LANES), _F32)
    for g in range(NSA_KV_GROUPS):
        xg = x_ref[0, 0, g]
        a = _dot(xg, w1[:half])
        bb = _dot(xg, w1[half:])
        h = a + pltpu.roll(bb, nc - 1, axis=0) + c1
        out = out + _dot(_silu(h).astype(_BF16), w2_ref[0, g])
    o_ref[0, 0] = out.astype(_BF16)


def _compress(halves, w1, pe, b1, w2p):
    b, _, g, nc, width = halves.shape
    return pl.pallas_call(
        _compress_kernel,
        grid=(b, 2),
        in_specs=[pl.BlockSpec((1, 1, g, nc, width), lambda i, k: (i, k, 0, 0, 0)),
                  pl.BlockSpec((1, 2 * width, CMP_HIDDEN), lambda i, k: (k, 0, 0)),
                  pl.BlockSpec((1, 1, 2 * width), lambda i, k: (k, 0, 0)),
                  pl.BlockSpec((1, 1, CMP_HIDDEN), lambda i, k: (k, 0, 0)),
                  pl.BlockSpec((1, g, CMP_HIDDEN, LANES), lambda i, k: (k, 0, 0, 0))],
        out_specs=pl.BlockSpec((1, 1, nc, LANES), lambda i, k: (i, k, 0, 0)),
        out_shape=jax.ShapeDtypeStruct((b, 2, nc, LANES), _BF16),
        compiler_params=_cparams(("arbitrary", "arbitrary")),
        name="nsa_compress",
    )(halves, w1, pe, b1, w2p)


def _sb_kernel(q_ref, k_ref, v_ref, z_ref, o_ref, qh_ref, c_ref, acc_ref):
    qi = pl.program_id(2)
    tq = q_ref.shape[1]
    lane = lax.broadcasted_iota(jnp.int32, (tq, LANES), 1)
    q2 = q_ref[0]
    qh = [jnp.where(lane < HEAD_DIM, q2, jnp.zeros_like(q2)),
          jnp.where(lane >= HEAD_DIM, q2, jnp.zeros_like(q2))]
    r_i = lax.broadcasted_iota(jnp.int32, (2 * TK, 2 * TK), 0) & (TK - 1)
    c_i = lax.broadcasted_iota(jnp.int32, (2 * TK, 2 * TK), 1)
    uu = jnp.where((c_i >= TK) | (r_i >= c_i), 1.0, 0.0).astype(_BF16)
    c_ref[...] = jnp.zeros_like(c_ref)
    acc_ref[...] = jnp.zeros_like(acc_ref)
    for h in range(2):
        qh_ref[h] = qh[h]
    row = lax.broadcasted_iota(jnp.int32, (tq, LANES), 0)
    n_blocks = tq // TK

    def chunk(start, diagonal):
        k2 = k_ref[0, pl.ds(start, tq), :]
        v2 = v_ref[0, pl.ds(start, tq), :]
        for h in range(2):
            s = _dot_nt(qh_ref[h], k2)
            carry = c_ref[h]
            w_blocks = [None] * n_blocks
            for c in reversed(range(n_blocks)):
                z = s[:, c * TK:(c + 1) * TK]
                l1m = -(jnp.maximum(z, 0.0) + jnp.log1p(jnp.exp(-jnp.abs(z))))
                if diagonal:
                    mask = lane + c * TK < row
                    l1m = jnp.where(mask, l1m, 0.0)
                rc = _dot(_split2(l1m), uu)
                w = jnp.exp(z + rc[:, :TK] + carry)
                if diagonal:
                    w = jnp.where(mask, w, 0.0)
                w_blocks[c] = w.astype(_BF16)
                carry = carry + rc[:, TK:]
            acc_ref[h] += _dot(jnp.concatenate(w_blocks, axis=1), v2)
            c_ref[h] = carry

    chunk(pl.multiple_of(qi * tq, tq), True)

    def far(it, carry):
        chunk(pl.multiple_of((qi - 1 - it) * tq, tq), False)
        return carry

    lax.fori_loop(0, qi, far, 0)
    o = jnp.where(lane < HEAD_DIM, acc_ref[0], acc_ref[1])
    o_ref[0] = (o * _silu(z_ref[0])).astype(o_ref.dtype)


def _sb_attention(pb3, pf3):
    b, t, _ = pb3.shape
    n_pairs = N_HEADS_SB // 2
    return pl.pallas_call(
        _sb_kernel,
        grid=(b, n_pairs, t // TQ_SB),
        in_specs=[pl.BlockSpec((1, TQ_SB, LANES), lambda i, p, q: (i, q, PB_QA + p)),
                  pl.BlockSpec((1, t, LANES), lambda i, p, q: (i, 0, PB_KA + p)),
                  pl.BlockSpec((1, t, LANES), lambda i, p, q: (i, 0, PB_VA + p)),
                  pl.BlockSpec((1, TQ_SB, LANES), lambda i, p, q: (i, q, PF_ZA + p))],
        out_specs=pl.BlockSpec((1, TQ_SB, LANES), lambda i, p, q: (i, q, p)),
        out_shape=jax.ShapeDtypeStruct((b, t, n_pairs * LANES), _BF16),
        scratch_shapes=[pltpu.VMEM((2, TQ_SB, LANES), _BF16),
                        pltpu.VMEM((2, TQ_SB, LANES), _F32),
                        pltpu.VMEM((2, TQ_SB, LANES), _F32)],
        compiler_params=_cparams(("arbitrary", "arbitrary", "arbitrary")),
        name="stick_breaking_attention",
    )(pb3, pb3, pb3, pf3)


def _fox_kernel(q_ref, k_ref, v_ref, augq_ref, augk_ref, z_ref, o_ref, qh_ref, m_ref, acc_ref):
    qi = pl.program_id(2)
    tq = q_ref.shape[1]
    tk = tq
    lane = lax.broadcasted_iota(jnp.int32, (tq, LANES), 1)
    row = lax.broadcasted_iota(jnp.int32, (tq, LANES), 0)
    keep = [lane < HEAD_DIM, lane >= HEAD_DIM]
    q2 = q_ref[0]
    aq = augq_ref[0, 0]
    for h in range(2):
        qh_ref[h] = jnp.where(keep[h], q2, aq)
    m_ref[...] = jnp.full_like(m_ref, NEG)
    acc_ref[...] = jnp.zeros_like(acc_ref)

    def chunk(start, diagonal):
        k2 = k_ref[0, pl.ds(start, tk), :]
        v2 = v_ref[0, pl.ds(start, tk), :]
        ak = augk_ref[0, 0, pl.ds(start, tk), :]
        for h in range(2):
            kk = jnp.where(keep[h], k2, ak)
            vv = jnp.where(keep[h], v2, jnp.ones_like(v2))
            s = _dot_nt(qh_ref[h], kk)
            blocks = [s[:, c * LANES:(c + 1) * LANES] for c in range(tk // LANES)]
            if diagonal:
                blocks = [jnp.where(lane + c * LANES <= row, blk, NEG) for c, blk in enumerate(blocks)]
            mx = blocks[0]
            for blk in blocks[1:]:
                mx = jnp.maximum(mx, blk)
            m_old = m_ref[h]
            m_new = jnp.maximum(m_old, jnp.max(mx, axis=-1, keepdims=True))
            p = jnp.concatenate([jnp.exp(blk - m_new).astype(_BF16) for blk in blocks], axis=1)
            acc_ref[h] = jnp.exp(m_old - m_new) * acc_ref[h] + _dot(p, vv)
            m_ref[h] = m_new

    def far(c, carry):
        chunk(pl.multiple_of(c * tk, tk), False)
        return carry

    lax.fori_loop(0, qi, far, 0)
    chunk(pl.multiple_of(qi * tk, tk), True)
    outs = []
    for h in range(2):
        acc = acc_ref[h]
        outs.append(acc / pltpu.roll(acc, HEAD_DIM, axis=1))
    o = jnp.where(keep[0], outs[0], outs[1])
    o_ref[0] = (o * _silu(z_ref[0])).astype(o_ref.dtype)


def _fox_attention(pb3, pf3, augq, augk):
    b, t, _ = pb3.shape
    n_pairs = N_HEADS_FOX // 2
    return pl.pallas_call(
        _fox_kernel,
        grid=(b, n_pairs, t // TQ_FOX),
        in_specs=[pl.BlockSpec((1, TQ_FOX, LANES), lambda i, p, q: (i, q, PB_QC + p)),
                  pl.BlockSpec((1, t, LANES), lambda i, p, q: (i, 0, PB_KF + p)),
                  pl.BlockSpec((1, t, LANES), lambda i, p, q: (i, 0, PB_VF + p)),
                  pl.BlockSpec((1, 1, TQ_FOX, LANES), lambda i, p, q: (i, p, q, 0)),
                  pl.BlockSpec((1, 1, t, LANES), lambda i, p, q: (i, p, 0, 0)),
                  pl.BlockSpec((1, TQ_FOX, LANES), lambda i, p, q: (i, q, PF_ZC + p))],
        out_specs=pl.BlockSpec((1, TQ_FOX, LANES), lambda i, p, q: (i, q, p)),
        out_shape=jax.ShapeDtypeStruct((b, t, n_pairs * LANES), _BF16),
        scratch_shapes=[pltpu.VMEM((2, TQ_FOX, LANES), _BF16),
                        pltpu.VMEM((2, TQ_FOX, LANES), _F32),
                        pltpu.VMEM((2, TQ_FOX, LANES), _F32)],
        compiler_params=_cparams(("arbitrary", "arbitrary", "arbitrary")),
        name="forgetting_attention",
    )(pb3, pb3, pb3, augq, augk, pf3)


def _nsa_kernel(q_ref, kc_ref, vc_ref, ks_ref, vs_ref, kw_ref, vw_ref, gl_ref, z_ref,
                bc_ref, bd_ref, ov_ref, gx_ref, o_ref,
                m_ref, acc_ref, osum_ref, qz_ref, qsel_ref):
    qi = pl.program_id(1)
    tq = q_ref.shape[1]
    hpg = NSA_HPG
    rows = hpg * tq
    n_cmp = kc_ref.shape[2]
    lane = lax.broadcasted_iota(jnp.int32, (tq, LANES), 1)
    qrow = qi * tq + lax.broadcasted_iota(jnp.int32, (tq, LANES), 0)
    lane_r = lax.broadcasted_iota(jnp.int32, (rows, LANES), 1)
    irow_r = lax.broadcasted_iota(jnp.int32, (rows, LANES), 0) & (tq - 1)
    half = [lane_r < HEAD_DIM, lane_r >= HEAD_DIM]

    gates = _dot(_split2(_sigmoid(gl_ref[0])), gx_ref[...])

    def gate(c, h):
        return gates[:, (c * hpg + h) * LANES:(c * hpg + h + 1) * LANES]

    q4 = jnp.concatenate([q_ref[0, :, h * LANES:(h + 1) * LANES] for h in range(hpg)], axis=0)
    for g in range(NSA_KV_GROUPS):
        qz_ref[g] = jnp.where(half[g], q4, jnp.zeros_like(q4))

    def heads_to_blocks(x0, x1):
        return [jnp.where(lane < HEAD_DIM, x0[h * tq:(h + 1) * tq], x1[h * tq:(h + 1) * tq])
                for h in range(hpg)]

    kc = kc_ref[0, 0]
    vc = vc_ref[0, 0]
    jc = lax.broadcasted_iota(jnp.int32, (rows, n_cmp), 1)
    qc_pos = qi * tq + (lax.broadcasted_iota(jnp.int32, (rows, n_cmp), 0) & (tq - 1))
    cmask = (jc * CMP_STRIDE + (CMP_BLOCK - 1) <= qc_pos) & (jc < n_cmp - 1)
    psum, o_cmp = [], []
    for g in range(NSA_KV_GROUPS):
        sc = _dot_nt(qz_ref[g], kc) + bc_ref[g * hpg:(g + 1) * hpg].reshape(rows, n_cmp)
        sc = jnp.where(cmask, sc, NEG)
        mx = jnp.max(sc, axis=-1, keepdims=True)
        e = jnp.where(cmask, jnp.exp(sc - mx), 0.0)
        den = jnp.sum(e, axis=-1, keepdims=True)
        pc = e * jnp.where(den > 0.0, 1.0 / den, 0.0)
        psum.append(jnp.sum(pc.reshape(hpg, tq, n_cmp), axis=0))
        o_cmp.append(_dot(pc.astype(_BF16), vc))
    for h, blk_o in enumerate(heads_to_blocks(*o_cmp)):
        osum_ref[h] = gate(0, h) * blk_o

    imp = _dot(_split3(jnp.concatenate(psum, axis=1)), ov_ref[...])
    blk = lane & (SLC_BLOCK - 1)
    cur = qrow >> int(math.log2(SLC_BLOCK))
    forced = (blk == 0) | (blk == cur) | (blk == cur - 1)
    imp = jnp.where(forced, FORCE_SCORE, imp)
    imp = jnp.where(blk > cur, -FORCE_SCORE, imp)
    imp_t = imp.T
    n_sel = LANES // NSA_KV_GROUPS
    sub = lax.broadcasted_iota(jnp.int32, (n_sel, tq), 0)
    neg_t = []
    for g in range(NSA_KV_GROUPS):
        a = imp_t[g * n_sel:(g + 1) * n_sel]
        cnt = jnp.zeros((n_sel, tq), _F32)
        for j in range(n_sel):
            rj = jnp.broadcast_to(a[j:j + 1], (n_sel, tq))
            tie = jnp.where(sub > j, jnp.where(rj == a, 1.0, 0.0), 0.0)
            cnt = cnt + jnp.where(rj > a, 1.0, tie)
        neg_t.append(jnp.where(cnt < float(SLC_TOP), 0.0, NEG))
    selneg = jnp.concatenate(neg_t[::-1], axis=0).T.astype(_BF16)
    selneg4 = jnp.concatenate([selneg] * hpg, axis=0)
    for g in range(NSA_KV_GROUPS):
        qsel_ref[g] = jnp.where(half[g], q4, selneg4)

    def reset():
        m_ref[...] = jnp.full_like(m_ref, NEG)
        acc_ref[...] = jnp.zeros_like(acc_ref)

    def chunk(q_src, k_ref, v_ref, start, tk, onehot, extra):
        k2 = k_ref[0, pl.ds(start, tk), :]
        v2 = v_ref[0, pl.ds(start, tk), :]
        lane_k = lax.broadcasted_iota(jnp.int32, (tk, LANES), 1)
        key_blk = (start + lax.broadcasted_iota(jnp.int32, (tk, LANES), 0)) >> int(math.log2(SLC_BLOCK))
        oh = jnp.where((lane_k & (SLC_BLOCK - 1)) == key_blk, 1.0, 0.0).astype(_BF16)
        for g in range(NSA_KV_GROUPS):
            keep = (lane_k < HEAD_DIM) if g == 0 else (lane_k >= HEAD_DIM)
            kk = jnp.where(keep, k2, oh) if onehot else k2
            vv = jnp.where(keep, v2, jnp.ones_like(v2))
            s = _dot_nt(q_src[g], kk)
            blocks = [s[:, c * LANES:(c + 1) * LANES] for c in range(tk // LANES)]
            if extra is not None:
                blocks = [blk_s + extra(g) for blk_s in blocks]
            mx = blocks[0]
            for blk_s in blocks[1:]:
                mx = jnp.maximum(mx, blk_s)
            m_old = m_ref[g]
            m_new = jnp.maximum(m_old, jnp.max(mx, axis=-1, keepdims=True))
            p = jnp.concatenate([jnp.exp(blk_s - m_new).astype(_BF16) for blk_s in blocks], axis=1)
            acc_ref[g] = jnp.exp(m_old - m_new) * acc_ref[g] + _dot(p, vv)
            m_ref[g] = m_new

    def finish(c):
        outs = []
        for g in range(NSA_KV_GROUPS):
            acc = acc_ref[g]
            den = pltpu.roll(acc, HEAD_DIM, axis=1)
            outs.append(acc * jnp.where(den > 0.0, 1.0 / den, 0.0))
        for h, blk_o in enumerate(heads_to_blocks(*outs)):
            osum_ref[h] += gate(c, h) * blk_o

    def near_bias(d):
        return lambda g: bd_ref[g * hpg:(g + 1) * hpg, d].reshape(rows, TK)

    causal = jnp.where(lane_r <= irow_r, 0.0, NEG)

    def diag_bias(g):
        return near_bias(0)(g) + causal

    reset()
    n_far = jnp.maximum(qi - 1, 0)
    big = 4 * TK
    n_big = n_far >> 2

    def sel_big(c, carry):
        chunk(qsel_ref, ks_ref, vs_ref, pl.multiple_of(c * big, big), big, True, None)
        return carry

    def sel_small(r, carry):
        chunk(qsel_ref, ks_ref, vs_ref, pl.multiple_of((n_big * 4 + r) * TK, TK), TK, True, None)
        return carry

    lax.fori_loop(0, n_big, sel_big, 0)
    lax.fori_loop(0, n_far - n_big * 4, sel_small, 0)

    @pl.when(qi >= 1)
    def _():
        chunk(qsel_ref, ks_ref, vs_ref, pl.multiple_of((qi - 1) * TK, TK), TK, True, near_bias(1))

    chunk(qsel_ref, ks_ref, vs_ref, pl.multiple_of(qi * TK, TK), TK, True, diag_bias)
    finish(1)

    reset()

    def win_far(kt, carry):
        dist = qi * tq + irow_r - (kt * TK + lane_r)
        wmask = jnp.where(dist < WINDOW, 0.0, NEG)
        chunk(qz_ref, kw_ref, vw_ref, pl.multiple_of(kt * TK, TK), TK, False, lambda g: wmask)
        return carry

    lax.fori_loop(jnp.maximum(qi - WINDOW // TK, 0), n_far, win_far, 0)

    @pl.when(qi >= 1)
    def _():
        chunk(qz_ref, kw_ref, vw_ref, pl.multiple_of((qi - 1) * TK, TK), TK, False, near_bias(1))

    chunk(qz_ref, kw_ref, vw_ref, pl.multiple_of(qi * TK, TK), TK, False, diag_bias)
    finish(2)

    for h in range(hpg):
        zp = z_ref[0, :, h * LANES:(h + 1) * LANES]
        o_ref[0, :, h * LANES:(h + 1) * LANES] = (osum_ref[h] * _silu(zp)).astype(o_ref.dtype)


def _nsa_attention(pb3, pf3, kvc, bias_c, bias_d, ov, gx):
    b, t, _ = pb3.shape
    n_cmp = kvc.shape[2]
    n_blk = N_HEADS_NSA // 2
    full = lambda shape: pl.BlockSpec(shape, lambda i, q: (0,) * len(shape))
    kv_spec = lambda col: pl.BlockSpec((1, t, LANES), lambda i, q: (i, 0, col))
    return pl.pallas_call(
        _nsa_kernel,
        grid=(b, t // TQ_NSA),
        in_specs=[pl.BlockSpec((1, TQ_NSA, n_blk * LANES), lambda i, q: (i, q, PB_QB // n_blk)),
                  pl.BlockSpec((1, 1, n_cmp, LANES), lambda i, q: (i, 0, 0, 0)),
                  pl.BlockSpec((1, 1, n_cmp, LANES), lambda i, q: (i, 1, 0, 0)),
                  kv_spec(PB_KS), kv_spec(PB_VS), kv_spec(PB_KW), kv_spec(PB_VW),
                  pl.BlockSpec((1, TQ_NSA, LANES), lambda i, q: (i, q, PF_GB)),
                  pl.BlockSpec((1, TQ_NSA, n_blk * LANES), lambda i, q: (i, q, PF_ZB // n_blk)),
                  pl.BlockSpec((N_HEADS_NSA, TQ_NSA, n_cmp), lambda i, q: (0, q, 0)),
                  full(bias_d.shape), full(ov.shape), full(gx.shape)],
        out_specs=pl.BlockSpec((1, TQ_NSA, n_blk * LANES), lambda i, q: (i, q, 0)),
        out_shape=jax.ShapeDtypeStruct((b, t, n_blk * LANES), _BF16),
        scratch_shapes=[pltpu.VMEM((NSA_KV_GROUPS, NSA_HPG * TQ_NSA, LANES), _F32),
                        pltpu.VMEM((NSA_KV_GROUPS, NSA_HPG * TQ_NSA, LANES), _F32),
                        pltpu.VMEM((n_blk, TQ_NSA, LANES), _F32),
                        pltpu.VMEM((NSA_KV_GROUPS, NSA_HPG * TQ_NSA, LANES), _BF16),
                        pltpu.VMEM((NSA_KV_GROUPS, NSA_HPG * TQ_NSA, LANES), _BF16)],
        compiler_params=_cparams(("arbitrary", "arbitrary")),
        name="native_sparse_attention",
    )(pb3, kvc, kvc, pb3, pb3, pb3, pb3, pf3, pf3, bias_c, bias_d, ov, gx)


def _out_kernel(x_ref, oa_ref, ob_ref, oc_ref, w_ref, g_ref, o_ref, *, final_norm):
    na, nb = oa_ref.shape[1], ob_ref.shape[1]
    mix = (_dot(oa_ref[...], w_ref[0:na])
           + _dot(ob_ref[...], w_ref[na:na + nb])
           + _dot(oc_ref[...], w_ref[na + nb:]))
    x = x_ref[...] + mix
    if final_norm:
        x = x * lax.rsqrt(jnp.mean(x * x, axis=-1, keepdims=True) + RMS_EPS) * g_ref[...]
    o_ref[...] = x


def _out_proj(x2, oa, ob, oc, w, g, final_norm):
    n = x2.shape[0]
    row = lambda width: pl.BlockSpec((TM_PROJ, width), lambda i: (i, 0))
    return pl.pallas_call(
        functools.partial(_out_kernel, final_norm=final_norm),
        grid=(n // TM_PROJ,),
        in_specs=[row(D_MODEL), row(oa.shape[1]), row(ob.shape[1]), row(oc.shape[1]),
                  pl.BlockSpec((D_MODEL, D_MODEL), lambda i: (0, 0)),
                  pl.BlockSpec((1, D_MODEL), lambda i: (0, 0))],
        out_specs=row(D_MODEL),
        out_shape=jax.ShapeDtypeStruct((n, D_MODEL), _F32),
        compiler_params=_cparams(("arbitrary",)),
        name="out_proj_residual",
    )(x2, oa, ob, oc, w, g)


def _head_perm_cols(width_per_head, order):
    return np.concatenate([np.arange(h * width_per_head, (h + 1) * width_per_head) for h in order])


def _layout_w_in(w):
    widths = [256, 256, 256, 256, 512, 128, 128, 128, 128, 128, 128, 24, 512, 256, 256, 256, 4, 256]
    offs = np.concatenate([[0], np.cumsum(widths)])
    (qa, ka, va, za, qb, kc, vc, ks, vs, kw, vw, gb, zb, qc, kf, vf, fc, zc) = [
        w[:, offs[i]:offs[i + 1]] for i in range(len(widths))]
    scale = HEAD_DIM ** -0.5
    perm = _head_perm_cols(HEAD_DIM, NSA_HEAD_ORDER)
    pad = lambda a: jnp.pad(a, ((0, 0), (0, LANES - a.shape[1])))
    wb = jnp.concatenate([qb[:, perm] * scale, qa * scale, ka, va, kc, vc, ks, vs, kw, vw,
                          qc * scale, kf, vf], axis=1)
    wf = jnp.concatenate([zb[:, perm], za, zc, pad(gb), pad(fc)], axis=1)
    return wb.astype(_BF16), wf.astype(_BF16)


def _static_tables(t):
    tq = TQ_NSA
    n_cmp_pad = t // CMP_STRIDE
    j = np.arange(n_cmp_pad)
    dist_c = np.arange(t)[:, None] - (j[None, :] * CMP_STRIDE + CMP_BLOCK - 1)
    i_, j_ = np.arange(tq)[:, None], np.arange(TK)[None, :]
    dist_d = np.concatenate([d * TK + i_ - j_ for d in range(3)], axis=0)
    n_slc = LANES // NSA_KV_GROUPS
    cmp_start = j * CMP_STRIDE
    cmp_end = cmp_start + CMP_BLOCK - 1
    slc_start = np.arange(n_slc) * SLC_BLOCK
    ov1 = np.clip(np.minimum(cmp_end[:, None], slc_start[None, :] + SLC_BLOCK - 1)
                  - np.maximum(cmp_start[:, None], slc_start[None, :]) + 1, 0, None) / CMP_BLOCK
    ov1[n_cmp_pad - 1:] = 0.0
    ov1 = ov1[:, :min(n_slc, t // SLC_BLOCK)]
    ov = np.zeros((NSA_KV_GROUPS * n_cmp_pad, LANES), np.float32)
    for g in range(NSA_KV_GROUPS):
        ov[g * n_cmp_pad:(g + 1) * n_cmp_pad, g * n_slc:g * n_slc + ov1.shape[1]] = ov1
    ov3 = np.concatenate([ov, ov, ov], axis=0)
    n_blk = N_HEADS_NSA // 2
    gx = np.zeros((LANES, 3 * n_blk * LANES), np.float32)
    for c in range(3):
        for p in range(n_blk):
            for s in range(2):
                head = NSA_HEAD_ORDER[2 * p + s]
                col0 = (c * n_blk + p) * LANES + s * HEAD_DIM
                gx[3 * head + c, col0:col0 + HEAD_DIM] = 1.0
    gx2 = np.concatenate([gx, gx], axis=0)
    return (jnp.asarray(dist_c, jnp.int32), jnp.asarray(dist_d, jnp.int32),
            jnp.asarray(ov3, _BF16), jnp.asarray(gx2, _BF16))


def kernel(x, norm_g, w_in, w_out, forget_b, cmp_w1, cmp_b1, cmp_w2, cmp_pe, rel_bias, final_g):
    b, t, d = x.shape
    depth = norm_g.shape[0]
    assert d == D_MODEL and t % TM_PROJ == 0 and t % (CMP_STRIDE * LANES) == 0
    assert t // SLC_BLOCK <= LANES // NSA_KV_GROUPS and TQ_NSA == TK
    n_cmp_pad = t // CMP_STRIDE

    dist_c, dist_d, ov3, gx2 = _static_tables(t)
    bias_c = _bias_table(rel_bias, dist_c, 32)
    bias_d = _bias_table(rel_bias, dist_d, 32).reshape(N_HEADS_NSA, 3, TQ_NSA, TK)
    bias_d = bias_d[:, :2] - bias_d[:, 2:3]

    perm_rows = _head_perm_cols(HEAD_DIM, NSA_HEAD_ORDER)
    x2 = x.reshape(b * t, d)
    out = None
    for l in range(depth):
        wb, wf = _layout_w_in(w_in[l])
        pb, pf = _proj(x2, norm_g[l].reshape(1, d), wb, wf)
        pb3 = pb.reshape(b, t, PB_BLOCKS * LANES)
        pf3 = pf.reshape(b, t, PF_BLOCKS * LANES)

        fb_row = jnp.pad(forget_b[l], (0, LANES - N_HEADS_FOX)).reshape(1, LANES)
        augq, augk = _fgate(pf3, fb_row)

        kcvc = pb3[:, :, PB_KC * LANES:(PB_VC + 1) * LANES]
        halves = kcvc.reshape(b, n_cmp_pad, CMP_STRIDE, 2, NSA_KV_GROUPS, HEAD_DIM)
        halves = halves.transpose(0, 3, 4, 1, 2, 5).reshape(
            b, 2, NSA_KV_GROUPS, n_cmp_pad, CMP_STRIDE * HEAD_DIM)
        w2 = cmp_w2[l]
        zeros = jnp.zeros_like(w2)
        w2p = jnp.stack([jnp.concatenate([w2, zeros], axis=-1),
                         jnp.concatenate([zeros, w2], axis=-1)], axis=1).astype(_BF16)
        kvc = _compress(halves, cmp_w1[l].astype(_BF16),
                        cmp_pe[l].reshape(2, 1, CMP_BLOCK * HEAD_DIM).astype(_BF16),
                        cmp_b1[l].reshape(2, 1, CMP_HIDDEN), w2p)

        o_a = _sb_attention(pb3, pf3)
        o_b = _nsa_attention(pb3, pf3, kvc, bias_c, bias_d, ov3, gx2)
        o_c = _fox_attention(pb3, pf3, augq, augk)

        wo = w_out[l]
        wo = jnp.concatenate([wo[:N_HEADS_SB * HEAD_DIM],
                              wo[N_HEADS_SB * HEAD_DIM:][:N_HEADS_NSA * HEAD_DIM][perm_rows],
                              wo[(N_HEADS_SB + N_HEADS_NSA) * HEAD_DIM:]], axis=0).astype(_BF16)
        last = l == depth - 1
        x2 = _out_proj(x2, o_a.reshape(b * t, -1), o_b.reshape(b * t, -1), o_c.reshape(b * t, -1),
                       wo, final_g.reshape(1, d), last)
    return x2.reshape(b, t, d)
```

```python
import functools
import math

import jax
import jax.numpy as jnp
import numpy as np
from jax import lax
from jax.experimental import pallas as pl
from jax.experimental.pallas import tpu as pltpu

D_MODEL = 1024
HEAD_DIM = 64
N_HEADS_SB = 4
N_HEADS_FOX = 4
N_HEADS_NSA = 8
NSA_KV_GROUPS = 2
NSA_HPG = N_HEADS_NSA // NSA_KV_GROUPS
CMP_BLOCK = 32
CMP_STRIDE = 16
CMP_HIDDEN = 256
SLC_BLOCK = 64
SLC_TOP = 16
WINDOW = 512
REL_BUCKETS = 32
REL_MAX_DIST = 128
FORCE_SCORE = 1e4
RMS_EPS = 1e-6
NEG = -1e30

LANES = 128
SUBLANES = 8
VMEM_LIMIT = 56 * 1024 * 1024

TM_PROJ = 512
TQ_SB = 512
TQ_FOX = 512
TQ_NSA = 128
TK = 128

PB_QB, PB_QA, PB_KA, PB_VA = 0, 4, 6, 8
PB_KS, PB_VS, PB_KW, PB_VW = 10, 11, 12, 13
PB_QC, PB_KF, PB_VF = 14, 16, 18
PB_BLOCKS = 20
N_CMP_SLABS = 2 * NSA_KV_GROUPS
PF_ZB, PF_ZA, PF_ZC, PF_GB, PF_FC = 0, 4, 6, 8, 9
PF_BLOCKS = 10

NSA_HEAD_ORDER = [0, 4, 1, 5, 2, 6, 3, 7]

_F32 = jnp.float32
_BF16 = jnp.bfloat16


def _cparams(sem):
    return pltpu.CompilerParams(dimension_semantics=sem, vmem_limit_bytes=VMEM_LIMIT)


def _dot(a, b):
    return jnp.dot(a, b, preferred_element_type=_F32)


def _dot_nt(a, b):
    return lax.dot_general(a, b, (((1,), (1,)), ((), ())), preferred_element_type=_F32)


def _split2(x):
    hi = x.astype(_BF16)
    lo = (x - hi.astype(_F32)).astype(_BF16)
    return jnp.concatenate([hi, lo], axis=1)


def _split3(x):
    h1 = x.astype(_BF16)
    r1 = x - h1.astype(_F32)
    h2 = r1.astype(_BF16)
    h3 = (r1 - h2.astype(_F32)).astype(_BF16)
    return jnp.concatenate([h1, h2, h3], axis=1)


def _sigmoid(x):
    return 1.0 / (1.0 + jnp.exp(-x))


def _silu(x):
    return x * _sigmoid(x)


def _rel_bucket_np(n):
    n = np.maximum(n, 0)
    max_exact = REL_BUCKETS // 2
    nf = np.maximum(n, 1).astype(np.float64)
    large = max_exact + (np.log(nf / max_exact) / math.log(REL_MAX_DIST / max_exact)
                         * (REL_BUCKETS - max_exact)).astype(np.int64)
    large = np.minimum(large, REL_BUCKETS - 1)
    return np.where(n < max_exact, n, large)


def _bucket_thresholds():
    n = np.arange(0, 4 * REL_MAX_DIST)
    bk = _rel_bucket_np(n)
    assert np.all(np.diff(bk) >= 0) and bk[-1] == REL_BUCKETS - 1
    return [int(np.argmax(bk >= b)) for b in range(REL_BUCKETS)]


_BUCKET_THR = _bucket_thresholds()


def _bias_kernel(tab_ref, dist_ref, o_ref):
    n = dist_ref[...]
    acc = [jnp.full(n.shape, tab_ref[0, h], _F32) for h in range(N_HEADS_NSA)]
    for b in range(1, REL_BUCKETS):
        ge = n >= _BUCKET_THR[b]
        for h in range(N_HEADS_NSA):
            acc[h] = jnp.where(ge, tab_ref[b, h], acc[h])
    valid = n >= 0
    for h in range(N_HEADS_NSA):
        o_ref[h] = jnp.where(valid, acc[h], NEG)


def _bias_table(rel_bias, dist, rows):
    n_rows, n_cols = dist.shape
    return pl.pallas_call(
        _bias_kernel,
        grid=(n_rows // rows,),
        in_specs=[pl.BlockSpec(memory_space=pltpu.SMEM),
                  pl.BlockSpec((rows, n_cols), lambda i: (i, 0))],
        out_specs=pl.BlockSpec((N_HEADS_NSA, rows, n_cols), lambda i: (0, i, 0)),
        out_shape=jax.ShapeDtypeStruct((N_HEADS_NSA, n_rows, n_cols), _F32),
        compiler_params=_cparams(("arbitrary",)),
        name="rel_bias_table",
    )(rel_bias, dist)


def _proj_kernel(x_ref, g_ref, wb_ref, wf_ref, wc_ref, pb_ref, pf_ref, pc_ref):
    x = x_ref[...]
    y = x * lax.rsqrt(jnp.mean(x * x, axis=-1, keepdims=True) + RMS_EPS)
    h = (y * g_ref[...]).astype(_BF16)
    for s in range(N_CMP_SLABS):
        pc_ref[0, s] = _dot(h, wc_ref[s]).astype(_BF16)
    chunk = 4 * LANES
    for c in range(0, PB_BLOCKS * LANES, chunk):
        w = min(chunk, PB_BLOCKS * LANES - c)
        pb_ref[:, c:c + w] = _dot(h, wb_ref[:, c:c + w]).astype(_BF16)
    for c in range(0, PF_BLOCKS * LANES, chunk):
        w = min(chunk, PF_BLOCKS * LANES - c)
        pf_ref[:, c:c + w] = _dot(h, wf_ref[:, c:c + w])


def _proj(x2, g, wb, wf, wc, t):
    n = x2.shape[0]
    per_seq = t // TM_PROJ
    return pl.pallas_call(
        _proj_kernel,
        grid=(n // TM_PROJ,),
        in_specs=[pl.BlockSpec((TM_PROJ, D_MODEL), lambda i: (i, 0)),
                  pl.BlockSpec((1, D_MODEL), lambda i: (0, 0)),
                  pl.BlockSpec((D_MODEL, PB_BLOCKS * LANES), lambda i: (0, 0)),
                  pl.BlockSpec((D_MODEL, PF_BLOCKS * LANES), lambda i: (0, 0)),
                  pl.BlockSpec((N_CMP_SLABS, D_MODEL, HEAD_DIM), lambda i: (0, 0, 0))],
        out_specs=[pl.BlockSpec((TM_PROJ, PB_BLOCKS * LANES), lambda i: (i, 0)),
                   pl.BlockSpec((TM_PROJ, PF_BLOCKS * LANES), lambda i: (i, 0)),
                   pl.BlockSpec((1, N_CMP_SLABS, TM_PROJ, HEAD_DIM),
                                lambda i: (i // per_seq, 0, i % per_seq, 0))],
        out_shape=[jax.ShapeDtypeStruct((n, PB_BLOCKS * LANES), _BF16),
                   jax.ShapeDtypeStruct((n, PF_BLOCKS * LANES), _F32),
                   jax.ShapeDtypeStruct((n // t, N_CMP_SLABS, t, HEAD_DIM), _BF16)],
        compiler_params=_cparams(("arbitrary",)),
        name="rmsnorm_in_proj",
    )(x2, g, wb, wf, wc)


N_SPLIT = 3


def _fgate_tables():
    n_pairs = N_HEADS_FOX // 2
    pq = np.zeros((N_SPLIT * LANES, n_pairs * LANES), np.float32)
    pk = np.zeros_like(pq)
    ones_q = np.zeros((1, n_pairs * LANES), np.float32)
    ones_k = np.zeros_like(ones_q)
    for head in range(N_HEADS_FOX):
        pair, slot = divmod(head, 2)
        base = pair * LANES + (HEAD_DIM if slot == 0 else 0)
        for j in range(N_SPLIT):
            pq[j * LANES + head, base + j] = 1.0
            pk[j * LANES + head, base + N_SPLIT + j] = 1.0
        ones_q[0, base + N_SPLIT:base + 2 * N_SPLIT] = 1.0
        ones_k[0, base:base + N_SPLIT] = 1.0
    return (jnp.asarray(pq, _BF16), jnp.asarray(pk, _BF16),
            jnp.asarray(ones_q), jnp.asarray(ones_k))


def _fgate_kernel(fc_ref, fb_ref, pq_ref, pk_ref, oq_ref, ok_ref, augq_ref, augk_ref):
    t = fc_ref.shape[1]
    z = fc_ref[0] + fb_ref[...]
    logf = jnp.minimum(z, 0.0) - jnp.log1p(jnp.exp(-jnp.abs(z)))
    row = lax.broadcasted_iota(jnp.int32, (t, LANES), 0)
    c = logf
    shift = 1
    while shift < t:
        c = c + jnp.where(row >= shift, pltpu.roll(c, shift, axis=0), 0.0)
        shift *= 2
    c3 = _split3(c)
    aq = _dot(c3, pq_ref[...]) + oq_ref[...]
    ak = ok_ref[...] - _dot(c3, pk_ref[...])
    for p in range(N_HEADS_FOX // 2):
        augq_ref[0, p] = aq[:, p * LANES:(p + 1) * LANES].astype(_BF16)
        augk_ref[0, p] = ak[:, p * LANES:(p + 1) * LANES].astype(_BF16)


def _fgate(pf3, fb_row):
    b, t, _ = pf3.shape
    n_pairs = N_HEADS_FOX // 2
    tables = _fgate_tables()
    full = lambda a: pl.BlockSpec(a.shape, lambda i: (0,) * a.ndim)
    aug = lambda: pl.BlockSpec((1, n_pairs, t, LANES), lambda i: (i, 0, 0, 0))
    return pl.pallas_call(
        _fgate_kernel,
        grid=(b,),
        in_specs=[pl.BlockSpec((1, t, LANES), lambda i: (i, 0, PF_FC)),
                  pl.BlockSpec((1, LANES), lambda i: (0, 0))] + [full(a) for a in tables],
        out_specs=[aug(), aug()],
        out_shape=[jax.ShapeDtypeStruct((b, n_pairs, t, LANES), _BF16),
                   jax.ShapeDtypeStruct((b, n_pairs, t, LANES), _BF16)],
        compiler_params=_cparams(("arbitrary",)),
        name="forget_gate_cumsum",
    )(pf3, fb_row, *tables)


def _compress_kernel(x_ref, w1_ref, pe_ref, b1_ref, w2_ref, o_ref):
    nc = x_ref.shape[3]
    half = CMP_STRIDE * HEAD_DIM
    w1 = w1_ref[0]
    c1 = _dot(jnp.broadcast_to(pe_ref[0], (SUBLANES, 2 * half)), w1)[0:1] + b1_ref[0]
    out = jnp.zeros((nc, LANES), _F32)
    for g in range(NSA_KV_GROUPS):
        xg = x_ref[0, 0, g]
        a = _dot(xg, w1[:half])
        bb = _dot(xg, w1[half:])
        h = a + pltpu.roll(bb, nc - 1, axis=0) + c1
        out = out + _dot(_silu(h).astype(_BF16), w2_ref[0, g])
    o_ref[0, 0] = out.astype(_BF16)


def _compress(halves, w1, pe, b1, w2p):
    b, _, g, nc, width = halves.shape
    return pl.pallas_call(
        _compress_kernel,
        grid=(b, 2),
        in_specs=[pl.BlockSpec((1, 1, g, nc, width), lambda i, k: (i, k, 0, 0, 0)),
                  pl.BlockSpec((1, 2 * width, CMP_HIDDEN), lambda i, k: (k, 0, 0)),
                  pl.BlockSpec((1, 1, 2 * width), lambda i, k: (k, 0, 0)),
                  pl.BlockSpec((1, 1, CMP_HIDDEN), lambda i, k: (k, 0, 0)),
                  pl.BlockSpec((1, g, CMP_HIDDEN, LANES), lambda i, k: (k, 0, 0, 0))],
        out_specs=pl.BlockSpec((1, 1, nc, LANES), lambda i, k: (i, k, 0, 0)),
        out_shape=jax.ShapeDtypeStruct((b, 2, nc, LANES), _BF16),
        compiler_params=_cparams(("arbitrary", "arbitrary")),
        name="nsa_compress",
    )(halves, w1, pe, b1, w2p)


def _sb_kernel(q_ref, k_ref, v_ref, z_ref, o_ref, qh_ref, c_ref, acc_ref):
    qi = pl.program_id(2)
    tq = q_ref.shape[1]
    lane = lax.broadcasted_iota(jnp.int32, (tq, LANES), 1)
    q2 = q_ref[0]
    qh = [jnp.where(lane < HEAD_DIM, q2, jnp.zeros_like(q2)),
          jnp.where(lane >= HEAD_DIM, q2, jnp.zeros_like(q2))]
    r_i = lax.broadcasted_iota(jnp.int32, (2 * TK, 2 * TK), 0) & (TK - 1)
    c_i = lax.broadcasted_iota(jnp.int32, (2 * TK, 2 * TK), 1)
    uu = jnp.where((c_i >= TK) | (r_i >= c_i), 1.0, 0.0).astype(_BF16)
    c_ref[...] = jnp.zeros_like(c_ref)
    acc_ref[...] = jnp.zeros_like(acc_ref)
    for h in range(2):
        qh_ref[h] = qh[h]
    row = lax.broadcasted_iota(jnp.int32, (tq, LANES), 0)
    n_blocks = tq // TK

    def chunk(start, diagonal):
        k2 = k_ref[0, pl.ds(start, tq), :]
        v2 = v_ref[0, pl.ds(start, tq), :]
        for h in range(2):
            s = _dot_nt(qh_ref[h], k2)
            carry = c_ref[h]
            w_blocks = [None] * n_blocks
            for c in reversed(range(n_blocks)):
                nz = s[:, c * TK:(c + 1) * TK]
                neg_abs = lax.bitcast_convert_type(
                    lax.bitcast_convert_type(nz, jnp.uint32) | jnp.uint32(0x80000000), _F32)
                l1m = jnp.minimum(nz, 0.0) - jnp.log(1.0 + jnp.exp(neg_abs))
                if diagonal:
                    mask = lane + c * TK < row
                    l1m = jnp.where(mask, l1m, 0.0)
                rc = _dot(_split2(l1m), uu)
                w = jnp.exp((rc[:, :TK] + carry) - nz)
                if diagonal:
                    w = jnp.where(mask, w, 0.0)
                w_blocks[c] = w.astype(_BF16)
                carry = carry + rc[:, TK:]
            acc_ref[h] += _dot(jnp.concatenate(w_blocks, axis=1), v2)
            c_ref[h] = carry

    chunk(pl.multiple_of(qi * tq, tq), True)

    def far(it, carry):
        chunk(pl.multiple_of((qi - 1 - it) * tq, tq), False)
        return carry

    lax.fori_loop(0, qi, far, 0)
    o = jnp.where(lane < HEAD_DIM, acc_ref[0], acc_ref[1])
    o_ref[0] = (o * _silu(z_ref[0])).astype(o_ref.dtype)


def _sb_attention(pb3, pf3):
    b, t, _ = pb3.shape
    n_pairs = N_HEADS_SB // 2
    return pl.pallas_call(
        _sb_kernel,
        grid=(b, n_pairs, t // TQ_SB),
        in_specs=[pl.BlockSpec((1, TQ_SB, LANES), lambda i, p, q: (i, q, PB_QA + p)),
                  pl.BlockSpec((1, t, LANES), lambda i, p, q: (i, 0, PB_KA + p)),
                  pl.BlockSpec((1, t, LANES), lambda i, p, q: (i, 0, PB_VA + p)),
                  pl.BlockSpec((1, TQ_SB, LANES), lambda i, p, q: (i, q, PF_ZA + p))],
        out_specs=pl.BlockSpec((1, TQ_SB, LANES), lambda i, p, q: (i, q, p)),
        out_shape=jax.ShapeDtypeStruct((b, t, n_pairs * LANES), _BF16),
        scratch_shapes=[pltpu.VMEM((2, TQ_SB, LANES), _BF16),
                        pltpu.VMEM((2, TQ_SB, LANES), _F32),
                        pltpu.VMEM((2, TQ_SB, LANES), _F32)],
        compiler_params=_cparams(("arbitrary", "arbitrary", "arbitrary")),
        name="stick_breaking_attention",
    )(pb3, pb3, pb3, pf3)


def _fox_kernel(q_ref, k_ref, v_ref, augq_ref, augk_ref, z_ref, o_ref, qh_ref, m_ref, acc_ref, s_ref):
    qi = pl.program_id(2)
    tq = q_ref.shape[1]
    tk = tq
    lane = lax.broadcasted_iota(jnp.int32, (tq, LANES), 1)
    row = lax.broadcasted_iota(jnp.int32, (tq, LANES), 0)
    keep = [lane < HEAD_DIM, lane >= HEAD_DIM]
    q2 = q_ref[0]
    aq = augq_ref[0, 0]
    for h in range(2):
        qh_ref[h] = jnp.where(keep[h], q2, aq)
    m_ref[...] = jnp.full_like(m_ref, NEG)
    acc_ref[...] = jnp.zeros_like(acc_ref)

    def scores(c, slot):
        start = c * tk if isinstance(c, int) else pl.multiple_of(c * tk, tk)
        k2 = k_ref[0, pl.ds(start, tk), :]
        ak = augk_ref[0, 0, pl.ds(start, tk), :]
        for h in range(2):
            s_ref[slot, h] = _dot_nt(qh_ref[h], jnp.where(keep[h], k2, ak))

    def consume(c, slot, diagonal):
        v2 = v_ref[0, pl.ds(pl.multiple_of(c * tk, tk), tk), :]
        for h in range(2):
            vv = jnp.where(keep[h], v2, jnp.ones_like(v2))
            blocks = [s_ref[slot, h, :, j * LANES:(j + 1) * LANES] for j in range(tk // LANES)]
            if diagonal:
                blocks = [jnp.where(lane + j * LANES <= row, blk, NEG) for j, blk in enumerate(blocks)]
            mx = blocks[0]
            for blk in blocks[1:]:
                mx = jnp.maximum(mx, blk)
            m_old = m_ref[h]
            m_new = jnp.maximum(m_old, jnp.max(mx, axis=-1, keepdims=True))
            p = jnp.concatenate([jnp.exp(blk - m_new).astype(_BF16) for blk in blocks], axis=1)
            acc_ref[h] = jnp.exp(m_old - m_new) * acc_ref[h] + _dot(p, vv)
            m_ref[h] = m_new

    scores(0, 0)
    n_pairs = qi >> 1

    def far_pair(i, carry):
        c = 2 * i
        scores(c + 1, 1)
        consume(c, 0, False)
        scores(c + 2, 0)
        consume(c + 1, 1, False)
        return carry

    lax.fori_loop(0, n_pairs, far_pair, 0)

    @pl.when(qi == 2 * n_pairs)
    def _():
        consume(qi, 0, True)

    @pl.when(qi != 2 * n_pairs)
    def _():
        scores(qi, 1)
        consume(qi - 1, 0, False)
        consume(qi, 1, True)
    outs = []
    for h in range(2):
        acc = acc_ref[h]
        outs.append(acc / pltpu.roll(acc, HEAD_DIM, axis=1))
    o = jnp.where(keep[0], outs[0], outs[1])
    o_ref[0] = (o * _silu(z_ref[0])).astype(o_ref.dtype)


def _fox_attention(pb3, pf3, augq, augk):
    b, t, _ = pb3.shape
    n_pairs = N_HEADS_FOX // 2
    return pl.pallas_call(
        _fox_kernel,
        grid=(b, n_pairs, t // TQ_FOX),
        in_specs=[pl.BlockSpec((1, TQ_FOX, LANES), lambda i, p, q: (i, q, PB_QC + p)),
                  pl.BlockSpec((1, t, LANES), lambda i, p, q: (i, 0, PB_KF + p)),
                  pl.BlockSpec((1, t, LANES), lambda i, p, q: (i, 0, PB_VF + p)),
                  pl.BlockSpec((1, 1, TQ_FOX, LANES), lambda i, p, q: (i, p, q, 0)),
                  pl.BlockSpec((1, 1, t, LANES), lambda i, p, q: (i, p, 0, 0)),
                  pl.BlockSpec((1, TQ_FOX, LANES), lambda i, p, q: (i, q, PF_ZC + p))],
        out_specs=pl.BlockSpec((1, TQ_FOX, LANES), lambda i, p, q: (i, q, p)),
        out_shape=jax.ShapeDtypeStruct((b, t, n_pairs * LANES), _BF16),
        scratch_shapes=[pltpu.VMEM((2, TQ_FOX, LANES), _BF16),
                        pltpu.VMEM((2, TQ_FOX, LANES), _F32),
                        pltpu.VMEM((2, TQ_FOX, LANES), _F32),
                        pltpu.VMEM((2, 2, TQ_FOX, TQ_FOX), _F32)],
        compiler_params=_cparams(("arbitrary", "arbitrary", "arbitrary")),
        name="forgetting_attention",
    )(pb3, pb3, pb3, augq, augk, pf3)


def _nsa_kernel(q_ref, kc_ref, vc_ref, ks_ref, vs_ref, kw_ref, vw_ref, gl_ref, z_ref,
                bc_ref, bd_ref, ov_ref, gx_ref, o_ref,
                m_ref, acc_ref, osum_ref, qz_ref, qsel_ref, gates_ref, psum_ref, s_ref):
    qi = pl.program_id(1)
    tq = q_ref.shape[1]
    hpg = NSA_HPG
    rows = hpg * tq
    n_cmp = kc_ref.shape[2]
    lane = lax.broadcasted_iota(jnp.int32, (tq, LANES), 1)
    qrow = qi * tq + lax.broadcasted_iota(jnp.int32, (tq, LANES), 0)
    lane_r = lax.broadcasted_iota(jnp.int32, (rows, LANES), 1)
    irow_r = lax.broadcasted_iota(jnp.int32, (rows, LANES), 0) & (tq - 1)
    half = [lane_r < HEAD_DIM, lane_r >= HEAD_DIM]

    gates_ref[...] = _dot(_split2(_sigmoid(gl_ref[0])), gx_ref[...])

    def gate(c, h):
        return gates_ref[:, (c * hpg + h) * LANES:(c * hpg + h + 1) * LANES]

    q4 = jnp.concatenate([q_ref[0, :, h * LANES:(h + 1) * LANES] for h in range(hpg)], axis=0)
    for g in range(NSA_KV_GROUPS):
        qz_ref[g] = jnp.where(half[g], q4, jnp.zeros_like(q4))

    def heads_to_blocks(x0, x1):
        return [jnp.where(lane < HEAD_DIM, x0[h * tq:(h + 1) * tq], x1[h * tq:(h + 1) * tq])
                for h in range(hpg)]

    kc = kc_ref[0, 0]
    vc = vc_ref[0, 0]
    o_cmp = []
    for g in range(NSA_KV_GROUPS):
        sc = _dot_nt(qz_ref[g], kc) + bc_ref[g * hpg:(g + 1) * hpg].reshape(rows, n_cmp)
        mx = jnp.max(sc, axis=-1, keepdims=True)
        e = jnp.exp(sc - mx)
        den = jnp.sum(e, axis=-1, keepdims=True)
        pc = e * jnp.where(mx > 0.5 * NEG, 1.0 / den, 0.0)
        psum_ref[:, g * n_cmp:(g + 1) * n_cmp] = jnp.sum(pc.reshape(hpg, tq, n_cmp), axis=0)
        o_cmp.append(_dot(pc.astype(_BF16), vc))
    for h, blk_o in enumerate(heads_to_blocks(*o_cmp)):
        osum_ref[h] = gate(0, h) * blk_o

    rank_from = SLC_TOP * SLC_BLOCK // tq

    @pl.when(qi < rank_from)
    def _():
        for g in range(NSA_KV_GROUPS):
            qsel_ref[g] = qz_ref[g]

    @pl.when(qi >= rank_from)
    def _():
        imp = _dot(_split3(psum_ref[...]), ov_ref[...])
        blk = lane & (SLC_BLOCK - 1)
        cur = qrow >> int(math.log2(SLC_BLOCK))
        forced = (blk == 0) | (blk == cur) | (blk == cur - 1)
        imp = jnp.where(forced, FORCE_SCORE, imp)
        imp = jnp.where(blk > cur, -FORCE_SCORE, imp)
        imp_t = imp.T
        n_sel = LANES // NSA_KV_GROUPS
        n_grp = n_sel // SUBLANES
        sub = lax.broadcasted_iota(jnp.int32, (SUBLANES, tq), 0)
        neg_t = []
        for g in range(NSA_KV_GROUPS):
            a = imp_t[g * n_sel:(g + 1) * n_sel]
            a_grp = [a[r * SUBLANES:(r + 1) * SUBLANES] for r in range(n_grp)]
            cnt = [jnp.zeros((SUBLANES, tq), _F32) for _ in range(n_grp)]
            for j in range(n_sel):
                rj = jnp.broadcast_to(a[j:j + 1], (SUBLANES, tq))
                jr, jo = divmod(j, SUBLANES)
                for r in range(n_grp):
                    if r > jr:
                        one = jnp.where(rj >= a_grp[r], 1.0, 0.0)
                    elif r < jr:
                        one = jnp.where(rj > a_grp[r], 1.0, 0.0)
                    else:
                        tie = jnp.where(sub > jo, jnp.where(rj == a_grp[r], 1.0, 0.0), 0.0)
                        one = jnp.where(rj > a_grp[r], 1.0, tie)
                    cnt[r] = cnt[r] + one
            neg_t.append(jnp.where(jnp.concatenate(cnt, axis=0) < float(SLC_TOP), 0.0, NEG))
        selneg = jnp.concatenate(neg_t[::-1], axis=0).T.astype(_BF16)
        selneg4 = jnp.concatenate([selneg] * hpg, axis=0)
        for g in range(NSA_KV_GROUPS):
            qsel_ref[g] = jnp.where(half[g], q4, selneg4)

    def reset():
        m_ref[...] = jnp.full_like(m_ref, NEG)
        acc_ref[...] = jnp.zeros_like(acc_ref)

    def chunk(q_src, k_ref, v_ref, start, tk, onehot, extras, limit=None):
        s = scores(q_src, k_ref, start, tk, onehot)
        consume(lambda g, c: s[g][:, c * LANES:(c + 1) * LANES], v_ref, start, tk, extras, limit)

    def scores(q_src, k_ref, start, tk, onehot):
        k2 = k_ref[0, pl.ds(start, tk), :]
        lane_k = lax.broadcasted_iota(jnp.int32, (tk, LANES), 1)
        key_blk = (start + lax.broadcasted_iota(jnp.int32, (tk, LANES), 0)) >> int(math.log2(SLC_BLOCK))
        oh = jnp.where((lane_k & (SLC_BLOCK - 1)) == key_blk, 1.0, 0.0).astype(_BF16)
        out = []
        for g in range(NSA_KV_GROUPS):
            keep = (lane_k < HEAD_DIM) if g == 0 else (lane_k >= HEAD_DIM)
            out.append(_dot_nt(q_src[g], jnp.where(keep, k2, oh) if onehot else k2))
        return out

    def consume(block_of, v_ref, start, tk, extras, limit=None):
        v2 = v_ref[0, pl.ds(start, tk), :]
        lane_k = lax.broadcasted_iota(jnp.int32, (tk, LANES), 1)
        key_pos = start + lax.broadcasted_iota(jnp.int32, (tk, LANES), 0)
        for g in range(NSA_KV_GROUPS):
            keep = (lane_k < HEAD_DIM) if g == 0 else (lane_k >= HEAD_DIM)
            vv = jnp.where(keep, v2, jnp.ones_like(v2))
            if limit is not None:
                vv = jnp.where(key_pos < limit, vv, jnp.zeros_like(vv))
            blocks = [block_of(g, c) for c in range(tk // LANES)]
            for c, extra in extras.items():
                blocks[c] = blocks[c] + extra(g)
            mx = blocks[0]
            for blk_s in blocks[1:]:
                mx = jnp.maximum(mx, blk_s)
            m_old = m_ref[g]
            m_new = jnp.maximum(m_old, jnp.max(mx, axis=-1, keepdims=True))
            p = jnp.concatenate([jnp.exp(blk_s - m_new).astype(_BF16) for blk_s in blocks], axis=1)
            acc_ref[g] = jnp.exp(m_old - m_new) * acc_ref[g] + _dot(p, vv)
            m_ref[g] = m_new

    def finish(c):
        outs = []
        for g in range(NSA_KV_GROUPS):
            acc = acc_ref[g]
            den = pltpu.roll(acc, HEAD_DIM, axis=1)
            outs.append(acc * jnp.where(den > 0.0, 1.0 / den, 0.0))
        for h, blk_o in enumerate(heads_to_blocks(*outs)):
            osum_ref[h] += gate(c, h) * blk_o

    def near_bias(d):
        return lambda g: bd_ref[g * hpg:(g + 1) * hpg, d].reshape(rows, TK)

    def short_path(q_src, k_ref, v_ref, onehot):
        def far(kt, carry):
            chunk(q_src, k_ref, v_ref, pl.multiple_of(kt * TK, TK), TK, onehot, {})
            return carry

        lax.fori_loop(0, jnp.maximum(qi - 1, 0), far, 0)

        @pl.when(qi >= 1)
        def _():
            chunk(q_src, k_ref, v_ref, pl.multiple_of((qi - 1) * TK, TK), TK, onehot,
                  {0: near_bias(1)})

        chunk(q_src, k_ref, v_ref, pl.multiple_of(qi * TK, TK), TK, onehot, {0: near_bias(0)})

    reset()
    big = 4 * TK
    sel_from = big // TK - 1

    @pl.when(qi < sel_from)
    def _():
        short_path(qsel_ref, ks_ref, vs_ref, True)

    @pl.when(qi >= sel_from)
    def _():
        last = pl.multiple_of((qi - sel_from) * TK, TK)
        n_far = (qi - sel_from + big // TK - 1) >> int(math.log2(big // TK))
        near = {big // TK - 2: near_bias(1), big // TK - 1: near_bias(0)}

        def start_of(k):
            return pl.multiple_of(jnp.where(k < n_far, k * big, last), TK)

        def put(k, slot):
            for g, s in enumerate(scores(qsel_ref, ks_ref, start_of(k), big, True)):
                s_ref[slot, g] = s

        def take(k, slot, extras, limit):
            consume(lambda g, c: s_ref[slot, g, :, c * LANES:(c + 1) * LANES], vs_ref,
                    start_of(k), big, extras, limit)

        put(0, 0)
        n_pair = n_far >> 1

        def far_pair(i, carry):
            k = 2 * i
            put(k + 1, 1)
            take(k, 0, {}, last)
            put(k + 2, 0)
            take(k + 1, 1, {}, last)
            return carry

        lax.fori_loop(0, n_pair, far_pair, 0)

        @pl.when(n_far == 2 * n_pair)
        def _():
            take(n_far, 0, near, None)

        @pl.when(n_far != 2 * n_pair)
        def _():
            put(n_far, 1)
            take(n_far - 1, 0, {}, last)
            take(n_far, 1, near, None)

    finish(1)

    reset()
    n_win = WINDOW // TK
    win_mask = jnp.where(lane_r > irow_r, 0.0, NEG)

    @pl.when(qi < n_win)
    def _():
        short_path(qz_ref, kw_ref, vw_ref, False)

    @pl.when(qi >= n_win)
    def _():
        chunk(qz_ref, kw_ref, vw_ref, pl.multiple_of((qi - n_win) * TK, TK), WINDOW + TK, False,
              {0: lambda g: win_mask, n_win - 1: near_bias(1), n_win: near_bias(0)})

    finish(2)

    for h in range(hpg):
        zp = z_ref[0, :, h * LANES:(h + 1) * LANES]
        o_ref[0, :, h * LANES:(h + 1) * LANES] = (osum_ref[h] * _silu(zp)).astype(o_ref.dtype)


def _nsa_attention(pb3, pf3, kvc, bias_c, bias_d, ov, gx):
    b, t, _ = pb3.shape
    n_cmp = kvc.shape[2]
    n_blk = N_HEADS_NSA // 2
    full = lambda shape: pl.BlockSpec(shape, lambda i, q: (0,) * len(shape))
    kv_spec = lambda col: pl.BlockSpec((1, t, LANES), lambda i, q: (i, 0, col))
    return pl.pallas_call(
        _nsa_kernel,
        grid=(b, t // TQ_NSA),
        in_specs=[pl.BlockSpec((1, TQ_NSA, n_blk * LANES), lambda i, q: (i, q, PB_QB // n_blk)),
                  pl.BlockSpec((1, 1, n_cmp, LANES), lambda i, q: (i, 0, 0, 0)),
                  pl.BlockSpec((1, 1, n_cmp, LANES), lambda i, q: (i, 1, 0, 0)),
                  kv_spec(PB_KS), kv_spec(PB_VS), kv_spec(PB_KW), kv_spec(PB_VW),
                  pl.BlockSpec((1, TQ_NSA, LANES), lambda i, q: (i, q, PF_GB)),
                  pl.BlockSpec((1, TQ_NSA, n_blk * LANES), lambda i, q: (i, q, PF_ZB // n_blk)),
                  pl.BlockSpec((N_HEADS_NSA, TQ_NSA, n_cmp), lambda i, q: (0, q, 0)),
                  full(bias_d.shape), full(ov.shape), full(gx.shape)],
        out_specs=pl.BlockSpec((1, TQ_NSA, n_blk * LANES), lambda i, q: (i, q, 0)),
        out_shape=jax.ShapeDtypeStruct((b, t, n_blk * LANES), _BF16),
        scratch_shapes=[pltpu.VMEM((NSA_KV_GROUPS, NSA_HPG * TQ_NSA, LANES), _F32),
                        pltpu.VMEM((NSA_KV_GROUPS, NSA_HPG * TQ_NSA, LANES), _F32),
                        pltpu.VMEM((n_blk, TQ_NSA, LANES), _F32),
                        pltpu.VMEM((NSA_KV_GROUPS, NSA_HPG * TQ_NSA, LANES), _BF16),
                        pltpu.VMEM((NSA_KV_GROUPS, NSA_HPG * TQ_NSA, LANES), _BF16),
                        pltpu.VMEM((TQ_NSA, 3 * n_blk * LANES), _F32),
                        pltpu.VMEM((TQ_NSA, NSA_KV_GROUPS * n_cmp), _F32),
                        pltpu.VMEM((2, NSA_KV_GROUPS, NSA_HPG * TQ_NSA, 4 * TK), _F32)],
        compiler_params=_cparams(("arbitrary", "arbitrary")),
        name="native_sparse_attention",
    )(pb3, kvc, kvc, pb3, pb3, pb3, pb3, pf3, pf3, bias_c, bias_d, ov, gx)


def _out_kernel(x_ref, oa_ref, ob_ref, oc_ref, w_ref, g_ref, o_ref, *, final_norm):
    na, nb = oa_ref.shape[1], ob_ref.shape[1]
    mix = (_dot(oa_ref[...], w_ref[0:na])
           + _dot(ob_ref[...], w_ref[na:na + nb])
           + _dot(oc_ref[...], w_ref[na + nb:]))
    x = x_ref[...] + mix
    if final_norm:
        x = x * lax.rsqrt(jnp.mean(x * x, axis=-1, keepdims=True) + RMS_EPS) * g_ref[...]
    o_ref[...] = x


def _out_proj(x2, oa, ob, oc, w, g, final_norm):
    n = x2.shape[0]
    row = lambda width: pl.BlockSpec((TM_PROJ, width), lambda i: (i, 0))
    return pl.pallas_call(
        functools.partial(_out_kernel, final_norm=final_norm),
        grid=(n // TM_PROJ,),
        in_specs=[row(D_MODEL), row(oa.shape[1]), row(ob.shape[1]), row(oc.shape[1]),
                  pl.BlockSpec((D_MODEL, D_MODEL), lambda i: (0, 0)),
                  pl.BlockSpec((1, D_MODEL), lambda i: (0, 0))],
        out_specs=row(D_MODEL),
        out_shape=jax.ShapeDtypeStruct((n, D_MODEL), _F32),
        compiler_params=_cparams(("arbitrary",)),
        name="out_proj_residual",
    )(x2, oa, ob, oc, w, g)


def _head_perm_cols(width_per_head, order):
    return np.concatenate([np.arange(h * width_per_head, (h + 1) * width_per_head) for h in order])


def _layout_w_in(w):
    widths = [256, 256, 256, 256, 512, 128, 128, 128, 128, 128, 128, 24, 512, 256, 256, 256, 4, 256]
    offs = np.concatenate([[0], np.cumsum(widths)])
    (qa, ka, va, za, qb, kc, vc, ks, vs, kw, vw, gb, zb, qc, kf, vf, fc, zc) = [
        w[:, offs[i]:offs[i + 1]] for i in range(len(widths))]
    scale = HEAD_DIM ** -0.5
    perm = _head_perm_cols(HEAD_DIM, NSA_HEAD_ORDER)
    pad = lambda a: jnp.pad(a, ((0, 0), (0, LANES - a.shape[1])))
    wb = jnp.concatenate([qb[:, perm] * scale, qa * (-scale), ka, va, ks, vs, kw, vw,
                          qc * scale, kf, vf], axis=1)
    wf = jnp.concatenate([zb[:, perm], za, zc, pad(gb), pad(fc)], axis=1)
    wc = jnp.stack([a[:, g * HEAD_DIM:(g + 1) * HEAD_DIM]
                    for a in (kc, vc) for g in range(NSA_KV_GROUPS)], axis=0)
    return wb.astype(_BF16), wf.astype(_BF16), wc.astype(_BF16)


def _static_tables(t):
    tq = TQ_NSA
    n_cmp_pad = t // CMP_STRIDE
    j = np.arange(n_cmp_pad)
    dist_c = np.arange(t)[:, None] - (j[None, :] * CMP_STRIDE + CMP_BLOCK - 1)
    dist_c[:, n_cmp_pad - 1] = -1
    i_, j_ = np.arange(tq)[:, None], np.arange(TK)[None, :]
    dist_d = np.concatenate([d * TK + i_ - j_ for d in range(3)], axis=0)
    n_slc = LANES // NSA_KV_GROUPS
    cmp_start = j * CMP_STRIDE
    cmp_end = cmp_start + CMP_BLOCK - 1
    slc_start = np.arange(n_slc) * SLC_BLOCK
    ov1 = np.clip(np.minimum(cmp_end[:, None], slc_start[None, :] + SLC_BLOCK - 1)
                  - np.maximum(cmp_start[:, None], slc_start[None, :]) + 1, 0, None) / CMP_BLOCK
    ov1[n_cmp_pad - 1:] = 0.0
    ov1 = ov1[:, :min(n_slc, t // SLC_BLOCK)]
    ov = np.zeros((NSA_KV_GROUPS * n_cmp_pad, LANES), np.float32)
    for g in range(NSA_KV_GROUPS):
        ov[g * n_cmp_pad:(g + 1) * n_cmp_pad, g * n_slc:g * n_slc + ov1.shape[1]] = ov1
    ov3 = np.concatenate([ov, ov, ov], axis=0)
    n_blk = N_HEADS_NSA // 2
    gx = np.zeros((LANES, 3 * n_blk * LANES), np.float32)
    for c in range(3):
        for p in range(n_blk):
            for s in range(2):
                head = NSA_HEAD_ORDER[2 * p + s]
                col0 = (c * n_blk + p) * LANES + s * HEAD_DIM
                gx[3 * head + c, col0:col0 + HEAD_DIM] = 1.0
    gx2 = np.concatenate([gx, gx], axis=0)
    return (jnp.asarray(dist_c, jnp.int32), jnp.asarray(dist_d, jnp.int32),
            jnp.asarray(ov3, _BF16), jnp.asarray(gx2, _BF16))


def kernel(x, norm_g, w_in, w_out, forget_b, cmp_w1, cmp_b1, cmp_w2, cmp_pe, rel_bias, final_g):
    b, t, d = x.shape
    depth = norm_g.shape[0]
    assert d == D_MODEL and t % TM_PROJ == 0 and t % (CMP_STRIDE * LANES) == 0
    assert t // SLC_BLOCK <= LANES // NSA_KV_GROUPS and TQ_NSA == TK
    n_cmp_pad = t // CMP_STRIDE

    dist_c, dist_d, ov3, gx2 = _static_tables(t)
    bias_c = _bias_table(rel_bias, dist_c, 32)
    bias_d = _bias_table(rel_bias, dist_d, 32).reshape(N_HEADS_NSA, 3, TQ_NSA, TK)
    bias_d = bias_d[:, :2] - bias_d[:, 2:3]

    perm_rows = _head_perm_cols(HEAD_DIM, NSA_HEAD_ORDER)
    x2 = x.reshape(b * t, d)
    out = None
    for l in range(depth):
        wb, wf, wc = _layout_w_in(w_in[l])
        pb, pf, pc = _proj(x2, norm_g[l].reshape(1, d), wb, wf, wc, t)
        pb3 = pb.reshape(b, t, PB_BLOCKS * LANES)
        pf3 = pf.reshape(b, t, PF_BLOCKS * LANES)

        fb_row = jnp.pad(forget_b[l], (0, LANES - N_HEADS_FOX)).reshape(1, LANES)
        augq, augk = _fgate(pf3, fb_row)

        halves = pc.reshape(b, 2, NSA_KV_GROUPS, n_cmp_pad, CMP_STRIDE * HEAD_DIM)
        w2 = cmp_w2[l]
        zeros = jnp.zeros_like(w2)
        w2p = jnp.stack([jnp.concatenate([w2, zeros], axis=-1),
                         jnp.concatenate([zeros, w2], axis=-1)], axis=1).astype(_BF16)
        kvc = _compress(halves, cmp_w1[l].astype(_BF16),
                        cmp_pe[l].reshape(2, 1, CMP_BLOCK * HEAD_DIM).astype(_BF16),
                        cmp_b1[l].reshape(2, 1, CMP_HIDDEN), w2p)

        o_a = _sb_attention(pb3, pf3)
        o_b = _nsa_attention(pb3, pf3, kvc, bias_c, bias_d, ov3, gx2)
        o_c = _fox_attention(pb3, pf3, augq, augk)

        wo = w_out[l]
        wo = jnp.concatenate([wo[:N_HEADS_SB * HEAD_DIM],
                              wo[N_HEADS_SB * HEAD_DIM:][:N_HEADS_NSA * HEAD_DIM][perm_rows],
                              wo[(N_HEADS_SB + N_HEADS_NSA) * HEAD_DIM:]], axis=0).astype(_BF16)
        last = l == depth - 1
        x2 = _out_proj(x2, o_a.reshape(b * t, -1), o_b.reshape(b * t, -1), o_c.reshape(b * t, -1),
                       wo, final_g.reshape(1, d), last)
    return x2.reshape(b, t, d)
```

```python
import functools
import math

import jax
import jax.numpy as jnp
import numpy as np
from jax import lax
from jax.experimental import pallas as pl
from jax.experimental.pallas import tpu as pltpu

D_MODEL = 1024
HEAD_DIM = 64
N_HEADS_SB = 4
N_HEADS_FOX = 4
N_HEADS_NSA = 8
NSA_KV_GROUPS = 2
NSA_HPG = N_HEADS_NSA // NSA_KV_GROUPS
CMP_BLOCK = 32
CMP_STRIDE = 16
CMP_HIDDEN = 256
SLC_BLOCK = 64
SLC_TOP = 16
WINDOW = 512
REL_BUCKETS = 32
REL_MAX_DIST = 128
FORCE_SCORE = 1e4
RMS_EPS = 1e-6
NEG = -1e30
LOG2E = math.log2(math.e)

LANES = 128
SUBLANES = 8
VMEM_LIMIT = 56 * 1024 * 1024

TM_PROJ = 512
TQ_SB = 512
TQ_FOX = 512
TQ_NSA = 128
TK = 128

PB_QB, PB_QA, PB_KA, PB_VA = 0, 4, 6, 8
PB_KS, PB_KW = 10, 11
PB_QC, PB_KF = 12, 14
PB_BLOCKS = 16
N_CMP_SLABS = 2 * NSA_KV_GROUPS
VT_VS, VT_VW, VT_VF = 0, 1, 2
N_VT = 4
PF_ZB, PF_ZA, PF_ZC, PF_GB, PF_FC = 0, 4, 6, 8, 9
PF_BLOCKS = 10

NSA_HEAD_ORDER = [0, 4, 1, 5, 2, 6, 3, 7]

_F32 = jnp.float32
_BF16 = jnp.bfloat16


def _cparams(sem):
    return pltpu.CompilerParams(dimension_semantics=sem, vmem_limit_bytes=VMEM_LIMIT)


def _dot(a, b):
    return jnp.dot(a, b, preferred_element_type=_F32)


def _dot_nt(a, b):
    return lax.dot_general(a, b, (((1,), (1,)), ((), ())), preferred_element_type=_F32)


def _split2(x):
    hi = x.astype(_BF16)
    lo = (x - hi.astype(_F32)).astype(_BF16)
    return jnp.concatenate([hi, lo], axis=1)


def _split3(x):
    h1 = x.astype(_BF16)
    r1 = x - h1.astype(_F32)
    h2 = r1.astype(_BF16)
    h3 = (r1 - h2.astype(_F32)).astype(_BF16)
    return jnp.concatenate([h1, h2, h3], axis=1)


def _sigmoid(x):
    return 1.0 / (1.0 + jnp.exp(-x))


def _silu(x):
    return x * _sigmoid(x)


def _rel_bucket_np(n):
    n = np.maximum(n, 0)
    max_exact = REL_BUCKETS // 2
    nf = np.maximum(n, 1).astype(np.float64)
    large = max_exact + (np.log(nf / max_exact) / math.log(REL_MAX_DIST / max_exact)
                         * (REL_BUCKETS - max_exact)).astype(np.int64)
    large = np.minimum(large, REL_BUCKETS - 1)
    return np.where(n < max_exact, n, large)


def _bucket_thresholds():
    n = np.arange(0, 4 * REL_MAX_DIST)
    bk = _rel_bucket_np(n)
    assert np.all(np.diff(bk) >= 0) and bk[-1] == REL_BUCKETS - 1
    return [int(np.argmax(bk >= b)) for b in range(REL_BUCKETS)]


_BUCKET_THR = _bucket_thresholds()


def _bias_kernel(tab_ref, dist_ref, o_ref):
    n = dist_ref[...]
    acc = [jnp.full(n.shape, tab_ref[0, h], _F32) for h in range(N_HEADS_NSA)]
    for b in range(1, REL_BUCKETS):
        ge = n >= _BUCKET_THR[b]
        for h in range(N_HEADS_NSA):
            acc[h] = jnp.where(ge, tab_ref[b, h], acc[h])
    valid = n >= 0
    for h in range(N_HEADS_NSA):
        o_ref[h] = jnp.where(valid, acc[h], NEG)


def _bias_table(rel_bias, dist, rows):
    n_rows, n_cols = dist.shape
    return pl.pallas_call(
        _bias_kernel,
        grid=(n_rows // rows,),
        in_specs=[pl.BlockSpec(memory_space=pltpu.SMEM),
                  pl.BlockSpec((rows, n_cols), lambda i: (i, 0))],
        out_specs=pl.BlockSpec((N_HEADS_NSA, rows, n_cols), lambda i: (0, i, 0)),
        out_shape=jax.ShapeDtypeStruct((N_HEADS_NSA, n_rows, n_cols), _F32),
        compiler_params=_cparams(("arbitrary",)),
        name="rel_bias_table",
    )(rel_bias, dist)


def _proj_kernel(x_ref, g_ref, wb_ref, wf_ref, wc_ref, wvt_ref, pb_ref, pf_ref, pc_ref, vt_ref):
    x = x_ref[...]
    y = x * lax.rsqrt(jnp.mean(x * x, axis=-1, keepdims=True) + RMS_EPS)
    h = (y * g_ref[...]).astype(_BF16)
    for j in range(N_VT):
        v_t = _dot_nt(wvt_ref[j], h).astype(_BF16)
        for c in range(TM_PROJ // TK):
            vt_ref[0, j, c] = v_t[:, c * TK:(c + 1) * TK]
    for s in range(N_CMP_SLABS):
        pc_ref[0, s] = _dot(h, wc_ref[s]).astype(_BF16)
    chunk = 4 * LANES
    for c in range(0, PB_BLOCKS * LANES, chunk):
        w = min(chunk, PB_BLOCKS * LANES - c)
        pb_ref[:, c:c + w] = _dot(h, wb_ref[:, c:c + w]).astype(_BF16)
    for c in range(0, PF_BLOCKS * LANES, chunk):
        w = min(chunk, PF_BLOCKS * LANES - c)
        pf_ref[:, c:c + w] = _dot(h, wf_ref[:, c:c + w])


def _proj(x2, g, wb, wf, wc, wvt, t):
    n = x2.shape[0]
    per_seq = t // TM_PROJ
    return pl.pallas_call(
        _proj_kernel,
        grid=(n // TM_PROJ,),
        in_specs=[pl.BlockSpec((TM_PROJ, D_MODEL), lambda i: (i, 0)),
                  pl.BlockSpec((1, D_MODEL), lambda i: (0, 0)),
                  pl.BlockSpec((D_MODEL, PB_BLOCKS * LANES), lambda i: (0, 0)),
                  pl.BlockSpec((D_MODEL, PF_BLOCKS * LANES), lambda i: (0, 0)),
                  pl.BlockSpec((N_CMP_SLABS, D_MODEL, HEAD_DIM), lambda i: (0, 0, 0)),
                  pl.BlockSpec((N_VT, LANES, D_MODEL), lambda i: (0, 0, 0))],
        out_specs=[pl.BlockSpec((TM_PROJ, PB_BLOCKS * LANES), lambda i: (i, 0)),
                   pl.BlockSpec((TM_PROJ, PF_BLOCKS * LANES), lambda i: (i, 0)),
                   pl.BlockSpec((1, N_CMP_SLABS, TM_PROJ, HEAD_DIM),
                                lambda i: (i // per_seq, 0, i % per_seq, 0)),
                   pl.BlockSpec((1, N_VT, TM_PROJ // TK, LANES, TK),
                                lambda i: (i // per_seq, 0, i % per_seq, 0, 0))],
        out_shape=[jax.ShapeDtypeStruct((n, PB_BLOCKS * LANES), _BF16),
                   jax.ShapeDtypeStruct((n, PF_BLOCKS * LANES), _F32),
                   jax.ShapeDtypeStruct((n // t, N_CMP_SLABS, t, HEAD_DIM), _BF16),
                   jax.ShapeDtypeStruct((n // t, N_VT, t // TK, LANES, TK), _BF16)],
        compiler_params=_cparams(("arbitrary",)),
        name="rmsnorm_in_proj",
    )(x2, g, wb, wf, wc, wvt)


N_SPLIT = 3


def _fgate_tables():
    n_pairs = N_HEADS_FOX // 2
    pq = np.zeros((N_SPLIT * LANES, n_pairs * LANES), np.float32)
    pk = np.zeros_like(pq)
    ones_q = np.zeros((1, n_pairs * LANES), np.float32)
    ones_k = np.zeros_like(ones_q)
    for head in range(N_HEADS_FOX):
        pair, slot = divmod(head, 2)
        base = pair * LANES + (HEAD_DIM if slot == 0 else 0)
        for j in range(N_SPLIT):
            pq[j * LANES + head, base + j] = 1.0
            pk[j * LANES + head, base + N_SPLIT + j] = 1.0
        ones_q[0, base + N_SPLIT:base + 2 * N_SPLIT] = 1.0
        ones_k[0, base:base + N_SPLIT] = 1.0
    return (jnp.asarray(pq, _BF16), jnp.asarray(pk, _BF16),
            jnp.asarray(ones_q), jnp.asarray(ones_k))


def _fgate_kernel(fc_ref, fb_ref, pq_ref, pk_ref, oq_ref, ok_ref, augq_ref, augk_ref):
    t = fc_ref.shape[1]
    z = fc_ref[0] + fb_ref[...]
    logf = jnp.minimum(z, 0.0) - jnp.log1p(jnp.exp(-jnp.abs(z)))
    row = lax.broadcasted_iota(jnp.int32, (t, LANES), 0)
    c = logf
    shift = 1
    while shift < t:
        c = c + jnp.where(row >= shift, pltpu.roll(c, shift, axis=0), 0.0)
        shift *= 2
    c3 = _split3(c * LOG2E)
    aq = _dot(c3, pq_ref[...]) + oq_ref[...]
    ak = ok_ref[...] - _dot(c3, pk_ref[...])
    for p in range(N_HEADS_FOX // 2):
        augq_ref[0, p] = aq[:, p * LANES:(p + 1) * LANES].astype(_BF16)
        augk_ref[0, p] = ak[:, p * LANES:(p + 1) * LANES].astype(_BF16)


def _fgate(pf3, fb_row):
    b, t, _ = pf3.shape
    n_pairs = N_HEADS_FOX // 2
    tables = _fgate_tables()
    full = lambda a: pl.BlockSpec(a.shape, lambda i: (0,) * a.ndim)
    aug = lambda: pl.BlockSpec((1, n_pairs, t, LANES), lambda i: (i, 0, 0, 0))
    return pl.pallas_call(
        _fgate_kernel,
        grid=(b,),
        in_specs=[pl.BlockSpec((1, t, LANES), lambda i: (i, 0, PF_FC)),
                  pl.BlockSpec((1, LANES), lambda i: (0, 0))] + [full(a) for a in tables],
        out_specs=[aug(), aug()],
        out_shape=[jax.ShapeDtypeStruct((b, n_pairs, t, LANES), _BF16),
                   jax.ShapeDtypeStruct((b, n_pairs, t, LANES), _BF16)],
        compiler_params=_cparams(("arbitrary",)),
        name="forget_gate_cumsum",
    )(pf3, fb_row, *tables)


def _compress_kernel(x_ref, w1_ref, pe_ref, b1_ref, w2_ref, o_ref):
    nc = x_ref.shape[3]
    half = CMP_STRIDE * HEAD_DIM
    w1 = w1_ref[0]
    c1 = _dot(jnp.broadcast_to(pe_ref[0], (SUBLANES, 2 * half)), w1)[0:1] + b1_ref[0]
    out = jnp.zeros((nc, LANES), _F32)
    for g in range(NSA_KV_GROUPS):
        xg = x_ref[0, 0, g]
        a = _dot(xg, w1[:half])
        bb = _dot(xg, w1[half:])
        h = a + pltpu.roll(bb, nc - 1, axis=0) + c1
        out = out + _dot(_silu(h).astype(_BF16), w2_ref[0, g])
    o_ref[0, 0] = out.astype(_BF16)


def _compress(halves, w1, pe, b1, w2p):
    b, _, g, nc, width = halves.shape
    return pl.pallas_call(
        _compress_kernel,
        grid=(b, 2),
        in_specs=[pl.BlockSpec((1, 1, g, nc, width), lambda i, k: (i, k, 0, 0, 0)),
                  pl.BlockSpec((1, 2 * width, CMP_HIDDEN), lambda i, k: (k, 0, 0)),
                  pl.BlockSpec((1, 1, 2 * width), lambda i, k: (k, 0, 0)),
                  pl.BlockSpec((1, 1, CMP_HIDDEN), lambda i, k: (k, 0, 0)),
                  pl.BlockSpec((1, g, CMP_HIDDEN, LANES), lambda i, k: (k, 0, 0, 0))],
        out_specs=pl.BlockSpec((1, 1, nc, LANES), lambda i, k: (i, k, 0, 0)),
        out_shape=jax.ShapeDtypeStruct((b, 2, nc, LANES), _BF16),
        compiler_params=_cparams(("arbitrary", "arbitrary")),
        name="nsa_compress",
    )(halves, w1, pe, b1, w2p)


def _sb_kernel(q_ref, k_ref, v_ref, z_ref, o_ref, qh_ref, c_ref, acc_ref):
    qi = pl.program_id(2)
    tq = q_ref.shape[1]
    lane = lax.broadcasted_iota(jnp.int32, (tq, LANES), 1)
    q2 = q_ref[0]
    qh = [jnp.where(lane < HEAD_DIM, q2, jnp.zeros_like(q2)),
          jnp.where(lane >= HEAD_DIM, q2, jnp.zeros_like(q2))]
    r_i = lax.broadcasted_iota(jnp.int32, (2 * TK, 2 * TK), 0) & (TK - 1)
    c_i = lax.broadcasted_iota(jnp.int32, (2 * TK, 2 * TK), 1)
    uu = jnp.where((c_i >= TK) | (r_i >= c_i), 1.0, 0.0).astype(_BF16)
    c_ref[...] = jnp.zeros_like(c_ref)
    acc_ref[...] = jnp.zeros_like(acc_ref)
    for h in range(2):
        qh_ref[h] = qh[h]
    row = lax.broadcasted_iota(jnp.int32, (tq, LANES), 0)
    n_blocks = tq // TK

    def chunk(start, diagonal):
        k2 = k_ref[0, pl.ds(start, tq), :]
        v2 = v_ref[0, pl.ds(start, tq), :]
        for h in range(2):
            s = _dot_nt(qh_ref[h], k2)
            carry = c_ref[h]
            w_blocks = [None] * n_blocks
            for c in reversed(range(n_blocks)):
                nz = s[:, c * TK:(c + 1) * TK]
                neg_abs = lax.bitcast_convert_type(
                    lax.bitcast_convert_type(nz, jnp.uint32) | jnp.uint32(0x80000000), _F32)
                l1m = jnp.minimum(nz, 0.0) - jnp.log(1.0 + jnp.exp(neg_abs))
                if diagonal:
                    mask = lane + c * TK < row
                    l1m = jnp.where(mask, l1m, 0.0)
                rc = _dot(_split2(l1m), uu)
                w = jnp.exp((rc[:, :TK] + carry) - nz)
                if diagonal:
                    w = jnp.where(mask, w, 0.0)
                w_blocks[c] = w.astype(_BF16)
                carry = carry + rc[:, TK:]
            acc_ref[h] += _dot(jnp.concatenate(w_blocks, axis=1), v2)
            c_ref[h] = carry

    chunk(pl.multiple_of(qi * tq, tq), True)

    def far_pair(it, carry):
        chunk(pl.multiple_of((qi - 1 - 2 * it) * tq, tq), False)
        chunk(pl.multiple_of((qi - 2 - 2 * it) * tq, tq), False)
        return carry

    lax.fori_loop(0, qi >> 1, far_pair, 0)

    @pl.when((qi & 1) == 1)
    def _():
        chunk(0, False)
    o = jnp.where(lane < HEAD_DIM, acc_ref[0], acc_ref[1])
    o_ref[0] = (o * _silu(z_ref[0])).astype(o_ref.dtype)


def _sb_attention(pb3, pf3):
    b, t, _ = pb3.shape
    n_pairs = N_HEADS_SB // 2
    return pl.pallas_call(
        _sb_kernel,
        grid=(b, n_pairs, t // TQ_SB),
        in_specs=[pl.BlockSpec((1, TQ_SB, LANES), lambda i, p, q: (i, q, PB_QA + p)),
                  pl.BlockSpec((1, t, LANES), lambda i, p, q: (i, 0, PB_KA + p)),
                  pl.BlockSpec((1, t, LANES), lambda i, p, q: (i, 0, PB_VA + p)),
                  pl.BlockSpec((1, TQ_SB, LANES), lambda i, p, q: (i, q, PF_ZA + p))],
        out_specs=pl.BlockSpec((1, TQ_SB, LANES), lambda i, p, q: (i, q, p)),
        out_shape=jax.ShapeDtypeStruct((b, t, n_pairs * LANES), _BF16),
        scratch_shapes=[pltpu.VMEM((2, TQ_SB, LANES), _BF16),
                        pltpu.VMEM((2, TQ_SB, LANES), _F32),
                        pltpu.VMEM((2, TQ_SB, LANES), _F32)],
        compiler_params=_cparams(("arbitrary", "arbitrary", "arbitrary")),
        name="stick_breaking_attention",
    )(pb3, pb3, pb3, pf3)


def _fox_kernel(q_ref, k_ref, vt_ref, augq_ref, augk_ref, z_ref, o_ref,
                qh_ref, m_ref, acc_ref, s_ref):
    qi = pl.program_id(2)
    tq = q_ref.shape[1]
    tk = tq
    lane = lax.broadcasted_iota(jnp.int32, (tq, LANES), 1)
    keep = [lane < HEAD_DIM, lane >= HEAD_DIM]
    q2 = q_ref[0]
    aq = augq_ref[0, 0]
    for h in range(2):
        qh_ref[h] = jnp.where(keep[h], q2, aq)
    m_ref[...] = jnp.full_like(m_ref, NEG)
    acc_ref[...] = jnp.zeros_like(acc_ref)
    key_i = lax.broadcasted_iota(jnp.int32, (TK, tq), 0)
    qry_i = lax.broadcasted_iota(jnp.int32, (TK, tq), 1)

    def scores(c, slot):
        start = c * tk if isinstance(c, int) else pl.multiple_of(c * tk, tk)
        k2 = k_ref[0, pl.ds(start, tk), :]
        ak = augk_ref[0, 0, pl.ds(start, tk), :]
        for h in range(2):
            s_ref[slot, h] = _dot_nt(jnp.where(keep[h], k2, ak), qh_ref[h])

    ones_rows = jnp.ones((SUBLANES, tk), _BF16)

    def consume(c, slot, diagonal):
        per = tk // TK
        vt = jnp.concatenate([vt_ref[0, 0, c * per + j] for j in range(per)], axis=1)
        for h in range(2):
            blocks = [s_ref[slot, h, j * TK:(j + 1) * TK, :] for j in range(tk // TK)]
            if diagonal:
                blocks = [jnp.where(key_i + j * TK <= qry_i, blk, NEG) for j, blk in enumerate(blocks)]
            mx = blocks[0]
            for blk in blocks[1:]:
                mx = jnp.maximum(mx, blk)
            m_old = m_ref[h]
            m_new = jnp.maximum(m_old, jnp.max(mx, axis=0, keepdims=True))
            m_row = m_new[0:1]
            pt = jnp.concatenate([jnp.exp2(blk - m_row).astype(_BF16) for blk in blocks], axis=0)
            vth = jnp.concatenate([vt[h * HEAD_DIM:(h + 1) * HEAD_DIM], ones_rows], axis=0)
            acc_ref[h] = jnp.exp2(m_old - m_new)[0:1] * acc_ref[h] + _dot(vth, pt)
            m_ref[h] = m_new

    scores(0, 0)
    n_pairs = qi >> 1

    def far_pair(i, carry):
        c = 2 * i
        scores(c + 1, 1)
        consume(c, 0, False)
        scores(c + 2, 0)
        consume(c + 1, 1, False)
        return carry

    lax.fori_loop(0, n_pairs, far_pair, 0)

    @pl.when(qi == 2 * n_pairs)
    def _():
        consume(qi, 0, True)

    @pl.when(qi != 2 * n_pairs)
    def _():
        scores(qi, 1)
        consume(qi - 1, 0, False)
        consume(qi, 1, True)
    o_t = jnp.concatenate([acc_ref[h, :HEAD_DIM] / acc_ref[h, HEAD_DIM:HEAD_DIM + 1]
                           for h in range(2)], axis=0)
    o_ref[0] = (o_t.T * _silu(z_ref[0])).astype(o_ref.dtype)


def _fox_attention(pb3, pf3, vt, augq, augk):
    b, t, _ = pb3.shape
    n_pairs = N_HEADS_FOX // 2
    n_chunks = t // TQ_FOX
    return pl.pallas_call(
        _fox_kernel,
        grid=(b, n_pairs, t // TQ_FOX),
        in_specs=[pl.BlockSpec((1, TQ_FOX, LANES), lambda i, p, q: (i, q, PB_QC + p)),
                  pl.BlockSpec((1, t, LANES), lambda i, p, q: (i, 0, PB_KF + p)),
                  pl.BlockSpec((1, 1, t // TK, LANES, TK), lambda i, p, q: (i, VT_VF + p, 0, 0, 0)),
                  pl.BlockSpec((1, 1, TQ_FOX, LANES), lambda i, p, q: (i, p, q, 0)),
                  pl.BlockSpec((1, 1, t, LANES), lambda i, p, q: (i, p, 0, 0)),
                  pl.BlockSpec((1, TQ_FOX, LANES), lambda i, p, q: (i, q, PF_ZC + p))],
        out_specs=pl.BlockSpec((1, TQ_FOX, LANES), lambda i, p, q: (i, q, p)),
        out_shape=jax.ShapeDtypeStruct((b, t, n_pairs * LANES), _BF16),
        scratch_shapes=[pltpu.VMEM((2, TQ_FOX, LANES), _BF16),
                        pltpu.VMEM((2, SUBLANES, TQ_FOX), _F32),
                        pltpu.VMEM((2, HEAD_DIM + SUBLANES, TQ_FOX), _F32),
                        pltpu.VMEM((2, 2, TQ_FOX, TQ_FOX), _F32)],
        compiler_params=_cparams(("arbitrary", "arbitrary", "arbitrary")),
        name="forgetting_attention",
    )(pb3, pb3, vt, augq, augk, pf3)


def _nsa_kernel(q_ref, kc_ref, vc_ref, ks_ref, vs_ref, kw_ref, vw_ref, gl_ref, z_ref,
                bc_ref, bd_ref, ov_ref, gx_ref, o_ref,
                m_ref, acc_ref, osum_ref, qz_ref, qsel_ref, gates_ref, psum_ref, s_ref):
    qi = pl.program_id(1)
    tq = q_ref.shape[1]
    hpg = NSA_HPG
    rows = hpg * tq
    n_cmp = kc_ref.shape[2]
    lane = lax.broadcasted_iota(jnp.int32, (tq, LANES), 1)
    qrow = qi * tq + lax.broadcasted_iota(jnp.int32, (tq, LANES), 0)
    lane_r = lax.broadcasted_iota(jnp.int32, (rows, LANES), 1)
    irow_r = lax.broadcasted_iota(jnp.int32, (rows, LANES), 0) & (tq - 1)
    half = [lane_r < HEAD_DIM, lane_r >= HEAD_DIM]

    gates_ref[...] = _dot(_split2(_sigmoid(gl_ref[0])), gx_ref[...])

    def gate(c, h):
        return gates_ref[:, (c * hpg + h) * LANES:(c * hpg + h + 1) * LANES]

    q4 = jnp.concatenate([q_ref[0, :, h * LANES:(h + 1) * LANES] for h in range(hpg)], axis=0)
    for g in range(NSA_KV_GROUPS):
        qz_ref[g] = jnp.where(half[g], q4, jnp.zeros_like(q4))

    def heads_to_blocks(x0, x1):
        return [jnp.where(lane < HEAD_DIM, x0[h * tq:(h + 1) * tq], x1[h * tq:(h + 1) * tq])
                for h in range(hpg)]

    kc = kc_ref[0, 0]
    vc = vc_ref[0, 0]
    o_cmp = []
    for g in range(NSA_KV_GROUPS):
        sc = _dot_nt(qz_ref[g], kc) + bc_ref[g * hpg:(g + 1) * hpg].reshape(rows, n_cmp)
        mx = jnp.max(sc, axis=-1, keepdims=True)
        e = jnp.exp2(sc - mx)
        den = jnp.sum(e, axis=-1, keepdims=True)
        pc = e * jnp.where(mx > 0.5 * NEG, 1.0 / den, 0.0)
        psum_ref[:, g * n_cmp:(g + 1) * n_cmp] = jnp.sum(pc.reshape(hpg, tq, n_cmp), axis=0)
        o_cmp.append(_dot(pc.astype(_BF16), vc))
    for h, blk_o in enumerate(heads_to_blocks(*o_cmp)):
        osum_ref[h] = gate(0, h) * blk_o

    rank_from = SLC_TOP * SLC_BLOCK // tq

    @pl.when(qi < rank_from)
    def _():
        for g in range(NSA_KV_GROUPS):
            qsel_ref[g] = qz_ref[g]

    @pl.when(qi >= rank_from)
    def _():
        imp = _dot(_split3(psum_ref[...]), ov_ref[...])
        blk = lane & (SLC_BLOCK - 1)
        cur = qrow >> int(math.log2(SLC_BLOCK))
        forced = (blk == 0) | (blk == cur) | (blk == cur - 1)
        imp = jnp.where(forced, FORCE_SCORE, imp)
        imp = jnp.where(blk > cur, -FORCE_SCORE, imp)
        imp_t = imp.T
        n_sel = LANES // NSA_KV_GROUPS
        n_grp = n_sel // SUBLANES
        sub = lax.broadcasted_iota(jnp.int32, (SUBLANES, tq), 0)
        neg_t = []
        for g in range(NSA_KV_GROUPS):
            a = imp_t[g * n_sel:(g + 1) * n_sel]
            a_grp = [a[r * SUBLANES:(r + 1) * SUBLANES] for r in range(n_grp)]
            cnt = [jnp.zeros((SUBLANES, tq), _F32) for _ in range(n_grp)]
            for j in range(n_sel):
                rj = jnp.broadcast_to(a[j:j + 1], (SUBLANES, tq))
                jr, jo = divmod(j, SUBLANES)
                for r in range(n_grp):
                    if r > jr:
                        one = jnp.where(rj >= a_grp[r], 1.0, 0.0)
                    elif r < jr:
                        one = jnp.where(rj > a_grp[r], 1.0, 0.0)
                    else:
                        tie = jnp.where(sub > jo, jnp.where(rj == a_grp[r], 1.0, 0.0), 0.0)
                        one = jnp.where(rj > a_grp[r], 1.0, tie)
                    cnt[r] = cnt[r] + one
            neg_t.append(jnp.where(jnp.concatenate(cnt, axis=0) < float(SLC_TOP), 0.0, NEG))
        selneg = jnp.concatenate(neg_t[::-1], axis=0).T.astype(_BF16)
        selneg4 = jnp.concatenate([selneg] * hpg, axis=0)
        for g in range(NSA_KV_GROUPS):
            qsel_ref[g] = jnp.where(half[g], q4, selneg4)

    def reset():
        m_ref[...] = jnp.full_like(m_ref, 2.0 * NEG)
        acc_ref[...] = jnp.zeros_like(acc_ref)

    def chunk(q_src, k_ref, vt_ref, start, tk, onehot, extras, limit=None):
        s = scores(q_src, k_ref, start, tk, onehot)
        consume(lambda g, c: s[g][c * TK:(c + 1) * TK], vt_ref, start, tk, extras, limit)

    def scores(q_src, k_ref, start, tk, onehot):
        k2 = k_ref[0, pl.ds(start, tk), :]
        lane_k = lax.broadcasted_iota(jnp.int32, (tk, LANES), 1)
        key_blk = (start + lax.broadcasted_iota(jnp.int32, (tk, LANES), 0)) >> int(math.log2(SLC_BLOCK))
        oh = jnp.where((lane_k & (SLC_BLOCK - 1)) == key_blk, 1.0, 0.0).astype(_BF16)
        out = []
        for g in range(NSA_KV_GROUPS):
            keep = (lane_k < HEAD_DIM) if g == 0 else (lane_k >= HEAD_DIM)
            out.append(_dot_nt(jnp.where(keep, k2, oh) if onehot else k2, q_src[g]))
        return out

    def consume(block_of, vt_ref, start, tk, extras, limit=None):
        first = start // TK if isinstance(start, int) else start >> int(math.log2(TK))
        vt = jnp.concatenate([vt_ref[0, 0, first + j] for j in range(tk // TK)], axis=1)
        ones_rows = jnp.ones((SUBLANES, tk), _BF16)
        key_pos = start + lax.broadcasted_iota(jnp.int32, (HEAD_DIM + SUBLANES, tk), 1)
        for g in range(NSA_KV_GROUPS):
            vth = jnp.concatenate([vt[g * HEAD_DIM:(g + 1) * HEAD_DIM], ones_rows], axis=0)
            if limit is not None:
                vth = jnp.where(key_pos < limit, vth, jnp.zeros_like(vth))
            blocks = [block_of(g, c) for c in range(tk // TK)]
            for c, extra in extras.items():
                blocks[c] = blocks[c] + extra(g)
            mx = blocks[0]
            for blk_s in blocks[1:]:
                mx = jnp.maximum(mx, blk_s)
            m_old = m_ref[g]
            m_new = jnp.maximum(m_old, jnp.max(mx, axis=0, keepdims=True))
            m_row = m_new[0:1]
            pt = jnp.concatenate([jnp.exp2(blk_s - m_row).astype(_BF16) for blk_s in blocks], axis=0)
            acc_ref[g] = jnp.exp2(m_old - m_new)[0:1] * acc_ref[g] + _dot(vth, pt)
            m_ref[g] = m_new

    def finish(c):
        outs = []
        for g in range(NSA_KV_GROUPS):
            den = acc_ref[g, HEAD_DIM:HEAD_DIM + 1]
            outs.append(acc_ref[g, :HEAD_DIM] * jnp.where(den > 0.0, 1.0 / den, 0.0))
        for h in range(hpg):
            blk_t = jnp.concatenate([o[:, h * tq:(h + 1) * tq] for o in outs], axis=0)
            osum_ref[h] += gate(c, h) * blk_t.T

    def near_bias(d):
        return lambda g: bd_ref[g, d]

    def short_path(q_src, k_ref, v_ref, onehot):
        def far(kt, carry):
            chunk(q_src, k_ref, v_ref, pl.multiple_of(kt * TK, TK), TK, onehot, {})
            return carry

        lax.fori_loop(0, jnp.maximum(qi - 1, 0), far, 0)

        @pl.when(qi >= 1)
        def _():
            chunk(q_src, k_ref, v_ref, pl.multiple_of((qi - 1) * TK, TK), TK, onehot,
                  {0: near_bias(1)})

        chunk(q_src, k_ref, v_ref, pl.multiple_of(qi * TK, TK), TK, onehot, {0: near_bias(0)})

    reset()
    big = 4 * TK
    sel_from = big // TK - 1

    @pl.when(qi < sel_from)
    def _():
        short_path(qsel_ref, ks_ref, vs_ref, True)

    @pl.when(qi >= sel_from)
    def _():
        last = pl.multiple_of((qi - sel_from) * TK, TK)
        n_far = (qi - sel_from + big // TK - 1) >> int(math.log2(big // TK))
        near = {big // TK - 2: near_bias(1), big // TK - 1: near_bias(0)}

        def start_of(k):
            return pl.multiple_of(jnp.where(k < n_far, k * big, last), TK)

        def put(k, slot):
            for g, s in enumerate(scores(qsel_ref, ks_ref, start_of(k), big, True)):
                s_ref[slot, g] = s

        def take(k, slot, extras, limit):
            consume(lambda g, c: s_ref[slot, g, c * TK:(c + 1) * TK, :], vs_ref,
                    start_of(k), big, extras, limit)

        put(0, 0)
        n_pair = n_far >> 1

        def far_pair(i, carry):
            k = 2 * i
            put(k + 1, 1)
            take(k, 0, {}, last)
            put(k + 2, 0)
            take(k + 1, 1, {}, last)
            return carry

        lax.fori_loop(0, n_pair, far_pair, 0)

        @pl.when(n_far == 2 * n_pair)
        def _():
            take(n_far, 0, near, None)

        @pl.when(n_far != 2 * n_pair)
        def _():
            put(n_far, 1)
            take(n_far - 1, 0, {}, last)
            take(n_far, 1, near, None)

    finish(1)

    reset()
    n_win = WINDOW // TK
    win_mask = jnp.where(lax.broadcasted_iota(jnp.int32, (TK, rows), 0)
                         > (lax.broadcasted_iota(jnp.int32, (TK, rows), 1) & (tq - 1)), 0.0, NEG)

    @pl.when(qi < n_win)
    def _():
        short_path(qz_ref, kw_ref, vw_ref, False)

    @pl.when(qi >= n_win)
    def _():
        chunk(qz_ref, kw_ref, vw_ref, pl.multiple_of((qi - n_win) * TK, TK), WINDOW + TK, False,
              {0: lambda g: win_mask, n_win - 1: near_bias(1), n_win: near_bias(0)})

    finish(2)

    for h in range(hpg):
        zp = z_ref[0, :, h * LANES:(h + 1) * LANES]
        o_ref[0, :, h * LANES:(h + 1) * LANES] = (osum_ref[h] * _silu(zp)).astype(o_ref.dtype)


def _nsa_attention(pb3, pf3, vt, kvc, bias_c, bias_d, ov, gx):
    b, t, _ = pb3.shape
    n_cmp = kvc.shape[2]
    n_blk = N_HEADS_NSA // 2
    full = lambda shape: pl.BlockSpec(shape, lambda i, q: (0,) * len(shape))
    kv_spec = lambda col: pl.BlockSpec((1, t, LANES), lambda i, q: (i, 0, col))
    vt_spec = lambda blk: pl.BlockSpec((1, 1, t // TK, LANES, TK), lambda i, q: (i, blk, 0, 0, 0))
    return pl.pallas_call(
        _nsa_kernel,
        grid=(b, t // TQ_NSA),
        in_specs=[pl.BlockSpec((1, TQ_NSA, n_blk * LANES), lambda i, q: (i, q, PB_QB // n_blk)),
                  pl.BlockSpec((1, 1, n_cmp, LANES), lambda i, q: (i, 0, 0, 0)),
                  pl.BlockSpec((1, 1, n_cmp, LANES), lambda i, q: (i, 1, 0, 0)),
                  kv_spec(PB_KS), vt_spec(VT_VS), kv_spec(PB_KW), vt_spec(VT_VW),
                  pl.BlockSpec((1, TQ_NSA, LANES), lambda i, q: (i, q, PF_GB)),
                  pl.BlockSpec((1, TQ_NSA, n_blk * LANES), lambda i, q: (i, q, PF_ZB // n_blk)),
                  pl.BlockSpec((N_HEADS_NSA, TQ_NSA, n_cmp), lambda i, q: (0, q, 0)),
                  full(bias_d.shape), full(ov.shape), full(gx.shape)],
        out_specs=pl.BlockSpec((1, TQ_NSA, n_blk * LANES), lambda i, q: (i, q, 0)),
        out_shape=jax.ShapeDtypeStruct((b, t, n_blk * LANES), _BF16),
        scratch_shapes=[pltpu.VMEM((NSA_KV_GROUPS, SUBLANES, NSA_HPG * TQ_NSA), _F32),
                        pltpu.VMEM((NSA_KV_GROUPS, HEAD_DIM + SUBLANES, NSA_HPG * TQ_NSA), _F32),
                        pltpu.VMEM((n_blk, TQ_NSA, LANES), _F32),
                        pltpu.VMEM((NSA_KV_GROUPS, NSA_HPG * TQ_NSA, LANES), _BF16),
                        pltpu.VMEM((NSA_KV_GROUPS, NSA_HPG * TQ_NSA, LANES), _BF16),
                        pltpu.VMEM((TQ_NSA, 3 * n_blk * LANES), _F32),
                        pltpu.VMEM((TQ_NSA, NSA_KV_GROUPS * n_cmp), _F32),
                        pltpu.VMEM((2, NSA_KV_GROUPS, 4 * TK, NSA_HPG * TQ_NSA), _F32)],
        compiler_params=_cparams(("arbitrary", "arbitrary")),
        name="native_sparse_attention",
    )(pb3, kvc, kvc, pb3, vt, pb3, vt, pf3, pf3, bias_c, bias_d, ov, gx)


def _out_kernel(x_ref, oa_ref, ob_ref, oc_ref, w_ref, g_ref, o_ref, *, final_norm):
    na, nb = oa_ref.shape[1], ob_ref.shape[1]
    mix = (_dot(oa_ref[...], w_ref[0:na])
           + _dot(ob_ref[...], w_ref[na:na + nb])
           + _dot(oc_ref[...], w_ref[na + nb:]))
    x = x_ref[...] + mix
    if final_norm:
        x = x * lax.rsqrt(jnp.mean(x * x, axis=-1, keepdims=True) + RMS_EPS) * g_ref[...]
    o_ref[...] = x


def _out_proj(x2, oa, ob, oc, w, g, final_norm):
    n = x2.shape[0]
    row = lambda width: pl.BlockSpec((TM_PROJ, width), lambda i: (i, 0))
    return pl.pallas_call(
        functools.partial(_out_kernel, final_norm=final_norm),
        grid=(n // TM_PROJ,),
        in_specs=[row(D_MODEL), row(oa.shape[1]), row(ob.shape[1]), row(oc.shape[1]),
                  pl.BlockSpec((D_MODEL, D_MODEL), lambda i: (0, 0)),
                  pl.BlockSpec((1, D_MODEL), lambda i: (0, 0))],
        out_specs=row(D_MODEL),
        out_shape=jax.ShapeDtypeStruct((n, D_MODEL), _F32),
        compiler_params=_cparams(("arbitrary",)),
        name="out_proj_residual",
    )(x2, oa, ob, oc, w, g)


def _head_perm_cols(width_per_head, order):
    return np.concatenate([np.arange(h * width_per_head, (h + 1) * width_per_head) for h in order])


def _layout_w_in(w):
    widths = [256, 256, 256, 256, 512, 128, 128, 128, 128, 128, 128, 24, 512, 256, 256, 256, 4, 256]
    offs = np.concatenate([[0], np.cumsum(widths)])
    (qa, ka, va, za, qb, kc, vc, ks, vs, kw, vw, gb, zb, qc, kf, vf, fc, zc) = [
        w[:, offs[i]:offs[i + 1]] for i in range(len(widths))]
    scale = HEAD_DIM ** -0.5
    perm = _head_perm_cols(HEAD_DIM, NSA_HEAD_ORDER)
    pad = lambda a: jnp.pad(a, ((0, 0), (0, LANES - a.shape[1])))
    wb = jnp.concatenate([qb[:, perm] * (scale * LOG2E), qa * (-scale), ka, va, ks, kw,
                          qc * (scale * LOG2E), kf], axis=1)
    wf = jnp.concatenate([zb[:, perm], za, zc, pad(gb), pad(fc)], axis=1)
    wc = jnp.stack([a[:, g * HEAD_DIM:(g + 1) * HEAD_DIM]
                    for a in (kc, vc) for g in range(NSA_KV_GROUPS)], axis=0)
    wvt = jnp.stack([vs.T, vw.T, vf[:, :LANES].T, vf[:, LANES:].T], axis=0)
    return wb.astype(_BF16), wf.astype(_BF16), wc.astype(_BF16), wvt.astype(_BF16)


def _static_tables(t):
    tq = TQ_NSA
    n_cmp_pad = t // CMP_STRIDE
    j = np.arange(n_cmp_pad)
    dist_c = np.arange(t)[:, None] - (j[None, :] * CMP_STRIDE + CMP_BLOCK - 1)
    dist_c[:, n_cmp_pad - 1] = -1
    i_, j_ = np.arange(tq)[:, None], np.arange(TK)[None, :]
    dist_d = np.concatenate([d * TK + i_ - j_ for d in range(3)], axis=0)
    n_slc = LANES // NSA_KV_GROUPS
    cmp_start = j * CMP_STRIDE
    cmp_end = cmp_start + CMP_BLOCK - 1
    slc_start = np.arange(n_slc) * SLC_BLOCK
    ov1 = np.clip(np.minimum(cmp_end[:, None], slc_start[None, :] + SLC_BLOCK - 1)
                  - np.maximum(cmp_start[:, None], slc_start[None, :]) + 1, 0, None) / CMP_BLOCK
    ov1[n_cmp_pad - 1:] = 0.0
    ov1 = ov1[:, :min(n_slc, t // SLC_BLOCK)]
    ov = np.zeros((NSA_KV_GROUPS * n_cmp_pad, LANES), np.float32)
    for g in range(NSA_KV_GROUPS):
        ov[g * n_cmp_pad:(g + 1) * n_cmp_pad, g * n_slc:g * n_slc + ov1.shape[1]] = ov1
    ov3 = np.concatenate([ov, ov, ov], axis=0)
    n_blk = N_HEADS_NSA // 2
    gx = np.zeros((LANES, 3 * n_blk * LANES), np.float32)
    for c in range(3):
        for p in range(n_blk):
            for s in range(2):
                head = NSA_HEAD_ORDER[2 * p + s]
                col0 = (c * n_blk + p) * LANES + s * HEAD_DIM
                gx[3 * head + c, col0:col0 + HEAD_DIM] = 1.0
    gx2 = np.concatenate([gx, gx], axis=0)
    return (jnp.asarray(dist_c, jnp.int32), jnp.asarray(dist_d, jnp.int32),
            jnp.asarray(ov3, _BF16), jnp.asarray(gx2, _BF16))


def kernel(x, norm_g, w_in, w_out, forget_b, cmp_w1, cmp_b1, cmp_w2, cmp_pe, rel_bias, final_g):
    b, t, d = x.shape
    depth = norm_g.shape[0]
    assert d == D_MODEL and t % TM_PROJ == 0 and t % (CMP_STRIDE * LANES) == 0
    assert t // SLC_BLOCK <= LANES // NSA_KV_GROUPS and TQ_NSA == TK
    n_cmp_pad = t // CMP_STRIDE

    dist_c, dist_d, ov3, gx2 = _static_tables(t)
    bias_c = _bias_table(rel_bias * LOG2E, dist_c, 32)
    bias_d = _bias_table(rel_bias * LOG2E, dist_d, 32).reshape(N_HEADS_NSA, 3, TQ_NSA, TK)
    bias_d = bias_d[:, :2] - bias_d[:, 2:3]
    bias_d = bias_d.reshape(NSA_KV_GROUPS, NSA_HPG, 2, TQ_NSA, TK).transpose(0, 2, 4, 1, 3).reshape(
        NSA_KV_GROUPS, 2, TK, NSA_HPG * TQ_NSA)

    perm_rows = _head_perm_cols(HEAD_DIM, NSA_HEAD_ORDER)
    x2 = x.reshape(b * t, d)
    out = None
    for l in range(depth):
        wb, wf, wc, wvt = _layout_w_in(w_in[l])
        pb, pf, pc, vt = _proj(x2, norm_g[l].reshape(1, d), wb, wf, wc, wvt, t)
        pb3 = pb.reshape(b, t, PB_BLOCKS * LANES)
        pf3 = pf.reshape(b, t, PF_BLOCKS * LANES)

        fb_row = jnp.pad(forget_b[l], (0, LANES - N_HEADS_FOX)).reshape(1, LANES)
        augq, augk = _fgate(pf3, fb_row)

        halves = pc.reshape(b, 2, NSA_KV_GROUPS, n_cmp_pad, CMP_STRIDE * HEAD_DIM)
        w2 = cmp_w2[l]
        zeros = jnp.zeros_like(w2)
        w2p = jnp.stack([jnp.concatenate([w2, zeros], axis=-1),
                         jnp.concatenate([zeros, w2], axis=-1)], axis=1).astype(_BF16)
        kvc = _compress(halves, cmp_w1[l].astype(_BF16),
                        cmp_pe[l].reshape(2, 1, CMP_BLOCK * HEAD_DIM).astype(_BF16),
                        cmp_b1[l].reshape(2, 1, CMP_HIDDEN), w2p)

        o_a = _sb_attention(pb3, pf3)
        o_b = _nsa_attention(pb3, pf3, vt, kvc, bias_c, bias_d, ov3, gx2)
        o_c = _fox_attention(pb3, pf3, vt, augq, augk)

        wo = w_out[l]
        wo = jnp.concatenate([wo[:N_HEADS_SB * HEAD_DIM],
                              wo[N_HEADS_SB * HEAD_DIM:][:N_HEADS_NSA * HEAD_DIM][perm_rows],
                              wo[(N_HEADS_SB + N_HEADS_NSA) * HEAD_DIM:]], axis=0).astype(_BF16)
        last = l == depth - 1
        x2 = _out_proj(x2, o_a.reshape(b * t, -1), o_b.reshape(b * t, -1), o_c.reshape(b * t, -1),
                       wo, final_g.reshape(1, d), last)
    return x2.reshape(b, t, d)
```

```python
import functools
import math

import jax
import jax.numpy as jnp
import numpy as np
from jax import lax
from jax.experimental import pallas as pl
from jax.experimental.pallas import tpu as pltpu

D_MODEL = 1024
HEAD_DIM = 64
N_HEADS_SB = 4
N_HEADS_FOX = 4
N_HEADS_NSA = 8
NSA_KV_GROUPS = 2
NSA_HPG = N_HEADS_NSA // NSA_KV_GROUPS
CMP_BLOCK = 32
CMP_STRIDE = 16
CMP_HIDDEN = 256
SLC_BLOCK = 64
SLC_TOP = 16
WINDOW = 512
REL_BUCKETS = 32
REL_MAX_DIST = 128
FORCE_SCORE = 1e4
RMS_EPS = 1e-6
NEG = -1e30
LOG2E = math.log2(math.e)

LANES = 128
SUBLANES = 8
VMEM_LIMIT = 56 * 1024 * 1024

TM_PROJ = 512
TQ_SB = 512
TQ_FOX = 512
TQ_NSA = 128
TK = 128

PB_QB, PB_QA, PB_KA, PB_VA = 0, 4, 6, 8
PB_KS, PB_KW = 10, 11
PB_QC, PB_KF = 12, 14
PB_BLOCKS = 16
N_CMP_SLABS = 2 * NSA_KV_GROUPS
VT_VS, VT_VW, VT_VF = 0, 1, 2
N_VT = 4
PF_ZB, PF_ZA, PF_ZC, PF_GB, PF_FC = 0, 4, 6, 8, 9
PF_BLOCKS = 10

NSA_HEAD_ORDER = [0, 4, 1, 5, 2, 6, 3, 7]

_F32 = jnp.float32
_BF16 = jnp.bfloat16


def _cparams(sem):
    return pltpu.CompilerParams(dimension_semantics=sem, vmem_limit_bytes=VMEM_LIMIT)


def _dot(a, b):
    return jnp.dot(a, b, preferred_element_type=_F32)


def _dot_nt(a, b):
    return lax.dot_general(a, b, (((1,), (1,)), ((), ())), preferred_element_type=_F32)


def _split2(x):
    hi = x.astype(_BF16)
    lo = (x - hi.astype(_F32)).astype(_BF16)
    return jnp.concatenate([hi, lo], axis=1)


def _split3(x):
    h1 = x.astype(_BF16)
    r1 = x - h1.astype(_F32)
    h2 = r1.astype(_BF16)
    h3 = (r1 - h2.astype(_F32)).astype(_BF16)
    return jnp.concatenate([h1, h2, h3], axis=1)


def _sigmoid(x):
    return 1.0 / (1.0 + jnp.exp(-x))


def _silu(x):
    return x * _sigmoid(x)


def _rel_bucket_np(n):
    n = np.maximum(n, 0)
    max_exact = REL_BUCKETS // 2
    nf = np.maximum(n, 1).astype(np.float64)
    large = max_exact + (np.log(nf / max_exact) / math.log(REL_MAX_DIST / max_exact)
                         * (REL_BUCKETS - max_exact)).astype(np.int64)
    large = np.minimum(large, REL_BUCKETS - 1)
    return np.where(n < max_exact, n, large)


def _bucket_thresholds():
    n = np.arange(0, 4 * REL_MAX_DIST)
    bk = _rel_bucket_np(n)
    assert np.all(np.diff(bk) >= 0) and bk[-1] == REL_BUCKETS - 1
    return [int(np.argmax(bk >= b)) for b in range(REL_BUCKETS)]


_BUCKET_THR = _bucket_thresholds()


def _bias_kernel(tab_ref, dist_ref, o_ref):
    n = dist_ref[...]
    acc = [jnp.full(n.shape, tab_ref[0, h], _F32) for h in range(N_HEADS_NSA)]
    for b in range(1, REL_BUCKETS):
        ge = n >= _BUCKET_THR[b]
        for h in range(N_HEADS_NSA):
            acc[h] = jnp.where(ge, tab_ref[b, h], acc[h])
    valid = n >= 0
    for h in range(N_HEADS_NSA):
        o_ref[h] = jnp.where(valid, acc[h], NEG)


def _bias_table(rel_bias, dist, rows):
    n_rows, n_cols = dist.shape
    return pl.pallas_call(
        _bias_kernel,
        grid=(n_rows // rows,),
        in_specs=[pl.BlockSpec(memory_space=pltpu.SMEM),
                  pl.BlockSpec((rows, n_cols), lambda i: (i, 0))],
        out_specs=pl.BlockSpec((N_HEADS_NSA, rows, n_cols), lambda i: (0, i, 0)),
        out_shape=jax.ShapeDtypeStruct((N_HEADS_NSA, n_rows, n_cols), _F32),
        compiler_params=_cparams(("arbitrary",)),
        name="rel_bias_table",
    )(rel_bias, dist)


def _proj_kernel(x_ref, g_ref, wb_ref, wf_ref, wc_ref, wvt_ref, pb_ref, pf_ref, pc_ref, vt_ref):
    x = x_ref[...]
    y = x * lax.rsqrt(jnp.mean(x * x, axis=-1, keepdims=True) + RMS_EPS)
    h = (y * g_ref[...]).astype(_BF16)
    v_all = _dot(h, wvt_ref[...])
    for j in range(N_VT):
        v_t = v_all[:, j * LANES:(j + 1) * LANES].T.astype(_BF16)
        for c in range(TM_PROJ // TK):
            vt_ref[0, j, c] = v_t[:, c * TK:(c + 1) * TK]
    kv_cmp = _dot(h, wc_ref[...]).astype(_BF16)
    for s in range(N_CMP_SLABS):
        pc_ref[0, s] = kv_cmp[:, s * HEAD_DIM:(s + 1) * HEAD_DIM]
    chunk = 4 * LANES
    for c in range(0, PB_BLOCKS * LANES, chunk):
        w = min(chunk, PB_BLOCKS * LANES - c)
        pb_ref[:, c:c + w] = _dot(h, wb_ref[:, c:c + w]).astype(_BF16)
    for c in range(0, PF_BLOCKS * LANES, chunk):
        w = min(chunk, PF_BLOCKS * LANES - c)
        pf_ref[:, c:c + w] = _dot(h, wf_ref[:, c:c + w])


def _proj(x2, g, wb, wf, wc, wvt, t):
    n = x2.shape[0]
    per_seq = t // TM_PROJ
    return pl.pallas_call(
        _proj_kernel,
        grid=(n // TM_PROJ,),
        in_specs=[pl.BlockSpec((TM_PROJ, D_MODEL), lambda i: (i, 0)),
                  pl.BlockSpec((1, D_MODEL), lambda i: (0, 0)),
                  pl.BlockSpec((D_MODEL, PB_BLOCKS * LANES), lambda i: (0, 0)),
                  pl.BlockSpec((D_MODEL, PF_BLOCKS * LANES), lambda i: (0, 0)),
                  pl.BlockSpec((D_MODEL, N_CMP_SLABS * HEAD_DIM), lambda i: (0, 0)),
                  pl.BlockSpec((D_MODEL, N_VT * LANES), lambda i: (0, 0))],
        out_specs=[pl.BlockSpec((TM_PROJ, PB_BLOCKS * LANES), lambda i: (i, 0)),
                   pl.BlockSpec((TM_PROJ, PF_BLOCKS * LANES), lambda i: (i, 0)),
                   pl.BlockSpec((1, N_CMP_SLABS, TM_PROJ, HEAD_DIM),
                                lambda i: (i // per_seq, 0, i % per_seq, 0)),
                   pl.BlockSpec((1, N_VT, TM_PROJ // TK, LANES, TK),
                                lambda i: (i // per_seq, 0, i % per_seq, 0, 0))],
        out_shape=[jax.ShapeDtypeStruct((n, PB_BLOCKS * LANES), _BF16),
                   jax.ShapeDtypeStruct((n, PF_BLOCKS * LANES), _F32),
                   jax.ShapeDtypeStruct((n // t, N_CMP_SLABS, t, HEAD_DIM), _BF16),
                   jax.ShapeDtypeStruct((n // t, N_VT, t // TK, LANES, TK), _BF16)],
        compiler_params=_cparams(("arbitrary",)),
        name="rmsnorm_in_proj",
    )(x2, g, wb, wf, wc, wvt)


N_SPLIT = 3


def _fgate_tables():
    n_pairs = N_HEADS_FOX // 2
    pq = np.zeros((N_SPLIT * LANES, n_pairs * LANES), np.float32)
    pk = np.zeros_like(pq)
    ones_q = np.zeros((1, n_pairs * LANES), np.float32)
    ones_k = np.zeros_like(ones_q)
    for head in range(N_HEADS_FOX):
        pair, slot = divmod(head, 2)
        base = pair * LANES + (HEAD_DIM if slot == 0 else 0)
        for j in range(N_SPLIT):
            pq[j * LANES + head, base + j] = 1.0
            pk[j * LANES + head, base + N_SPLIT + j] = 1.0
        ones_q[0, base + N_SPLIT:base + 2 * N_SPLIT] = 1.0
        ones_k[0, base:base + N_SPLIT] = 1.0
    return (jnp.asarray(pq, _BF16), jnp.asarray(pk, _BF16),
            jnp.asarray(ones_q), jnp.asarray(ones_k))


def _fgate_kernel(fc_ref, fb_ref, pq_ref, pk_ref, oq_ref, ok_ref, augq_ref, augk_ref):
    t = fc_ref.shape[1]
    z = fc_ref[0] + fb_ref[...]
    logf = jnp.minimum(z, 0.0) - jnp.log1p(jnp.exp(-jnp.abs(z)))
    row = lax.broadcasted_iota(jnp.int32, (t, LANES), 0)
    c = logf
    shift = 1
    while shift < t:
        c = c + jnp.where(row >= shift, pltpu.roll(c, shift, axis=0), 0.0)
        shift *= 2
    c3 = _split3(c * LOG2E)
    aq = _dot(c3, pq_ref[...]) + oq_ref[...]
    ak = ok_ref[...] - _dot(c3, pk_ref[...])
    for p in range(N_HEADS_FOX // 2):
        augq_ref[0, p] = aq[:, p * LANES:(p + 1) * LANES].astype(_BF16)
        augk_ref[0, p] = ak[:, p * LANES:(p + 1) * LANES].astype(_BF16)


def _fgate(pf3, fb_row):
    b, t, _ = pf3.shape
    n_pairs = N_HEADS_FOX // 2
    tables = _fgate_tables()
    full = lambda a: pl.BlockSpec(a.shape, lambda i: (0,) * a.ndim)
    aug = lambda: pl.BlockSpec((1, n_pairs, t, LANES), lambda i: (i, 0, 0, 0))
    return pl.pallas_call(
        _fgate_kernel,
        grid=(b,),
        in_specs=[pl.BlockSpec((1, t, LANES), lambda i: (i, 0, PF_FC)),
                  pl.BlockSpec((1, LANES), lambda i: (0, 0))] + [full(a) for a in tables],
        out_specs=[aug(), aug()],
        out_shape=[jax.ShapeDtypeStruct((b, n_pairs, t, LANES), _BF16),
                   jax.ShapeDtypeStruct((b, n_pairs, t, LANES), _BF16)],
        compiler_params=_cparams(("arbitrary",)),
        name="forget_gate_cumsum",
    )(pf3, fb_row, *tables)


def _compress_kernel(x_ref, w1_ref, pe_ref, b1_ref, w2_ref, o_ref):
    nc = x_ref.shape[3]
    half = CMP_STRIDE * HEAD_DIM
    w1 = w1_ref[0]
    c1 = _dot(jnp.broadcast_to(pe_ref[0], (SUBLANES, 2 * half)), w1)[0:1] + b1_ref[0]
    out = jnp.zeros((nc, LANES), _F32)
    for g in range(NSA_KV_GROUPS):
        xg = x_ref[0, 0, g]
        a = _dot(xg, w1[:half])
        bb = _dot(xg, w1[half:])
        h = a + pltpu.roll(bb, nc - 1, axis=0) + c1
        out = out + _dot(_silu(h).astype(_BF16), w2_ref[0, g])
    o_ref[0, 0] = out.astype(_BF16)


def _compress(halves, w1, pe, b1, w2p):
    b, _, g, nc, width = halves.shape
    return pl.pallas_call(
        _compress_kernel,
        grid=(b, 2),
        in_specs=[pl.BlockSpec((1, 1, g, nc, width), lambda i, k: (i, k, 0, 0, 0)),
                  pl.BlockSpec((1, 2 * width, CMP_HIDDEN), lambda i, k: (k, 0, 0)),
                  pl.BlockSpec((1, 1, 2 * width), lambda i, k: (k, 0, 0)),
                  pl.BlockSpec((1, 1, CMP_HIDDEN), lambda i, k: (k, 0, 0)),
                  pl.BlockSpec((1, g, CMP_HIDDEN, LANES), lambda i, k: (k, 0, 0, 0))],
        out_specs=pl.BlockSpec((1, 1, nc, LANES), lambda i, k: (i, k, 0, 0)),
        out_shape=jax.ShapeDtypeStruct((b, 2, nc, LANES), _BF16),
        compiler_params=_cparams(("arbitrary", "arbitrary")),
        name="nsa_compress",
    )(halves, w1, pe, b1, w2p)


def _sb_kernel(q_ref, k_ref, v_ref, z_ref, o_ref, qh_ref, c_ref, acc_ref):
    qi = pl.program_id(2)
    tq = q_ref.shape[1]
    lane = lax.broadcasted_iota(jnp.int32, (tq, LANES), 1)
    q2 = q_ref[0]
    qh = [jnp.where(lane < HEAD_DIM, q2, jnp.zeros_like(q2)),
          jnp.where(lane >= HEAD_DIM, q2, jnp.zeros_like(q2))]
    r_i = lax.broadcasted_iota(jnp.int32, (2 * TK, 2 * TK), 0)
    c_i = lax.broadcasted_iota(jnp.int32, (2 * TK, 2 * TK), 1)
    uu = jnp.where(r_i >= c_i, 1.0, 0.0).astype(_BF16)
    c_ref[...] = jnp.zeros_like(c_ref)
    acc_ref[...] = jnp.zeros_like(acc_ref)
    for h in range(2):
        qh_ref[h] = qh[h]
    row = lax.broadcasted_iota(jnp.int32, (tq, LANES), 0)
    n_blocks = tq // TK

    def chunk(start, diagonal):
        k2 = k_ref[0, pl.ds(start, tq), :]
        v2 = v_ref[0, pl.ds(start, tq), :]
        for h in range(2):
            s = _dot_nt(qh_ref[h], k2)
            carry = c_ref[h]
            w_blocks = [None] * n_blocks
            for c in reversed(range(0, n_blocks, 2)):
                nz, l1m, mask = [], [], []
                for cc in (c, c + 1):
                    nzc = s[:, cc * TK:(cc + 1) * TK]
                    neg_abs = lax.bitcast_convert_type(
                        lax.bitcast_convert_type(nzc, jnp.uint32) | jnp.uint32(0x80000000), _F32)
                    lc = jnp.minimum(nzc, 0.0) - jnp.log(1.0 + jnp.exp(neg_abs))
                    if diagonal:
                        mask.append(lane + cc * TK < row)
                        lc = jnp.where(mask[-1], lc, 0.0)
                    nz.append(nzc)
                    l1m.append(lc.astype(_BF16))
                rc = _dot(jnp.concatenate(l1m, axis=1), uu)
                for i, cc in enumerate((c, c + 1)):
                    w = jnp.exp((rc[:, i * TK:(i + 1) * TK] + carry) - nz[i])
                    if diagonal:
                        w = jnp.where(mask[i], w, 0.0)
                    w_blocks[cc] = w.astype(_BF16)
                carry = carry + jnp.broadcast_to(rc[:, 0:1], carry.shape)
            acc_ref[h] += _dot(jnp.concatenate(w_blocks, axis=1), v2)
            c_ref[h] = carry

    chunk(pl.multiple_of(qi * tq, tq), True)

    def far_pair(it, carry):
        chunk(pl.multiple_of((qi - 1 - 2 * it) * tq, tq), False)
        chunk(pl.multiple_of((qi - 2 - 2 * it) * tq, tq), False)
        return carry

    lax.fori_loop(0, qi >> 1, far_pair, 0)

    @pl.when((qi & 1) == 1)
    def _():
        chunk(0, False)
    o = jnp.where(lane < HEAD_DIM, acc_ref[0], acc_ref[1])
    o_ref[0] = (o * _silu(z_ref[0])).astype(o_ref.dtype)


def _sb_attention(pb3, pf3):
    b, t, _ = pb3.shape
    n_pairs = N_HEADS_SB // 2
    return pl.pallas_call(
        _sb_kernel,
        grid=(b, n_pairs, t // TQ_SB),
        in_specs=[pl.BlockSpec((1, TQ_SB, LANES), lambda i, p, q: (i, q, PB_QA + p)),
                  pl.BlockSpec((1, t, LANES), lambda i, p, q: (i, 0, PB_KA + p)),
                  pl.BlockSpec((1, t, LANES), lambda i, p, q: (i, 0, PB_VA + p)),
                  pl.BlockSpec((1, TQ_SB, LANES), lambda i, p, q: (i, q, PF_ZA + p))],
        out_specs=pl.BlockSpec((1, TQ_SB, LANES), lambda i, p, q: (i, q, p)),
        out_shape=jax.ShapeDtypeStruct((b, t, n_pairs * LANES), _BF16),
        scratch_shapes=[pltpu.VMEM((2, TQ_SB, LANES), _BF16),
                        pltpu.VMEM((2, TQ_SB, LANES), _F32),
                        pltpu.VMEM((2, TQ_SB, LANES), _F32)],
        compiler_params=_cparams(("arbitrary", "arbitrary", "arbitrary")),
        name="stick_breaking_attention",
    )(pb3, pb3, pb3, pf3)


def _fox_kernel(q_ref, k_ref, vt_ref, augq_ref, augk_ref, z_ref, o_ref,
                qh_ref, m_ref, acc_ref, s_ref):
    qi = pl.program_id(2)
    tq = q_ref.shape[1]
    tk = tq
    lane = lax.broadcasted_iota(jnp.int32, (tq, LANES), 1)
    keep = [lane < HEAD_DIM, lane >= HEAD_DIM]
    q2 = q_ref[0]
    aq = augq_ref[0, 0]
    for h in range(2):
        qh_ref[h] = jnp.where(keep[h], q2, aq)
    m_ref[...] = jnp.full_like(m_ref, NEG)
    acc_ref[...] = jnp.zeros_like(acc_ref)
    key_i = lax.broadcasted_iota(jnp.int32, (TK, tq), 0)
    qry_i = lax.broadcasted_iota(jnp.int32, (TK, tq), 1)

    def scores(c, slot):
        start = c * tk if isinstance(c, int) else pl.multiple_of(c * tk, tk)
        k2 = k_ref[0, pl.ds(start, tk), :]
        ak = augk_ref[0, 0, pl.ds(start, tk), :]
        for h in range(2):
            s_ref[slot, h] = _dot_nt(jnp.where(keep[h], k2, ak), qh_ref[h])

    ones_rows = jnp.ones((SUBLANES, tk), _BF16)

    def consume(c, slot, diagonal):
        per = tk // TK
        vt = jnp.concatenate([vt_ref[0, 0, c * per + j] for j in range(per)], axis=1)
        for h in range(2):
            blocks = [s_ref[slot, h, j * TK:(j + 1) * TK, :] for j in range(tk // TK)]
            if diagonal:
                blocks = [jnp.where(key_i + j * TK <= qry_i, blk, NEG) for j, blk in enumerate(blocks)]
            mx = blocks[0]
            for blk in blocks[1:]:
                mx = jnp.maximum(mx, blk)
            m_old = m_ref[h]
            m_new = jnp.maximum(m_old, jnp.max(mx, axis=0, keepdims=True))
            m_row = m_new[0:1]
            pt = jnp.concatenate([jnp.exp2(blk - m_row).astype(_BF16) for blk in blocks], axis=0)
            vth = jnp.concatenate([vt[h * HEAD_DIM:(h + 1) * HEAD_DIM], ones_rows], axis=0)
            acc_ref[h] = jnp.exp2(m_old - m_new)[0:1] * acc_ref[h] + _dot(vth, pt)
            m_ref[h] = m_new

    scores(0, 0)
    n_pairs = qi >> 1

    def far_pair(i, carry):
        c = 2 * i
        scores(c + 1, 1)
        consume(c, 0, False)
        scores(c + 2, 0)
        consume(c + 1, 1, False)
        return carry

    lax.fori_loop(0, n_pairs, far_pair, 0)

    @pl.when(qi == 2 * n_pairs)
    def _():
        consume(qi, 0, True)

    @pl.when(qi != 2 * n_pairs)
    def _():
        scores(qi, 1)
        consume(qi - 1, 0, False)
        consume(qi, 1, True)
    o_t = jnp.concatenate([acc_ref[h, :HEAD_DIM] / acc_ref[h, HEAD_DIM:HEAD_DIM + 1]
                           for h in range(2)], axis=0)
    o_ref[0] = (o_t.T * _silu(z_ref[0])).astype(o_ref.dtype)


def _fox_attention(pb3, pf3, vt, augq, augk):
    b, t, _ = pb3.shape
    n_pairs = N_HEADS_FOX // 2
    n_chunks = t // TQ_FOX
    return pl.pallas_call(
        _fox_kernel,
        grid=(b, n_pairs, t // TQ_FOX),
        in_specs=[pl.BlockSpec((1, TQ_FOX, LANES), lambda i, p, q: (i, q, PB_QC + p)),
                  pl.BlockSpec((1, t, LANES), lambda i, p, q: (i, 0, PB_KF + p)),
                  pl.BlockSpec((1, 1, t // TK, LANES, TK), lambda i, p, q: (i, VT_VF + p, 0, 0, 0)),
                  pl.BlockSpec((1, 1, TQ_FOX, LANES), lambda i, p, q: (i, p, q, 0)),
                  pl.BlockSpec((1, 1, t, LANES), lambda i, p, q: (i, p, 0, 0)),
                  pl.BlockSpec((1, TQ_FOX, LANES), lambda i, p, q: (i, q, PF_ZC + p))],
        out_specs=pl.BlockSpec((1, TQ_FOX, LANES), lambda i, p, q: (i, q, p)),
        out_shape=jax.ShapeDtypeStruct((b, t, n_pairs * LANES), _BF16),
        scratch_shapes=[pltpu.VMEM((2, TQ_FOX, LANES), _BF16),
                        pltpu.VMEM((2, SUBLANES, TQ_FOX), _F32),
                        pltpu.VMEM((2, HEAD_DIM + SUBLANES, TQ_FOX), _F32),
                        pltpu.VMEM((2, 2, TQ_FOX, TQ_FOX), _F32)],
        compiler_params=_cparams(("arbitrary", "arbitrary", "arbitrary")),
        name="forgetting_attention",
    )(pb3, pb3, vt, augq, augk, pf3)


def _nsa_kernel(q_ref, kc_ref, vc_ref, ks_ref, vs_ref, kw_ref, vw_ref, gl_ref, z_ref,
                bc_ref, bd_ref, ov_ref, gx_ref, o_ref,
                m_ref, acc_ref, osum_ref, qz_ref, qsel_ref, gates_ref, psum_ref, s_ref):
    qi = pl.program_id(1)
    tq = q_ref.shape[1]
    hpg = NSA_HPG
    rows = hpg * tq
    n_cmp = kc_ref.shape[2]
    lane = lax.broadcasted_iota(jnp.int32, (tq, LANES), 1)
    qrow = qi * tq + lax.broadcasted_iota(jnp.int32, (tq, LANES), 0)
    lane_r = lax.broadcasted_iota(jnp.int32, (rows, LANES), 1)
    irow_r = lax.broadcasted_iota(jnp.int32, (rows, LANES), 0) & (tq - 1)
    half = [lane_r < HEAD_DIM, lane_r >= HEAD_DIM]

    gates_ref[...] = _dot(_split2(_sigmoid(gl_ref[0])), gx_ref[...])

    def gate(c, h):
        return gates_ref[:, (c * hpg + h) * LANES:(c * hpg + h + 1) * LANES]

    q4 = jnp.concatenate([q_ref[0, :, h * LANES:(h + 1) * LANES] for h in range(hpg)], axis=0)
    for g in range(NSA_KV_GROUPS):
        qz_ref[g] = jnp.where(half[g], q4, jnp.zeros_like(q4))

    def heads_to_blocks(x0, x1):
        return [jnp.where(lane < HEAD_DIM, x0[h * tq:(h + 1) * tq], x1[h * tq:(h + 1) * tq])
                for h in range(hpg)]

    kc = kc_ref[0, 0]
    vc = vc_ref[0, 0]
    o_cmp = []
    for g in range(NSA_KV_GROUPS):
        sc = _dot_nt(qz_ref[g], kc) + bc_ref[g * hpg:(g + 1) * hpg].reshape(rows, n_cmp)
        mx = jnp.max(sc, axis=-1, keepdims=True)
        e = jnp.exp2(sc - mx)
        den = jnp.sum(e, axis=-1, keepdims=True)
        pc = e * jnp.where(mx > 0.5 * NEG, 1.0 / den, 0.0)
        psum_ref[:, g * n_cmp:(g + 1) * n_cmp] = jnp.sum(pc.reshape(hpg, tq, n_cmp), axis=0)
        o_cmp.append(_dot(pc.astype(_BF16), vc))
    for h, blk_o in enumerate(heads_to_blocks(*o_cmp)):
        osum_ref[h] = gate(0, h) * blk_o

    rank_from = SLC_TOP * SLC_BLOCK // tq

    @pl.when(qi < rank_from)
    def _():
        for g in range(NSA_KV_GROUPS):
            qsel_ref[g] = qz_ref[g]

    @pl.when(qi >= rank_from)
    def _():
        imp = _dot(_split3(psum_ref[...]), ov_ref[...])
        blk = lane & (SLC_BLOCK - 1)
        cur = qrow >> int(math.log2(SLC_BLOCK))
        forced = (blk == 0) | (blk == cur) | (blk == cur - 1)
        imp = jnp.where(forced, FORCE_SCORE, imp)
        imp = jnp.where(blk > cur, -FORCE_SCORE, imp)
        imp_t = imp.T
        n_sel = LANES // NSA_KV_GROUPS
        n_grp = n_sel // SUBLANES
        sub = lax.broadcasted_iota(jnp.int32, (SUBLANES, tq), 0)
        neg_t = []
        for g in range(NSA_KV_GROUPS):
            a = imp_t[g * n_sel:(g + 1) * n_sel]
            a_grp = [a[r * SUBLANES:(r + 1) * SUBLANES] for r in range(n_grp)]
            cnt = [jnp.zeros((SUBLANES, tq), _F32) for _ in range(n_grp)]
            for j in range(n_sel):
                rj = jnp.broadcast_to(a[j:j + 1], (SUBLANES, tq))
                jr, jo = divmod(j, SUBLANES)
                for r in range(n_grp):
                    if r > jr:
                        one = jnp.where(rj >= a_grp[r], 1.0, 0.0)
                    elif r < jr:
                        one = jnp.where(rj > a_grp[r], 1.0, 0.0)
                    else:
                        tie = jnp.where(sub > jo, jnp.where(rj == a_grp[r], 1.0, 0.0), 0.0)
                        one = jnp.where(rj > a_grp[r], 1.0, tie)
                    cnt[r] = cnt[r] + one
            neg_t.append(jnp.where(jnp.concatenate(cnt, axis=0) < float(SLC_TOP), 0.0, NEG))
        selneg = jnp.concatenate(neg_t[::-1], axis=0).T.astype(_BF16)
        selneg4 = jnp.concatenate([selneg] * hpg, axis=0)
        for g in range(NSA_KV_GROUPS):
            qsel_ref[g] = jnp.where(half[g], q4, selneg4)

    def reset():
        m_ref[...] = jnp.full_like(m_ref, 2.0 * NEG)
        acc_ref[...] = jnp.zeros_like(acc_ref)

    def chunk(q_src, k_ref, vt_ref, start, tk, onehot, extras, limit=None):
        s = scores(q_src, k_ref, start, tk, onehot)
        consume(lambda g, c: s[g][c * TK:(c + 1) * TK], vt_ref, start, tk, extras, limit)

    def scores(q_src, k_ref, start, tk, onehot):
        k2 = k_ref[0, pl.ds(start, tk), :]
        lane_k = lax.broadcasted_iota(jnp.int32, (tk, LANES), 1)
        key_blk = (start + lax.broadcasted_iota(jnp.int32, (tk, LANES), 0)) >> int(math.log2(SLC_BLOCK))
        oh = jnp.where((lane_k & (SLC_BLOCK - 1)) == key_blk, 1.0, 0.0).astype(_BF16)
        out = []
        for g in range(NSA_KV_GROUPS):
            keep = (lane_k < HEAD_DIM) if g == 0 else (lane_k >= HEAD_DIM)
            out.append(_dot_nt(jnp.where(keep, k2, oh) if onehot else k2, q_src[g]))
        return out

    def consume(block_of, vt_ref, start, tk, extras, limit=None):
        first = start // TK if isinstance(start, int) else start >> int(math.log2(TK))
        vt = jnp.concatenate([vt_ref[0, 0, first + j] for j in range(tk // TK)], axis=1)
        ones_rows = jnp.ones((SUBLANES, tk), _BF16)
        key_pos = start + lax.broadcasted_iota(jnp.int32, (HEAD_DIM + SUBLANES, tk), 1)
        for g in range(NSA_KV_GROUPS):
            vth = jnp.concatenate([vt[g * HEAD_DIM:(g + 1) * HEAD_DIM], ones_rows], axis=0)
            if limit is not None:
                vth = jnp.where(key_pos < limit, vth, jnp.zeros_like(vth))
            blocks = [block_of(g, c) for c in range(tk // TK)]
            for c, extra in extras.items():
                blocks[c] = blocks[c] + extra(g)
            mx = blocks[0]
            for blk_s in blocks[1:]:
                mx = jnp.maximum(mx, blk_s)
            m_old = m_ref[g]
            m_new = jnp.maximum(m_old, jnp.max(mx, axis=0, keepdims=True))
            m_row = m_new[0:1]
            pt = jnp.concatenate([jnp.exp2(blk_s - m_row).astype(_BF16) for blk_s in blocks], axis=0)
            acc_ref[g] = jnp.exp2(m_old - m_new)[0:1] * acc_ref[g] + _dot(vth, pt)
            m_ref[g] = m_new

    def finish(c):
        outs = []
        for g in range(NSA_KV_GROUPS):
            den = acc_ref[g, HEAD_DIM:HEAD_DIM + 1]
            outs.append(acc_ref[g, :HEAD_DIM] * jnp.where(den > 0.0, 1.0 / den, 0.0))
        for h in range(hpg):
            blk_t = jnp.concatenate([o[:, h * tq:(h + 1) * tq] for o in outs], axis=0)
            osum_ref[h] += gate(c, h) * blk_t.T

    def near_bias(d):
        return lambda g: bd_ref[g, d]

    def short_path(q_src, k_ref, v_ref, onehot):
        def far(kt, carry):
            chunk(q_src, k_ref, v_ref, pl.multiple_of(kt * TK, TK), TK, onehot, {})
            return carry

        lax.fori_loop(0, jnp.maximum(qi - 1, 0), far, 0)

        @pl.when(qi >= 1)
        def _():
            chunk(q_src, k_ref, v_ref, pl.multiple_of((qi - 1) * TK, TK), TK, onehot,
                  {0: near_bias(1)})

        chunk(q_src, k_ref, v_ref, pl.multiple_of(qi * TK, TK), TK, onehot, {0: near_bias(0)})

    reset()
    big = 4 * TK
    sel_from = big // TK - 1

    @pl.when(qi < sel_from)
    def _():
        short_path(qsel_ref, ks_ref, vs_ref, True)

    @pl.when(qi >= sel_from)
    def _():
        last = pl.multiple_of((qi - sel_from) * TK, TK)
        n_far = (qi - sel_from + big // TK - 1) >> int(math.log2(big // TK))
        near = {big // TK - 2: near_bias(1), big // TK - 1: near_bias(0)}

        def start_of(k):
            return pl.multiple_of(jnp.where(k < n_far, k * big, last), TK)

        def put(k, slot):
            for g, s in enumerate(scores(qsel_ref, ks_ref, start_of(k), big, True)):
                s_ref[slot, g] = s

        def take(k, slot, extras, limit):
            consume(lambda g, c: s_ref[slot, g, c * TK:(c + 1) * TK, :], vs_ref,
                    start_of(k), big, extras, limit)

        put(0, 0)
        n_pair = n_far >> 1

        def far_pair(i, carry):
            k = 2 * i
            put(k + 1, 1)
            take(k, 0, {}, last)
            put(k + 2, 0)
            take(k + 1, 1, {}, last)
            return carry

        lax.fori_loop(0, n_pair, far_pair, 0)

        @pl.when(n_far == 2 * n_pair)
        def _():
            take(n_far, 0, near, None)

        @pl.when(n_far != 2 * n_pair)
        def _():
            put(n_far, 1)
            take(n_far - 1, 0, {}, last)
            take(n_far, 1, near, None)

    finish(1)

    reset()
    n_win = WINDOW // TK
    win_mask = jnp.where(lax.broadcasted_iota(jnp.int32, (TK, rows), 0)
                         > (lax.broadcasted_iota(jnp.int32, (TK, rows), 1) & (tq - 1)), 0.0, NEG)

    @pl.when(qi < n_win)
    def _():
        short_path(qz_ref, kw_ref, vw_ref, False)

    @pl.when(qi >= n_win)
    def _():
        chunk(qz_ref, kw_ref, vw_ref, pl.multiple_of((qi - n_win) * TK, TK), WINDOW + TK, False,
              {0: lambda g: win_mask, n_win - 1: near_bias(1), n_win: near_bias(0)})

    finish(2)

    for h in range(hpg):
        zp = z_ref[0, :, h * LANES:(h + 1) * LANES]
        o_ref[0, :, h * LANES:(h + 1) * LANES] = (osum_ref[h] * _silu(zp)).astype(o_ref.dtype)


def _nsa_attention(pb3, pf3, vt, kvc, bias_c, bias_d, ov, gx):
    b, t, _ = pb3.shape
    n_cmp = kvc.shape[2]
    n_blk = N_HEADS_NSA // 2
    full = lambda shape: pl.BlockSpec(shape, lambda i, q: (0,) * len(shape))
    kv_spec = lambda col: pl.BlockSpec((1, t, LANES), lambda i, q: (i, 0, col))
    vt_spec = lambda blk: pl.BlockSpec((1, 1, t // TK, LANES, TK), lambda i, q: (i, blk, 0, 0, 0))
    return pl.pallas_call(
        _nsa_kernel,
        grid=(b, t // TQ_NSA),
        in_specs=[pl.BlockSpec((1, TQ_NSA, n_blk * LANES), lambda i, q: (i, q, PB_QB // n_blk)),
                  pl.BlockSpec((1, 1, n_cmp, LANES), lambda i, q: (i, 0, 0, 0)),
                  pl.BlockSpec((1, 1, n_cmp, LANES), lambda i, q: (i, 1, 0, 0)),
                  kv_spec(PB_KS), vt_spec(VT_VS), kv_spec(PB_KW), vt_spec(VT_VW),
                  pl.BlockSpec((1, TQ_NSA, LANES), lambda i, q: (i, q, PF_GB)),
                  pl.BlockSpec((1, TQ_NSA, n_blk * LANES), lambda i, q: (i, q, PF_ZB // n_blk)),
                  pl.BlockSpec((N_HEADS_NSA, TQ_NSA, n_cmp), lambda i, q: (0, q, 0)),
                  full(bias_d.shape), full(ov.shape), full(gx.shape)],
        out_specs=pl.BlockSpec((1, TQ_NSA, n_blk * LANES), lambda i, q: (i, q, 0)),
        out_shape=jax.ShapeDtypeStruct((b, t, n_blk * LANES), _BF16),
        scratch_shapes=[pltpu.VMEM((NSA_KV_GROUPS, SUBLANES, NSA_HPG * TQ_NSA), _F32),
                        pltpu.VMEM((NSA_KV_GROUPS, HEAD_DIM + SUBLANES, NSA_HPG * TQ_NSA), _F32),
                        pltpu.VMEM((n_blk, TQ_NSA, LANES), _F32),
                        pltpu.VMEM((NSA_KV_GROUPS, NSA_HPG * TQ_NSA, LANES), _BF16),
                        pltpu.VMEM((NSA_KV_GROUPS, NSA_HPG * TQ_NSA, LANES), _BF16),
                        pltpu.VMEM((TQ_NSA, 3 * n_blk * LANES), _F32),
                        pltpu.VMEM((TQ_NSA, NSA_KV_GROUPS * n_cmp), _F32),
                        pltpu.VMEM((2, NSA_KV_GROUPS, 4 * TK, NSA_HPG * TQ_NSA), _F32)],
        compiler_params=_cparams(("arbitrary", "arbitrary")),
        name="native_sparse_attention",
    )(pb3, kvc, kvc, pb3, vt, pb3, vt, pf3, pf3, bias_c, bias_d, ov, gx)


def _out_kernel(x_ref, oa_ref, ob_ref, oc_ref, w_ref, g_ref, o_ref, *, final_norm):
    na, nb = oa_ref.shape[1], ob_ref.shape[1]
    mix = (_dot(oa_ref[...], w_ref[0:na])
           + _dot(ob_ref[...], w_ref[na:na + nb])
           + _dot(oc_ref[...], w_ref[na + nb:]))
    x = x_ref[...] + mix
    if final_norm:
        x = x * lax.rsqrt(jnp.mean(x * x, axis=-1, keepdims=True) + RMS_EPS) * g_ref[...]
    o_ref[...] = x


def _out_proj(x2, oa, ob, oc, w, g, final_norm):
    n = x2.shape[0]
    row = lambda width: pl.BlockSpec((TM_PROJ, width), lambda i: (i, 0))
    return pl.pallas_call(
        functools.partial(_out_kernel, final_norm=final_norm),
        grid=(n // TM_PROJ,),
        in_specs=[row(D_MODEL), row(oa.shape[1]), row(ob.shape[1]), row(oc.shape[1]),
                  pl.BlockSpec((D_MODEL, D_MODEL), lambda i: (0, 0)),
                  pl.BlockSpec((1, D_MODEL), lambda i: (0, 0))],
        out_specs=row(D_MODEL),
        out_shape=jax.ShapeDtypeStruct((n, D_MODEL), _F32),
        compiler_params=_cparams(("arbitrary",)),
        name="out_proj_residual",
    )(x2, oa, ob, oc, w, g)


def _head_perm_cols(width_per_head, order):
    return np.concatenate([np.arange(h * width_per_head, (h + 1) * width_per_head) for h in order])


def _layout_w_in(w):
    widths = [256, 256, 256, 256, 512, 128, 128, 128, 128, 128, 128, 24, 512, 256, 256, 256, 4, 256]
    offs = np.concatenate([[0], np.cumsum(widths)])
    (qa, ka, va, za, qb, kc, vc, ks, vs, kw, vw, gb, zb, qc, kf, vf, fc, zc) = [
        w[:, offs[i]:offs[i + 1]] for i in range(len(widths))]
    scale = HEAD_DIM ** -0.5
    perm = _head_perm_cols(HEAD_DIM, NSA_HEAD_ORDER)
    pad = lambda a: jnp.pad(a, ((0, 0), (0, LANES - a.shape[1])))
    wb = jnp.concatenate([qb[:, perm] * (scale * LOG2E), qa * (-scale), ka, va, ks, kw,
                          qc * (scale * LOG2E), kf], axis=1)
    wf = jnp.concatenate([zb[:, perm], za, zc, pad(gb), pad(fc)], axis=1)
    wc = jnp.concatenate([kc, vc], axis=1)
    wvt = jnp.concatenate([vs, vw, vf], axis=1)
    return wb.astype(_BF16), wf.astype(_BF16), wc.astype(_BF16), wvt.astype(_BF16)


def _static_tables(t):
    tq = TQ_NSA
    n_cmp_pad = t // CMP_STRIDE
    j = np.arange(n_cmp_pad)
    dist_c = np.arange(t)[:, None] - (j[None, :] * CMP_STRIDE + CMP_BLOCK - 1)
    dist_c[:, n_cmp_pad - 1] = -1
    i_, j_ = np.arange(tq)[:, None], np.arange(TK)[None, :]
    dist_d = np.concatenate([d * TK + i_ - j_ for d in range(3)], axis=0)
    n_slc = LANES // NSA_KV_GROUPS
    cmp_start = j * CMP_STRIDE
    cmp_end = cmp_start + CMP_BLOCK - 1
    slc_start = np.arange(n_slc) * SLC_BLOCK
    ov1 = np.clip(np.minimum(cmp_end[:, None], slc_start[None, :] + SLC_BLOCK - 1)
                  - np.maximum(cmp_start[:, None], slc_start[None, :]) + 1, 0, None) / CMP_BLOCK
    ov1[n_cmp_pad - 1:] = 0.0
    ov1 = ov1[:, :min(n_slc, t // SLC_BLOCK)]
    ov = np.zeros((NSA_KV_GROUPS * n_cmp_pad, LANES), np.float32)
    for g in range(NSA_KV_GROUPS):
        ov[g * n_cmp_pad:(g + 1) * n_cmp_pad, g * n_slc:g * n_slc + ov1.shape[1]] = ov1
    ov3 = np.concatenate([ov, ov, ov], axis=0)
    n_blk = N_HEADS_NSA // 2
    gx = np.zeros((LANES, 3 * n_blk * LANES), np.float32)
    for c in range(3):
        for p in range(n_blk):
            for s in range(2):
                head = NSA_HEAD_ORDER[2 * p + s]
                col0 = (c * n_blk + p) * LANES + s * HEAD_DIM
                gx[3 * head + c, col0:col0 + HEAD_DIM] = 1.0
    gx2 = np.concatenate([gx, gx], axis=0)
    return (jnp.asarray(dist_c, jnp.int32), jnp.asarray(dist_d, jnp.int32),
            jnp.asarray(ov3, _BF16), jnp.asarray(gx2, _BF16))


def kernel(x, norm_g, w_in, w_out, forget_b, cmp_w1, cmp_b1, cmp_w2, cmp_pe, rel_bias, final_g):
    b, t, d = x.shape
    depth = norm_g.shape[0]
    assert d == D_MODEL and t % TM_PROJ == 0 and t % (CMP_STRIDE * LANES) == 0
    assert t // SLC_BLOCK <= LANES // NSA_KV_GROUPS and TQ_NSA == TK
    n_cmp_pad = t // CMP_STRIDE

    dist_c, dist_d, ov3, gx2 = _static_tables(t)
    bias_c = _bias_table(rel_bias * LOG2E, dist_c, 32)
    bias_d = _bias_table(rel_bias * LOG2E, dist_d, 32).reshape(N_HEADS_NSA, 3, TQ_NSA, TK)
    bias_d = bias_d[:, :2] - bias_d[:, 2:3]
    bias_d = bias_d.reshape(NSA_KV_GROUPS, NSA_HPG, 2, TQ_NSA, TK).transpose(0, 2, 4, 1, 3).reshape(
        NSA_KV_GROUPS, 2, TK, NSA_HPG * TQ_NSA)

    perm_rows = _head_perm_cols(HEAD_DIM, NSA_HEAD_ORDER)
    x2 = x.reshape(b * t, d)
    out = None
    for l in range(depth):
        wb, wf, wc, wvt = _layout_w_in(w_in[l])
        pb, pf, pc, vt = _proj(x2, norm_g[l].reshape(1, d), wb, wf, wc, wvt, t)
        pb3 = pb.reshape(b, t, PB_BLOCKS * LANES)
        pf3 = pf.reshape(b, t, PF_BLOCKS * LANES)

        fb_row = jnp.pad(forget_b[l], (0, LANES - N_HEADS_FOX)).reshape(1, LANES)
        augq, augk = _fgate(pf3, fb_row)

        halves = pc.reshape(b, 2, NSA_KV_GROUPS, n_cmp_pad, CMP_STRIDE * HEAD_DIM)
        w2 = cmp_w2[l]
        zeros = jnp.zeros_like(w2)
        w2p = jnp.stack([jnp.concatenate([w2, zeros], axis=-1),
                         jnp.concatenate([zeros, w2], axis=-1)], axis=1).astype(_BF16)
        kvc = _compress(halves, cmp_w1[l].astype(_BF16),
                        cmp_pe[l].reshape(2, 1, CMP_BLOCK * HEAD_DIM).astype(_BF16),
                        cmp_b1[l].reshape(2, 1, CMP_HIDDEN), w2p)

        o_a = _sb_attention(pb3, pf3)
        o_b = _nsa_attention(pb3, pf3, vt, kvc, bias_c, bias_d, ov3, gx2)
        o_c = _fox_attention(pb3, pf3, vt, augq, augk)

        wo = w_out[l]
        wo = jnp.concatenate([wo[:N_HEADS_SB * HEAD_DIM],
                              wo[N_HEADS_SB * HEAD_DIM:][:N_HEADS_NSA * HEAD_DIM][perm_rows],
                              wo[(N_HEADS_SB + N_HEADS_NSA) * HEAD_DIM:]], axis=0).astype(_BF16)
        last = l == depth - 1
        x2 = _out_proj(x2, o_a.reshape(b * t, -1), o_b.reshape(b * t, -1), o_c.reshape(b * t, -1),
                       wo, final_g.reshape(1, d), last)
    return x2.reshape(b, t, d)
```

```python
import functools
import math

import jax
import jax.numpy as jnp
import numpy as np
from jax import lax
from jax.experimental import pallas as pl
from jax.experimental.pallas import tpu as pltpu

D_MODEL = 1024
HEAD_DIM = 64
N_HEADS_SB = 4
N_HEADS_FOX = 4
N_HEADS_NSA = 8
NSA_KV_GROUPS = 2
NSA_HPG = N_HEADS_NSA // NSA_KV_GROUPS
CMP_BLOCK = 32
CMP_STRIDE = 16
CMP_HIDDEN = 256
SLC_BLOCK = 64
SLC_TOP = 16
WINDOW = 512
REL_BUCKETS = 32
REL_MAX_DIST = 128
FORCE_SCORE = 1e4
RMS_EPS = 1e-6
NEG = -1e30
LOG2E = math.log2(math.e)

LANES = 128
SUBLANES = 8
VMEM_LIMIT = 56 * 1024 * 1024

TM_PROJ = 512
TQ_SB = 512
TQ_FOX = 512
FOX_PARTS = 2
TQ_NSA = 128
TK = 128

PB_QB, PB_QA, PB_KA, PB_VA = 0, 4, 6, 8
PB_KS, PB_KW = 10, 11
PB_QC, PB_KF = 12, 14
PB_BLOCKS = 16
N_CMP_SLABS = 2 * NSA_KV_GROUPS
VT_VS, VT_VW, VT_VF = 0, 1, 2
N_VT = 4
PF_ZB, PF_ZA, PF_ZC, PF_GB, PF_FC = 0, 4, 6, 8, 9
PF_BLOCKS = 10

NSA_HEAD_ORDER = [0, 4, 1, 5, 2, 6, 3, 7]

_F32 = jnp.float32
_BF16 = jnp.bfloat16


def _cparams(sem):
    return pltpu.CompilerParams(dimension_semantics=sem, vmem_limit_bytes=VMEM_LIMIT)


def _dot(a, b):
    return jnp.dot(a, b, preferred_element_type=_F32)


def _dot_nt(a, b):
    return lax.dot_general(a, b, (((1,), (1,)), ((), ())), preferred_element_type=_F32)


def _split2(x):
    hi = x.astype(_BF16)
    lo = (x - hi.astype(_F32)).astype(_BF16)
    return jnp.concatenate([hi, lo], axis=1)


def _split3(x):
    h1 = x.astype(_BF16)
    r1 = x - h1.astype(_F32)
    h2 = r1.astype(_BF16)
    h3 = (r1 - h2.astype(_F32)).astype(_BF16)
    return jnp.concatenate([h1, h2, h3], axis=1)


def _sigmoid(x):
    return 1.0 / (1.0 + jnp.exp(-x))


def _silu(x):
    return x * _sigmoid(x)


def _rel_bucket_np(n):
    n = np.maximum(n, 0)
    max_exact = REL_BUCKETS // 2
    nf = np.maximum(n, 1).astype(np.float64)
    large = max_exact + (np.log(nf / max_exact) / math.log(REL_MAX_DIST / max_exact)
                         * (REL_BUCKETS - max_exact)).astype(np.int64)
    large = np.minimum(large, REL_BUCKETS - 1)
    return np.where(n < max_exact, n, large)


def _bucket_thresholds():
    n = np.arange(0, 4 * REL_MAX_DIST)
    bk = _rel_bucket_np(n)
    assert np.all(np.diff(bk) >= 0) and bk[-1] == REL_BUCKETS - 1
    return [int(np.argmax(bk >= b)) for b in range(REL_BUCKETS)]


_BUCKET_THR = _bucket_thresholds()


def _bias_kernel(tab_ref, dist_ref, o_ref):
    n = dist_ref[...]
    acc = [jnp.full(n.shape, tab_ref[0, h], _F32) for h in range(N_HEADS_NSA)]
    for b in range(1, REL_BUCKETS):
        ge = n >= _BUCKET_THR[b]
        for h in range(N_HEADS_NSA):
            acc[h] = jnp.where(ge, tab_ref[b, h], acc[h])
    valid = n >= 0
    for h in range(N_HEADS_NSA):
        o_ref[h] = jnp.where(valid, acc[h], NEG)


def _bias_table(rel_bias, dist, rows):
    n_rows, n_cols = dist.shape
    return pl.pallas_call(
        _bias_kernel,
        grid=(n_rows // rows,),
        in_specs=[pl.BlockSpec(memory_space=pltpu.SMEM),
                  pl.BlockSpec((rows, n_cols), lambda i: (i, 0))],
        out_specs=pl.BlockSpec((N_HEADS_NSA, rows, n_cols), lambda i: (0, i, 0)),
        out_shape=jax.ShapeDtypeStruct((N_HEADS_NSA, n_rows, n_cols), _F32),
        compiler_params=_cparams(("arbitrary",)),
        name="rel_bias_table",
    )(rel_bias, dist)


def _proj_kernel(x_ref, g_ref, wb_ref, wf_ref, wc_ref, wvt_ref, pb_ref, pf_ref, pc_ref, vt_ref):
    x = x_ref[...]
    y = x * lax.rsqrt(jnp.mean(x * x, axis=-1, keepdims=True) + RMS_EPS)
    h = (y * g_ref[...]).astype(_BF16)
    v_all = _dot(h, wvt_ref[...])
    for j in range(N_VT):
        v_t = v_all[:, j * LANES:(j + 1) * LANES].T.astype(_BF16)
        for c in range(TM_PROJ // TK):
            vt_ref[0, j, c] = v_t[:, c * TK:(c + 1) * TK]
    kv_cmp = _dot(h, wc_ref[...]).astype(_BF16)
    for s in range(N_CMP_SLABS):
        pc_ref[0, s] = kv_cmp[:, s * HEAD_DIM:(s + 1) * HEAD_DIM]
    chunk = 4 * LANES
    for c in range(0, PB_BLOCKS * LANES, chunk):
        w = min(chunk, PB_BLOCKS * LANES - c)
        pb_ref[:, c:c + w] = _dot(h, wb_ref[:, c:c + w]).astype(_BF16)
    for c in range(0, PF_BLOCKS * LANES, chunk):
        w = min(chunk, PF_BLOCKS * LANES - c)
        pf_ref[:, c:c + w] = _dot(h, wf_ref[:, c:c + w])


def _proj(x2, g, wb, wf, wc, wvt, t):
    n = x2.shape[0]
    per_seq = t // TM_PROJ
    return pl.pallas_call(
        _proj_kernel,
        grid=(n // TM_PROJ,),
        in_specs=[pl.BlockSpec((TM_PROJ, D_MODEL), lambda i: (i, 0)),
                  pl.BlockSpec((1, D_MODEL), lambda i: (0, 0)),
                  pl.BlockSpec((D_MODEL, PB_BLOCKS * LANES), lambda i: (0, 0)),
                  pl.BlockSpec((D_MODEL, PF_BLOCKS * LANES), lambda i: (0, 0)),
                  pl.BlockSpec((D_MODEL, N_CMP_SLABS * HEAD_DIM), lambda i: (0, 0)),
                  pl.BlockSpec((D_MODEL, N_VT * LANES), lambda i: (0, 0))],
        out_specs=[pl.BlockSpec((TM_PROJ, PB_BLOCKS * LANES), lambda i: (i, 0)),
                   pl.BlockSpec((TM_PROJ, PF_BLOCKS * LANES), lambda i: (i, 0)),
                   pl.BlockSpec((1, N_CMP_SLABS, TM_PROJ, HEAD_DIM),
                                lambda i: (i // per_seq, 0, i % per_seq, 0)),
                   pl.BlockSpec((1, N_VT, TM_PROJ // TK, LANES, TK),
                                lambda i: (i // per_seq, 0, i % per_seq, 0, 0))],
        out_shape=[jax.ShapeDtypeStruct((n, PB_BLOCKS * LANES), _BF16),
                   jax.ShapeDtypeStruct((n, PF_BLOCKS * LANES), _F32),
                   jax.ShapeDtypeStruct((n // t, N_CMP_SLABS, t, HEAD_DIM), _BF16),
                   jax.ShapeDtypeStruct((n // t, N_VT, t // TK, LANES, TK), _BF16)],
        compiler_params=_cparams(("arbitrary",)),
        name="rmsnorm_in_proj",
    )(x2, g, wb, wf, wc, wvt)


N_SPLIT = 3


def _fgate_tables():
    n_pairs = N_HEADS_FOX // 2
    pq = np.zeros((N_SPLIT * LANES, n_pairs * LANES), np.float32)
    pk = np.zeros_like(pq)
    ones_q = np.zeros((1, n_pairs * LANES), np.float32)
    ones_k = np.zeros_like(ones_q)
    for head in range(N_HEADS_FOX):
        pair, slot = divmod(head, 2)
        base = pair * LANES + (HEAD_DIM if slot == 0 else 0)
        for j in range(N_SPLIT):
            pq[j * LANES + head, base + j] = 1.0
            pk[j * LANES + head, base + N_SPLIT + j] = 1.0
        ones_q[0, base + N_SPLIT:base + 2 * N_SPLIT] = 1.0
        ones_k[0, base:base + N_SPLIT] = 1.0
    return (jnp.asarray(pq, _BF16), jnp.asarray(pk, _BF16),
            jnp.asarray(ones_q), jnp.asarray(ones_k))


def _fgate_kernel(fc_ref, fb_ref, pq_ref, pk_ref, oq_ref, ok_ref, augq_ref, augk_ref):
    t = fc_ref.shape[1]
    z = fc_ref[0] + fb_ref[...]
    logf = jnp.minimum(z, 0.0) - jnp.log1p(jnp.exp(-jnp.abs(z)))
    row = lax.broadcasted_iota(jnp.int32, (t, LANES), 0)
    c = logf
    shift = 1
    while shift < t:
        c = c + jnp.where(row >= shift, pltpu.roll(c, shift, axis=0), 0.0)
        shift *= 2
    c3 = _split3(c * LOG2E)
    aq = _dot(c3, pq_ref[...]) + oq_ref[...]
    ak = ok_ref[...] - _dot(c3, pk_ref[...])
    for p in range(N_HEADS_FOX // 2):
        augq_ref[0, p] = aq[:, p * LANES:(p + 1) * LANES].astype(_BF16)
        augk_ref[0, p] = ak[:, p * LANES:(p + 1) * LANES].astype(_BF16)


def _fgate(pf3, fb_row):
    b, t, _ = pf3.shape
    n_pairs = N_HEADS_FOX // 2
    tables = _fgate_tables()
    full = lambda a: pl.BlockSpec(a.shape, lambda i: (0,) * a.ndim)
    aug = lambda: pl.BlockSpec((1, n_pairs, t, LANES), lambda i: (i, 0, 0, 0))
    return pl.pallas_call(
        _fgate_kernel,
        grid=(b,),
        in_specs=[pl.BlockSpec((1, t, LANES), lambda i: (i, 0, PF_FC)),
                  pl.BlockSpec((1, LANES), lambda i: (0, 0))] + [full(a) for a in tables],
        out_specs=[aug(), aug()],
        out_shape=[jax.ShapeDtypeStruct((b, n_pairs, t, LANES), _BF16),
                   jax.ShapeDtypeStruct((b, n_pairs, t, LANES), _BF16)],
        compiler_params=_cparams(("arbitrary",)),
        name="forget_gate_cumsum",
    )(pf3, fb_row, *tables)


def _compress_kernel(x_ref, w1_ref, pe_ref, b1_ref, w2_ref, o_ref, ot_ref):
    nc = x_ref.shape[3]
    half = CMP_STRIDE * HEAD_DIM
    w1 = w1_ref[0]
    c1 = _dot(jnp.broadcast_to(pe_ref[0], (SUBLANES, 2 * half)), w1)[0:1] + b1_ref[0]
    out = jnp.zeros((nc, LANES), _F32)
    for g in range(NSA_KV_GROUPS):
        xg = x_ref[0, 0, g]
        a = _dot(xg, w1[:half])
        bb = _dot(xg, w1[half:])
        h = a + pltpu.roll(bb, nc - 1, axis=0) + c1
        out = out + _dot(_silu(h).astype(_BF16), w2_ref[0, g])
    o_ref[0, 0] = out.astype(_BF16)
    ot_ref[0, 0] = out.T.astype(_BF16)


def _compress(halves, w1, pe, b1, w2p):
    b, _, g, nc, width = halves.shape
    return pl.pallas_call(
        _compress_kernel,
        grid=(b, 2),
        in_specs=[pl.BlockSpec((1, 1, g, nc, width), lambda i, k: (i, k, 0, 0, 0)),
                  pl.BlockSpec((1, 2 * width, CMP_HIDDEN), lambda i, k: (k, 0, 0)),
                  pl.BlockSpec((1, 1, 2 * width), lambda i, k: (k, 0, 0)),
                  pl.BlockSpec((1, 1, CMP_HIDDEN), lambda i, k: (k, 0, 0)),
                  pl.BlockSpec((1, g, CMP_HIDDEN, LANES), lambda i, k: (k, 0, 0, 0))],
        out_specs=[pl.BlockSpec((1, 1, nc, LANES), lambda i, k: (i, k, 0, 0)),
                   pl.BlockSpec((1, 1, LANES, nc), lambda i, k: (i, k, 0, 0))],
        out_shape=[jax.ShapeDtypeStruct((b, 2, nc, LANES), _BF16),
                   jax.ShapeDtypeStruct((b, 2, LANES, nc), _BF16)],
        compiler_params=_cparams(("arbitrary", "arbitrary")),
        name="nsa_compress",
    )(halves, w1, pe, b1, w2p)


def _sb_kernel(q_ref, k_ref, v_ref, z_ref, o_ref, qh_ref, c_ref, acc_ref):
    qi = pl.program_id(2)
    tq = q_ref.shape[1]
    lane = lax.broadcasted_iota(jnp.int32, (tq, LANES), 1)
    q2 = q_ref[0]
    qh = [jnp.where(lane < HEAD_DIM, q2, jnp.zeros_like(q2)),
          jnp.where(lane >= HEAD_DIM, q2, jnp.zeros_like(q2))]
    r_i = lax.broadcasted_iota(jnp.int32, (2 * TK, 2 * TK), 0)
    c_i = lax.broadcasted_iota(jnp.int32, (2 * TK, 2 * TK), 1)
    uu = jnp.where(r_i >= c_i, 1.0, 0.0).astype(_BF16)
    c_ref[...] = jnp.zeros_like(c_ref)
    acc_ref[...] = jnp.zeros_like(acc_ref)
    for h in range(2):
        qh_ref[h] = qh[h]
    row = lax.broadcasted_iota(jnp.int32, (tq, LANES), 0)
    n_blocks = tq // TK

    def chunk(start, diagonal):
        k2 = k_ref[0, pl.ds(start, tq), :]
        v2 = v_ref[0, pl.ds(start, tq), :]
        for h in range(2):
            s = _dot_nt(qh_ref[h], k2)
            carry = c_ref[h]
            w_blocks = [None] * n_blocks
            for c in reversed(range(0, n_blocks, 2)):
                nz, l1m, mask = [], [], []
                for cc in (c, c + 1):
                    nzc = s[:, cc * TK:(cc + 1) * TK]
                    neg_abs = lax.bitcast_convert_type(
                        lax.bitcast_convert_type(nzc, jnp.uint32) | jnp.uint32(0x80000000), _F32)
                    lc = jnp.minimum(nzc, 0.0) - jnp.log(1.0 + jnp.exp(neg_abs))
                    if diagonal:
                        mask.append(lane + cc * TK < row)
                        lc = jnp.where(mask[-1], lc, 0.0)
                    nz.append(nzc)
                    l1m.append(lc.astype(_BF16))
                rc = _dot(jnp.concatenate(l1m, axis=1), uu)
                for i, cc in enumerate((c, c + 1)):
                    w = jnp.exp((rc[:, i * TK:(i + 1) * TK] + carry) - nz[i])
                    if diagonal:
                        w = jnp.where(mask[i], w, 0.0)
                    w_blocks[cc] = w.astype(_BF16)
                carry = carry + jnp.broadcast_to(rc[:, 0:1], carry.shape)
            acc_ref[h] += _dot(jnp.concatenate(w_blocks, axis=1), v2)
            c_ref[h] = carry

    chunk(pl.multiple_of(qi * tq, tq), True)

    def far_pair(it, carry):
        chunk(pl.multiple_of((qi - 1 - 2 * it) * tq, tq), False)
        chunk(pl.multiple_of((qi - 2 - 2 * it) * tq, tq), False)
        return carry

    lax.fori_loop(0, qi >> 1, far_pair, 0)

    @pl.when((qi & 1) == 1)
    def _():
        chunk(0, False)
    o = jnp.where(lane < HEAD_DIM, acc_ref[0], acc_ref[1])
    o_ref[0] = (o * _silu(z_ref[0])).astype(o_ref.dtype)


def _sb_attention(pb3, pf3):
    b, t, _ = pb3.shape
    n_pairs = N_HEADS_SB // 2
    return pl.pallas_call(
        _sb_kernel,
        grid=(b, n_pairs, t // TQ_SB),
        in_specs=[pl.BlockSpec((1, TQ_SB, LANES), lambda i, p, q: (i, q, PB_QA + p)),
                  pl.BlockSpec((1, t, LANES), lambda i, p, q: (i, 0, PB_KA + p)),
                  pl.BlockSpec((1, t, LANES), lambda i, p, q: (i, 0, PB_VA + p)),
                  pl.BlockSpec((1, TQ_SB, LANES), lambda i, p, q: (i, q, PF_ZA + p))],
        out_specs=pl.BlockSpec((1, TQ_SB, LANES), lambda i, p, q: (i, q, p)),
        out_shape=jax.ShapeDtypeStruct((b, t, n_pairs * LANES), _BF16),
        scratch_shapes=[pltpu.VMEM((2, TQ_SB, LANES), _BF16),
                        pltpu.VMEM((2, TQ_SB, LANES), _F32),
                        pltpu.VMEM((2, TQ_SB, LANES), _F32)],
        compiler_params=_cparams(("arbitrary", "arbitrary", "arbitrary")),
        name="stick_breaking_attention",
    )(pb3, pb3, pb3, pf3)


def _fox_kernel(q_ref, k_ref, vt_ref, augq_ref, augk_ref, z_ref, o_ref,
                qh_ref, m_ref, acc_ref, s_ref):
    qi = pl.program_id(2)
    tq = q_ref.shape[1]
    tk = tq
    lane = lax.broadcasted_iota(jnp.int32, (tq, LANES), 1)
    keep = [lane < HEAD_DIM, lane >= HEAD_DIM]
    q2 = q_ref[0]
    aq = augq_ref[0, 0]
    for h in range(2):
        qh_ref[h] = jnp.where(keep[h], q2, aq)
    m_ref[...] = jnp.full_like(m_ref, NEG)
    acc_ref[...] = jnp.zeros_like(acc_ref)
    key_i = lax.broadcasted_iota(jnp.int32, (TK, tq), 0)
    qry_i = lax.broadcasted_iota(jnp.int32, (TK, tq), 1)

    def scores(c, slot):
        start = c * tk if isinstance(c, int) else pl.multiple_of(c * tk, tk)
        k2 = k_ref[0, pl.ds(start, tk), :]
        ak = augk_ref[0, 0, pl.ds(start, tk), :]
        for h in range(2):
            s_ref[slot, h] = _dot_nt(jnp.where(keep[h], k2, ak), qh_ref[h])

    ones_rows = jnp.ones((SUBLANES, tk), _BF16)

    def consume(c, slot, diagonal):
        per = tk // TK
        vt = jnp.concatenate([vt_ref[0, 0, c * per + j] for j in range(per)], axis=1)
        for h in range(2):
            vth = jnp.concatenate([vt[h * HEAD_DIM:(h + 1) * HEAD_DIM], ones_rows], axis=0)
            for part in range(FOX_PARTS):
                js = range(part * per // FOX_PARTS, (part + 1) * per // FOX_PARTS)
                blocks = [s_ref[slot, h, j * TK:(j + 1) * TK, :] for j in js]
                if diagonal:
                    blocks = [jnp.where(key_i + j * TK <= qry_i, blk, NEG) for j, blk in zip(js, blocks)]
                mx = blocks[0]
                for blk in blocks[1:]:
                    mx = jnp.maximum(mx, blk)
                m_old = m_ref[h]
                m_new = jnp.maximum(m_old, jnp.max(mx, axis=0, keepdims=True))
                m_row = m_new[0:1]
                pt = jnp.concatenate([jnp.exp2(blk - m_row).astype(_BF16) for blk in blocks], axis=0)
                acc_ref[h] = (jnp.exp2(m_old - m_new)[0:1] * acc_ref[h]
                              + _dot(vth[:, js[0] * TK:(js[-1] + 1) * TK], pt))
                m_ref[h] = m_new

    scores(0, 0)
    n_pairs = qi >> 1

    def far_pair(i, carry):
        c = 2 * i
        scores(c + 1, 1)
        consume(c, 0, False)
        scores(c + 2, 0)
        consume(c + 1, 1, False)
        return carry

    lax.fori_loop(0, n_pairs, far_pair, 0)

    @pl.when(qi == 2 * n_pairs)
    def _():
        consume(qi, 0, True)

    @pl.when(qi != 2 * n_pairs)
    def _():
        scores(qi, 1)
        consume(qi - 1, 0, False)
        consume(qi, 1, True)
    o_t = jnp.concatenate([acc_ref[h, :HEAD_DIM] / acc_ref[h, HEAD_DIM:HEAD_DIM + 1]
                           for h in range(2)], axis=0)
    o_ref[0] = (o_t.T * _silu(z_ref[0])).astype(o_ref.dtype)


def _fox_attention(pb3, pf3, vt, augq, augk):
    b, t, _ = pb3.shape
    n_pairs = N_HEADS_FOX // 2
    n_chunks = t // TQ_FOX
    return pl.pallas_call(
        _fox_kernel,
        grid=(b, n_pairs, t // TQ_FOX),
        in_specs=[pl.BlockSpec((1, TQ_FOX, LANES), lambda i, p, q: (i, q, PB_QC + p)),
                  pl.BlockSpec((1, t, LANES), lambda i, p, q: (i, 0, PB_KF + p)),
                  pl.BlockSpec((1, 1, t // TK, LANES, TK), lambda i, p, q: (i, VT_VF + p, 0, 0, 0)),
                  pl.BlockSpec((1, 1, TQ_FOX, LANES), lambda i, p, q: (i, p, q, 0)),
                  pl.BlockSpec((1, 1, t, LANES), lambda i, p, q: (i, p, 0, 0)),
                  pl.BlockSpec((1, TQ_FOX, LANES), lambda i, p, q: (i, q, PF_ZC + p))],
        out_specs=pl.BlockSpec((1, TQ_FOX, LANES), lambda i, p, q: (i, q, p)),
        out_shape=jax.ShapeDtypeStruct((b, t, n_pairs * LANES), _BF16),
        scratch_shapes=[pltpu.VMEM((2, TQ_FOX, LANES), _BF16),
                        pltpu.VMEM((2, SUBLANES, TQ_FOX), _F32),
                        pltpu.VMEM((2, HEAD_DIM + SUBLANES, TQ_FOX), _F32),
                        pltpu.VMEM((2, 2, TQ_FOX, TQ_FOX), _F32)],
        compiler_params=_cparams(("arbitrary", "arbitrary", "arbitrary")),
        name="forgetting_attention",
    )(pb3, pb3, vt, augq, augk, pf3)


def _nsa_kernel(q_ref, kc_ref, vct_ref, ks_ref, vs_ref, kw_ref, vw_ref, gl_ref, z_ref,
                bc_ref, bd_ref, ov_ref, o_ref,
                m_ref, acc_ref, osum_ref, qz_ref, qsel_ref, gates_ref, psum_ref, s_ref, ws_ref):
    qi = pl.program_id(1)
    tq = q_ref.shape[1]
    hpg = NSA_HPG
    rows = hpg * tq
    n_cmp = kc_ref.shape[2]
    lane = lax.broadcasted_iota(jnp.int32, (tq, LANES), 1)
    qrow = qi * tq + lax.broadcasted_iota(jnp.int32, (tq, LANES), 0)
    lane_r = lax.broadcasted_iota(jnp.int32, (rows, LANES), 1)
    irow_r = lax.broadcasted_iota(jnp.int32, (rows, LANES), 0) & (tq - 1)
    half = [lane_r < HEAD_DIM, lane_r >= HEAD_DIM]

    gates_ref[...] = _sigmoid(gl_ref[0]).T

    def gated(c, h, per_group):
        parts = []
        for g, o in enumerate(per_group):
            r = 3 * (g * hpg + h) + c
            parts.append(gates_ref[r:r + 1, :] * o[:, h * tq:(h + 1) * tq])
        return jnp.concatenate(parts, axis=0)

    q4 = jnp.concatenate([q_ref[0, :, h * LANES:(h + 1) * LANES] for h in range(hpg)], axis=0)
    for g in range(NSA_KV_GROUPS):
        qz_ref[g] = jnp.where(half[g], q4, jnp.zeros_like(q4))

    def scores(q_src, k_ref, start, tk, onehot):
        k2 = k_ref[0, pl.ds(start, tk), :]
        lane_k = lax.broadcasted_iota(jnp.int32, (tk, LANES), 1)
        key_blk = (start + lax.broadcasted_iota(jnp.int32, (tk, LANES), 0)) >> int(math.log2(SLC_BLOCK))
        oh = jnp.where((lane_k & (SLC_BLOCK - 1)) == key_blk, 1.0, 0.0).astype(_BF16)
        out = []
        for g in range(NSA_KV_GROUPS):
            keep = (lane_k < HEAD_DIM) if g == 0 else (lane_k >= HEAD_DIM)
            out.append(_dot_nt(jnp.where(keep, k2, oh) if onehot else k2, q_src[g]))
        return out


    def heads_to_blocks(x0, x1):
        return [jnp.where(lane < HEAD_DIM, x0[h * tq:(h + 1) * tq], x1[h * tq:(h + 1) * tq])
                for h in range(hpg)]

    kc = kc_ref[0, 0]
    vct = vct_ref[0, 0]
    o_cmp_t = []
    raw = [_dot_nt(kc, qz_ref[g]) for g in range(NSA_KV_GROUPS)]
    for g in range(NSA_KV_GROUPS):
        bias = jnp.concatenate([bc_ref[g * hpg + h] for h in range(hpg)], axis=1)
        sc = raw[g] + bias
        mx = jnp.max(sc, axis=0, keepdims=True)
        e = jnp.exp2(sc - mx)
        den = jnp.sum(e, axis=0, keepdims=True)
        pc = e * jnp.where(mx > 0.5 * NEG, 1.0 / den, 0.0)
        psum = pc[:, 0:tq]
        for h in range(1, hpg):
            psum = psum + pc[:, h * tq:(h + 1) * tq]
        psum_ref[g] = psum
        o_cmp_t.append(_dot(vct[g * HEAD_DIM:(g + 1) * HEAD_DIM], pc.astype(_BF16)))
    for h in range(hpg):
        osum_ref[h] = gated(0, h, o_cmp_t)

    rank_from = SLC_TOP * SLC_BLOCK // tq

    @pl.when(qi < rank_from)
    def _():
        for g in range(NSA_KV_GROUPS):
            qsel_ref[g] = qz_ref[g]

    @pl.when(qi >= rank_from)
    def _():
        n_sel = LANES // NSA_KV_GROUPS
        n_grp = n_sel // SUBLANES
        blk = lax.broadcasted_iota(jnp.int32, (n_sel, tq), 0)
        cur = (qi * tq + lax.broadcasted_iota(jnp.int32, (n_sel, tq), 1)) >> int(math.log2(SLC_BLOCK))
        forced = (blk == 0) | (blk == cur) | (blk == cur - 1)
        sub = lax.broadcasted_iota(jnp.int32, (SUBLANES, tq), 0)
        neg_t = []
        for g in range(NSA_KV_GROUPS):
            p = psum_ref[g]
            p1 = p.astype(_BF16)
            r1 = p - p1.astype(_F32)
            p2 = r1.astype(_BF16)
            p3 = (r1 - p2.astype(_F32)).astype(_BF16)
            a = _dot(ov_ref[...], jnp.concatenate([p1, p2, p3], axis=0))
            a = jnp.where(forced, FORCE_SCORE, a)
            a = jnp.where(blk > cur, -FORCE_SCORE, a)
            a_grp = [a[r * SUBLANES:(r + 1) * SUBLANES] for r in range(n_grp)]
            cnt = [jnp.zeros((SUBLANES, tq), _F32) for _ in range(n_grp)]
            for j in range(n_sel):
                rj = jnp.broadcast_to(a[j:j + 1], (SUBLANES, tq))
                jr, jo = divmod(j, SUBLANES)
                for r in range(n_grp):
                    if r > jr:
                        one = jnp.where(rj >= a_grp[r], 1.0, 0.0)
                    elif r < jr:
                        one = jnp.where(rj > a_grp[r], 1.0, 0.0)
                    else:
                        tie = jnp.where(sub > jo, jnp.where(rj == a_grp[r], 1.0, 0.0), 0.0)
                        one = jnp.where(rj > a_grp[r], 1.0, tie)
                    cnt[r] = cnt[r] + one
            neg_t.append(jnp.where(jnp.concatenate(cnt, axis=0) < float(SLC_TOP), 0.0, NEG))
        selneg = jnp.concatenate(neg_t[::-1], axis=0).T.astype(_BF16)
        selneg4 = jnp.concatenate([selneg] * hpg, axis=0)
        for g in range(NSA_KV_GROUPS):
            qsel_ref[g] = jnp.where(half[g], q4, selneg4)

    SEL, WIN = 0, 1

    def reset(st):
        m_ref[st] = jnp.full(m_ref.shape[1:], 2.0 * NEG, _F32)
        acc_ref[st] = jnp.zeros(acc_ref.shape[1:], _F32)

    def chunk(q_src, k_ref, vt_ref, start, tk, onehot, extras, st):
        s = scores(q_src, k_ref, start, tk, onehot)
        consume(lambda g, c: s[g][c * TK:(c + 1) * TK], vt_ref, start, tk, extras, st)

    def consume(block_of, vt_ref, start, tk, extras, st, limit=None):
        first = start // TK if isinstance(start, int) else start >> int(math.log2(TK))
        vt = jnp.concatenate([vt_ref[0, 0, first + j] for j in range(tk // TK)], axis=1)
        ones_rows = jnp.ones((SUBLANES, tk), _BF16)
        key_pos = start + lax.broadcasted_iota(jnp.int32, (HEAD_DIM + SUBLANES, tk), 1)
        for g in range(NSA_KV_GROUPS):
            vth = jnp.concatenate([vt[g * HEAD_DIM:(g + 1) * HEAD_DIM], ones_rows], axis=0)
            if limit is not None:
                vth = jnp.where(key_pos < limit, vth, jnp.zeros_like(vth))
            blocks = [block_of(g, c) for c in range(tk // TK)]
            for c, extra in extras.items():
                blocks[c] = blocks[c] + extra(g)
            mx = blocks[0]
            for blk_s in blocks[1:]:
                mx = jnp.maximum(mx, blk_s)
            m_old = m_ref[st, g]
            m_new = jnp.maximum(m_old, jnp.max(mx, axis=0, keepdims=True))
            m_row = m_new[0:1]
            pt = jnp.concatenate([jnp.exp2(blk_s - m_row).astype(_BF16) for blk_s in blocks], axis=0)
            acc_ref[st, g] = jnp.exp2(m_old - m_new)[0:1] * acc_ref[st, g] + _dot(vth, pt)
            m_ref[st, g] = m_new

    def finish(c, st):
        outs = []
        for g in range(NSA_KV_GROUPS):
            den = acc_ref[st, g, HEAD_DIM:HEAD_DIM + 1]
            outs.append(acc_ref[st, g, :HEAD_DIM] * jnp.where(den > 0.0, 1.0 / den, 0.0))
        for h in range(hpg):
            osum_ref[h] += gated(c, h, outs)

    def near_bias(d):
        return lambda g: bd_ref[g, d]

    def short_path(q_src, k_ref, v_ref, onehot, st):
        def far(kt, carry):
            chunk(q_src, k_ref, v_ref, pl.multiple_of(kt * TK, TK), TK, onehot, {}, st)
            return carry

        lax.fori_loop(0, jnp.maximum(qi - 1, 0), far, 0)

        @pl.when(qi >= 1)
        def _():
            chunk(q_src, k_ref, v_ref, pl.multiple_of((qi - 1) * TK, TK), TK, onehot,
                  {0: near_bias(1)}, st)

        chunk(q_src, k_ref, v_ref, pl.multiple_of(qi * TK, TK), TK, onehot, {0: near_bias(0)}, st)

    reset(SEL)
    reset(WIN)
    big = 4 * TK
    n_win = WINDOW // TK
    main_from = max(big // TK - 1, n_win)
    win_mask = jnp.where(lax.broadcasted_iota(jnp.int32, (TK, rows), 0)
                         > (lax.broadcasted_iota(jnp.int32, (TK, rows), 1) & (tq - 1)), 0.0, NEG)

    @pl.when(qi < main_from)
    def _():
        short_path(qsel_ref, ks_ref, vs_ref, True, SEL)
        short_path(qz_ref, kw_ref, vw_ref, False, WIN)

    @pl.when(qi >= main_from)
    def _():
        last = pl.multiple_of((qi - (big // TK - 1)) * TK, TK)
        n_far = (qi - (big // TK - 1) + big // TK - 1) >> int(math.log2(big // TK))
        near = {big // TK - 2: near_bias(1), big // TK - 1: near_bias(0)}

        def start_of(k):
            return pl.multiple_of(jnp.where(k < n_far, k * big, last), TK)

        def put(k, slot):
            for g, s in enumerate(scores(qsel_ref, ks_ref, start_of(k), big, True)):
                s_ref[slot, g] = s

        def take(k, slot, extras, limit):
            consume(lambda g, c: s_ref[slot, g, c * TK:(c + 1) * TK, :], vs_ref,
                    start_of(k), big, extras, SEL, limit)

        win_start = pl.multiple_of((qi - n_win) * TK, TK)
        for g, s in enumerate(scores(qz_ref, kw_ref, win_start, WINDOW + TK, False)):
            ws_ref[g] = s
        put(0, 0)
        consume(lambda g, c: ws_ref[g, c * TK:(c + 1) * TK, :], vw_ref, win_start, WINDOW + TK,
                {0: lambda g: win_mask, n_win - 1: near_bias(1), n_win: near_bias(0)}, WIN)
        n_pair = n_far >> 1

        def far_pair(i, carry):
            k = 2 * i
            put(k + 1, 1)
            take(k, 0, {}, last)
            put(k + 2, 0)
            take(k + 1, 1, {}, last)
            return carry

        lax.fori_loop(0, n_pair, far_pair, 0)

        @pl.when(n_far == 2 * n_pair)
        def _():
            take(n_far, 0, near, None)

        @pl.when(n_far != 2 * n_pair)
        def _():
            put(n_far, 1)
            take(n_far - 1, 0, {}, last)
            take(n_far, 1, near, None)

    finish(1, SEL)
    finish(2, WIN)

    for h in range(hpg):
        zp = z_ref[0, :, h * LANES:(h + 1) * LANES]
        o_ref[0, :, h * LANES:(h + 1) * LANES] = (osum_ref[h].T * _silu(zp)).astype(o_ref.dtype)


def _nsa_attention(pb3, pf3, vt, kvc, kvc_t, bias_c, bias_d, ov):
    b, t, _ = pb3.shape
    n_cmp = kvc.shape[2]
    n_blk = N_HEADS_NSA // 2
    full = lambda shape: pl.BlockSpec(shape, lambda i, q: (0,) * len(shape))
    kv_spec = lambda col: pl.BlockSpec((1, t, LANES), lambda i, q: (i, 0, col))
    vt_spec = lambda blk: pl.BlockSpec((1, 1, t // TK, LANES, TK), lambda i, q: (i, blk, 0, 0, 0))
    return pl.pallas_call(
        _nsa_kernel,
        grid=(b, t // TQ_NSA),
        in_specs=[pl.BlockSpec((1, TQ_NSA, n_blk * LANES), lambda i, q: (i, q, PB_QB // n_blk)),
                  pl.BlockSpec((1, 1, n_cmp, LANES), lambda i, q: (i, 0, 0, 0)),
                  pl.BlockSpec((1, 1, LANES, n_cmp), lambda i, q: (i, 1, 0, 0)),
                  kv_spec(PB_KS), vt_spec(VT_VS), kv_spec(PB_KW), vt_spec(VT_VW),
                  pl.BlockSpec((1, TQ_NSA, LANES), lambda i, q: (i, q, PF_GB)),
                  pl.BlockSpec((1, TQ_NSA, n_blk * LANES), lambda i, q: (i, q, PF_ZB // n_blk)),
                  pl.BlockSpec((N_HEADS_NSA, n_cmp, TQ_NSA), lambda i, q: (0, 0, q)),
                  full(bias_d.shape), full(ov.shape)],
        out_specs=pl.BlockSpec((1, TQ_NSA, n_blk * LANES), lambda i, q: (i, q, 0)),
        out_shape=jax.ShapeDtypeStruct((b, t, n_blk * LANES), _BF16),
        scratch_shapes=[pltpu.VMEM((2, NSA_KV_GROUPS, SUBLANES, NSA_HPG * TQ_NSA), _F32),
                        pltpu.VMEM((2, NSA_KV_GROUPS, HEAD_DIM + SUBLANES, NSA_HPG * TQ_NSA), _F32),
                        pltpu.VMEM((n_blk, TQ_NSA, LANES), _F32),
                        pltpu.VMEM((NSA_KV_GROUPS, NSA_HPG * TQ_NSA, LANES), _BF16),
                        pltpu.VMEM((NSA_KV_GROUPS, NSA_HPG * TQ_NSA, LANES), _BF16),
                        pltpu.VMEM((LANES, TQ_NSA), _F32),
                        pltpu.VMEM((NSA_KV_GROUPS, n_cmp, TQ_NSA), _F32),
                        pltpu.VMEM((2, NSA_KV_GROUPS, 4 * TK, NSA_HPG * TQ_NSA), _F32),
                        pltpu.VMEM((NSA_KV_GROUPS, WINDOW + TK, NSA_HPG * TQ_NSA), _F32)],
        compiler_params=_cparams(("arbitrary", "arbitrary")),
        name="native_sparse_attention",
    )(pb3, kvc, kvc_t, pb3, vt, pb3, vt, pf3, pf3, bias_c, bias_d, ov)


def _out_kernel(x_ref, oa_ref, ob_ref, oc_ref, w_ref, g_ref, o_ref, *, final_norm):
    na, nb = oa_ref.shape[1], ob_ref.shape[1]
    mix = (_dot(oa_ref[...], w_ref[0:na])
           + _dot(ob_ref[...], w_ref[na:na + nb])
           + _dot(oc_ref[...], w_ref[na + nb:]))
    x = x_ref[...] + mix
    if final_norm:
        x = x * lax.rsqrt(jnp.mean(x * x, axis=-1, keepdims=True) + RMS_EPS) * g_ref[...]
    o_ref[...] = x


def _out_proj(x2, oa, ob, oc, w, g, final_norm):
    n = x2.shape[0]
    row = lambda width: pl.BlockSpec((TM_PROJ, width), lambda i: (i, 0))
    return pl.pallas_call(
        functools.partial(_out_kernel, final_norm=final_norm),
        grid=(n // TM_PROJ,),
        in_specs=[row(D_MODEL), row(oa.shape[1]), row(ob.shape[1]), row(oc.shape[1]),
                  pl.BlockSpec((D_MODEL, D_MODEL), lambda i: (0, 0)),
                  pl.BlockSpec((1, D_MODEL), lambda i: (0, 0))],
        out_specs=row(D_MODEL),
        out_shape=jax.ShapeDtypeStruct((n, D_MODEL), _F32),
        compiler_params=_cparams(("arbitrary",)),
        name="out_proj_residual",
    )(x2, oa, ob, oc, w, g)


def _head_perm_cols(width_per_head, order):
    return np.concatenate([np.arange(h * width_per_head, (h + 1) * width_per_head) for h in order])


def _layout_w_in(w):
    widths = [256, 256, 256, 256, 512, 128, 128, 128, 128, 128, 128, 24, 512, 256, 256, 256, 4, 256]
    offs = np.concatenate([[0], np.cumsum(widths)])
    (qa, ka, va, za, qb, kc, vc, ks, vs, kw, vw, gb, zb, qc, kf, vf, fc, zc) = [
        w[:, offs[i]:offs[i + 1]] for i in range(len(widths))]
    scale = HEAD_DIM ** -0.5
    perm = _head_perm_cols(HEAD_DIM, NSA_HEAD_ORDER)
    pad = lambda a: jnp.pad(a, ((0, 0), (0, LANES - a.shape[1])))
    wb = jnp.concatenate([qb[:, perm] * (scale * LOG2E), qa * (-scale), ka, va, ks, kw,
                          qc * (scale * LOG2E), kf], axis=1)
    wf = jnp.concatenate([zb[:, perm], za, zc, pad(gb), pad(fc)], axis=1)
    wc = jnp.concatenate([kc, vc], axis=1)
    wvt = jnp.concatenate([vs, vw, vf], axis=1)
    return wb.astype(_BF16), wf.astype(_BF16), wc.astype(_BF16), wvt.astype(_BF16)


def _static_tables(t):
    tq = TQ_NSA
    n_cmp_pad = t // CMP_STRIDE
    j = np.arange(n_cmp_pad)
    dist_c = (np.arange(t)[None, :] - (j[:, None] * CMP_STRIDE + CMP_BLOCK - 1))
    dist_c[n_cmp_pad - 1, :] = -1
    i_, j_ = np.arange(tq)[:, None], np.arange(TK)[None, :]
    dist_d = np.concatenate([d * TK + i_ - j_ for d in range(3)], axis=0)
    n_slc = LANES // NSA_KV_GROUPS
    cmp_start = j * CMP_STRIDE
    cmp_end = cmp_start + CMP_BLOCK - 1
    slc_start = np.arange(n_slc) * SLC_BLOCK
    ov1 = np.clip(np.minimum(cmp_end[:, None], slc_start[None, :] + SLC_BLOCK - 1)
                  - np.maximum(cmp_start[:, None], slc_start[None, :]) + 1, 0, None) / CMP_BLOCK
    ov1[n_cmp_pad - 1:] = 0.0
    ov1[:, t // SLC_BLOCK:] = 0.0
    ov3 = np.concatenate([ov1.T, ov1.T, ov1.T], axis=1)
    return jnp.asarray(dist_c, jnp.int32), jnp.asarray(dist_d, jnp.int32), jnp.asarray(ov3, _BF16)


def kernel(x, norm_g, w_in, w_out, forget_b, cmp_w1, cmp_b1, cmp_w2, cmp_pe, rel_bias, final_g):
    b, t, d = x.shape
    depth = norm_g.shape[0]
    assert d == D_MODEL and t % TM_PROJ == 0 and t % (CMP_STRIDE * LANES) == 0
    assert t // SLC_BLOCK <= LANES // NSA_KV_GROUPS and TQ_NSA == TK
    n_cmp_pad = t // CMP_STRIDE

    dist_c, dist_d, ov3 = _static_tables(t)
    bias_c = _bias_table(rel_bias * LOG2E, dist_c, 32)
    bias_d = _bias_table(rel_bias * LOG2E, dist_d, 32).reshape(N_HEADS_NSA, 3, TQ_NSA, TK)
    bias_d = bias_d[:, :2] - bias_d[:, 2:3]
    bias_d = bias_d.reshape(NSA_KV_GROUPS, NSA_HPG, 2, TQ_NSA, TK).transpose(0, 2, 4, 1, 3).reshape(
        NSA_KV_GROUPS, 2, TK, NSA_HPG * TQ_NSA)

    perm_rows = _head_perm_cols(HEAD_DIM, NSA_HEAD_ORDER)
    x2 = x.reshape(b * t, d)
    out = None
    for l in range(depth):
        wb, wf, wc, wvt = _layout_w_in(w_in[l])
        pb, pf, pc, vt = _proj(x2, norm_g[l].reshape(1, d), wb, wf, wc, wvt, t)
        pb3 = pb.reshape(b, t, PB_BLOCKS * LANES)
        pf3 = pf.reshape(b, t, PF_BLOCKS * LANES)

        fb_row = jnp.pad(forget_b[l], (0, LANES - N_HEADS_FOX)).reshape(1, LANES)
        augq, augk = _fgate(pf3, fb_row)

        halves = pc.reshape(b, 2, NSA_KV_GROUPS, n_cmp_pad, CMP_STRIDE * HEAD_DIM)
        w2 = cmp_w2[l]
        zeros = jnp.zeros_like(w2)
        w2p = jnp.stack([jnp.concatenate([w2, zeros], axis=-1),
                         jnp.concatenate([zeros, w2], axis=-1)], axis=1).astype(_BF16)
        kvc, kvc_t = _compress(halves, cmp_w1[l].astype(_BF16),
                               cmp_pe[l].reshape(2, 1, CMP_BLOCK * HEAD_DIM).astype(_BF16),
                               cmp_b1[l].reshape(2, 1, CMP_HIDDEN), w2p)

        o_a = _sb_attention(pb3, pf3)
        o_b = _nsa_attention(pb3, pf3, vt, kvc, kvc_t, bias_c, bias_d, ov3)
        o_c = _fox_attention(pb3, pf3, vt, augq, augk)

        wo = w_out[l]
        wo = jnp.concatenate([wo[:N_HEADS_SB * HEAD_DIM],
                              wo[N_HEADS_SB * HEAD_DIM:][:N_HEADS_NSA * HEAD_DIM][perm_rows],
                              wo[(N_HEADS_SB + N_HEADS_NSA) * HEAD_DIM:]], axis=0).astype(_BF16)
        last = l == depth - 1
        x2 = _out_proj(x2, o_a.reshape(b * t, -1), o_b.reshape(b * t, -1), o_c.reshape(b * t, -1),
                       wo, final_g.reshape(1, d), last)
    return x2.reshape(b, t, d)
```

```python
import functools
import math

import jax
import jax.numpy as jnp
import numpy as np
from jax import lax
from jax.experimental import pallas as pl
from jax.experimental.pallas import tpu as pltpu

D_MODEL = 1024
HEAD_DIM = 64
N_HEADS_SB = 4
N_HEADS_FOX = 4
N_HEADS_NSA = 8
NSA_KV_GROUPS = 2
NSA_HPG = N_HEADS_NSA // NSA_KV_GROUPS
CMP_BLOCK = 32
CMP_STRIDE = 16
CMP_HIDDEN = 256
SLC_BLOCK = 64
SLC_TOP = 16
WINDOW = 512
REL_BUCKETS = 32
REL_MAX_DIST = 128
FORCE_SCORE = 1e4
RMS_EPS = 1e-6
NEG = -1e30
LOG2E = math.log2(math.e)

LANES = 128
SUBLANES = 8
VMEM_LIMIT = 56 * 1024 * 1024

TM_PROJ = 512
TQ_SB = 512
TQ_FOX = 512
FOX_PARTS = 2
TQ_NSA = 128
TK = 128

PB_QB, PB_QA, PB_KA, PB_VA = 0, 4, 6, 8
PB_KS, PB_KW = 10, 11
PB_QC, PB_KF = 12, 14
PB_BLOCKS = 16
N_CMP_SLABS = 2 * NSA_KV_GROUPS
VT_VS, VT_VW, VT_VF = 0, 1, 2
N_VT = 4
PF_ZB, PF_ZA, PF_ZC, PF_GB, PF_FC = 0, 4, 6, 8, 9
PF_BLOCKS = 10

NSA_HEAD_ORDER = [0, 4, 1, 5, 2, 6, 3, 7]

_F32 = jnp.float32
_BF16 = jnp.bfloat16


def _cparams(sem):
    return pltpu.CompilerParams(dimension_semantics=sem, vmem_limit_bytes=VMEM_LIMIT)


def _dot(a, b):
    return jnp.dot(a, b, preferred_element_type=_F32)


def _dot_nt(a, b):
    return lax.dot_general(a, b, (((1,), (1,)), ((), ())), preferred_element_type=_F32)


def _split2(x):
    hi = x.astype(_BF16)
    lo = (x - hi.astype(_F32)).astype(_BF16)
    return jnp.concatenate([hi, lo], axis=1)


def _split3(x):
    h1 = x.astype(_BF16)
    r1 = x - h1.astype(_F32)
    h2 = r1.astype(_BF16)
    h3 = (r1 - h2.astype(_F32)).astype(_BF16)
    return jnp.concatenate([h1, h2, h3], axis=1)


def _sigmoid(x):
    return 1.0 / (1.0 + jnp.exp(-x))


def _silu(x):
    return x * _sigmoid(x)


def _rel_bucket_np(n):
    n = np.maximum(n, 0)
    max_exact = REL_BUCKETS // 2
    nf = np.maximum(n, 1).astype(np.float64)
    large = max_exact + (np.log(nf / max_exact) / math.log(REL_MAX_DIST / max_exact)
                         * (REL_BUCKETS - max_exact)).astype(np.int64)
    large = np.minimum(large, REL_BUCKETS - 1)
    return np.where(n < max_exact, n, large)


def _bucket_thresholds():
    n = np.arange(0, 4 * REL_MAX_DIST)
    bk = _rel_bucket_np(n)
    assert np.all(np.diff(bk) >= 0) and bk[-1] == REL_BUCKETS - 1
    return [int(np.argmax(bk >= b)) for b in range(REL_BUCKETS)]


_BUCKET_THR = _bucket_thresholds()


def _bias_kernel(tab_ref, dist_ref, o_ref):
    n = dist_ref[...]
    acc = [jnp.full(n.shape, tab_ref[0, h], _F32) for h in range(N_HEADS_NSA)]
    for b in range(1, REL_BUCKETS):
        ge = n >= _BUCKET_THR[b]
        for h in range(N_HEADS_NSA):
            acc[h] = jnp.where(ge, tab_ref[b, h], acc[h])
    valid = n >= 0
    for h in range(N_HEADS_NSA):
        o_ref[h] = jnp.where(valid, acc[h], NEG)


def _bias_table(rel_bias, dist, rows):
    n_rows, n_cols = dist.shape
    return pl.pallas_call(
        _bias_kernel,
        grid=(n_rows // rows,),
        in_specs=[pl.BlockSpec(memory_space=pltpu.SMEM),
                  pl.BlockSpec((rows, n_cols), lambda i: (i, 0))],
        out_specs=pl.BlockSpec((N_HEADS_NSA, rows, n_cols), lambda i: (0, i, 0)),
        out_shape=jax.ShapeDtypeStruct((N_HEADS_NSA, n_rows, n_cols), _F32),
        compiler_params=_cparams(("arbitrary",)),
        name="rel_bias_table",
    )(rel_bias, dist)


def _proj_kernel(x_ref, g_ref, wb_ref, wf_ref, wc_ref, wvt_ref, pb_ref, pf_ref, pc_ref, vt_ref):
    x = x_ref[...]
    y = x * lax.rsqrt(jnp.mean(x * x, axis=-1, keepdims=True) + RMS_EPS)
    h = (y * g_ref[...]).astype(_BF16)
    v_all = _dot(h, wvt_ref[...])
    for j in range(N_VT):
        v_t = v_all[:, j * LANES:(j + 1) * LANES].T.astype(_BF16)
        for c in range(TM_PROJ // TK):
            vt_ref[0, j, c] = v_t[:, c * TK:(c + 1) * TK]
    kv_cmp = _dot(h, wc_ref[...]).astype(_BF16)
    for s in range(N_CMP_SLABS):
        pc_ref[0, s] = kv_cmp[:, s * HEAD_DIM:(s + 1) * HEAD_DIM]
    chunk = 4 * LANES
    for c in range(0, PB_BLOCKS * LANES, chunk):
        w = min(chunk, PB_BLOCKS * LANES - c)
        pb_ref[:, c:c + w] = _dot(h, wb_ref[:, c:c + w]).astype(_BF16)
    for c in range(0, PF_BLOCKS * LANES, chunk):
        w = min(chunk, PF_BLOCKS * LANES - c)
        pf_ref[:, c:c + w] = _dot(h, wf_ref[:, c:c + w])


def _proj(x2, g, wb, wf, wc, wvt, t):
    n = x2.shape[0]
    per_seq = t // TM_PROJ
    return pl.pallas_call(
        _proj_kernel,
        grid=(n // TM_PROJ,),
        in_specs=[pl.BlockSpec((TM_PROJ, D_MODEL), lambda i: (i, 0)),
                  pl.BlockSpec((1, D_MODEL), lambda i: (0, 0)),
                  pl.BlockSpec((D_MODEL, PB_BLOCKS * LANES), lambda i: (0, 0)),
                  pl.BlockSpec((D_MODEL, PF_BLOCKS * LANES), lambda i: (0, 0)),
                  pl.BlockSpec((D_MODEL, N_CMP_SLABS * HEAD_DIM), lambda i: (0, 0)),
                  pl.BlockSpec((D_MODEL, N_VT * LANES), lambda i: (0, 0))],
        out_specs=[pl.BlockSpec((TM_PROJ, PB_BLOCKS * LANES), lambda i: (i, 0)),
                   pl.BlockSpec((TM_PROJ, PF_BLOCKS * LANES), lambda i: (i, 0)),
                   pl.BlockSpec((1, N_CMP_SLABS, TM_PROJ, HEAD_DIM),
                                lambda i: (i // per_seq, 0, i % per_seq, 0)),
                   pl.BlockSpec((1, N_VT, TM_PROJ // TK, LANES, TK),
                                lambda i: (i // per_seq, 0, i % per_seq, 0, 0))],
        out_shape=[jax.ShapeDtypeStruct((n, PB_BLOCKS * LANES), _BF16),
                   jax.ShapeDtypeStruct((n, PF_BLOCKS * LANES), _F32),
                   jax.ShapeDtypeStruct((n // t, N_CMP_SLABS, t, HEAD_DIM), _BF16),
                   jax.ShapeDtypeStruct((n // t, N_VT, t // TK, LANES, TK), _BF16)],
        compiler_params=_cparams(("arbitrary",)),
        name="rmsnorm_in_proj",
    )(x2, g, wb, wf, wc, wvt)


N_SPLIT = 3


def _fgate_tables():
    n_pairs = N_HEADS_FOX // 2
    pq = np.zeros((N_SPLIT * LANES, n_pairs * LANES), np.float32)
    pk = np.zeros_like(pq)
    ones_q = np.zeros((1, n_pairs * LANES), np.float32)
    ones_k = np.zeros_like(ones_q)
    for head in range(N_HEADS_FOX):
        pair, slot = divmod(head, 2)
        base = pair * LANES + (HEAD_DIM if slot == 0 else 0)
        for j in range(N_SPLIT):
            pq[j * LANES + head, base + j] = 1.0
            pk[j * LANES + head, base + N_SPLIT + j] = 1.0
        ones_q[0, base + N_SPLIT:base + 2 * N_SPLIT] = 1.0
        ones_k[0, base:base + N_SPLIT] = 1.0
    return (jnp.asarray(pq, _BF16), jnp.asarray(pk, _BF16),
            jnp.asarray(ones_q), jnp.asarray(ones_k))


def _fgate_kernel(fc_ref, fb_ref, pq_ref, pk_ref, oq_ref, ok_ref, augq_ref, augk_ref):
    t = fc_ref.shape[1]
    z = fc_ref[0] + fb_ref[...]
    logf = jnp.minimum(z, 0.0) - jnp.log1p(jnp.exp(-jnp.abs(z)))
    row = lax.broadcasted_iota(jnp.int32, (t, LANES), 0)
    c = logf
    shift = 1
    while shift < t:
        c = c + jnp.where(row >= shift, pltpu.roll(c, shift, axis=0), 0.0)
        shift *= 2
    c3 = _split3(c * LOG2E)
    aq = _dot(c3, pq_ref[...]) + oq_ref[...]
    ak = ok_ref[...] - _dot(c3, pk_ref[...])
    for p in range(N_HEADS_FOX // 2):
        augq_ref[0, p] = aq[:, p * LANES:(p + 1) * LANES].astype(_BF16)
        augk_ref[0, p] = ak[:, p * LANES:(p + 1) * LANES].astype(_BF16)


def _fgate(pf3, fb_row):
    b, t, _ = pf3.shape
    n_pairs = N_HEADS_FOX // 2
    tables = _fgate_tables()
    full = lambda a: pl.BlockSpec(a.shape, lambda i: (0,) * a.ndim)
    aug = lambda: pl.BlockSpec((1, n_pairs, t, LANES), lambda i: (i, 0, 0, 0))
    return pl.pallas_call(
        _fgate_kernel,
        grid=(b,),
        in_specs=[pl.BlockSpec((1, t, LANES), lambda i: (i, 0, PF_FC)),
                  pl.BlockSpec((1, LANES), lambda i: (0, 0))] + [full(a) for a in tables],
        out_specs=[aug(), aug()],
        out_shape=[jax.ShapeDtypeStruct((b, n_pairs, t, LANES), _BF16),
                   jax.ShapeDtypeStruct((b, n_pairs, t, LANES), _BF16)],
        compiler_params=_cparams(("arbitrary",)),
        name="forget_gate_cumsum",
    )(pf3, fb_row, *tables)


def _compress_kernel(x_ref, w1_ref, pe_ref, b1_ref, w2_ref, o_ref, ot_ref):
    nc = x_ref.shape[3]
    half = CMP_STRIDE * HEAD_DIM
    w1 = w1_ref[0]
    c1 = _dot(jnp.broadcast_to(pe_ref[0], (SUBLANES, 2 * half)), w1)[0:1] + b1_ref[0]
    out = jnp.zeros((nc, LANES), _F32)
    for g in range(NSA_KV_GROUPS):
        xg = x_ref[0, 0, g]
        a = _dot(xg, w1[:half])
        bb = _dot(xg, w1[half:])
        h = a + pltpu.roll(bb, nc - 1, axis=0) + c1
        out = out + _dot(_silu(h).astype(_BF16), w2_ref[0, g])
    o_ref[0, 0] = out.astype(_BF16)
    ot_ref[0, 0] = out.T.astype(_BF16)


def _compress(halves, w1, pe, b1, w2p):
    b, _, g, nc, width = halves.shape
    return pl.pallas_call(
        _compress_kernel,
        grid=(b, 2),
        in_specs=[pl.BlockSpec((1, 1, g, nc, width), lambda i, k: (i, k, 0, 0, 0)),
                  pl.BlockSpec((1, 2 * width, CMP_HIDDEN), lambda i, k: (k, 0, 0)),
                  pl.BlockSpec((1, 1, 2 * width), lambda i, k: (k, 0, 0)),
                  pl.BlockSpec((1, 1, CMP_HIDDEN), lambda i, k: (k, 0, 0)),
                  pl.BlockSpec((1, g, CMP_HIDDEN, LANES), lambda i, k: (k, 0, 0, 0))],
        out_specs=[pl.BlockSpec((1, 1, nc, LANES), lambda i, k: (i, k, 0, 0)),
                   pl.BlockSpec((1, 1, LANES, nc), lambda i, k: (i, k, 0, 0))],
        out_shape=[jax.ShapeDtypeStruct((b, 2, nc, LANES), _BF16),
                   jax.ShapeDtypeStruct((b, 2, LANES, nc), _BF16)],
        compiler_params=_cparams(("arbitrary", "arbitrary")),
        name="nsa_compress",
    )(halves, w1, pe, b1, w2p)


def _sb_kernel(q_ref, k_ref, v_ref, z_ref, o_ref, qh_ref, c_ref, acc_ref, s_ref):
    qi = pl.program_id(2)
    tq = q_ref.shape[1]
    lane = lax.broadcasted_iota(jnp.int32, (tq, LANES), 1)
    q2 = q_ref[0]
    qh = [jnp.where(lane < HEAD_DIM, q2, jnp.zeros_like(q2)),
          jnp.where(lane >= HEAD_DIM, q2, jnp.zeros_like(q2))]
    r_i = lax.broadcasted_iota(jnp.int32, (2 * TK, 2 * TK), 0)
    c_i = lax.broadcasted_iota(jnp.int32, (2 * TK, 2 * TK), 1)
    uu = jnp.where(r_i >= c_i, 1.0, 0.0).astype(_BF16)
    c_ref[...] = jnp.zeros_like(c_ref)
    acc_ref[...] = jnp.zeros_like(acc_ref)
    for h in range(2):
        qh_ref[h] = qh[h]
    row = lax.broadcasted_iota(jnp.int32, (tq, LANES), 0)
    n_blocks = tq // TK

    def start_of(k):
        return pl.multiple_of((qi - k) * tq, tq)

    def put(k, slot):
        k2 = k_ref[0, pl.ds(start_of(k), tq), :]
        for h in range(2):
            s_ref[slot, h] = _dot_nt(qh_ref[h], k2)

    def take(k, slot, diagonal):
        v2 = v_ref[0, pl.ds(start_of(k), tq), :]
        for h in range(2):
            carry = c_ref[h]
            w_blocks = [None] * n_blocks
            for c in reversed(range(0, n_blocks, 2)):
                nz, l1m, mask = [], [], []
                for cc in (c, c + 1):
                    nzc = s_ref[slot, h, :, cc * TK:(cc + 1) * TK]
                    neg_abs = lax.bitcast_convert_type(
                        lax.bitcast_convert_type(nzc, jnp.uint32) | jnp.uint32(0x80000000), _F32)
                    lc = jnp.minimum(nzc, 0.0) - jnp.log(1.0 + jnp.exp(neg_abs))
                    if diagonal:
                        mask.append(lane + cc * TK < row)
                        lc = jnp.where(mask[-1], lc, 0.0)
                    nz.append(nzc)
                    l1m.append(lc.astype(_BF16))
                rc = _dot(jnp.concatenate(l1m, axis=1), uu)
                for i, cc in enumerate((c, c + 1)):
                    w = jnp.exp((rc[:, i * TK:(i + 1) * TK] + carry) - nz[i])
                    if diagonal:
                        w = jnp.where(mask[i], w, 0.0)
                    w_blocks[cc] = w.astype(_BF16)
                carry = carry + jnp.broadcast_to(rc[:, 0:1], carry.shape)
            acc_ref[h] += _dot(jnp.concatenate(w_blocks, axis=1), v2)
            c_ref[h] = carry

    put(0, 0)

    @pl.when(qi == 0)
    def _():
        take(0, 0, True)

    @pl.when(qi >= 1)
    def _():
        put(1, 1)
        take(0, 0, True)
        n_loop = (qi - 1) >> 1

        def far_pair(p, carry):
            put(2 * p + 2, 0)
            take(2 * p + 1, 1, False)
            put(2 * p + 3, 1)
            take(2 * p + 2, 0, False)
            return carry

        lax.fori_loop(0, n_loop, far_pair, 0)
        nxt = 2 * n_loop + 1

        @pl.when(nxt == qi)
        def _():
            take(nxt, 1, False)

        @pl.when(nxt != qi)
        def _():
            put(nxt + 1, 0)
            take(nxt, 1, False)
            take(nxt + 1, 0, False)

    o = jnp.where(lane < HEAD_DIM, acc_ref[0], acc_ref[1])
    o_ref[0] = (o * _silu(z_ref[0])).astype(o_ref.dtype)


def _sb_attention(pb3, pf3):
    b, t, _ = pb3.shape
    n_pairs = N_HEADS_SB // 2
    return pl.pallas_call(
        _sb_kernel,
        grid=(b, n_pairs, t // TQ_SB),
        in_specs=[pl.BlockSpec((1, TQ_SB, LANES), lambda i, p, q: (i, q, PB_QA + p)),
                  pl.BlockSpec((1, t, LANES), lambda i, p, q: (i, 0, PB_KA + p)),
                  pl.BlockSpec((1, t, LANES), lambda i, p, q: (i, 0, PB_VA + p)),
                  pl.BlockSpec((1, TQ_SB, LANES), lambda i, p, q: (i, q, PF_ZA + p))],
        out_specs=pl.BlockSpec((1, TQ_SB, LANES), lambda i, p, q: (i, q, p)),
        out_shape=jax.ShapeDtypeStruct((b, t, n_pairs * LANES), _BF16),
        scratch_shapes=[pltpu.VMEM((2, TQ_SB, LANES), _BF16),
                        pltpu.VMEM((2, TQ_SB, LANES), _F32),
                        pltpu.VMEM((2, TQ_SB, LANES), _F32),
                        pltpu.VMEM((2, 2, TQ_SB, TQ_SB), _F32)],
        compiler_params=_cparams(("arbitrary", "arbitrary", "arbitrary")),
        name="stick_breaking_attention",
    )(pb3, pb3, pb3, pf3)


def _fox_kernel(q_ref, k_ref, vt_ref, augq_ref, augk_ref, z_ref, o_ref,
                qh_ref, m_ref, acc_ref, s_ref):
    qi = pl.program_id(2)
    tq = q_ref.shape[1]
    tk = tq
    lane = lax.broadcasted_iota(jnp.int32, (tq, LANES), 1)
    keep = [lane < HEAD_DIM, lane >= HEAD_DIM]
    q2 = q_ref[0]
    aq = augq_ref[0, 0]
    for h in range(2):
        qh_ref[h] = jnp.where(keep[h], q2, aq)
    m_ref[...] = jnp.full_like(m_ref, NEG)
    acc_ref[...] = jnp.zeros_like(acc_ref)
    key_i = lax.broadcasted_iota(jnp.int32, (TK, tq), 0)
    qry_i = lax.broadcasted_iota(jnp.int32, (TK, tq), 1)

    def scores(c, slot):
        start = c * tk if isinstance(c, int) else pl.multiple_of(c * tk, tk)
        k2 = k_ref[0, pl.ds(start, tk), :]
        ak = augk_ref[0, 0, pl.ds(start, tk), :]
        for h in range(2):
            s_ref[slot, h] = _dot_nt(jnp.where(keep[h], k2, ak), qh_ref[h])

    ones_rows = jnp.ones((SUBLANES, tk), _BF16)

    def consume(c, slot, diagonal):
        per = tk // TK
        vt = jnp.concatenate([vt_ref[0, 0, c * per + j] for j in range(per)], axis=1)
        for h in range(2):
            vth = jnp.concatenate([vt[h * HEAD_DIM:(h + 1) * HEAD_DIM], ones_rows], axis=0)
            for part in range(FOX_PARTS):
                js = range(part * per // FOX_PARTS, (part + 1) * per // FOX_PARTS)
                blocks = [s_ref[slot, h, j * TK:(j + 1) * TK, :] for j in js]
                if diagonal:
                    blocks = [jnp.where(key_i + j * TK <= qry_i, blk, NEG) for j, blk in zip(js, blocks)]
                mx = blocks[0]
                for blk in blocks[1:]:
                    mx = jnp.maximum(mx, blk)
                m_old = m_ref[h]
                m_new = jnp.maximum(m_old, jnp.max(mx, axis=0, keepdims=True))
                m_row = m_new[0:1]
                pt = jnp.concatenate([jnp.exp2(blk - m_row).astype(_BF16) for blk in blocks], axis=0)
                acc_ref[h] = (jnp.exp2(m_old - m_new)[0:1] * acc_ref[h]
                              + _dot(vth[:, js[0] * TK:(js[-1] + 1) * TK], pt))
                m_ref[h] = m_new

    scores(0, 0)
    n_pairs = qi >> 1

    def far_pair(i, carry):
        c = 2 * i
        scores(c + 1, 1)
        consume(c, 0, False)
        scores(c + 2, 0)
        consume(c + 1, 1, False)
        return carry

    lax.fori_loop(0, n_pairs, far_pair, 0)

    @pl.when(qi == 2 * n_pairs)
    def _():
        consume(qi, 0, True)

    @pl.when(qi != 2 * n_pairs)
    def _():
        scores(qi, 1)
        consume(qi - 1, 0, False)
        consume(qi, 1, True)
    o_t = jnp.concatenate([acc_ref[h, :HEAD_DIM] / acc_ref[h, HEAD_DIM:HEAD_DIM + 1]
                           for h in range(2)], axis=0)
    o_ref[0] = (o_t.T * _silu(z_ref[0])).astype(o_ref.dtype)


def _fox_attention(pb3, pf3, vt, augq, augk):
    b, t, _ = pb3.shape
    n_pairs = N_HEADS_FOX // 2
    n_chunks = t // TQ_FOX
    return pl.pallas_call(
        _fox_kernel,
        grid=(b, n_pairs, t // TQ_FOX),
        in_specs=[pl.BlockSpec((1, TQ_FOX, LANES), lambda i, p, q: (i, q, PB_QC + p)),
                  pl.BlockSpec((1, t, LANES), lambda i, p, q: (i, 0, PB_KF + p)),
                  pl.BlockSpec((1, 1, t // TK, LANES, TK), lambda i, p, q: (i, VT_VF + p, 0, 0, 0)),
                  pl.BlockSpec((1, 1, TQ_FOX, LANES), lambda i, p, q: (i, p, q, 0)),
                  pl.BlockSpec((1, 1, t, LANES), lambda i, p, q: (i, p, 0, 0)),
                  pl.BlockSpec((1, TQ_FOX, LANES), lambda i, p, q: (i, q, PF_ZC + p))],
        out_specs=pl.BlockSpec((1, TQ_FOX, LANES), lambda i, p, q: (i, q, p)),
        out_shape=jax.ShapeDtypeStruct((b, t, n_pairs * LANES), _BF16),
        scratch_shapes=[pltpu.VMEM((2, TQ_FOX, LANES), _BF16),
                        pltpu.VMEM((2, SUBLANES, TQ_FOX), _F32),
                        pltpu.VMEM((2, HEAD_DIM + SUBLANES, TQ_FOX), _F32),
                        pltpu.VMEM((2, 2, TQ_FOX, TQ_FOX), _F32)],
        compiler_params=_cparams(("arbitrary", "arbitrary", "arbitrary")),
        name="forgetting_attention",
    )(pb3, pb3, vt, augq, augk, pf3)


def _nsa_kernel(q_ref, kc_ref, vct_ref, ks_ref, vs_ref, kw_ref, vw_ref, gl_ref, z_ref,
                bc_ref, bd_ref, ov_ref, o_ref,
                m_ref, acc_ref, osum_ref, qz_ref, qsel_ref, gates_ref, psum_ref, s_ref, ws_ref):
    qi = pl.program_id(1)
    tq = q_ref.shape[1]
    hpg = NSA_HPG
    rows = hpg * tq
    n_cmp = kc_ref.shape[2]
    lane = lax.broadcasted_iota(jnp.int32, (tq, LANES), 1)
    qrow = qi * tq + lax.broadcasted_iota(jnp.int32, (tq, LANES), 0)
    lane_r = lax.broadcasted_iota(jnp.int32, (rows, LANES), 1)
    irow_r = lax.broadcasted_iota(jnp.int32, (rows, LANES), 0) & (tq - 1)
    half = [lane_r < HEAD_DIM, lane_r >= HEAD_DIM]

    gates_ref[...] = _sigmoid(gl_ref[0]).T

    def gated(c, h, per_group):
        parts = []
        for g, o in enumerate(per_group):
            r = 3 * (g * hpg + h) + c
            parts.append(gates_ref[r:r + 1, :] * o[:, h * tq:(h + 1) * tq])
        return jnp.concatenate(parts, axis=0)

    q4 = jnp.concatenate([q_ref[0, :, h * LANES:(h + 1) * LANES] for h in range(hpg)], axis=0)
    for g in range(NSA_KV_GROUPS):
        qz_ref[g] = jnp.where(half[g], q4, jnp.zeros_like(q4))

    def scores(q_src, k_ref, start, tk, onehot):
        k2 = k_ref[0, pl.ds(start, tk), :]
        lane_k = lax.broadcasted_iota(jnp.int32, (tk, LANES), 1)
        key_blk = (start + lax.broadcasted_iota(jnp.int32, (tk, LANES), 0)) >> int(math.log2(SLC_BLOCK))
        oh = jnp.where((lane_k & (SLC_BLOCK - 1)) == key_blk, 1.0, 0.0).astype(_BF16)
        out = []
        for g in range(NSA_KV_GROUPS):
            keep = (lane_k < HEAD_DIM) if g == 0 else (lane_k >= HEAD_DIM)
            out.append(_dot_nt(jnp.where(keep, k2, oh) if onehot else k2, q_src[g]))
        return out


    def heads_to_blocks(x0, x1):
        return [jnp.where(lane < HEAD_DIM, x0[h * tq:(h + 1) * tq], x1[h * tq:(h + 1) * tq])
                for h in range(hpg)]

    kc = kc_ref[0, 0]
    vct = vct_ref[0, 0]
    o_cmp_t = []
    raw = [_dot_nt(kc, qz_ref[g]) for g in range(NSA_KV_GROUPS)]
    for g in range(NSA_KV_GROUPS):
        bias = jnp.concatenate([bc_ref[g * hpg + h] for h in range(hpg)], axis=1)
        sc = raw[g] + bias
        mx = jnp.max(sc, axis=0, keepdims=True)
        e = jnp.exp2(sc - mx)
        den = jnp.sum(e, axis=0, keepdims=True)
        pc = e * jnp.where(mx > 0.5 * NEG, 1.0 / den, 0.0)
        psum = pc[:, 0:tq]
        for h in range(1, hpg):
            psum = psum + pc[:, h * tq:(h + 1) * tq]
        psum_ref[g] = psum
        o_cmp_t.append(_dot(vct[g * HEAD_DIM:(g + 1) * HEAD_DIM], pc.astype(_BF16)))
    for h in range(hpg):
        osum_ref[h] = gated(0, h, o_cmp_t)

    rank_from = SLC_TOP * SLC_BLOCK // tq

    @pl.when(qi < rank_from)
    def _():
        for g in range(NSA_KV_GROUPS):
            qsel_ref[g] = qz_ref[g]

    @pl.when(qi >= rank_from)
    def _():
        n_sel = LANES // NSA_KV_GROUPS
        n_grp = n_sel // SUBLANES
        blk = lax.broadcasted_iota(jnp.int32, (n_sel, tq), 0)
        cur = (qi * tq + lax.broadcasted_iota(jnp.int32, (n_sel, tq), 1)) >> int(math.log2(SLC_BLOCK))
        forced = (blk == 0) | (blk == cur) | (blk == cur - 1)
        sub = lax.broadcasted_iota(jnp.int32, (SUBLANES, tq), 0)
        neg_t = []
        for g in range(NSA_KV_GROUPS):
            p = psum_ref[g]
            p1 = p.astype(_BF16)
            r1 = p - p1.astype(_F32)
            p2 = r1.astype(_BF16)
            p3 = (r1 - p2.astype(_F32)).astype(_BF16)
            a = _dot(ov_ref[...], jnp.concatenate([p1, p2, p3], axis=0))
            a = jnp.where(forced, FORCE_SCORE, a)
            a = jnp.where(blk > cur, -FORCE_SCORE, a)
            a_grp = [a[r * SUBLANES:(r + 1) * SUBLANES] for r in range(n_grp)]
            cnt = [jnp.zeros((SUBLANES, tq), _F32) for _ in range(n_grp)]
            for j in range(n_sel):
                rj = jnp.broadcast_to(a[j:j + 1], (SUBLANES, tq))
                jr, jo = divmod(j, SUBLANES)
                for r in range(n_grp):
                    if r > jr:
                        one = jnp.where(rj >= a_grp[r], 1.0, 0.0)
                    elif r < jr:
                        one = jnp.where(rj > a_grp[r], 1.0, 0.0)
                    else:
                        tie = jnp.where(sub > jo, jnp.where(rj == a_grp[r], 1.0, 0.0), 0.0)
                        one = jnp.where(rj > a_grp[r], 1.0, tie)
                    cnt[r] = cnt[r] + one
            neg_t.append(jnp.where(jnp.concatenate(cnt, axis=0) < float(SLC_TOP), 0.0, NEG))
        selneg = jnp.concatenate(neg_t[::-1], axis=0).T.astype(_BF16)
        selneg4 = jnp.concatenate([selneg] * hpg, axis=0)
        for g in range(NSA_KV_GROUPS):
            qsel_ref[g] = jnp.where(half[g], q4, selneg4)

    SEL, WIN = 0, 1

    def reset(st):
        m_ref[st] = jnp.full(m_ref.shape[1:], 2.0 * NEG, _F32)
        acc_ref[st] = jnp.zeros(acc_ref.shape[1:], _F32)

    def chunk(q_src, k_ref, vt_ref, start, tk, onehot, extras, st):
        s = scores(q_src, k_ref, start, tk, onehot)
        consume(lambda g, c: s[g][c * TK:(c + 1) * TK], vt_ref, start, tk, extras, st)

    def consume(block_of, vt_ref, start, tk, extras, st, limit=None):
        first = start // TK if isinstance(start, int) else start >> int(math.log2(TK))
        vt = jnp.concatenate([vt_ref[0, 0, first + j] for j in range(tk // TK)], axis=1)
        ones_rows = jnp.ones((SUBLANES, tk), _BF16)
        key_pos = start + lax.broadcasted_iota(jnp.int32, (HEAD_DIM + SUBLANES, tk), 1)
        for g in range(NSA_KV_GROUPS):
            vth = jnp.concatenate([vt[g * HEAD_DIM:(g + 1) * HEAD_DIM], ones_rows], axis=0)
            if limit is not None:
                vth = jnp.where(key_pos < limit, vth, jnp.zeros_like(vth))
            blocks = [block_of(g, c) for c in range(tk // TK)]
            for c, extra in extras.items():
                blocks[c] = blocks[c] + extra(g)
            mx = blocks[0]
            for blk_s in blocks[1:]:
                mx = jnp.maximum(mx, blk_s)
            m_old = m_ref[st, g]
            m_new = jnp.maximum(m_old, jnp.max(mx, axis=0, keepdims=True))
            m_row = m_new[0:1]
            pt = jnp.concatenate([jnp.exp2(blk_s - m_row).astype(_BF16) for blk_s in blocks], axis=0)
            acc_ref[st, g] = jnp.exp2(m_old - m_new)[0:1] * acc_ref[st, g] + _dot(vth, pt)
            m_ref[st, g] = m_new

    def finish(c, st):
        outs = []
        for g in range(NSA_KV_GROUPS):
            den = acc_ref[st, g, HEAD_DIM:HEAD_DIM + 1]
            outs.append(acc_ref[st, g, :HEAD_DIM] * jnp.where(den > 0.0, 1.0 / den, 0.0))
        for h in range(hpg):
            osum_ref[h] += gated(c, h, outs)

    def near_bias(d):
        return lambda g: bd_ref[g, d]

    def short_path(q_src, k_ref, v_ref, onehot, st):
        def far(kt, carry):
            chunk(q_src, k_ref, v_ref, pl.multiple_of(kt * TK, TK), TK, onehot, {}, st)
            return carry

        lax.fori_loop(0, jnp.maximum(qi - 1, 0), far, 0)

        @pl.when(qi >= 1)
        def _():
            chunk(q_src, k_ref, v_ref, pl.multiple_of((qi - 1) * TK, TK), TK, onehot,
                  {0: near_bias(1)}, st)

        chunk(q_src, k_ref, v_ref, pl.multiple_of(qi * TK, TK), TK, onehot, {0: near_bias(0)}, st)

    reset(SEL)
    reset(WIN)
    big = 4 * TK
    n_win = WINDOW // TK
    main_from = max(big // TK - 1, n_win)
    win_mask = jnp.where(lax.broadcasted_iota(jnp.int32, (TK, rows), 0)
                         > (lax.broadcasted_iota(jnp.int32, (TK, rows), 1) & (tq - 1)), 0.0, NEG)

    @pl.when(qi < main_from)
    def _():
        short_path(qsel_ref, ks_ref, vs_ref, True, SEL)
        short_path(qz_ref, kw_ref, vw_ref, False, WIN)

    @pl.when(qi >= main_from)
    def _():
        last = pl.multiple_of((qi - (big // TK - 1)) * TK, TK)
        n_far = (qi - (big // TK - 1) + big // TK - 1) >> int(math.log2(big // TK))
        near = {big // TK - 2: near_bias(1), big // TK - 1: near_bias(0)}

        def start_of(k):
            return pl.multiple_of(jnp.where(k < n_far, k * big, last), TK)

        def put(k, slot):
            for g, s in enumerate(scores(qsel_ref, ks_ref, start_of(k), big, True)):
                s_ref[slot, g] = s

        def take(k, slot, extras, limit):
            consume(lambda g, c: s_ref[slot, g, c * TK:(c + 1) * TK, :], vs_ref,
                    start_of(k), big, extras, SEL, limit)

        win_start = pl.multiple_of((qi - n_win) * TK, TK)
        for g, s in enumerate(scores(qz_ref, kw_ref, win_start, WINDOW + TK, False)):
            ws_ref[g] = s
        put(0, 0)
        consume(lambda g, c: ws_ref[g, c * TK:(c + 1) * TK, :], vw_ref, win_start, WINDOW + TK,
                {0: lambda g: win_mask, n_win - 1: near_bias(1), n_win: near_bias(0)}, WIN)
        n_pair = n_far >> 1

        def far_pair(i, carry):
            k = 2 * i
            put(k + 1, 1)
            take(k, 0, {}, last)
            put(k + 2, 0)
            take(k + 1, 1, {}, last)
            return carry

        lax.fori_loop(0, n_pair, far_pair, 0)

        @pl.when(n_far == 2 * n_pair)
        def _():
            take(n_far, 0, near, None)

        @pl.when(n_far != 2 * n_pair)
        def _():
            put(n_far, 1)
            take(n_far - 1, 0, {}, last)
            take(n_far, 1, near, None)

    finish(1, SEL)
    finish(2, WIN)

    for h in range(hpg):
        zp = z_ref[0, :, h * LANES:(h + 1) * LANES]
        o_ref[0, :, h * LANES:(h + 1) * LANES] = (osum_ref[h].T * _silu(zp)).astype(o_ref.dtype)


def _nsa_attention(pb3, pf3, vt, kvc, kvc_t, bias_c, bias_d, ov):
    b, t, _ = pb3.shape
    n_cmp = kvc.shape[2]
    n_blk = N_HEADS_NSA // 2
    full = lambda shape: pl.BlockSpec(shape, lambda i, q: (0,) * len(shape))
    kv_spec = lambda col: pl.BlockSpec((1, t, LANES), lambda i, q: (i, 0, col))
    vt_spec = lambda blk: pl.BlockSpec((1, 1, t // TK, LANES, TK), lambda i, q: (i, blk, 0, 0, 0))
    return pl.pallas_call(
        _nsa_kernel,
        grid=(b, t // TQ_NSA),
        in_specs=[pl.BlockSpec((1, TQ_NSA, n_blk * LANES), lambda i, q: (i, q, PB_QB // n_blk)),
                  pl.BlockSpec((1, 1, n_cmp, LANES), lambda i, q: (i, 0, 0, 0)),
                  pl.BlockSpec((1, 1, LANES, n_cmp), lambda i, q: (i, 1, 0, 0)),
                  kv_spec(PB_KS), vt_spec(VT_VS), kv_spec(PB_KW), vt_spec(VT_VW),
                  pl.BlockSpec((1, TQ_NSA, LANES), lambda i, q: (i, q, PF_GB)),
                  pl.BlockSpec((1, TQ_NSA, n_blk * LANES), lambda i, q: (i, q, PF_ZB // n_blk)),
                  pl.BlockSpec((N_HEADS_NSA, n_cmp, TQ_NSA), lambda i, q: (0, 0, q)),
                  full(bias_d.shape), full(ov.shape)],
        out_specs=pl.BlockSpec((1, TQ_NSA, n_blk * LANES), lambda i, q: (i, q, 0)),
        out_shape=jax.ShapeDtypeStruct((b, t, n_blk * LANES), _BF16),
        scratch_shapes=[pltpu.VMEM((2, NSA_KV_GROUPS, SUBLANES, NSA_HPG * TQ_NSA), _F32),
                        pltpu.VMEM((2, NSA_KV_GROUPS, HEAD_DIM + SUBLANES, NSA_HPG * TQ_NSA), _F32),
                        pltpu.VMEM((n_blk, TQ_NSA, LANES), _F32),
                        pltpu.VMEM((NSA_KV_GROUPS, NSA_HPG * TQ_NSA, LANES), _BF16),
                        pltpu.VMEM((NSA_KV_GROUPS, NSA_HPG * TQ_NSA, LANES), _BF16),
                        pltpu.VMEM((LANES, TQ_NSA), _F32),
                        pltpu.VMEM((NSA_KV_GROUPS, n_cmp, TQ_NSA), _F32),
                        pltpu.VMEM((2, NSA_KV_GROUPS, 4 * TK, NSA_HPG * TQ_NSA), _F32),
                        pltpu.VMEM((NSA_KV_GROUPS, WINDOW + TK, NSA_HPG * TQ_NSA), _F32)],
        compiler_params=_cparams(("arbitrary", "arbitrary")),
        name="native_sparse_attention",
    )(pb3, kvc, kvc_t, pb3, vt, pb3, vt, pf3, pf3, bias_c, bias_d, ov)


def _out_kernel(x_ref, oa_ref, ob_ref, oc_ref, w_ref, g_ref, o_ref, *, final_norm):
    na, nb = oa_ref.shape[1], ob_ref.shape[1]
    mix = (_dot(oa_ref[...], w_ref[0:na])
           + _dot(ob_ref[...], w_ref[na:na + nb])
           + _dot(oc_ref[...], w_ref[na + nb:]))
    x = x_ref[...] + mix
    if final_norm:
        x = x * lax.rsqrt(jnp.mean(x * x, axis=-1, keepdims=True) + RMS_EPS) * g_ref[...]
    o_ref[...] = x


def _out_proj(x2, oa, ob, oc, w, g, final_norm):
    n = x2.shape[0]
    row = lambda width: pl.BlockSpec((TM_PROJ, width), lambda i: (i, 0))
    return pl.pallas_call(
        functools.partial(_out_kernel, final_norm=final_norm),
        grid=(n // TM_PROJ,),
        in_specs=[row(D_MODEL), row(oa.shape[1]), row(ob.shape[1]), row(oc.shape[1]),
                  pl.BlockSpec((D_MODEL, D_MODEL), lambda i: (0, 0)),
                  pl.BlockSpec((1, D_MODEL), lambda i: (0, 0))],
        out_specs=row(D_MODEL),
        out_shape=jax.ShapeDtypeStruct((n, D_MODEL), _F32),
        compiler_params=_cparams(("arbitrary",)),
        name="out_proj_residual",
    )(x2, oa, ob, oc, w, g)


def _head_perm_cols(width_per_head, order):
    return np.concatenate([np.arange(h * width_per_head, (h + 1) * width_per_head) for h in order])


def _layout_w_in(w):
    widths = [256, 256, 256, 256, 512, 128, 128, 128, 128, 128, 128, 24, 512, 256, 256, 256, 4, 256]
    offs = np.concatenate([[0], np.cumsum(widths)])
    (qa, ka, va, za, qb, kc, vc, ks, vs, kw, vw, gb, zb, qc, kf, vf, fc, zc) = [
        w[:, offs[i]:offs[i + 1]] for i in range(len(widths))]
    scale = HEAD_DIM ** -0.5
    perm = _head_perm_cols(HEAD_DIM, NSA_HEAD_ORDER)
    pad = lambda a: jnp.pad(a, ((0, 0), (0, LANES - a.shape[1])))
    wb = jnp.concatenate([qb[:, perm] * (scale * LOG2E), qa * (-scale), ka, va, ks, kw,
                          qc * (scale * LOG2E), kf], axis=1)
    wf = jnp.concatenate([zb[:, perm], za, zc, pad(gb), pad(fc)], axis=1)
    wc = jnp.concatenate([kc, vc], axis=1)
    wvt = jnp.concatenate([vs, vw, vf], axis=1)
    return wb.astype(_BF16), wf.astype(_BF16), wc.astype(_BF16), wvt.astype(_BF16)


def _static_tables(t):
    tq = TQ_NSA
    n_cmp_pad = t // CMP_STRIDE
    j = np.arange(n_cmp_pad)
    dist_c = (np.arange(t)[None, :] - (j[:, None] * CMP_STRIDE + CMP_BLOCK - 1))
    dist_c[n_cmp_pad - 1, :] = -1
    i_, j_ = np.arange(tq)[:, None], np.arange(TK)[None, :]
    dist_d = np.concatenate([d * TK + i_ - j_ for d in range(3)], axis=0)
    n_slc = LANES // NSA_KV_GROUPS
    cmp_start = j * CMP_STRIDE
    cmp_end = cmp_start + CMP_BLOCK - 1
    slc_start = np.arange(n_slc) * SLC_BLOCK
    ov1 = np.clip(np.minimum(cmp_end[:, None], slc_start[None, :] + SLC_BLOCK - 1)
                  - np.maximum(cmp_start[:, None], slc_start[None, :]) + 1, 0, None) / CMP_BLOCK
    ov1[n_cmp_pad - 1:] = 0.0
    ov1[:, t // SLC_BLOCK:] = 0.0
    ov3 = np.concatenate([ov1.T, ov1.T, ov1.T], axis=1)
    return jnp.asarray(dist_c, jnp.int32), jnp.asarray(dist_d, jnp.int32), jnp.asarray(ov3, _BF16)


def kernel(x, norm_g, w_in, w_out, forget_b, cmp_w1, cmp_b1, cmp_w2, cmp_pe, rel_bias, final_g):
    b, t, d = x.shape
    depth = norm_g.shape[0]
    assert d == D_MODEL and t % TM_PROJ == 0 and t % (CMP_STRIDE * LANES) == 0
    assert t // SLC_BLOCK <= LANES // NSA_KV_GROUPS and TQ_NSA == TK
    n_cmp_pad = t // CMP_STRIDE

    dist_c, dist_d, ov3 = _static_tables(t)
    bias_c = _bias_table(rel_bias * LOG2E, dist_c, 32)
    bias_d = _bias_table(rel_bias * LOG2E, dist_d, 32).reshape(N_HEADS_NSA, 3, TQ_NSA, TK)
    bias_d = bias_d[:, :2] - bias_d[:, 2:3]
    bias_d = bias_d.reshape(NSA_KV_GROUPS, NSA_HPG, 2, TQ_NSA, TK).transpose(0, 2, 4, 1, 3).reshape(
        NSA_KV_GROUPS, 2, TK, NSA_HPG * TQ_NSA)

    perm_rows = _head_perm_cols(HEAD_DIM, NSA_HEAD_ORDER)
    x2 = x.reshape(b * t, d)
    out = None
    for l in range(depth):
        wb, wf, wc, wvt = _layout_w_in(w_in[l])
        pb, pf, pc, vt = _proj(x2, norm_g[l].reshape(1, d), wb, wf, wc, wvt, t)
        pb3 = pb.reshape(b, t, PB_BLOCKS * LANES)
        pf3 = pf.reshape(b, t, PF_BLOCKS * LANES)

        fb_row = jnp.pad(forget_b[l], (0, LANES - N_HEADS_FOX)).reshape(1, LANES)
        augq, augk = _fgate(pf3, fb_row)

        halves = pc.reshape(b, 2, NSA_KV_GROUPS, n_cmp_pad, CMP_STRIDE * HEAD_DIM)
        w2 = cmp_w2[l]
        zeros = jnp.zeros_like(w2)
        w2p = jnp.stack([jnp.concatenate([w2, zeros], axis=-1),
                         jnp.concatenate([zeros, w2], axis=-1)], axis=1).astype(_BF16)
        kvc, kvc_t = _compress(halves, cmp_w1[l].astype(_BF16),
                               cmp_pe[l].reshape(2, 1, CMP_BLOCK * HEAD_DIM).astype(_BF16),
                               cmp_b1[l].reshape(2, 1, CMP_HIDDEN), w2p)

        o_a = _sb_attention(pb3, pf3)
        o_b = _nsa_attention(pb3, pf3, vt, kvc, kvc_t, bias_c, bias_d, ov3)
        o_c = _fox_attention(pb3, pf3, vt, augq, augk)

        wo = w_out[l]
        wo = jnp.concatenate([wo[:N_HEADS_SB * HEAD_DIM],
                              wo[N_HEADS_SB * HEAD_DIM:][:N_HEADS_NSA * HEAD_DIM][perm_rows],
                              wo[(N_HEADS_SB + N_HEADS_NSA) * HEAD_DIM:]], axis=0).astype(_BF16)
        last = l == depth - 1
        x2 = _out_proj(x2, o_a.reshape(b * t, -1), o_b.reshape(b * t, -1), o_c.reshape(b * t, -1),
                       wo, final_g.reshape(1, d), last)
    return x2.reshape(b, t, d)
```

```python
import functools
import math

import jax
import jax.numpy as jnp
import numpy as np
from jax import lax
from jax.experimental import pallas as pl
from jax.experimental.pallas import tpu as pltpu

D_MODEL = 1024
HEAD_DIM = 64
N_HEADS_SB = 4
N_HEADS_FOX = 4
N_HEADS_NSA = 8
NSA_KV_GROUPS = 2
NSA_HPG = N_HEADS_NSA // NSA_KV_GROUPS
CMP_BLOCK = 32
CMP_STRIDE = 16
CMP_HIDDEN = 256
SLC_BLOCK = 64
SLC_TOP = 16
WINDOW = 512
REL_BUCKETS = 32
REL_MAX_DIST = 128
FORCE_SCORE = 1e4
RMS_EPS = 1e-6
NEG = -1e30
LOG2E = math.log2(math.e)

LANES = 128
SUBLANES = 8
VMEM_LIMIT = 56 * 1024 * 1024

TM_PROJ = 512
TQ_SB = 512
TQ_FOX = 512
FOX_PARTS = 2
TQ_NSA = 128
TK = 128

PB_QB, PB_QA, PB_KA, PB_VA = 0, 4, 6, 8
PB_KS, PB_KW = 10, 11
PB_QC, PB_KF = 12, 14
PB_BLOCKS = 16
N_CMP_SLABS = 2 * NSA_KV_GROUPS
VT_VS, VT_VW, VT_VF = 0, 1, 2
N_VT = 4
PF_ZB, PF_ZA, PF_ZC, PF_GB, PF_FC = 0, 4, 6, 8, 9
PF_BLOCKS = 10

NSA_HEAD_ORDER = [0, 4, 1, 5, 2, 6, 3, 7]

_F32 = jnp.float32
_BF16 = jnp.bfloat16


def _cparams(sem):
    return pltpu.CompilerParams(dimension_semantics=sem, vmem_limit_bytes=VMEM_LIMIT)


def _dot(a, b):
    return jnp.dot(a, b, preferred_element_type=_F32)


def _dot_nt(a, b):
    return lax.dot_general(a, b, (((1,), (1,)), ((), ())), preferred_element_type=_F32)


def _split2(x):
    hi = x.astype(_BF16)
    lo = (x - hi.astype(_F32)).astype(_BF16)
    return jnp.concatenate([hi, lo], axis=1)


def _split3(x):
    h1 = x.astype(_BF16)
    r1 = x - h1.astype(_F32)
    h2 = r1.astype(_BF16)
    h3 = (r1 - h2.astype(_F32)).astype(_BF16)
    return jnp.concatenate([h1, h2, h3], axis=1)


def _sigmoid(x):
    return 1.0 / (1.0 + jnp.exp(-x))


def _silu(x):
    return x * _sigmoid(x)


def _rel_bucket_np(n):
    n = np.maximum(n, 0)
    max_exact = REL_BUCKETS // 2
    nf = np.maximum(n, 1).astype(np.float64)
    large = max_exact + (np.log(nf / max_exact) / math.log(REL_MAX_DIST / max_exact)
                         * (REL_BUCKETS - max_exact)).astype(np.int64)
    large = np.minimum(large, REL_BUCKETS - 1)
    return np.where(n < max_exact, n, large)


def _bucket_thresholds():
    n = np.arange(0, 4 * REL_MAX_DIST)
    bk = _rel_bucket_np(n)
    assert np.all(np.diff(bk) >= 0) and bk[-1] == REL_BUCKETS - 1
    return [int(np.argmax(bk >= b)) for b in range(REL_BUCKETS)]


_BUCKET_THR = _bucket_thresholds()


def _bias_kernel(tab_ref, dist_ref, o_ref):
    n = dist_ref[...]
    acc = [jnp.full(n.shape, tab_ref[0, h], _F32) for h in range(N_HEADS_NSA)]
    for b in range(1, REL_BUCKETS):
        ge = n >= _BUCKET_THR[b]
        for h in range(N_HEADS_NSA):
            acc[h] = jnp.where(ge, tab_ref[b, h], acc[h])
    valid = n >= 0
    for h in range(N_HEADS_NSA):
        o_ref[h] = jnp.where(valid, acc[h], NEG)


def _bias_table(rel_bias, dist, rows):
    n_rows, n_cols = dist.shape
    return pl.pallas_call(
        _bias_kernel,
        grid=(n_rows // rows,),
        in_specs=[pl.BlockSpec(memory_space=pltpu.SMEM),
                  pl.BlockSpec((rows, n_cols), lambda i: (i, 0))],
        out_specs=pl.BlockSpec((N_HEADS_NSA, rows, n_cols), lambda i: (0, i, 0)),
        out_shape=jax.ShapeDtypeStruct((N_HEADS_NSA, n_rows, n_cols), _F32),
        compiler_params=_cparams(("arbitrary",)),
        name="rel_bias_table",
    )(rel_bias, dist)


def _proj_kernel(x_ref, g_ref, wb_ref, wf_ref, wc_ref, wvt_ref, pb_ref, pf_ref, pc_ref, vt_ref):
    x = x_ref[...]
    y = x * lax.rsqrt(jnp.mean(x * x, axis=-1, keepdims=True) + RMS_EPS)
    h = (y * g_ref[...]).astype(_BF16)
    v_all = _dot(h, wvt_ref[...])
    for j in range(N_VT):
        v_t = v_all[:, j * LANES:(j + 1) * LANES].T.astype(_BF16)
        for c in range(TM_PROJ // TK):
            vt_ref[0, j, c] = v_t[:, c * TK:(c + 1) * TK]
    kv_cmp = _dot(h, wc_ref[...]).astype(_BF16)
    for s in range(N_CMP_SLABS):
        pc_ref[0, s] = kv_cmp[:, s * HEAD_DIM:(s + 1) * HEAD_DIM]
    chunk = 4 * LANES
    for c in range(0, PB_BLOCKS * LANES, chunk):
        w = min(chunk, PB_BLOCKS * LANES - c)
        res = _dot(h, wb_ref[:, c:c + w]).astype(_BF16)
        for j in range(w // LANES):
            pb_ref[0, c // LANES + j] = res[:, j * LANES:(j + 1) * LANES]
    for c in range(0, PF_BLOCKS * LANES, chunk):
        w = min(chunk, PF_BLOCKS * LANES - c)
        res = _dot(h, wf_ref[:, c:c + w])
        for j in range(w // LANES):
            pf_ref[0, c // LANES + j] = res[:, j * LANES:(j + 1) * LANES]


def _proj(x2, g, wb, wf, wc, wvt, t):
    n = x2.shape[0]
    per_seq = t // TM_PROJ
    return pl.pallas_call(
        _proj_kernel,
        grid=(n // TM_PROJ,),
        in_specs=[pl.BlockSpec((TM_PROJ, D_MODEL), lambda i: (i, 0)),
                  pl.BlockSpec((1, D_MODEL), lambda i: (0, 0)),
                  pl.BlockSpec((D_MODEL, PB_BLOCKS * LANES), lambda i: (0, 0)),
                  pl.BlockSpec((D_MODEL, PF_BLOCKS * LANES), lambda i: (0, 0)),
                  pl.BlockSpec((D_MODEL, N_CMP_SLABS * HEAD_DIM), lambda i: (0, 0)),
                  pl.BlockSpec((D_MODEL, N_VT * LANES), lambda i: (0, 0))],
        out_specs=[pl.BlockSpec((1, PB_BLOCKS, TM_PROJ, LANES), lambda i: (i // per_seq, 0, i % per_seq, 0)),
                   pl.BlockSpec((1, PF_BLOCKS, TM_PROJ, LANES), lambda i: (i // per_seq, 0, i % per_seq, 0)),
                   pl.BlockSpec((1, N_CMP_SLABS, TM_PROJ, HEAD_DIM),
                                lambda i: (i // per_seq, 0, i % per_seq, 0)),
                   pl.BlockSpec((1, N_VT, TM_PROJ // TK, LANES, TK),
                                lambda i: (i // per_seq, 0, i % per_seq, 0, 0))],
        out_shape=[jax.ShapeDtypeStruct((n // t, PB_BLOCKS, t, LANES), _BF16),
                   jax.ShapeDtypeStruct((n // t, PF_BLOCKS, t, LANES), _F32),
                   jax.ShapeDtypeStruct((n // t, N_CMP_SLABS, t, HEAD_DIM), _BF16),
                   jax.ShapeDtypeStruct((n // t, N_VT, t // TK, LANES, TK), _BF16)],
        compiler_params=_cparams(("arbitrary",)),
        name="rmsnorm_in_proj",
    )(x2, g, wb, wf, wc, wvt)


N_SPLIT = 3


def _fgate_tables():
    n_pairs = N_HEADS_FOX // 2
    pq = np.zeros((N_SPLIT * LANES, n_pairs * LANES), np.float32)
    pk = np.zeros_like(pq)
    ones_q = np.zeros((1, n_pairs * LANES), np.float32)
    ones_k = np.zeros_like(ones_q)
    for head in range(N_HEADS_FOX):
        pair, slot = divmod(head, 2)
        base = pair * LANES + (HEAD_DIM if slot == 0 else 0)
        for j in range(N_SPLIT):
            pq[j * LANES + head, base + j] = 1.0
            pk[j * LANES + head, base + N_SPLIT + j] = 1.0
        ones_q[0, base + N_SPLIT:base + 2 * N_SPLIT] = 1.0
        ones_k[0, base:base + N_SPLIT] = 1.0
    return (jnp.asarray(pq, _BF16), jnp.asarray(pk, _BF16),
            jnp.asarray(ones_q), jnp.asarray(ones_k))


def _fgate_kernel(fc_ref, fb_ref, pq_ref, pk_ref, oq_ref, ok_ref, augq_ref, augk_ref):
    t = fc_ref.shape[2]
    z = fc_ref[0, 0] + fb_ref[...]
    logf = jnp.minimum(z, 0.0) - jnp.log1p(jnp.exp(-jnp.abs(z)))
    row = lax.broadcasted_iota(jnp.int32, (t, LANES), 0)
    c = logf
    shift = 1
    while shift < t:
        c = c + jnp.where(row >= shift, pltpu.roll(c, shift, axis=0), 0.0)
        shift *= 2
    c3 = _split3(c * LOG2E)
    aq = _dot(c3, pq_ref[...]) + oq_ref[...]
    ak = ok_ref[...] - _dot(c3, pk_ref[...])
    for p in range(N_HEADS_FOX // 2):
        augq_ref[0, p] = aq[:, p * LANES:(p + 1) * LANES].astype(_BF16)
        augk_ref[0, p] = ak[:, p * LANES:(p + 1) * LANES].astype(_BF16)


def _fgate(pf3, fb_row):
    b, _, t, _ = pf3.shape
    n_pairs = N_HEADS_FOX // 2
    tables = _fgate_tables()
    full = lambda a: pl.BlockSpec(a.shape, lambda i: (0,) * a.ndim)
    aug = lambda: pl.BlockSpec((1, n_pairs, t, LANES), lambda i: (i, 0, 0, 0))
    return pl.pallas_call(
        _fgate_kernel,
        grid=(b,),
        in_specs=[pl.BlockSpec((1, 1, t, LANES), lambda i: (i, PF_FC, 0, 0)),
                  pl.BlockSpec((1, LANES), lambda i: (0, 0))] + [full(a) for a in tables],
        out_specs=[aug(), aug()],
        out_shape=[jax.ShapeDtypeStruct((b, n_pairs, t, LANES), _BF16),
                   jax.ShapeDtypeStruct((b, n_pairs, t, LANES), _BF16)],
        compiler_params=_cparams(("arbitrary",)),
        name="forget_gate_cumsum",
    )(pf3, fb_row, *tables)


def _compress_kernel(x_ref, w1_ref, pe_ref, b1_ref, w2_ref, o_ref, ot_ref):
    nc = x_ref.shape[3]
    half = CMP_STRIDE * HEAD_DIM
    w1 = w1_ref[0]
    c1 = _dot(jnp.broadcast_to(pe_ref[0], (SUBLANES, 2 * half)), w1)[0:1] + b1_ref[0]
    out = jnp.zeros((nc, LANES), _F32)
    for g in range(NSA_KV_GROUPS):
        xg = x_ref[0, 0, g]
        a = _dot(xg, w1[:half])
        bb = _dot(xg, w1[half:])
        h = a + pltpu.roll(bb, nc - 1, axis=0) + c1
        out = out + _dot(_silu(h).astype(_BF16), w2_ref[0, g])
    o_ref[0, 0] = out.astype(_BF16)
    ot_ref[0, 0] = out.T.astype(_BF16)


def _compress(halves, w1, pe, b1, w2p):
    b, _, g, nc, width = halves.shape
    return pl.pallas_call(
        _compress_kernel,
        grid=(b, 2),
        in_specs=[pl.BlockSpec((1, 1, g, nc, width), lambda i, k: (i, k, 0, 0, 0)),
                  pl.BlockSpec((1, 2 * width, CMP_HIDDEN), lambda i, k: (k, 0, 0)),
                  pl.BlockSpec((1, 1, 2 * width), lambda i, k: (k, 0, 0)),
                  pl.BlockSpec((1, 1, CMP_HIDDEN), lambda i, k: (k, 0, 0)),
                  pl.BlockSpec((1, g, CMP_HIDDEN, LANES), lambda i, k: (k, 0, 0, 0))],
        out_specs=[pl.BlockSpec((1, 1, nc, LANES), lambda i, k: (i, k, 0, 0)),
                   pl.BlockSpec((1, 1, LANES, nc), lambda i, k: (i, k, 0, 0))],
        out_shape=[jax.ShapeDtypeStruct((b, 2, nc, LANES), _BF16),
                   jax.ShapeDtypeStruct((b, 2, LANES, nc), _BF16)],
        compiler_params=_cparams(("arbitrary", "arbitrary")),
        name="nsa_compress",
    )(halves, w1, pe, b1, w2p)


def _sb_kernel(q_ref, k_ref, v_ref, z_ref, o_ref, qh_ref, c_ref, acc_ref):
    qi = pl.program_id(2)
    tq = q_ref.shape[2]
    lane = lax.broadcasted_iota(jnp.int32, (tq, LANES), 1)
    q2 = q_ref[0, 0]
    qh = [jnp.where(lane < HEAD_DIM, q2, jnp.zeros_like(q2)),
          jnp.where(lane >= HEAD_DIM, q2, jnp.zeros_like(q2))]
    r_i = lax.broadcasted_iota(jnp.int32, (2 * TK, 2 * TK), 0)
    c_i = lax.broadcasted_iota(jnp.int32, (2 * TK, 2 * TK), 1)
    uu = jnp.where(r_i >= c_i, 1.0, 0.0).astype(_BF16)
    c_ref[...] = jnp.zeros_like(c_ref)
    acc_ref[...] = jnp.zeros_like(acc_ref)
    for h in range(2):
        qh_ref[h] = qh[h]
    row = lax.broadcasted_iota(jnp.int32, (tq, LANES), 0)
    n_blocks = tq // TK

    def chunk(start, diagonal):
        k2 = k_ref[0, 0, pl.ds(start, tq), :]
        v2 = v_ref[0, 0, pl.ds(start, tq), :]
        for h in range(2):
            s = _dot_nt(qh_ref[h], k2)
            carry = c_ref[h]
            w_blocks = [None] * n_blocks
            for c in reversed(range(0, n_blocks, 2)):
                nz, l1m, mask = [], [], []
                for cc in (c, c + 1):
                    nzc = s[:, cc * TK:(cc + 1) * TK]
                    neg_abs = lax.bitcast_convert_type(
                        lax.bitcast_convert_type(nzc, jnp.uint32) | jnp.uint32(0x80000000), _F32)
                    lc = jnp.minimum(nzc, 0.0) - jnp.log(1.0 + jnp.exp(neg_abs))
                    if diagonal:
                        mask.append(lane + cc * TK < row)
                        lc = jnp.where(mask[-1], lc, 0.0)
                    nz.append(nzc)
                    l1m.append(lc.astype(_BF16))
                rc = _dot(jnp.concatenate(l1m, axis=1), uu)
                for i, cc in enumerate((c, c + 1)):
                    w = jnp.exp((rc[:, i * TK:(i + 1) * TK] + carry) - nz[i])
                    if diagonal:
                        w = jnp.where(mask[i], w, 0.0)
                    w_blocks[cc] = w.astype(_BF16)
                carry = carry + jnp.broadcast_to(rc[:, 0:1], carry.shape)
            acc_ref[h] += _dot(jnp.concatenate(w_blocks, axis=1), v2)
            c_ref[h] = carry

    chunk(pl.multiple_of(qi * tq, tq), True)

    def far_pair(it, carry):
        chunk(pl.multiple_of((qi - 1 - 2 * it) * tq, tq), False)
        chunk(pl.multiple_of((qi - 2 - 2 * it) * tq, tq), False)
        return carry

    lax.fori_loop(0, qi >> 1, far_pair, 0)

    @pl.when((qi & 1) == 1)
    def _():
        chunk(0, False)
    o = jnp.where(lane < HEAD_DIM, acc_ref[0], acc_ref[1])
    o_ref[0, 0] = (o * _silu(z_ref[0, 0])).astype(o_ref.dtype)


def _sb_attention(pb3, pf3):
    b, _, t, _ = pb3.shape
    n_pairs = N_HEADS_SB // 2
    return pl.pallas_call(
        _sb_kernel,
        grid=(b, n_pairs, t // TQ_SB),
        in_specs=[pl.BlockSpec((1, 1, TQ_SB, LANES), lambda i, p, q: (i, PB_QA + p, q, 0)),
                  pl.BlockSpec((1, 1, t, LANES), lambda i, p, q: (i, PB_KA + p, 0, 0)),
                  pl.BlockSpec((1, 1, t, LANES), lambda i, p, q: (i, PB_VA + p, 0, 0)),
                  pl.BlockSpec((1, 1, TQ_SB, LANES), lambda i, p, q: (i, PF_ZA + p, q, 0))],
        out_specs=pl.BlockSpec((1, 1, TQ_SB, LANES), lambda i, p, q: (i, p, q, 0)),
        out_shape=jax.ShapeDtypeStruct((b, n_pairs, t, LANES), _BF16),
        scratch_shapes=[pltpu.VMEM((2, TQ_SB, LANES), _BF16),
                        pltpu.VMEM((2, TQ_SB, LANES), _F32),
                        pltpu.VMEM((2, TQ_SB, LANES), _F32)],
        compiler_params=_cparams(("arbitrary", "arbitrary", "arbitrary")),
        name="stick_breaking_attention",
    )(pb3, pb3, pb3, pf3)


def _fox_kernel(q_ref, k_ref, vt_ref, augq_ref, augk_ref, z_ref, o_ref,
                qh_ref, m_ref, acc_ref, s_ref):
    qi = pl.program_id(2)
    tq = q_ref.shape[2]
    tk = tq
    lane = lax.broadcasted_iota(jnp.int32, (tq, LANES), 1)
    keep = [lane < HEAD_DIM, lane >= HEAD_DIM]
    q2 = q_ref[0, 0]
    aq = augq_ref[0, 0]
    for h in range(2):
        qh_ref[h] = jnp.where(keep[h], q2, aq)
    m_ref[...] = jnp.full_like(m_ref, NEG)
    acc_ref[...] = jnp.zeros_like(acc_ref)
    key_i = lax.broadcasted_iota(jnp.int32, (TK, tq), 0)
    qry_i = lax.broadcasted_iota(jnp.int32, (TK, tq), 1)

    def scores(c, slot):
        start = c * tk if isinstance(c, int) else pl.multiple_of(c * tk, tk)
        k2 = k_ref[0, 0, pl.ds(start, tk), :]
        ak = augk_ref[0, 0, pl.ds(start, tk), :]
        for h in range(2):
            s_ref[slot, h] = _dot_nt(jnp.where(keep[h], k2, ak), qh_ref[h])

    ones_rows = jnp.ones((SUBLANES, tk), _BF16)

    def consume(c, slot, diagonal):
        per = tk // TK
        vt = jnp.concatenate([vt_ref[0, 0, c * per + j] for j in range(per)], axis=1)
        for h in range(2):
            vth = jnp.concatenate([vt[h * HEAD_DIM:(h + 1) * HEAD_DIM], ones_rows], axis=0)
            for part in range(FOX_PARTS):
                js = range(part * per // FOX_PARTS, (part + 1) * per // FOX_PARTS)
                blocks = [s_ref[slot, h, j * TK:(j + 1) * TK, :] for j in js]
                if diagonal:
                    blocks = [jnp.where(key_i + j * TK <= qry_i, blk, NEG) for j, blk in zip(js, blocks)]
                mx = blocks[0]
                for blk in blocks[1:]:
                    mx = jnp.maximum(mx, blk)
                m_old = m_ref[h]
                m_new = jnp.maximum(m_old, jnp.max(mx, axis=0, keepdims=True))
                m_row = m_new[0:1]
                pt = jnp.concatenate([jnp.exp2(blk - m_row).astype(_BF16) for blk in blocks], axis=0)
                acc_ref[h] = (jnp.exp2(m_old - m_new)[0:1] * acc_ref[h]
                              + _dot(vth[:, js[0] * TK:(js[-1] + 1) * TK], pt))
                m_ref[h] = m_new

    scores(0, 0)
    n_pairs = qi >> 1

    def far_pair(i, carry):
        c = 2 * i
        scores(c + 1, 1)
        consume(c, 0, False)
        scores(c + 2, 0)
        consume(c + 1, 1, False)
        return carry

    lax.fori_loop(0, n_pairs, far_pair, 0)

    @pl.when(qi == 2 * n_pairs)
    def _():
        consume(qi, 0, True)

    @pl.when(qi != 2 * n_pairs)
    def _():
        scores(qi, 1)
        consume(qi - 1, 0, False)
        consume(qi, 1, True)
    o_t = jnp.concatenate([acc_ref[h, :HEAD_DIM] / acc_ref[h, HEAD_DIM:HEAD_DIM + 1]
                           for h in range(2)], axis=0)
    o_ref[0, 0] = (o_t.T * _silu(z_ref[0, 0])).astype(o_ref.dtype)


def _fox_attention(pb3, pf3, vt, augq, augk):
    b, _, t, _ = pb3.shape
    n_pairs = N_HEADS_FOX // 2
    return pl.pallas_call(
        _fox_kernel,
        grid=(b, n_pairs, t // TQ_FOX),
        in_specs=[pl.BlockSpec((1, 1, TQ_FOX, LANES), lambda i, p, q: (i, PB_QC + p, q, 0)),
                  pl.BlockSpec((1, 1, t, LANES), lambda i, p, q: (i, PB_KF + p, 0, 0)),
                  pl.BlockSpec((1, 1, t // TK, LANES, TK), lambda i, p, q: (i, VT_VF + p, 0, 0, 0)),
                  pl.BlockSpec((1, 1, TQ_FOX, LANES), lambda i, p, q: (i, p, q, 0)),
                  pl.BlockSpec((1, 1, t, LANES), lambda i, p, q: (i, p, 0, 0)),
                  pl.BlockSpec((1, 1, TQ_FOX, LANES), lambda i, p, q: (i, PF_ZC + p, q, 0))],
        out_specs=pl.BlockSpec((1, 1, TQ_FOX, LANES), lambda i, p, q: (i, p, q, 0)),
        out_shape=jax.ShapeDtypeStruct((b, n_pairs, t, LANES), _BF16),
        scratch_shapes=[pltpu.VMEM((2, TQ_FOX, LANES), _BF16),
                        pltpu.VMEM((2, SUBLANES, TQ_FOX), _F32),
                        pltpu.VMEM((2, HEAD_DIM + SUBLANES, TQ_FOX), _F32),
                        pltpu.VMEM((2, 2, TQ_FOX, TQ_FOX), _F32)],
        compiler_params=_cparams(("arbitrary", "arbitrary", "arbitrary")),
        name="forgetting_attention",
    )(pb3, pb3, vt, augq, augk, pf3)


def _nsa_kernel(q_ref, kc_ref, vct_ref, ks_ref, vs_ref, kw_ref, vw_ref, gl_ref, z_ref,
                bc_ref, bd_ref, ov_ref, o_ref,
                m_ref, acc_ref, osum_ref, qz_ref, qsel_ref, gates_ref, psum_ref, s_ref, ws_ref):
    qi = pl.program_id(1)
    tq = q_ref.shape[2]
    hpg = NSA_HPG
    rows = hpg * tq
    n_cmp = kc_ref.shape[2]
    lane = lax.broadcasted_iota(jnp.int32, (tq, LANES), 1)
    qrow = qi * tq + lax.broadcasted_iota(jnp.int32, (tq, LANES), 0)
    lane_r = lax.broadcasted_iota(jnp.int32, (rows, LANES), 1)
    irow_r = lax.broadcasted_iota(jnp.int32, (rows, LANES), 0) & (tq - 1)
    half = [lane_r < HEAD_DIM, lane_r >= HEAD_DIM]

    gates_ref[...] = _sigmoid(gl_ref[0, 0]).T

    def gated(c, h, per_group):
        parts = []
        for g, o in enumerate(per_group):
            r = 3 * (g * hpg + h) + c
            parts.append(gates_ref[r:r + 1, :] * o[:, h * tq:(h + 1) * tq])
        return jnp.concatenate(parts, axis=0)

    q4 = jnp.concatenate([q_ref[0, h] for h in range(hpg)], axis=0)
    for g in range(NSA_KV_GROUPS):
        qz_ref[g] = jnp.where(half[g], q4, jnp.zeros_like(q4))

    def scores(q_src, k_ref, start, tk, onehot):
        k2 = k_ref[0, 0, pl.ds(start, tk), :]
        lane_k = lax.broadcasted_iota(jnp.int32, (tk, LANES), 1)
        key_blk = (start + lax.broadcasted_iota(jnp.int32, (tk, LANES), 0)) >> int(math.log2(SLC_BLOCK))
        oh = jnp.where((lane_k & (SLC_BLOCK - 1)) == key_blk, 1.0, 0.0).astype(_BF16)
        out = []
        for g in range(NSA_KV_GROUPS):
            keep = (lane_k < HEAD_DIM) if g == 0 else (lane_k >= HEAD_DIM)
            out.append(_dot_nt(jnp.where(keep, k2, oh) if onehot else k2, q_src[g]))
        return out


    def heads_to_blocks(x0, x1):
        return [jnp.where(lane < HEAD_DIM, x0[h * tq:(h + 1) * tq], x1[h * tq:(h + 1) * tq])
                for h in range(hpg)]

    kc = kc_ref[0, 0]
    vct = vct_ref[0, 0]
    cmp_row0 = pl.multiple_of(n_cmp - qi * (tq // CMP_STRIDE), SUBLANES)
    o_cmp_t = []
    raw = [_dot_nt(kc, qz_ref[g]) for g in range(NSA_KV_GROUPS)]
    for g in range(NSA_KV_GROUPS):
        bias = jnp.concatenate([bc_ref[g * hpg + h, pl.ds(cmp_row0, n_cmp), :]
                                for h in range(hpg)], axis=1)
        sc = raw[g] + bias
        mx = jnp.max(sc, axis=0, keepdims=True)
        e = jnp.exp2(sc - mx)
        den = jnp.sum(e, axis=0, keepdims=True)
        pc = e * jnp.where(mx > 0.5 * NEG, 1.0 / den, 0.0)
        psum = pc[:, 0:tq]
        for h in range(1, hpg):
            psum = psum + pc[:, h * tq:(h + 1) * tq]
        psum_ref[g] = psum
        o_cmp_t.append(_dot(vct[g * HEAD_DIM:(g + 1) * HEAD_DIM], pc.astype(_BF16)))
    for h in range(hpg):
        osum_ref[h] = gated(0, h, o_cmp_t)

    rank_from = SLC_TOP * SLC_BLOCK // tq

    @pl.when(qi < rank_from)
    def _():
        for g in range(NSA_KV_GROUPS):
            qsel_ref[g] = qz_ref[g]

    @pl.when(qi >= rank_from)
    def _():
        n_sel = LANES // NSA_KV_GROUPS
        n_grp = n_sel // SUBLANES
        blk = lax.broadcasted_iota(jnp.int32, (n_sel, tq), 0)
        cur = (qi * tq + lax.broadcasted_iota(jnp.int32, (n_sel, tq), 1)) >> int(math.log2(SLC_BLOCK))
        forced = (blk == 0) | (blk == cur) | (blk == cur - 1)
        sub = lax.broadcasted_iota(jnp.int32, (SUBLANES, tq), 0)
        neg_t = []
        for g in range(NSA_KV_GROUPS):
            p = psum_ref[g]
            p1 = p.astype(_BF16)
            r1 = p - p1.astype(_F32)
            p2 = r1.astype(_BF16)
            p3 = (r1 - p2.astype(_F32)).astype(_BF16)
            a = _dot(ov_ref[...], jnp.concatenate([p1, p2, p3], axis=0))
            a = jnp.where(forced, FORCE_SCORE, a)
            a = jnp.where(blk > cur, -FORCE_SCORE, a)
            a_grp = [a[r * SUBLANES:(r + 1) * SUBLANES] for r in range(n_grp)]
            cnt = [jnp.zeros((SUBLANES, tq), _F32) for _ in range(n_grp)]
            for j in range(n_sel):
                rj = jnp.broadcast_to(a[j:j + 1], (SUBLANES, tq))
                jr, jo = divmod(j, SUBLANES)
                for r in range(n_grp):
                    if r > jr:
                        one = jnp.where(rj >= a_grp[r], 1.0, 0.0)
                    elif r < jr:
                        one = jnp.where(rj > a_grp[r], 1.0, 0.0)
                    else:
                        tie = jnp.where(sub > jo, jnp.where(rj == a_grp[r], 1.0, 0.0), 0.0)
                        one = jnp.where(rj > a_grp[r], 1.0, tie)
                    cnt[r] = cnt[r] + one
            neg_t.append(jnp.where(jnp.concatenate(cnt, axis=0) < float(SLC_TOP), 0.0, NEG))
        selneg = jnp.concatenate(neg_t[::-1], axis=0).T.astype(_BF16)
        selneg4 = jnp.concatenate([selneg] * hpg, axis=0)
        for g in range(NSA_KV_GROUPS):
            qsel_ref[g] = jnp.where(half[g], q4, selneg4)

    SEL, WIN = 0, 1

    def reset(st):
        m_ref[st] = jnp.full(m_ref.shape[1:], 2.0 * NEG, _F32)
        acc_ref[st] = jnp.zeros(acc_ref.shape[1:], _F32)

    def chunk(q_src, k_ref, vt_ref, start, tk, onehot, extras, st):
        s = scores(q_src, k_ref, start, tk, onehot)
        consume(lambda g, c: s[g][c * TK:(c + 1) * TK], vt_ref, start, tk, extras, st)

    def consume(block_of, vt_ref, start, tk, extras, st, limit=None):
        first = start // TK if isinstance(start, int) else start >> int(math.log2(TK))
        vt = jnp.concatenate([vt_ref[0, 0, first + j] for j in range(tk // TK)], axis=1)
        ones_rows = jnp.ones((SUBLANES, tk), _BF16)
        key_pos = start + lax.broadcasted_iota(jnp.int32, (HEAD_DIM + SUBLANES, tk), 1)
        for g in range(NSA_KV_GROUPS):
            vth = jnp.concatenate([vt[g * HEAD_DIM:(g + 1) * HEAD_DIM], ones_rows], axis=0)
            if limit is not None:
                vth = jnp.where(key_pos < limit, vth, jnp.zeros_like(vth))
            blocks = [block_of(g, c) for c in range(tk // TK)]
            for c, extra in extras.items():
                blocks[c] = blocks[c] + extra(g)
            mx = blocks[0]
            for blk_s in blocks[1:]:
                mx = jnp.maximum(mx, blk_s)
            m_old = m_ref[st, g]
            m_new = jnp.maximum(m_old, jnp.max(mx, axis=0, keepdims=True))
            m_row = m_new[0:1]
            pt = jnp.concatenate([jnp.exp2(blk_s - m_row).astype(_BF16) for blk_s in blocks], axis=0)
            acc_ref[st, g] = jnp.exp2(m_old - m_new)[0:1] * acc_ref[st, g] + _dot(vth, pt)
            m_ref[st, g] = m_new

    def finish(c, st):
        outs = []
        for g in range(NSA_KV_GROUPS):
            den = acc_ref[st, g, HEAD_DIM:HEAD_DIM + 1]
            outs.append(acc_ref[st, g, :HEAD_DIM] * jnp.where(den > 0.0, 1.0 / den, 0.0))
        for h in range(hpg):
            osum_ref[h] += gated(c, h, outs)

    def near_bias(d):
        return lambda g: bd_ref[g, d]

    def short_path(q_src, k_ref, v_ref, onehot, st):
        def far(kt, carry):
            chunk(q_src, k_ref, v_ref, pl.multiple_of(kt * TK, TK), TK, onehot, {}, st)
            return carry

        lax.fori_loop(0, jnp.maximum(qi - 1, 0), far, 0)

        @pl.when(qi >= 1)
        def _():
            chunk(q_src, k_ref, v_ref, pl.multiple_of((qi - 1) * TK, TK), TK, onehot,
                  {0: near_bias(1)}, st)

        chunk(q_src, k_ref, v_ref, pl.multiple_of(qi * TK, TK), TK, onehot, {0: near_bias(0)}, st)

    reset(SEL)
    reset(WIN)
    big = 4 * TK
    n_win = WINDOW // TK
    main_from = max(big // TK - 1, n_win)
    win_mask = jnp.where(lax.broadcasted_iota(jnp.int32, (TK, rows), 0)
                         > (lax.broadcasted_iota(jnp.int32, (TK, rows), 1) & (tq - 1)), 0.0, NEG)

    @pl.when(qi < main_from)
    def _():
        short_path(qsel_ref, ks_ref, vs_ref, True, SEL)
        short_path(qz_ref, kw_ref, vw_ref, False, WIN)

    @pl.when(qi >= main_from)
    def _():
        last = pl.multiple_of((qi - (big // TK - 1)) * TK, TK)
        n_far = (qi - (big // TK - 1) + big // TK - 1) >> int(math.log2(big // TK))
        near = {big // TK - 2: near_bias(1), big // TK - 1: near_bias(0)}

        def start_of(k):
            return pl.multiple_of(jnp.where(k < n_far, k * big, last), TK)

        def put(k, slot):
            for g, s in enumerate(scores(qsel_ref, ks_ref, start_of(k), big, True)):
                s_ref[slot, g] = s

        def take(k, slot, extras, limit):
            consume(lambda g, c: s_ref[slot, g, c * TK:(c + 1) * TK, :], vs_ref,
                    start_of(k), big, extras, SEL, limit)

        win_start = pl.multiple_of((qi - n_win) * TK, TK)
        for g, s in enumerate(scores(qz_ref, kw_ref, win_start, WINDOW + TK, False)):
            ws_ref[g] = s
        put(0, 0)
        consume(lambda g, c: ws_ref[g, c * TK:(c + 1) * TK, :], vw_ref, win_start, WINDOW + TK,
                {0: lambda g: win_mask, n_win - 1: near_bias(1), n_win: near_bias(0)}, WIN)
        n_pair = n_far >> 1

        def far_pair(i, carry):
            k = 2 * i
            put(k + 1, 1)
            take(k, 0, {}, last)
            put(k + 2, 0)
            take(k + 1, 1, {}, last)
            return carry

        lax.fori_loop(0, n_pair, far_pair, 0)

        @pl.when(n_far == 2 * n_pair)
        def _():
            take(n_far, 0, near, None)

        @pl.when(n_far != 2 * n_pair)
        def _():
            put(n_far, 1)
            take(n_far - 1, 0, {}, last)
            take(n_far, 1, near, None)

    finish(1, SEL)
    finish(2, WIN)

    for h in range(hpg):
        o_ref[0, h] = (osum_ref[h].T * _silu(z_ref[0, h])).astype(o_ref.dtype)


def _nsa_attention(pb3, pf3, vt, kvc, kvc_t, bias_c, bias_d, ov):
    b, _, t, _ = pb3.shape
    n_cmp = kvc.shape[2]
    n_blk = N_HEADS_NSA // 2
    full = lambda shape: pl.BlockSpec(shape, lambda i, q: (0,) * len(shape))
    kv_spec = lambda col: pl.BlockSpec((1, 1, t, LANES), lambda i, q: (i, col, 0, 0))
    vt_spec = lambda blk: pl.BlockSpec((1, 1, t // TK, LANES, TK), lambda i, q: (i, blk, 0, 0, 0))
    return pl.pallas_call(
        _nsa_kernel,
        grid=(b, t // TQ_NSA),
        in_specs=[pl.BlockSpec((1, n_blk, TQ_NSA, LANES), lambda i, q: (i, PB_QB // n_blk, q, 0)),
                  pl.BlockSpec((1, 1, n_cmp, LANES), lambda i, q: (i, 0, 0, 0)),
                  pl.BlockSpec((1, 1, LANES, n_cmp), lambda i, q: (i, 1, 0, 0)),
                  kv_spec(PB_KS), vt_spec(VT_VS), kv_spec(PB_KW), vt_spec(VT_VW),
                  pl.BlockSpec((1, 1, TQ_NSA, LANES), lambda i, q: (i, PF_GB, q, 0)),
                  pl.BlockSpec((1, n_blk, TQ_NSA, LANES), lambda i, q: (i, PF_ZB // n_blk, q, 0)),
                  full(bias_c.shape),
                  full(bias_d.shape), full(ov.shape)],
        out_specs=pl.BlockSpec((1, n_blk, TQ_NSA, LANES), lambda i, q: (i, 0, q, 0)),
        out_shape=jax.ShapeDtypeStruct((b, n_blk, t, LANES), _BF16),
        scratch_shapes=[pltpu.VMEM((2, NSA_KV_GROUPS, SUBLANES, NSA_HPG * TQ_NSA), _F32),
                        pltpu.VMEM((2, NSA_KV_GROUPS, HEAD_DIM + SUBLANES, NSA_HPG * TQ_NSA), _F32),
                        pltpu.VMEM((n_blk, TQ_NSA, LANES), _F32),
                        pltpu.VMEM((NSA_KV_GROUPS, NSA_HPG * TQ_NSA, LANES), _BF16),
                        pltpu.VMEM((NSA_KV_GROUPS, NSA_HPG * TQ_NSA, LANES), _BF16),
                        pltpu.VMEM((LANES, TQ_NSA), _F32),
                        pltpu.VMEM((NSA_KV_GROUPS, n_cmp, TQ_NSA), _F32),
                        pltpu.VMEM((2, NSA_KV_GROUPS, 4 * TK, NSA_HPG * TQ_NSA), _F32),
                        pltpu.VMEM((NSA_KV_GROUPS, WINDOW + TK, NSA_HPG * TQ_NSA), _F32)],
        compiler_params=_cparams(("arbitrary", "arbitrary")),
        name="native_sparse_attention",
    )(pb3, kvc, kvc_t, pb3, vt, pb3, vt, pf3, pf3, bias_c, bias_d, ov)


def _out_kernel(x_ref, oa_ref, ob_ref, oc_ref, w_ref, g_ref, o_ref, *, final_norm):
    x = x_ref[...]
    row0 = 0
    for o_ref_in in (oa_ref, ob_ref, oc_ref):
        for j in range(o_ref_in.shape[1]):
            x = x + _dot(o_ref_in[0, j], w_ref[row0:row0 + LANES])
            row0 += LANES
    if final_norm:
        x = x * lax.rsqrt(jnp.mean(x * x, axis=-1, keepdims=True) + RMS_EPS) * g_ref[...]
    o_ref[...] = x


def _out_proj(x2, oa, ob, oc, w, g, final_norm):
    n = x2.shape[0]
    per_seq = oa.shape[2] // TM_PROJ
    row = lambda width: pl.BlockSpec((TM_PROJ, width), lambda i: (i, 0))
    blocks = lambda a: pl.BlockSpec((1, a.shape[1], TM_PROJ, LANES),
                                    lambda i: (i // per_seq, 0, i % per_seq, 0))
    return pl.pallas_call(
        functools.partial(_out_kernel, final_norm=final_norm),
        grid=(n // TM_PROJ,),
        in_specs=[row(D_MODEL), blocks(oa), blocks(ob), blocks(oc),
                  pl.BlockSpec((D_MODEL, D_MODEL), lambda i: (0, 0)),
                  pl.BlockSpec((1, D_MODEL), lambda i: (0, 0))],
        out_specs=row(D_MODEL),
        out_shape=jax.ShapeDtypeStruct((n, D_MODEL), _F32),
        compiler_params=_cparams(("arbitrary",)),
        name="out_proj_residual",
    )(x2, oa, ob, oc, w, g)


def _head_perm_cols(width_per_head, order):
    return np.concatenate([np.arange(h * width_per_head, (h + 1) * width_per_head) for h in order])


def _layout_w_in(w):
    widths = [256, 256, 256, 256, 512, 128, 128, 128, 128, 128, 128, 24, 512, 256, 256, 256, 4, 256]
    offs = np.concatenate([[0], np.cumsum(widths)])
    (qa, ka, va, za, qb, kc, vc, ks, vs, kw, vw, gb, zb, qc, kf, vf, fc, zc) = [
        w[:, offs[i]:offs[i + 1]] for i in range(len(widths))]
    scale = HEAD_DIM ** -0.5
    perm = _head_perm_cols(HEAD_DIM, NSA_HEAD_ORDER)
    pad = lambda a: jnp.pad(a, ((0, 0), (0, LANES - a.shape[1])))
    wb = jnp.concatenate([qb[:, perm] * (scale * LOG2E), qa * (-scale), ka, va, ks, kw,
                          qc * (scale * LOG2E), kf], axis=1)
    wf = jnp.concatenate([zb[:, perm], za, zc, pad(gb), pad(fc)], axis=1)
    wc = jnp.concatenate([kc, vc], axis=1)
    wvt = jnp.concatenate([vs, vw, vf], axis=1)
    return wb.astype(_BF16), wf.astype(_BF16), wc.astype(_BF16), wvt.astype(_BF16)


def _static_tables(t):
    tq = TQ_NSA
    n_cmp_pad = t // CMP_STRIDE
    j = np.arange(n_cmp_pad)
    rel = np.arange(2 * n_cmp_pad) - n_cmp_pad
    dist_c = np.arange(tq)[None, :] - (rel[:, None] * CMP_STRIDE + CMP_BLOCK - 1)
    i_, j_ = np.arange(tq)[:, None], np.arange(TK)[None, :]
    dist_d = np.concatenate([d * TK + i_ - j_ for d in range(3)], axis=0)
    n_slc = LANES // NSA_KV_GROUPS
    cmp_start = j * CMP_STRIDE
    cmp_end = cmp_start + CMP_BLOCK - 1
    slc_start = np.arange(n_slc) * SLC_BLOCK
    ov1 = np.clip(np.minimum(cmp_end[:, None], slc_start[None, :] + SLC_BLOCK - 1)
                  - np.maximum(cmp_start[:, None], slc_start[None, :]) + 1, 0, None) / CMP_BLOCK
    ov1[n_cmp_pad - 1:] = 0.0
    ov1[:, t // SLC_BLOCK:] = 0.0
    ov3 = np.concatenate([ov1.T, ov1.T, ov1.T], axis=1)
    return jnp.asarray(dist_c, jnp.int32), jnp.asarray(dist_d, jnp.int32), jnp.asarray(ov3, _BF16)


def kernel(x, norm_g, w_in, w_out, forget_b, cmp_w1, cmp_b1, cmp_w2, cmp_pe, rel_bias, final_g):
    b, t, d = x.shape
    depth = norm_g.shape[0]
    assert d == D_MODEL and t % TM_PROJ == 0 and t % (CMP_STRIDE * LANES) == 0
    assert t // SLC_BLOCK <= LANES // NSA_KV_GROUPS and TQ_NSA == TK
    n_cmp_pad = t // CMP_STRIDE

    dist_c, dist_d, ov3 = _static_tables(t)
    bias_c = _bias_table(rel_bias * LOG2E, dist_c, 32)
    bias_d = _bias_table(rel_bias * LOG2E, dist_d, 32).reshape(N_HEADS_NSA, 3, TQ_NSA, TK)
    bias_d = bias_d[:, :2] - bias_d[:, 2:3]
    bias_d = bias_d.reshape(NSA_KV_GROUPS, NSA_HPG, 2, TQ_NSA, TK).transpose(0, 2, 4, 1, 3).reshape(
        NSA_KV_GROUPS, 2, TK, NSA_HPG * TQ_NSA)

    perm_rows = _head_perm_cols(HEAD_DIM, NSA_HEAD_ORDER)
    x2 = x.reshape(b * t, d)
    out = None
    for l in range(depth):
        wb, wf, wc, wvt = _layout_w_in(w_in[l])
        pb, pf, pc, vt = _proj(x2, norm_g[l].reshape(1, d), wb, wf, wc, wvt, t)
        pb3, pf3 = pb, pf

        fb_row = jnp.pad(forget_b[l], (0, LANES - N_HEADS_FOX)).reshape(1, LANES)
        augq, augk = _fgate(pf3, fb_row)

        halves = pc.reshape(b, 2, NSA_KV_GROUPS, n_cmp_pad, CMP_STRIDE * HEAD_DIM)
        w2 = cmp_w2[l]
        zeros = jnp.zeros_like(w2)
        w2p = jnp.stack([jnp.concatenate([w2, zeros], axis=-1),
                         jnp.concatenate([zeros, w2], axis=-1)], axis=1).astype(_BF16)
        kvc, kvc_t = _compress(halves, cmp_w1[l].astype(_BF16),
                               cmp_pe[l].reshape(2, 1, CMP_BLOCK * HEAD_DIM).astype(_BF16),
                               cmp_b1[l].reshape(2, 1, CMP_HIDDEN), w2p)

        o_a = _sb_attention(pb3, pf3)
        o_b = _nsa_attention(pb3, pf3, vt, kvc, kvc_t, bias_c, bias_d, ov3)
        o_c = _fox_attention(pb3, pf3, vt, augq, augk)

        wo = w_out[l]
        wo = jnp.concatenate([wo[:N_HEADS_SB * HEAD_DIM],
                              wo[N_HEADS_SB * HEAD_DIM:][:N_HEADS_NSA * HEAD_DIM][perm_rows],
                              wo[(N_HEADS_SB + N_HEADS_NSA) * HEAD_DIM:]], axis=0).astype(_BF16)
        last = l == depth - 1
        x2 = _out_proj(x2, o_a, o_b, o_c, wo, final_g.reshape(1, d), last)
    return x2.reshape(b, t, d)
```

```python
import functools
import math

import jax
import jax.numpy as jnp
import numpy as np
from jax import lax
from jax.experimental import pallas as pl
from jax.experimental.pallas import tpu as pltpu

D_MODEL = 1024
HEAD_DIM = 64
N_HEADS_SB = 4
N_HEADS_FOX = 4
N_HEADS_NSA = 8
NSA_KV_GROUPS = 2
NSA_HPG = N_HEADS_NSA // NSA_KV_GROUPS
CMP_BLOCK = 32
CMP_STRIDE = 16
CMP_HIDDEN = 256
SLC_BLOCK = 64
SLC_TOP = 16
WINDOW = 512
REL_BUCKETS = 32
REL_MAX_DIST = 128
FORCE_SCORE = 1e4
RMS_EPS = 1e-6
NEG = -1e30
LOG2E = math.log2(math.e)

LANES = 128
SUBLANES = 8
VMEM_LIMIT = 56 * 1024 * 1024

TM_PROJ = 512
TQ_SB = 512
TQ_FOX = 512
FOX_PARTS = 2
TQ_NSA = 128
TK = 128

PB_QB, PB_QA, PB_KA, PB_VA = 0, 4, 6, 8
PB_KS, PB_KW = 10, 11
PB_QC, PB_KF = 12, 14
PB_BLOCKS = 16
N_CMP_SLABS = 2 * NSA_KV_GROUPS
VT_VS, VT_VW, VT_VF = 0, 1, 2
N_VT = 4
PF_ZB, PF_ZA, PF_ZC, PF_GB, PF_FC = 0, 4, 6, 8, 9
PF_BLOCKS = 10

NSA_HEAD_ORDER = [0, 4, 1, 5, 2, 6, 3, 7]

_F32 = jnp.float32
_BF16 = jnp.bfloat16


def _cparams(sem):
    return pltpu.CompilerParams(dimension_semantics=sem, vmem_limit_bytes=VMEM_LIMIT)


def _dot(a, b):
    return jnp.dot(a, b, preferred_element_type=_F32)


def _dot_nt(a, b):
    return lax.dot_general(a, b, (((1,), (1,)), ((), ())), preferred_element_type=_F32)


def _split2(x):
    hi = x.astype(_BF16)
    lo = (x - hi.astype(_F32)).astype(_BF16)
    return jnp.concatenate([hi, lo], axis=1)


def _split3(x):
    h1 = x.astype(_BF16)
    r1 = x - h1.astype(_F32)
    h2 = r1.astype(_BF16)
    h3 = (r1 - h2.astype(_F32)).astype(_BF16)
    return jnp.concatenate([h1, h2, h3], axis=1)


def _sigmoid(x):
    return 1.0 / (1.0 + jnp.exp(-x))


def _silu(x):
    return x * _sigmoid(x)


def _rel_bucket_np(n):
    n = np.maximum(n, 0)
    max_exact = REL_BUCKETS // 2
    nf = np.maximum(n, 1).astype(np.float64)
    large = max_exact + (np.log(nf / max_exact) / math.log(REL_MAX_DIST / max_exact)
                         * (REL_BUCKETS - max_exact)).astype(np.int64)
    large = np.minimum(large, REL_BUCKETS - 1)
    return np.where(n < max_exact, n, large)


def _bucket_thresholds():
    n = np.arange(0, 4 * REL_MAX_DIST)
    bk = _rel_bucket_np(n)
    assert np.all(np.diff(bk) >= 0) and bk[-1] == REL_BUCKETS - 1
    return [int(np.argmax(bk >= b)) for b in range(REL_BUCKETS)]


_BUCKET_THR = _bucket_thresholds()


def _bias_kernel(tab_ref, dist_ref, o_ref):
    n = dist_ref[...]
    acc = [jnp.full(n.shape, tab_ref[0, h], _F32) for h in range(N_HEADS_NSA)]
    for b in range(1, REL_BUCKETS):
        ge = n >= _BUCKET_THR[b]
        for h in range(N_HEADS_NSA):
            acc[h] = jnp.where(ge, tab_ref[b, h], acc[h])
    valid = n >= 0
    for h in range(N_HEADS_NSA):
        o_ref[h] = jnp.where(valid, acc[h], NEG)


def _bias_table(rel_bias, dist, rows):
    n_rows, n_cols = dist.shape
    return pl.pallas_call(
        _bias_kernel,
        grid=(n_rows // rows,),
        in_specs=[pl.BlockSpec(memory_space=pltpu.SMEM),
                  pl.BlockSpec((rows, n_cols), lambda i: (i, 0))],
        out_specs=pl.BlockSpec((N_HEADS_NSA, rows, n_cols), lambda i: (0, i, 0)),
        out_shape=jax.ShapeDtypeStruct((N_HEADS_NSA, n_rows, n_cols), _F32),
        compiler_params=_cparams(("arbitrary",)),
        name="rel_bias_table",
    )(rel_bias, dist)


def _proj_kernel(x_ref, g_ref, wb_ref, wf_ref, wc_ref, wvt_ref, pb_ref, pf_ref, pc_ref, vt_ref):
    x = x_ref[...]
    y = x * lax.rsqrt(jnp.mean(x * x, axis=-1, keepdims=True) + RMS_EPS)
    h = (y * g_ref[...]).astype(_BF16)
    v_all = _dot(h, wvt_ref[...])
    for j in range(N_VT):
        v_t = v_all[:, j * LANES:(j + 1) * LANES].T.astype(_BF16)
        for c in range(TM_PROJ // TK):
            vt_ref[0, j, c] = v_t[:, c * TK:(c + 1) * TK]
    kv_cmp = _dot(h, wc_ref[...]).astype(_BF16)
    for s in range(N_CMP_SLABS):
        pc_ref[0, s] = kv_cmp[:, s * HEAD_DIM:(s + 1) * HEAD_DIM]
    chunk = 4 * LANES
    for c in range(0, PB_BLOCKS * LANES, chunk):
        w = min(chunk, PB_BLOCKS * LANES - c)
        res = _dot(h, wb_ref[:, c:c + w]).astype(_BF16)
        for j in range(w // LANES):
            pb_ref[0, c // LANES + j] = res[:, j * LANES:(j + 1) * LANES]
    for c in range(0, PF_BLOCKS * LANES, chunk):
        w = min(chunk, PF_BLOCKS * LANES - c)
        res = _dot(h, wf_ref[:, c:c + w])
        for j in range(w // LANES):
            pf_ref[0, c // LANES + j] = res[:, j * LANES:(j + 1) * LANES]


def _proj(x2, g, wb, wf, wc, wvt, t):
    n = x2.shape[0]
    per_seq = t // TM_PROJ
    return pl.pallas_call(
        _proj_kernel,
        grid=(n // TM_PROJ,),
        in_specs=[pl.BlockSpec((TM_PROJ, D_MODEL), lambda i: (i, 0)),
                  pl.BlockSpec((1, D_MODEL), lambda i: (0, 0)),
                  pl.BlockSpec((D_MODEL, PB_BLOCKS * LANES), lambda i: (0, 0)),
                  pl.BlockSpec((D_MODEL, PF_BLOCKS * LANES), lambda i: (0, 0)),
                  pl.BlockSpec((D_MODEL, N_CMP_SLABS * HEAD_DIM), lambda i: (0, 0)),
                  pl.BlockSpec((D_MODEL, N_VT * LANES), lambda i: (0, 0))],
        out_specs=[pl.BlockSpec((1, PB_BLOCKS, TM_PROJ, LANES), lambda i: (i // per_seq, 0, i % per_seq, 0)),
                   pl.BlockSpec((1, PF_BLOCKS, TM_PROJ, LANES), lambda i: (i // per_seq, 0, i % per_seq, 0)),
                   pl.BlockSpec((1, N_CMP_SLABS, TM_PROJ, HEAD_DIM),
                                lambda i: (i // per_seq, 0, i % per_seq, 0)),
                   pl.BlockSpec((1, N_VT, TM_PROJ // TK, LANES, TK),
                                lambda i: (i // per_seq, 0, i % per_seq, 0, 0))],
        out_shape=[jax.ShapeDtypeStruct((n // t, PB_BLOCKS, t, LANES), _BF16),
                   jax.ShapeDtypeStruct((n // t, PF_BLOCKS, t, LANES), _F32),
                   jax.ShapeDtypeStruct((n // t, N_CMP_SLABS, t, HEAD_DIM), _BF16),
                   jax.ShapeDtypeStruct((n // t, N_VT, t // TK, LANES, TK), _BF16)],
        compiler_params=_cparams(("arbitrary",)),
        name="rmsnorm_in_proj",
    )(x2, g, wb, wf, wc, wvt)


N_SPLIT = 3


def _fgate_tables():
    n_pairs = N_HEADS_FOX // 2
    pq = np.zeros((N_SPLIT * LANES, n_pairs * LANES), np.float32)
    pk = np.zeros_like(pq)
    ones_q = np.zeros((1, n_pairs * LANES), np.float32)
    ones_k = np.zeros_like(ones_q)
    for head in range(N_HEADS_FOX):
        pair, slot = divmod(head, 2)
        base = pair * LANES + (HEAD_DIM if slot == 0 else 0)
        for j in range(N_SPLIT):
            pq[j * LANES + head, base + j] = 1.0
            pk[j * LANES + head, base + N_SPLIT + j] = 1.0
        ones_q[0, base + N_SPLIT:base + 2 * N_SPLIT] = 1.0
        ones_k[0, base:base + N_SPLIT] = 1.0
    return (jnp.asarray(pq, _BF16), jnp.asarray(pk, _BF16),
            jnp.asarray(ones_q), jnp.asarray(ones_k))


def _fgate_kernel(fc_ref, fb_ref, pq_ref, pk_ref, oq_ref, ok_ref, augq_ref, augk_ref):
    t = fc_ref.shape[2]
    z = fc_ref[0, 0] + fb_ref[...]
    logf = jnp.minimum(z, 0.0) - jnp.log1p(jnp.exp(-jnp.abs(z)))
    row = lax.broadcasted_iota(jnp.int32, (t, LANES), 0)
    c = logf
    shift = 1
    while shift < t:
        c = c + jnp.where(row >= shift, pltpu.roll(c, shift, axis=0), 0.0)
        shift *= 2
    c3 = _split3(c * LOG2E)
    aq = _dot(c3, pq_ref[...]) + oq_ref[...]
    ak = ok_ref[...] - _dot(c3, pk_ref[...])
    for p in range(N_HEADS_FOX // 2):
        augq_ref[0, p] = aq[:, p * LANES:(p + 1) * LANES].astype(_BF16)
        augk_ref[0, p] = ak[:, p * LANES:(p + 1) * LANES].astype(_BF16)


def _fgate(pf3, fb_row):
    b, _, t, _ = pf3.shape
    n_pairs = N_HEADS_FOX // 2
    tables = _fgate_tables()
    full = lambda a: pl.BlockSpec(a.shape, lambda i: (0,) * a.ndim)
    aug = lambda: pl.BlockSpec((1, n_pairs, t, LANES), lambda i: (i, 0, 0, 0))
    return pl.pallas_call(
        _fgate_kernel,
        grid=(b,),
        in_specs=[pl.BlockSpec((1, 1, t, LANES), lambda i: (i, PF_FC, 0, 0)),
                  pl.BlockSpec((1, LANES), lambda i: (0, 0))] + [full(a) for a in tables],
        out_specs=[aug(), aug()],
        out_shape=[jax.ShapeDtypeStruct((b, n_pairs, t, LANES), _BF16),
                   jax.ShapeDtypeStruct((b, n_pairs, t, LANES), _BF16)],
        compiler_params=_cparams(("arbitrary",)),
        name="forget_gate_cumsum",
    )(pf3, fb_row, *tables)


def _compress_kernel(x_ref, w1_ref, pe_ref, b1_ref, w2_ref, o_ref, ot_ref):
    nc = x_ref.shape[3]
    half = CMP_STRIDE * HEAD_DIM
    w1 = w1_ref[0]
    c1 = _dot(jnp.broadcast_to(pe_ref[0], (SUBLANES, 2 * half)), w1)[0:1] + b1_ref[0]
    out = jnp.zeros((nc, LANES), _F32)
    for g in range(NSA_KV_GROUPS):
        xg = x_ref[0, 0, g]
        a = _dot(xg, w1[:half])
        bb = _dot(xg, w1[half:])
        h = a + pltpu.roll(bb, nc - 1, axis=0) + c1
        out = out + _dot(_silu(h).astype(_BF16), w2_ref[0, g])
    o_ref[0, 0] = out.astype(_BF16)
    ot_ref[0, 0] = out.T.astype(_BF16)


def _compress(halves, w1, pe, b1, w2p):
    b, _, g, nc, width = halves.shape
    return pl.pallas_call(
        _compress_kernel,
        grid=(b, 2),
        in_specs=[pl.BlockSpec((1, 1, g, nc, width), lambda i, k: (i, k, 0, 0, 0)),
                  pl.BlockSpec((1, 2 * width, CMP_HIDDEN), lambda i, k: (k, 0, 0)),
                  pl.BlockSpec((1, 1, 2 * width), lambda i, k: (k, 0, 0)),
                  pl.BlockSpec((1, 1, CMP_HIDDEN), lambda i, k: (k, 0, 0)),
                  pl.BlockSpec((1, g, CMP_HIDDEN, LANES), lambda i, k: (k, 0, 0, 0))],
        out_specs=[pl.BlockSpec((1, 1, nc, LANES), lambda i, k: (i, k, 0, 0)),
                   pl.BlockSpec((1, 1, LANES, nc), lambda i, k: (i, k, 0, 0))],
        out_shape=[jax.ShapeDtypeStruct((b, 2, nc, LANES), _BF16),
                   jax.ShapeDtypeStruct((b, 2, LANES, nc), _BF16)],
        compiler_params=_cparams(("arbitrary", "arbitrary")),
        name="nsa_compress",
    )(halves, w1, pe, b1, w2p)


def _sb_kernel(q_ref, k_ref, v_ref, z_ref, o_ref, qh_ref, c_ref, acc_ref):
    qi = pl.program_id(2)
    tq = q_ref.shape[2]
    lane = lax.broadcasted_iota(jnp.int32, (tq, LANES), 1)
    q2 = q_ref[0, 0]
    qh = [jnp.where(lane < HEAD_DIM, q2, jnp.zeros_like(q2)),
          jnp.where(lane >= HEAD_DIM, q2, jnp.zeros_like(q2))]
    r_i = lax.broadcasted_iota(jnp.int32, (2 * TK, 2 * TK), 0)
    c_i = lax.broadcasted_iota(jnp.int32, (2 * TK, 2 * TK), 1)
    uu = jnp.where(r_i >= c_i, 1.0, 0.0).astype(_BF16)
    c_ref[...] = jnp.zeros_like(c_ref)
    acc_ref[...] = jnp.zeros_like(acc_ref)
    for h in range(2):
        qh_ref[h] = qh[h]
    row = lax.broadcasted_iota(jnp.int32, (tq, LANES), 0)
    n_blocks = tq // TK

    def chunk(start, diagonal):
        k2 = k_ref[0, 0, pl.ds(start, tq), :]
        v2 = v_ref[0, 0, pl.ds(start, tq), :]
        for h in range(2):
            s = _dot_nt(qh_ref[h], k2)
            carry = c_ref[h]
            w_blocks = [None] * n_blocks
            for c in reversed(range(0, n_blocks, 2)):
                nz, l1m, mask = [], [], []
                for cc in (c, c + 1):
                    nzc = s[:, cc * TK:(cc + 1) * TK]
                    neg_abs = lax.bitcast_convert_type(
                        lax.bitcast_convert_type(nzc, jnp.uint32) | jnp.uint32(0x80000000), _F32)
                    lc = jnp.minimum(nzc, 0.0) - jnp.log(1.0 + jnp.exp(neg_abs))
                    if diagonal:
                        mask.append(lane + cc * TK < row)
                        lc = jnp.where(mask[-1], lc, 0.0)
                    nz.append(nzc)
                    l1m.append(lc.astype(_BF16))
                rc = _dot(jnp.concatenate(l1m, axis=1), uu)
                for i, cc in enumerate((c, c + 1)):
                    w = jnp.exp((rc[:, i * TK:(i + 1) * TK] + carry) - nz[i])
                    if diagonal:
                        w = jnp.where(mask[i], w, 0.0)
                    w_blocks[cc] = w.astype(_BF16)
                carry = carry + jnp.broadcast_to(rc[:, 0:1], carry.shape)
            acc_ref[h] += _dot(jnp.concatenate(w_blocks, axis=1), v2)
            c_ref[h] = carry

    chunk(pl.multiple_of(qi * tq, tq), True)

    def far_pair(it, carry):
        chunk(pl.multiple_of((qi - 1 - 2 * it) * tq, tq), False)
        chunk(pl.multiple_of((qi - 2 - 2 * it) * tq, tq), False)
        return carry

    lax.fori_loop(0, qi >> 1, far_pair, 0)

    @pl.when((qi & 1) == 1)
    def _():
        chunk(0, False)
    o = jnp.where(lane < HEAD_DIM, acc_ref[0], acc_ref[1])
    o_ref[0, 0] = (o * _silu(z_ref[0, 0])).astype(o_ref.dtype)


def _sb_attention(pb3, pf3):
    b, _, t, _ = pb3.shape
    n_pairs = N_HEADS_SB // 2
    return pl.pallas_call(
        _sb_kernel,
        grid=(b, n_pairs, t // TQ_SB),
        in_specs=[pl.BlockSpec((1, 1, TQ_SB, LANES), lambda i, p, q: (i, PB_QA + p, q, 0)),
                  pl.BlockSpec((1, 1, t, LANES), lambda i, p, q: (i, PB_KA + p, 0, 0)),
                  pl.BlockSpec((1, 1, t, LANES), lambda i, p, q: (i, PB_VA + p, 0, 0)),
                  pl.BlockSpec((1, 1, TQ_SB, LANES), lambda i, p, q: (i, PF_ZA + p, q, 0))],
        out_specs=pl.BlockSpec((1, 1, TQ_SB, LANES), lambda i, p, q: (i, p, q, 0)),
        out_shape=jax.ShapeDtypeStruct((b, n_pairs, t, LANES), _BF16),
        scratch_shapes=[pltpu.VMEM((2, TQ_SB, LANES), _BF16),
                        pltpu.VMEM((2, TQ_SB, LANES), _F32),
                        pltpu.VMEM((2, TQ_SB, LANES), _F32)],
        compiler_params=_cparams(("arbitrary", "arbitrary", "arbitrary")),
        name="stick_breaking_attention",
    )(pb3, pb3, pb3, pf3)


def _fox_kernel(q_ref, k_ref, vt_ref, augq_ref, augk_ref, z_ref, o_ref,
                qh_ref, m_ref, acc_ref, s_ref):
    qi = pl.program_id(2)
    tq = q_ref.shape[2]
    tk = tq
    lane = lax.broadcasted_iota(jnp.int32, (tq, LANES), 1)
    keep = [lane < HEAD_DIM, lane >= HEAD_DIM]
    q2 = q_ref[0, 0]
    aq = augq_ref[0, 0]
    for h in range(2):
        qh_ref[h] = jnp.where(keep[h], q2, aq)
    m_ref[...] = jnp.full_like(m_ref, NEG)
    acc_ref[...] = jnp.zeros_like(acc_ref)
    key_i = lax.broadcasted_iota(jnp.int32, (TK, tq), 0)
    qry_i = lax.broadcasted_iota(jnp.int32, (TK, tq), 1)

    def scores(c, slot):
        start = c * tk if isinstance(c, int) else pl.multiple_of(c * tk, tk)
        k2 = k_ref[0, 0, pl.ds(start, tk), :]
        ak = augk_ref[0, 0, pl.ds(start, tk), :]
        for h in range(2):
            s_ref[slot, h] = _dot_nt(jnp.where(keep[h], k2, ak), qh_ref[h])

    ones_rows = jnp.ones((SUBLANES, tk), _BF16)

    def consume(c, slot, diagonal):
        per = tk // TK
        vt = jnp.concatenate([vt_ref[0, 0, c * per + j] for j in range(per)], axis=1)
        for h in range(2):
            vth = jnp.concatenate([vt[h * HEAD_DIM:(h + 1) * HEAD_DIM], ones_rows], axis=0)
            for part in range(FOX_PARTS):
                js = range(part * per // FOX_PARTS, (part + 1) * per // FOX_PARTS)
                blocks = [s_ref[slot, h, j * TK:(j + 1) * TK, :] for j in js]
                if diagonal:
                    blocks = [jnp.where(key_i + j * TK <= qry_i, blk, NEG) for j, blk in zip(js, blocks)]
                mx = blocks[0]
                for blk in blocks[1:]:
                    mx = jnp.maximum(mx, blk)
                m_old = m_ref[h]
                m_new = jnp.maximum(m_old, jnp.max(mx, axis=0, keepdims=True))
                m_row = m_new[0:1]
                pt = jnp.concatenate([jnp.exp2(blk - m_row).astype(_BF16) for blk in blocks], axis=0)
                acc_ref[h] = (jnp.exp2(m_old - m_new)[0:1] * acc_ref[h]
                              + _dot(vth[:, js[0] * TK:(js[-1] + 1) * TK], pt))
                m_ref[h] = m_new

    scores(0, 0)
    n_pairs = qi >> 1

    def far_pair(i, carry):
        c = 2 * i
        scores(c + 1, 1)
        consume(c, 0, False)
        scores(c + 2, 0)
        consume(c + 1, 1, False)
        return carry

    lax.fori_loop(0, n_pairs, far_pair, 0)

    @pl.when(qi == 2 * n_pairs)
    def _():
        consume(qi, 0, True)

    @pl.when(qi != 2 * n_pairs)
    def _():
        scores(qi, 1)
        consume(qi - 1, 0, False)
        consume(qi, 1, True)
    o_t = jnp.concatenate([acc_ref[h, :HEAD_DIM] / acc_ref[h, HEAD_DIM:HEAD_DIM + 1]
                           for h in range(2)], axis=0)
    o_ref[0, 0] = (o_t.T * _silu(z_ref[0, 0])).astype(o_ref.dtype)


def _fox_attention(pb3, pf3, vt, augq, augk):
    b, _, t, _ = pb3.shape
    n_pairs = N_HEADS_FOX // 2
    return pl.pallas_call(
        _fox_kernel,
        grid=(b, n_pairs, t // TQ_FOX),
        in_specs=[pl.BlockSpec((1, 1, TQ_FOX, LANES), lambda i, p, q: (i, PB_QC + p, q, 0)),
                  pl.BlockSpec((1, 1, t, LANES), lambda i, p, q: (i, PB_KF + p, 0, 0)),
                  pl.BlockSpec((1, 1, t // TK, LANES, TK), lambda i, p, q: (i, VT_VF + p, 0, 0, 0)),
                  pl.BlockSpec((1, 1, TQ_FOX, LANES), lambda i, p, q: (i, p, q, 0)),
                  pl.BlockSpec((1, 1, t, LANES), lambda i, p, q: (i, p, 0, 0)),
                  pl.BlockSpec((1, 1, TQ_FOX, LANES), lambda i, p, q: (i, PF_ZC + p, q, 0))],
        out_specs=pl.BlockSpec((1, 1, TQ_FOX, LANES), lambda i, p, q: (i, p, q, 0)),
        out_shape=jax.ShapeDtypeStruct((b, n_pairs, t, LANES), _BF16),
        scratch_shapes=[pltpu.VMEM((2, TQ_FOX, LANES), _BF16),
                        pltpu.VMEM((2, SUBLANES, TQ_FOX), _F32),
                        pltpu.VMEM((2, HEAD_DIM + SUBLANES, TQ_FOX), _F32),
                        pltpu.VMEM((2, 2, TQ_FOX, TQ_FOX), _F32)],
        compiler_params=_cparams(("arbitrary", "arbitrary", "arbitrary")),
        name="forgetting_attention",
    )(pb3, pb3, vt, augq, augk, pf3)


def _nsa_kernel(q_ref, kc_ref, vct_ref, ks_ref, vs_ref, kw_ref, vw_ref, gl_ref, z_ref,
                bc_ref, bd_ref, ov_ref, o_ref,
                m_ref, acc_ref, osum_ref, qz_ref, qsel_ref, gates_ref, psum_ref, s_ref, ws_ref):
    qi = pl.program_id(1)
    tq = q_ref.shape[2]
    hpg = NSA_HPG
    rows = hpg * tq
    n_cmp = kc_ref.shape[2]
    lane = lax.broadcasted_iota(jnp.int32, (tq, LANES), 1)
    qrow = qi * tq + lax.broadcasted_iota(jnp.int32, (tq, LANES), 0)
    lane_r = lax.broadcasted_iota(jnp.int32, (rows, LANES), 1)
    irow_r = lax.broadcasted_iota(jnp.int32, (rows, LANES), 0) & (tq - 1)
    half = [lane_r < HEAD_DIM, lane_r >= HEAD_DIM]

    gates_ref[...] = _sigmoid(gl_ref[0, 0]).T

    def gated(c, h, per_group):
        parts = []
        for g, o in enumerate(per_group):
            r = 3 * (g * hpg + h) + c
            parts.append(gates_ref[r:r + 1, :] * o[:, h * tq:(h + 1) * tq])
        return jnp.concatenate(parts, axis=0)

    q4 = jnp.concatenate([q_ref[0, h] for h in range(hpg)], axis=0)
    for g in range(NSA_KV_GROUPS):
        qz_ref[g] = jnp.where(half[g], q4, jnp.zeros_like(q4))

    def scores(q_src, k_ref, start, tk, onehot):
        k2 = k_ref[0, 0, pl.ds(start, tk), :]
        lane_k = lax.broadcasted_iota(jnp.int32, (tk, LANES), 1)
        key_blk = (start + lax.broadcasted_iota(jnp.int32, (tk, LANES), 0)) >> int(math.log2(SLC_BLOCK))
        oh = jnp.where((lane_k & (SLC_BLOCK - 1)) == key_blk, 1.0, 0.0).astype(_BF16)
        out = []
        for g in range(NSA_KV_GROUPS):
            keep = (lane_k < HEAD_DIM) if g == 0 else (lane_k >= HEAD_DIM)
            out.append(_dot_nt(jnp.where(keep, k2, oh) if onehot else k2, q_src[g]))
        return out


    def heads_to_blocks(x0, x1):
        return [jnp.where(lane < HEAD_DIM, x0[h * tq:(h + 1) * tq], x1[h * tq:(h + 1) * tq])
                for h in range(hpg)]

    kc = kc_ref[0, 0]
    vct = vct_ref[0, 0]
    cmp_row0 = pl.multiple_of(n_cmp - qi * (tq // CMP_STRIDE), SUBLANES)
    o_cmp_t = []
    raw = [_dot_nt(kc, qz_ref[g]) for g in range(NSA_KV_GROUPS)]
    for g in range(NSA_KV_GROUPS):
        bias = jnp.concatenate([bc_ref[g * hpg + h, pl.ds(cmp_row0, n_cmp), :]
                                for h in range(hpg)], axis=1)
        sc = raw[g] + bias
        mx = jnp.max(sc, axis=0, keepdims=True)
        e = jnp.exp2(sc - mx)
        den = jnp.sum(e, axis=0, keepdims=True)
        pc = e * jnp.where(mx > 0.5 * NEG, 1.0 / den, 0.0)
        psum = pc[:, 0:tq]
        for h in range(1, hpg):
            psum = psum + pc[:, h * tq:(h + 1) * tq]
        psum_ref[g] = psum
        o_cmp_t.append(_dot(vct[g * HEAD_DIM:(g + 1) * HEAD_DIM], pc.astype(_BF16)))
    for h in range(hpg):
        osum_ref[h] = gated(0, h, o_cmp_t)

    rank_from = SLC_TOP * SLC_BLOCK // tq

    @pl.when(qi < rank_from)
    def _():
        for g in range(NSA_KV_GROUPS):
            qsel_ref[g] = qz_ref[g]

    @pl.when(qi >= rank_from)
    def _():
        n_sel = LANES // NSA_KV_GROUPS
        n_grp = n_sel // SUBLANES
        blk = lax.broadcasted_iota(jnp.int32, (n_sel, tq), 0)
        cur = (qi * tq + lax.broadcasted_iota(jnp.int32, (n_sel, tq), 1)) >> int(math.log2(SLC_BLOCK))
        forced = (blk == 0) | (blk == cur) | (blk == cur - 1)
        sub = lax.broadcasted_iota(jnp.int32, (SUBLANES, tq), 0)
        neg_t = []
        for g in range(NSA_KV_GROUPS):
            p = psum_ref[g]
            p1 = p.astype(_BF16)
            r1 = p - p1.astype(_F32)
            p2 = r1.astype(_BF16)
            p3 = (r1 - p2.astype(_F32)).astype(_BF16)
            a = _dot(ov_ref[...], jnp.concatenate([p1, p2, p3], axis=0))
            a = jnp.where(forced, FORCE_SCORE, a)
            a = jnp.where(blk > cur, -FORCE_SCORE, a)
            a_grp = [a[r * SUBLANES:(r + 1) * SUBLANES] for r in range(n_grp)]
            cnt = [jnp.zeros((SUBLANES, tq), _F32) for _ in range(n_grp)]
            for j in range(n_sel):
                rj = jnp.broadcast_to(a[j:j + 1], (SUBLANES, tq))
                jr, jo = divmod(j, SUBLANES)
                for r in range(n_grp):
                    if r > jr:
                        one = jnp.where(rj >= a_grp[r], 1.0, 0.0)
                    elif r < jr:
                        one = jnp.where(rj > a_grp[r], 1.0, 0.0)
                    else:
                        tie = jnp.where(sub > jo, jnp.where(rj == a_grp[r], 1.0, 0.0), 0.0)
                        one = jnp.where(rj > a_grp[r], 1.0, tie)
                    cnt[r] = cnt[r] + one
            neg_t.append(jnp.where(jnp.concatenate(cnt, axis=0) < float(SLC_TOP), 0.0, NEG))
        selneg = jnp.concatenate(neg_t[::-1], axis=0).T.astype(_BF16)
        selneg4 = jnp.concatenate([selneg] * hpg, axis=0)
        for g in range(NSA_KV_GROUPS):
            qsel_ref[g] = jnp.where(half[g], q4, selneg4)

    SEL, WIN = 0, 1

    def reset(st):
        m_ref[st] = jnp.full(m_ref.shape[1:], 2.0 * NEG, _F32)
        acc_ref[st] = jnp.zeros(acc_ref.shape[1:], _F32)

    def chunk(q_src, k_ref, vt_ref, start, tk, onehot, extras, st):
        s = scores(q_src, k_ref, start, tk, onehot)
        consume(lambda g, c: s[g][c * TK:(c + 1) * TK], vt_ref, start, tk, extras, st)

    def consume(block_of, vt_ref, start, tk, extras, st, limit=None):
        first = start // TK if isinstance(start, int) else start >> int(math.log2(TK))
        vt = jnp.concatenate([vt_ref[0, 0, first + j] for j in range(tk // TK)], axis=1)
        ones_rows = jnp.ones((SUBLANES, tk), _BF16)
        key_pos = start + lax.broadcasted_iota(jnp.int32, (HEAD_DIM + SUBLANES, tk), 1)
        for g in range(NSA_KV_GROUPS):
            vth = jnp.concatenate([vt[g * HEAD_DIM:(g + 1) * HEAD_DIM], ones_rows], axis=0)
            if limit is not None:
                vth = jnp.where(key_pos < limit, vth, jnp.zeros_like(vth))
            blocks = [block_of(g, c) for c in range(tk // TK)]
            for c, extra in extras.items():
                blocks[c] = blocks[c] + extra(g)
            mx = blocks[0]
            for blk_s in blocks[1:]:
                mx = jnp.maximum(mx, blk_s)
            m_old = m_ref[st, g]
            m_new = jnp.maximum(m_old, jnp.max(mx, axis=0, keepdims=True))
            m_row = m_new[0:1]
            pt = jnp.concatenate([jnp.exp2(blk_s - m_row).astype(_BF16) for blk_s in blocks], axis=0)
            acc_ref[st, g] = jnp.exp2(m_old - m_new)[0:1] * acc_ref[st, g] + _dot(vth, pt)
            m_ref[st, g] = m_new

    def finish(c, st):
        outs = []
        for g in range(NSA_KV_GROUPS):
            den = acc_ref[st, g, HEAD_DIM:HEAD_DIM + 1]
            outs.append(acc_ref[st, g, :HEAD_DIM] * jnp.where(den > 0.0, 1.0 / den, 0.0))
        for h in range(hpg):
            osum_ref[h] += gated(c, h, outs)

    def near_bias(d):
        return lambda g: bd_ref[g, d]

    def short_path(q_src, k_ref, v_ref, onehot, st):
        def far(kt, carry):
            chunk(q_src, k_ref, v_ref, pl.multiple_of(kt * TK, TK), TK, onehot, {}, st)
            return carry

        lax.fori_loop(0, jnp.maximum(qi - 1, 0), far, 0)

        @pl.when(qi >= 1)
        def _():
            chunk(q_src, k_ref, v_ref, pl.multiple_of((qi - 1) * TK, TK), TK, onehot,
                  {0: near_bias(1)}, st)

        chunk(q_src, k_ref, v_ref, pl.multiple_of(qi * TK, TK), TK, onehot, {0: near_bias(0)}, st)

    reset(SEL)
    reset(WIN)
    big = 4 * TK
    n_win = WINDOW // TK
    main_from = max(big // TK - 1, n_win)
    win_mask = jnp.where(lax.broadcasted_iota(jnp.int32, (TK, rows), 0)
                         > (lax.broadcasted_iota(jnp.int32, (TK, rows), 1) & (tq - 1)), 0.0, NEG)

    @pl.when(qi < main_from)
    def _():
        short_path(qsel_ref, ks_ref, vs_ref, True, SEL)
        short_path(qz_ref, kw_ref, vw_ref, False, WIN)

    @pl.when(qi >= main_from)
    def _():
        last = pl.multiple_of((qi - (big // TK - 1)) * TK, TK)
        n_far = (qi - (big // TK - 1) + big // TK - 1) >> int(math.log2(big // TK))
        near = {big // TK - 2: near_bias(1), big // TK - 1: near_bias(0)}

        def start_of(k):
            return pl.multiple_of(jnp.where(k < n_far, k * big, last), TK)

        def put(k, slot):
            for g, s in enumerate(scores(qsel_ref, ks_ref, start_of(k), big, True)):
                s_ref[slot, g] = s

        def take(k, slot, extras, limit):
            consume(lambda g, c: s_ref[slot, g, c * TK:(c + 1) * TK, :], vs_ref,
                    start_of(k), big, extras, SEL, limit)

        win_start = pl.multiple_of((qi - n_win) * TK, TK)
        for g, s in enumerate(scores(qz_ref, kw_ref, win_start, WINDOW + TK, False)):
            ws_ref[g] = s
        put(0, 0)
        consume(lambda g, c: ws_ref[g, c * TK:(c + 1) * TK, :], vw_ref, win_start, WINDOW + TK,
                {0: lambda g: win_mask, n_win - 1: near_bias(1), n_win: near_bias(0)}, WIN)
        n_pair = n_far >> 1

        def far_pair(i, carry):
            k = 2 * i
            put(k + 1, 1)
            take(k, 0, {}, last)
            put(k + 2, 0)
            take(k + 1, 1, {}, last)
            return carry

        lax.fori_loop(0, n_pair, far_pair, 0)

        @pl.when(n_far == 2 * n_pair)
        def _():
            take(n_far, 0, near, None)

        @pl.when(n_far != 2 * n_pair)
        def _():
            put(n_far, 1)
            take(n_far - 1, 0, {}, last)
            take(n_far, 1, near, None)

    finish(1, SEL)
    finish(2, WIN)

    for h in range(hpg):
        o_ref[0, h] = (osum_ref[h].T * _silu(z_ref[0, h])).astype(o_ref.dtype)


def _nsa_attention(pb3, pf3, vt, kvc, kvc_t, bias_c, bias_d, ov):
    b, _, t, _ = pb3.shape
    n_cmp = kvc.shape[2]
    n_blk = N_HEADS_NSA // 2
    full = lambda shape: pl.BlockSpec(shape, lambda i, q: (0,) * len(shape))
    kv_spec = lambda col: pl.BlockSpec((1, 1, t, LANES), lambda i, q: (i, col, 0, 0))
    vt_spec = lambda blk: pl.BlockSpec((1, 1, t // TK, LANES, TK), lambda i, q: (i, blk, 0, 0, 0))
    return pl.pallas_call(
        _nsa_kernel,
        grid=(b, t // TQ_NSA),
        in_specs=[pl.BlockSpec((1, n_blk, TQ_NSA, LANES), lambda i, q: (i, PB_QB // n_blk, q, 0)),
                  pl.BlockSpec((1, 1, n_cmp, LANES), lambda i, q: (i, 0, 0, 0)),
                  pl.BlockSpec((1, 1, LANES, n_cmp), lambda i, q: (i, 1, 0, 0)),
                  kv_spec(PB_KS), vt_spec(VT_VS), kv_spec(PB_KW), vt_spec(VT_VW),
                  pl.BlockSpec((1, 1, TQ_NSA, LANES), lambda i, q: (i, PF_GB, q, 0)),
                  pl.BlockSpec((1, n_blk, TQ_NSA, LANES), lambda i, q: (i, PF_ZB // n_blk, q, 0)),
                  full(bias_c.shape),
                  full(bias_d.shape), full(ov.shape)],
        out_specs=pl.BlockSpec((1, n_blk, TQ_NSA, LANES), lambda i, q: (i, 0, q, 0)),
        out_shape=jax.ShapeDtypeStruct((b, n_blk, t, LANES), _BF16),
        scratch_shapes=[pltpu.VMEM((2, NSA_KV_GROUPS, SUBLANES, NSA_HPG * TQ_NSA), _F32),
                        pltpu.VMEM((2, NSA_KV_GROUPS, HEAD_DIM + SUBLANES, NSA_HPG * TQ_NSA), _F32),
                        pltpu.VMEM((n_blk, TQ_NSA, LANES), _F32),
                        pltpu.VMEM((NSA_KV_GROUPS, NSA_HPG * TQ_NSA, LANES), _BF16),
                        pltpu.VMEM((NSA_KV_GROUPS, NSA_HPG * TQ_NSA, LANES), _BF16),
                        pltpu.VMEM((LANES, TQ_NSA), _F32),
                        pltpu.VMEM((NSA_KV_GROUPS, n_cmp, TQ_NSA), _F32),
                        pltpu.VMEM((2, NSA_KV_GROUPS, 4 * TK, NSA_HPG * TQ_NSA), _F32),
                        pltpu.VMEM((NSA_KV_GROUPS, WINDOW + TK, NSA_HPG * TQ_NSA), _F32)],
        compiler_params=_cparams(("arbitrary", "arbitrary")),
        name="native_sparse_attention",
    )(pb3, kvc, kvc_t, pb3, vt, pb3, vt, pf3, pf3, bias_c, bias_d, ov)


def _out_kernel(x_ref, oa_ref, ob_ref, oc_ref, w_ref, g_ref, o_ref, *, final_norm):
    mixed = jnp.concatenate([o_ref_in[0, j] for o_ref_in in (oa_ref, ob_ref, oc_ref)
                             for j in range(o_ref_in.shape[1])], axis=1)
    x = x_ref[...] + _dot(mixed, w_ref[...])
    if final_norm:
        x = x * lax.rsqrt(jnp.mean(x * x, axis=-1, keepdims=True) + RMS_EPS) * g_ref[...]
    o_ref[...] = x


def _out_proj(x2, oa, ob, oc, w, g, final_norm):
    n = x2.shape[0]
    per_seq = oa.shape[2] // TM_PROJ
    row = lambda width: pl.BlockSpec((TM_PROJ, width), lambda i: (i, 0))
    blocks = lambda a: pl.BlockSpec((1, a.shape[1], TM_PROJ, LANES),
                                    lambda i: (i // per_seq, 0, i % per_seq, 0))
    return pl.pallas_call(
        functools.partial(_out_kernel, final_norm=final_norm),
        grid=(n // TM_PROJ,),
        in_specs=[row(D_MODEL), blocks(oa), blocks(ob), blocks(oc),
                  pl.BlockSpec((D_MODEL, D_MODEL), lambda i: (0, 0)),
                  pl.BlockSpec((1, D_MODEL), lambda i: (0, 0))],
        out_specs=row(D_MODEL),
        out_shape=jax.ShapeDtypeStruct((n, D_MODEL), _F32),
        compiler_params=_cparams(("arbitrary",)),
        name="out_proj_residual",
    )(x2, oa, ob, oc, w, g)


def _head_perm_cols(width_per_head, order):
    return np.concatenate([np.arange(h * width_per_head, (h + 1) * width_per_head) for h in order])


def _layout_w_in(w):
    widths = [256, 256, 256, 256, 512, 128, 128, 128, 128, 128, 128, 24, 512, 256, 256, 256, 4, 256]
    offs = np.concatenate([[0], np.cumsum(widths)])
    (qa, ka, va, za, qb, kc, vc, ks, vs, kw, vw, gb, zb, qc, kf, vf, fc, zc) = [
        w[:, offs[i]:offs[i + 1]] for i in range(len(widths))]
    scale = HEAD_DIM ** -0.5
    perm = _head_perm_cols(HEAD_DIM, NSA_HEAD_ORDER)
    pad = lambda a: jnp.pad(a, ((0, 0), (0, LANES - a.shape[1])))
    wb = jnp.concatenate([qb[:, perm] * (scale * LOG2E), qa * (-scale), ka, va, ks, kw,
                          qc * (scale * LOG2E), kf], axis=1)
    wf = jnp.concatenate([zb[:, perm], za, zc, pad(gb), pad(fc)], axis=1)
    wc = jnp.concatenate([kc, vc], axis=1)
    wvt = jnp.concatenate([vs, vw, vf], axis=1)
    return wb.astype(_BF16), wf.astype(_BF16), wc.astype(_BF16), wvt.astype(_BF16)


def _static_tables(t):
    tq = TQ_NSA
    n_cmp_pad = t // CMP_STRIDE
    j = np.arange(n_cmp_pad)
    rel = np.arange(2 * n_cmp_pad) - n_cmp_pad
    dist_c = np.arange(tq)[None, :] - (rel[:, None] * CMP_STRIDE + CMP_BLOCK - 1)
    i_, j_ = np.arange(tq)[:, None], np.arange(TK)[None, :]
    dist_d = np.concatenate([d * TK + i_ - j_ for d in range(3)], axis=0)
    n_slc = LANES // NSA_KV_GROUPS
    cmp_start = j * CMP_STRIDE
    cmp_end = cmp_start + CMP_BLOCK - 1
    slc_start = np.arange(n_slc) * SLC_BLOCK
    ov1 = np.clip(np.minimum(cmp_end[:, None], slc_start[None, :] + SLC_BLOCK - 1)
                  - np.maximum(cmp_start[:, None], slc_start[None, :]) + 1, 0, None) / CMP_BLOCK
    ov1[n_cmp_pad - 1:] = 0.0
    ov1[:, t // SLC_BLOCK:] = 0.0
    ov3 = np.concatenate([ov1.T, ov1.T, ov1.T], axis=1)
    return jnp.asarray(dist_c, jnp.int32), jnp.asarray(dist_d, jnp.int32), jnp.asarray(ov3, _BF16)


def kernel(x, norm_g, w_in, w_out, forget_b, cmp_w1, cmp_b1, cmp_w2, cmp_pe, rel_bias, final_g):
    b, t, d = x.shape
    depth = norm_g.shape[0]
    assert d == D_MODEL and t % TM_PROJ == 0 and t % (CMP_STRIDE * LANES) == 0
    assert t // SLC_BLOCK <= LANES // NSA_KV_GROUPS and TQ_NSA == TK
    n_cmp_pad = t // CMP_STRIDE

    dist_c, dist_d, ov3 = _static_tables(t)
    bias_c = _bias_table(rel_bias * LOG2E, dist_c, 32)
    bias_d = _bias_table(rel_bias * LOG2E, dist_d, 32).reshape(N_HEADS_NSA, 3, TQ_NSA, TK)
    bias_d = bias_d[:, :2] - bias_d[:, 2:3]
    bias_d = bias_d.reshape(NSA_KV_GROUPS, NSA_HPG, 2, TQ_NSA, TK).transpose(0, 2, 4, 1, 3).reshape(
        NSA_KV_GROUPS, 2, TK, NSA_HPG * TQ_NSA)

    perm_rows = _head_perm_cols(HEAD_DIM, NSA_HEAD_ORDER)
    x2 = x.reshape(b * t, d)
    out = None
    for l in range(depth):
        wb, wf, wc, wvt = _layout_w_in(w_in[l])
        pb, pf, pc, vt = _proj(x2, norm_g[l].reshape(1, d), wb, wf, wc, wvt, t)
        pb3, pf3 = pb, pf

        fb_row = jnp.pad(forget_b[l], (0, LANES - N_HEADS_FOX)).reshape(1, LANES)
        augq, augk = _fgate(pf3, fb_row)

        halves = pc.reshape(b, 2, NSA_KV_GROUPS, n_cmp_pad, CMP_STRIDE * HEAD_DIM)
        w2 = cmp_w2[l]
        zeros = jnp.zeros_like(w2)
        w2p = jnp.stack([jnp.concatenate([w2, zeros], axis=-1),
                         jnp.concatenate([zeros, w2], axis=-1)], axis=1).astype(_BF16)
        kvc, kvc_t = _compress(halves, cmp_w1[l].astype(_BF16),
                               cmp_pe[l].reshape(2, 1, CMP_BLOCK * HEAD_DIM).astype(_BF16),
                               cmp_b1[l].reshape(2, 1, CMP_HIDDEN), w2p)

        o_a = _sb_attention(pb3, pf3)
        o_b = _nsa_attention(pb3, pf3, vt, kvc, kvc_t, bias_c, bias_d, ov3)
        o_c = _fox_attention(pb3, pf3, vt, augq, augk)

        wo = w_out[l]
        wo = jnp.concatenate([wo[:N_HEADS_SB * HEAD_DIM],
                              wo[N_HEADS_SB * HEAD_DIM:][:N_HEADS_NSA * HEAD_DIM][perm_rows],
                              wo[(N_HEADS_SB + N_HEADS_NSA) * HEAD_DIM:]], axis=0).astype(_BF16)
        last = l == depth - 1
        x2 = _out_proj(x2, o_a, o_b, o_c, wo, final_g.reshape(1, d), last)
    return x2.reshape(b, t, d)
```

```python
import functools
import math

import jax
import jax.numpy as jnp
import numpy as np
from jax import lax
from jax.experimental import pallas as pl
from jax.experimental.pallas import tpu as pltpu

D_MODEL = 1024
HEAD_DIM = 64
N_HEADS_SB = 4
N_HEADS_FOX = 4
N_HEADS_NSA = 8
NSA_KV_GROUPS = 2
NSA_HPG = N_HEADS_NSA // NSA_KV_GROUPS
CMP_BLOCK = 32
CMP_STRIDE = 16
CMP_HIDDEN = 256
SLC_BLOCK = 64
SLC_TOP = 16
WINDOW = 512
REL_BUCKETS = 32
REL_MAX_DIST = 128
FORCE_SCORE = 1e4
RMS_EPS = 1e-6
NEG = -1e30
LOG2E = math.log2(math.e)

LANES = 128
SUBLANES = 8
VMEM_LIMIT = 56 * 1024 * 1024

TM_PROJ = 512
TQ_SB = 512
TQ_FOX = 512
FOX_PARTS = 2
TQ_NSA = 128
TK = 128

PB_QB, PB_QA, PB_KA, PB_VA = 0, 4, 6, 8
PB_KS, PB_KW = 10, 11
PB_QC, PB_KF = 12, 14
PB_BLOCKS = 16
N_CMP_SLABS = 2 * NSA_KV_GROUPS
VT_VS, VT_VW, VT_VF = 0, 1, 2
N_VT = 4
PF_ZB, PF_ZA, PF_ZC, PF_GB, PF_FC = 0, 4, 6, 8, 9
PF_BLOCKS = 10

NSA_HEAD_ORDER = [0, 4, 1, 5, 2, 6, 3, 7]

_F32 = jnp.float32
_BF16 = jnp.bfloat16


def _cparams(sem):
    return pltpu.CompilerParams(dimension_semantics=sem, vmem_limit_bytes=VMEM_LIMIT)


def _dot(a, b):
    return jnp.dot(a, b, preferred_element_type=_F32)


def _dot_nt(a, b):
    return lax.dot_general(a, b, (((1,), (1,)), ((), ())), preferred_element_type=_F32)


def _split2(x):
    hi = x.astype(_BF16)
    lo = (x - hi.astype(_F32)).astype(_BF16)
    return jnp.concatenate([hi, lo], axis=1)


def _split3(x):
    h1 = x.astype(_BF16)
    r1 = x - h1.astype(_F32)
    h2 = r1.astype(_BF16)
    h3 = (r1 - h2.astype(_F32)).astype(_BF16)
    return jnp.concatenate([h1, h2, h3], axis=1)


def _sigmoid(x):
    return 1.0 / (1.0 + jnp.exp(-x))


def _silu(x):
    return x * _sigmoid(x)


def _rel_bucket_np(n):
    n = np.maximum(n, 0)
    max_exact = REL_BUCKETS // 2
    nf = np.maximum(n, 1).astype(np.float64)
    large = max_exact + (np.log(nf / max_exact) / math.log(REL_MAX_DIST / max_exact)
                         * (REL_BUCKETS - max_exact)).astype(np.int64)
    large = np.minimum(large, REL_BUCKETS - 1)
    return np.where(n < max_exact, n, large)


def _bucket_thresholds():
    n = np.arange(0, 4 * REL_MAX_DIST)
    bk = _rel_bucket_np(n)
    assert np.all(np.diff(bk) >= 0) and bk[-1] == REL_BUCKETS - 1
    return [int(np.argmax(bk >= b)) for b in range(REL_BUCKETS)]


_BUCKET_THR = _bucket_thresholds()


def _bias_kernel(tab_ref, dist_ref, o_ref):
    n = dist_ref[...]
    acc = [jnp.full(n.shape, tab_ref[0, h], _F32) for h in range(N_HEADS_NSA)]
    for b in range(1, REL_BUCKETS):
        ge = n >= _BUCKET_THR[b]
        for h in range(N_HEADS_NSA):
            acc[h] = jnp.where(ge, tab_ref[b, h], acc[h])
    valid = n >= 0
    for h in range(N_HEADS_NSA):
        o_ref[h] = jnp.where(valid, acc[h], NEG)


def _bias_table(rel_bias, dist, rows):
    n_rows, n_cols = dist.shape
    return pl.pallas_call(
        _bias_kernel,
        grid=(n_rows // rows,),
        in_specs=[pl.BlockSpec(memory_space=pltpu.SMEM),
                  pl.BlockSpec((rows, n_cols), lambda i: (i, 0))],
        out_specs=pl.BlockSpec((N_HEADS_NSA, rows, n_cols), lambda i: (0, i, 0)),
        out_shape=jax.ShapeDtypeStruct((N_HEADS_NSA, n_rows, n_cols), _F32),
        compiler_params=_cparams(("arbitrary",)),
        name="rel_bias_table",
    )(rel_bias, dist)


def _proj_kernel(x_ref, g_ref, wb_ref, wf_ref, wc_ref, wvt_ref, pb_ref, pf_ref, pc_ref, vt_ref):
    x = x_ref[...]
    y = x * lax.rsqrt(jnp.mean(x * x, axis=-1, keepdims=True) + RMS_EPS)
    h = (y * g_ref[...]).astype(_BF16)
    v_all = _dot(h, wvt_ref[...])
    for j in range(N_VT):
        v_t = v_all[:, j * LANES:(j + 1) * LANES].T.astype(_BF16)
        for c in range(TM_PROJ // TK):
            vt_ref[0, j, c] = v_t[:, c * TK:(c + 1) * TK]
    kv_cmp = _dot(h, wc_ref[...]).astype(_BF16)
    for s in range(N_CMP_SLABS):
        pc_ref[0, s] = kv_cmp[:, s * HEAD_DIM:(s + 1) * HEAD_DIM]
    chunk = 4 * LANES
    for c in range(0, PB_BLOCKS * LANES, chunk):
        w = min(chunk, PB_BLOCKS * LANES - c)
        res = _dot(h, wb_ref[:, c:c + w]).astype(_BF16)
        for j in range(w // LANES):
            pb_ref[0, c // LANES + j] = res[:, j * LANES:(j + 1) * LANES]
    for c in range(0, PF_BLOCKS * LANES, chunk):
        w = min(chunk, PF_BLOCKS * LANES - c)
        res = _dot(h, wf_ref[:, c:c + w])
        for j in range(w // LANES):
            pf_ref[0, c // LANES + j] = res[:, j * LANES:(j + 1) * LANES]


def _proj(x2, g, wb, wf, wc, wvt, t):
    n = x2.shape[0]
    per_seq = t // TM_PROJ
    return pl.pallas_call(
        _proj_kernel,
        grid=(n // TM_PROJ,),
        in_specs=[pl.BlockSpec((TM_PROJ, D_MODEL), lambda i: (i, 0)),
                  pl.BlockSpec((1, D_MODEL), lambda i: (0, 0)),
                  pl.BlockSpec((D_MODEL, PB_BLOCKS * LANES), lambda i: (0, 0)),
                  pl.BlockSpec((D_MODEL, PF_BLOCKS * LANES), lambda i: (0, 0)),
                  pl.BlockSpec((D_MODEL, N_CMP_SLABS * HEAD_DIM), lambda i: (0, 0)),
                  pl.BlockSpec((D_MODEL, N_VT * LANES), lambda i: (0, 0))],
        out_specs=[pl.BlockSpec((1, PB_BLOCKS, TM_PROJ, LANES), lambda i: (i // per_seq, 0, i % per_seq, 0)),
                   pl.BlockSpec((1, PF_BLOCKS, TM_PROJ, LANES), lambda i: (i // per_seq, 0, i % per_seq, 0)),
                   pl.BlockSpec((1, N_CMP_SLABS, TM_PROJ, HEAD_DIM),
                                lambda i: (i // per_seq, 0, i % per_seq, 0)),
                   pl.BlockSpec((1, N_VT, TM_PROJ // TK, LANES, TK),
                                lambda i: (i // per_seq, 0, i % per_seq, 0, 0))],
        out_shape=[jax.ShapeDtypeStruct((n // t, PB_BLOCKS, t, LANES), _BF16),
                   jax.ShapeDtypeStruct((n // t, PF_BLOCKS, t, LANES), _F32),
                   jax.ShapeDtypeStruct((n // t, N_CMP_SLABS, t, HEAD_DIM), _BF16),
                   jax.ShapeDtypeStruct((n // t, N_VT, t // TK, LANES, TK), _BF16)],
        compiler_params=_cparams(("arbitrary",)),
        name="rmsnorm_in_proj",
    )(x2, g, wb, wf, wc, wvt)


N_SPLIT = 3


def _fgate_tables():
    n_pairs = N_HEADS_FOX // 2
    pq = np.zeros((N_SPLIT * LANES, n_pairs * LANES), np.float32)
    pk = np.zeros_like(pq)
    ones_q = np.zeros((1, n_pairs * LANES), np.float32)
    ones_k = np.zeros_like(ones_q)
    for head in range(N_HEADS_FOX):
        pair, slot = divmod(head, 2)
        base = pair * LANES + (HEAD_DIM if slot == 0 else 0)
        for j in range(N_SPLIT):
            pq[j * LANES + head, base + j] = 1.0
            pk[j * LANES + head, base + N_SPLIT + j] = 1.0
        ones_q[0, base + N_SPLIT:base + 2 * N_SPLIT] = 1.0
        ones_k[0, base:base + N_SPLIT] = 1.0
    return (jnp.asarray(pq, _BF16), jnp.asarray(pk, _BF16),
            jnp.asarray(ones_q), jnp.asarray(ones_k))


def _fgate_kernel(fc_ref, fb_ref, pq_ref, pk_ref, oq_ref, ok_ref, augq_ref, augk_ref):
    t = fc_ref.shape[2]
    z = fc_ref[0, 0] + fb_ref[...]
    logf = jnp.minimum(z, 0.0) - jnp.log1p(jnp.exp(-jnp.abs(z)))
    row = lax.broadcasted_iota(jnp.int32, (t, LANES), 0)
    c = logf
    shift = 1
    while shift < t:
        c = c + jnp.where(row >= shift, pltpu.roll(c, shift, axis=0), 0.0)
        shift *= 2
    c3 = _split3(c * LOG2E)
    aq = _dot(c3, pq_ref[...]) + oq_ref[...]
    ak = ok_ref[...] - _dot(c3, pk_ref[...])
    for p in range(N_HEADS_FOX // 2):
        augq_ref[0, p] = aq[:, p * LANES:(p + 1) * LANES].astype(_BF16)
        augk_ref[0, p] = ak[:, p * LANES:(p + 1) * LANES].astype(_BF16)


def _fgate(pf3, fb_row):
    b, _, t, _ = pf3.shape
    n_pairs = N_HEADS_FOX // 2
    tables = _fgate_tables()
    full = lambda a: pl.BlockSpec(a.shape, lambda i: (0,) * a.ndim)
    aug = lambda: pl.BlockSpec((1, n_pairs, t, LANES), lambda i: (i, 0, 0, 0))
    return pl.pallas_call(
        _fgate_kernel,
        grid=(b,),
        in_specs=[pl.BlockSpec((1, 1, t, LANES), lambda i: (i, PF_FC, 0, 0)),
                  pl.BlockSpec((1, LANES), lambda i: (0, 0))] + [full(a) for a in tables],
        out_specs=[aug(), aug()],
        out_shape=[jax.ShapeDtypeStruct((b, n_pairs, t, LANES), _BF16),
                   jax.ShapeDtypeStruct((b, n_pairs, t, LANES), _BF16)],
        compiler_params=_cparams(("arbitrary",)),
        name="forget_gate_cumsum",
    )(pf3, fb_row, *tables)


def _compress_kernel(x_ref, w1_ref, pe_ref, b1_ref, w2_ref, o_ref, ot_ref):
    nc = x_ref.shape[3]
    half = CMP_STRIDE * HEAD_DIM
    w1 = w1_ref[0]
    c1 = _dot(jnp.broadcast_to(pe_ref[0], (SUBLANES, 2 * half)), w1)[0:1] + b1_ref[0]
    out = jnp.zeros((nc, LANES), _F32)
    for g in range(NSA_KV_GROUPS):
        xg = x_ref[0, 0, g]
        a = _dot(xg, w1[:half])
        bb = _dot(xg, w1[half:])
        h = a + pltpu.roll(bb, nc - 1, axis=0) + c1
        out = out + _dot(_silu(h).astype(_BF16), w2_ref[0, g])
    o_ref[0, 0] = out.astype(_BF16)
    ot_ref[0, 0] = out.T.astype(_BF16)


def _compress(halves, w1, pe, b1, w2p):
    b, _, g, nc, width = halves.shape
    return pl.pallas_call(
        _compress_kernel,
        grid=(b, 2),
        in_specs=[pl.BlockSpec((1, 1, g, nc, width), lambda i, k: (i, k, 0, 0, 0)),
                  pl.BlockSpec((1, 2 * width, CMP_HIDDEN), lambda i, k: (k, 0, 0)),
                  pl.BlockSpec((1, 1, 2 * width), lambda i, k: (k, 0, 0)),
                  pl.BlockSpec((1, 1, CMP_HIDDEN), lambda i, k: (k, 0, 0)),
                  pl.BlockSpec((1, g, CMP_HIDDEN, LANES), lambda i, k: (k, 0, 0, 0))],
        out_specs=[pl.BlockSpec((1, 1, nc, LANES), lambda i, k: (i, k, 0, 0)),
                   pl.BlockSpec((1, 1, LANES, nc), lambda i, k: (i, k, 0, 0))],
        out_shape=[jax.ShapeDtypeStruct((b, 2, nc, LANES), _BF16),
                   jax.ShapeDtypeStruct((b, 2, LANES, nc), _BF16)],
        compiler_params=_cparams(("arbitrary", "arbitrary")),
        name="nsa_compress",
    )(halves, w1, pe, b1, w2p)


def _sb_kernel(q_ref, k_ref, v_ref, z_ref, o_ref, qh_ref, c_ref, acc_ref):
    qi = pl.program_id(2)
    tq = q_ref.shape[2]
    lane = lax.broadcasted_iota(jnp.int32, (tq, LANES), 1)
    q2 = q_ref[0, 0]
    qh = [jnp.where(lane < HEAD_DIM, q2, jnp.zeros_like(q2)),
          jnp.where(lane >= HEAD_DIM, q2, jnp.zeros_like(q2))]
    r_i = lax.broadcasted_iota(jnp.int32, (2 * TK, 2 * TK), 0)
    c_i = lax.broadcasted_iota(jnp.int32, (2 * TK, 2 * TK), 1)
    uu = jnp.where(r_i >= c_i, 1.0, 0.0).astype(_BF16)
    c_ref[...] = jnp.zeros_like(c_ref)
    acc_ref[...] = jnp.zeros_like(acc_ref)
    for h in range(2):
        qh_ref[h] = qh[h]
    row = lax.broadcasted_iota(jnp.int32, (tq, LANES), 0)
    n_blocks = tq // TK

    def chunk(start, diagonal):
        k2 = k_ref[0, 0, pl.ds(start, tq), :]
        v2 = v_ref[0, 0, pl.ds(start, tq), :]
        for h in range(2):
            s = _dot_nt(qh_ref[h], k2)
            carry = c_ref[h]
            w_blocks = [None] * n_blocks
            for c in reversed(range(0, n_blocks, 2)):
                nz, l1m, mask = [], [], []
                for cc in (c, c + 1):
                    nzc = s[:, cc * TK:(cc + 1) * TK]
                    neg_abs = lax.bitcast_convert_type(
                        lax.bitcast_convert_type(nzc, jnp.uint32) | jnp.uint32(0x80000000), _F32)
                    lc = jnp.minimum(nzc, 0.0) - jnp.log(1.0 + jnp.exp(neg_abs))
                    if diagonal:
                        mask.append(lane + cc * TK < row)
                        lc = jnp.where(mask[-1], lc, 0.0)
                    nz.append(nzc)
                    l1m.append(lc.astype(_BF16))
                rc = _dot(jnp.concatenate(l1m, axis=1), uu)
                for i, cc in enumerate((c, c + 1)):
                    w = jnp.exp((rc[:, i * TK:(i + 1) * TK] + carry) - nz[i])
                    if diagonal:
                        w = jnp.where(mask[i], w, 0.0)
                    w_blocks[cc] = w.astype(_BF16)
                carry = carry + jnp.broadcast_to(rc[:, 0:1], carry.shape)
            acc_ref[h] += _dot(jnp.concatenate(w_blocks, axis=1), v2)
            c_ref[h] = carry

    chunk(pl.multiple_of(qi * tq, tq), True)

    def far_pair(it, carry):
        chunk(pl.multiple_of((qi - 1 - 2 * it) * tq, tq), False)
        chunk(pl.multiple_of((qi - 2 - 2 * it) * tq, tq), False)
        return carry

    lax.fori_loop(0, qi >> 1, far_pair, 0)

    @pl.when((qi & 1) == 1)
    def _():
        chunk(0, False)
    o = jnp.where(lane < HEAD_DIM, acc_ref[0], acc_ref[1])
    o_ref[0, 0] = (o * _silu(z_ref[0, 0])).astype(o_ref.dtype)


def _sb_attention(pb3, pf3):
    b, _, t, _ = pb3.shape
    n_pairs = N_HEADS_SB // 2
    return pl.pallas_call(
        _sb_kernel,
        grid=(b, n_pairs, t // TQ_SB),
        in_specs=[pl.BlockSpec((1, 1, TQ_SB, LANES), lambda i, p, q: (i, PB_QA + p, q, 0)),
                  pl.BlockSpec((1, 1, t, LANES), lambda i, p, q: (i, PB_KA + p, 0, 0)),
                  pl.BlockSpec((1, 1, t, LANES), lambda i, p, q: (i, PB_VA + p, 0, 0)),
                  pl.BlockSpec((1, 1, TQ_SB, LANES), lambda i, p, q: (i, PF_ZA + p, q, 0))],
        out_specs=pl.BlockSpec((1, 1, TQ_SB, LANES), lambda i, p, q: (i, p, q, 0)),
        out_shape=jax.ShapeDtypeStruct((b, n_pairs, t, LANES), _BF16),
        scratch_shapes=[pltpu.VMEM((2, TQ_SB, LANES), _BF16),
                        pltpu.VMEM((2, TQ_SB, LANES), _F32),
                        pltpu.VMEM((2, TQ_SB, LANES), _F32)],
        compiler_params=_cparams(("arbitrary", "arbitrary", "arbitrary")),
        name="stick_breaking_attention",
    )(pb3, pb3, pb3, pf3)


def _fox_kernel(q_ref, k_ref, vt_ref, augq_ref, augk_ref, z_ref, o_ref,
                qh_ref, m_ref, acc_ref, s_ref):
    qi = pl.program_id(2)
    tq = q_ref.shape[2]
    tk = tq
    lane = lax.broadcasted_iota(jnp.int32, (tq, LANES), 1)
    keep = [lane < HEAD_DIM, lane >= HEAD_DIM]
    q2 = q_ref[0, 0]
    aq = augq_ref[0, 0]
    for h in range(2):
        qh_ref[h] = jnp.where(keep[h], q2, aq)
    m_ref[...] = jnp.full_like(m_ref, NEG)
    acc_ref[...] = jnp.zeros_like(acc_ref)
    key_i = lax.broadcasted_iota(jnp.int32, (TK, tq), 0)
    qry_i = lax.broadcasted_iota(jnp.int32, (TK, tq), 1)

    def scores(c, slot):
        start = c * tk if isinstance(c, int) else pl.multiple_of(c * tk, tk)
        k2 = k_ref[0, 0, pl.ds(start, tk), :]
        ak = augk_ref[0, 0, pl.ds(start, tk), :]
        for h in range(2):
            s_ref[slot, h] = _dot_nt(jnp.where(keep[h], k2, ak), qh_ref[h])

    ones_rows = jnp.ones((SUBLANES, tk), _BF16)

    def direct(c, diagonal):
        start = c * tk if isinstance(c, int) else pl.multiple_of(c * tk, tk)
        k2 = k_ref[0, 0, pl.ds(start, tk), :]
        ak = augk_ref[0, 0, pl.ds(start, tk), :]
        s = [_dot_nt(jnp.where(keep[h], k2, ak), qh_ref[h]) for h in range(2)]
        consume(c, None, diagonal, lambda h, j: s[h][j * TK:(j + 1) * TK])

    def consume(c, slot, diagonal, get=None):
        if get is None:
            get = lambda h, j: s_ref[slot, h, j * TK:(j + 1) * TK, :]
        per = tk // TK
        vt = jnp.concatenate([vt_ref[0, 0, c * per + j] for j in range(per)], axis=1)
        for h in range(2):
            vth = jnp.concatenate([vt[h * HEAD_DIM:(h + 1) * HEAD_DIM], ones_rows], axis=0)
            for part in range(FOX_PARTS):
                js = range(part * per // FOX_PARTS, (part + 1) * per // FOX_PARTS)
                blocks = [get(h, j) for j in js]
                if diagonal:
                    blocks = [jnp.where(key_i + j * TK <= qry_i, blk, NEG) for j, blk in zip(js, blocks)]
                mx = blocks[0]
                for blk in blocks[1:]:
                    mx = jnp.maximum(mx, blk)
                m_old = m_ref[h]
                m_new = jnp.maximum(m_old, jnp.max(mx, axis=0, keepdims=True))
                m_row = m_new[0:1]
                pt = jnp.concatenate([jnp.exp2(blk - m_row).astype(_BF16) for blk in blocks], axis=0)
                acc_ref[h] = (jnp.exp2(m_old - m_new)[0:1] * acc_ref[h]
                              + _dot(vth[:, js[0] * TK:(js[-1] + 1) * TK], pt))
                m_ref[h] = m_new

    n_pairs = qi >> 1

    def far_pair(i, carry):
        direct(2 * i, False)
        direct(2 * i + 1, False)
        return carry

    lax.fori_loop(0, n_pairs, far_pair, 0)

    @pl.when(qi == 2 * n_pairs)
    def _():
        direct(qi, True)

    @pl.when(qi != 2 * n_pairs)
    def _():
        direct(qi - 1, False)
        direct(qi, True)
    o_t = jnp.concatenate([acc_ref[h, :HEAD_DIM] / acc_ref[h, HEAD_DIM:HEAD_DIM + 1]
                           for h in range(2)], axis=0)
    o_ref[0, 0] = (o_t.T * _silu(z_ref[0, 0])).astype(o_ref.dtype)


def _fox_attention(pb3, pf3, vt, augq, augk):
    b, _, t, _ = pb3.shape
    n_pairs = N_HEADS_FOX // 2
    return pl.pallas_call(
        _fox_kernel,
        grid=(b, n_pairs, t // TQ_FOX),
        in_specs=[pl.BlockSpec((1, 1, TQ_FOX, LANES), lambda i, p, q: (i, PB_QC + p, q, 0)),
                  pl.BlockSpec((1, 1, t, LANES), lambda i, p, q: (i, PB_KF + p, 0, 0)),
                  pl.BlockSpec((1, 1, t // TK, LANES, TK), lambda i, p, q: (i, VT_VF + p, 0, 0, 0)),
                  pl.BlockSpec((1, 1, TQ_FOX, LANES), lambda i, p, q: (i, p, q, 0)),
                  pl.BlockSpec((1, 1, t, LANES), lambda i, p, q: (i, p, 0, 0)),
                  pl.BlockSpec((1, 1, TQ_FOX, LANES), lambda i, p, q: (i, PF_ZC + p, q, 0))],
        out_specs=pl.BlockSpec((1, 1, TQ_FOX, LANES), lambda i, p, q: (i, p, q, 0)),
        out_shape=jax.ShapeDtypeStruct((b, n_pairs, t, LANES), _BF16),
        scratch_shapes=[pltpu.VMEM((2, TQ_FOX, LANES), _BF16),
                        pltpu.VMEM((2, SUBLANES, TQ_FOX), _F32),
                        pltpu.VMEM((2, HEAD_DIM + SUBLANES, TQ_FOX), _F32),
                        pltpu.VMEM((2, 2, TQ_FOX, TQ_FOX), _F32)],
        compiler_params=_cparams(("arbitrary", "arbitrary", "arbitrary")),
        name="forgetting_attention",
    )(pb3, pb3, vt, augq, augk, pf3)


def _nsa_kernel(q_ref, kc_ref, vct_ref, ks_ref, vs_ref, kw_ref, vw_ref, gl_ref, z_ref,
                bc_ref, bd_ref, ov_ref, o_ref,
                m_ref, acc_ref, osum_ref, qz_ref, qsel_ref, gates_ref, psum_ref, s_ref, ws_ref):
    qi = pl.program_id(1)
    tq = q_ref.shape[2]
    hpg = NSA_HPG
    rows = hpg * tq
    n_cmp = kc_ref.shape[2]
    lane = lax.broadcasted_iota(jnp.int32, (tq, LANES), 1)
    qrow = qi * tq + lax.broadcasted_iota(jnp.int32, (tq, LANES), 0)
    lane_r = lax.broadcasted_iota(jnp.int32, (rows, LANES), 1)
    irow_r = lax.broadcasted_iota(jnp.int32, (rows, LANES), 0) & (tq - 1)
    half = [lane_r < HEAD_DIM, lane_r >= HEAD_DIM]

    gates_ref[...] = _sigmoid(gl_ref[0, 0]).T

    def gated(c, h, per_group):
        parts = []
        for g, o in enumerate(per_group):
            r = 3 * (g * hpg + h) + c
            parts.append(gates_ref[r:r + 1, :] * o[:, h * tq:(h + 1) * tq])
        return jnp.concatenate(parts, axis=0)

    q4 = jnp.concatenate([q_ref[0, h] for h in range(hpg)], axis=0)
    for g in range(NSA_KV_GROUPS):
        qz_ref[g] = jnp.where(half[g], q4, jnp.zeros_like(q4))

    def scores(q_src, k_ref, start, tk, onehot):
        k2 = k_ref[0, 0, pl.ds(start, tk), :]
        lane_k = lax.broadcasted_iota(jnp.int32, (tk, LANES), 1)
        key_blk = (start + lax.broadcasted_iota(jnp.int32, (tk, LANES), 0)) >> int(math.log2(SLC_BLOCK))
        oh = jnp.where((lane_k & (SLC_BLOCK - 1)) == key_blk, 1.0, 0.0).astype(_BF16)
        out = []
        for g in range(NSA_KV_GROUPS):
            keep = (lane_k < HEAD_DIM) if g == 0 else (lane_k >= HEAD_DIM)
            out.append(_dot_nt(jnp.where(keep, k2, oh) if onehot else k2, q_src[g]))
        return out


    def heads_to_blocks(x0, x1):
        return [jnp.where(lane < HEAD_DIM, x0[h * tq:(h + 1) * tq], x1[h * tq:(h + 1) * tq])
                for h in range(hpg)]

    kc = kc_ref[0, 0]
    vct = vct_ref[0, 0]
    cmp_row0 = pl.multiple_of(n_cmp - qi * (tq // CMP_STRIDE), SUBLANES)
    o_cmp_t = []
    raw = [_dot_nt(kc, qz_ref[g]) for g in range(NSA_KV_GROUPS)]
    for g in range(NSA_KV_GROUPS):
        bias = jnp.concatenate([bc_ref[g * hpg + h, pl.ds(cmp_row0, n_cmp), :]
                                for h in range(hpg)], axis=1)
        sc = raw[g] + bias
        mx = jnp.max(sc, axis=0, keepdims=True)
        e = jnp.exp2(sc - mx)
        den = jnp.sum(e, axis=0, keepdims=True)
        pc = e * jnp.where(mx > 0.5 * NEG, 1.0 / den, 0.0)
        psum = pc[:, 0:tq]
        for h in range(1, hpg):
            psum = psum + pc[:, h * tq:(h + 1) * tq]
        psum_ref[g] = psum
        o_cmp_t.append(_dot(vct[g * HEAD_DIM:(g + 1) * HEAD_DIM], pc.astype(_BF16)))
    for h in range(hpg):
        osum_ref[h] = gated(0, h, o_cmp_t)

    rank_from = SLC_TOP * SLC_BLOCK // tq

    @pl.when(qi < rank_from)
    def _():
        for g in range(NSA_KV_GROUPS):
            qsel_ref[g] = qz_ref[g]

    @pl.when(qi >= rank_from)
    def _():
        n_sel = LANES // NSA_KV_GROUPS
        n_grp = n_sel // SUBLANES
        blk = lax.broadcasted_iota(jnp.int32, (n_sel, tq), 0)
        cur = (qi * tq + lax.broadcasted_iota(jnp.int32, (n_sel, tq), 1)) >> int(math.log2(SLC_BLOCK))
        forced = (blk == 0) | (blk == cur) | (blk == cur - 1)
        sub = lax.broadcasted_iota(jnp.int32, (SUBLANES, tq), 0)
        neg_t = []
        for g in range(NSA_KV_GROUPS):
            p = psum_ref[g]
            p1 = p.astype(_BF16)
            r1 = p - p1.astype(_F32)
            p2 = r1.astype(_BF16)
            p3 = (r1 - p2.astype(_F32)).astype(_BF16)
            a = _dot(ov_ref[...], jnp.concatenate([p1, p2, p3], axis=0))
            a = jnp.where(forced, FORCE_SCORE, a)
            a = jnp.where(blk > cur, -FORCE_SCORE, a)
            a_grp = [a[r * SUBLANES:(r + 1) * SUBLANES] for r in range(n_grp)]
            cnt = [jnp.zeros((SUBLANES, tq), _F32) for _ in range(n_grp)]
            for j in range(n_sel):
                rj = jnp.broadcast_to(a[j:j + 1], (SUBLANES, tq))
                jr, jo = divmod(j, SUBLANES)
                for r in range(n_grp):
                    if r > jr:
                        one = jnp.where(rj >= a_grp[r], 1.0, 0.0)
                    elif r < jr:
                        one = jnp.where(rj > a_grp[r], 1.0, 0.0)
                    else:
                        tie = jnp.where(sub > jo, jnp.where(rj == a_grp[r], 1.0, 0.0), 0.0)
                        one = jnp.where(rj > a_grp[r], 1.0, tie)
                    cnt[r] = cnt[r] + one
            neg_t.append(jnp.where(jnp.concatenate(cnt, axis=0) < float(SLC_TOP), 0.0, NEG))
        selneg = jnp.concatenate(neg_t[::-1], axis=0).T.astype(_BF16)
        selneg4 = jnp.concatenate([selneg] * hpg, axis=0)
        for g in range(NSA_KV_GROUPS):
            qsel_ref[g] = jnp.where(half[g], q4, selneg4)

    SEL, WIN = 0, 1

    def reset(st):
        m_ref[st] = jnp.full(m_ref.shape[1:], 2.0 * NEG, _F32)
        acc_ref[st] = jnp.zeros(acc_ref.shape[1:], _F32)

    def chunk(q_src, k_ref, vt_ref, start, tk, onehot, extras, st):
        s = scores(q_src, k_ref, start, tk, onehot)
        consume(lambda g, c: s[g][c * TK:(c + 1) * TK], vt_ref, start, tk, extras, st)

    def consume(block_of, vt_ref, start, tk, extras, st, limit=None):
        first = start // TK if isinstance(start, int) else start >> int(math.log2(TK))
        vt = jnp.concatenate([vt_ref[0, 0, first + j] for j in range(tk // TK)], axis=1)
        ones_rows = jnp.ones((SUBLANES, tk), _BF16)
        key_pos = start + lax.broadcasted_iota(jnp.int32, (HEAD_DIM + SUBLANES, tk), 1)
        for g in range(NSA_KV_GROUPS):
            vth = jnp.concatenate([vt[g * HEAD_DIM:(g + 1) * HEAD_DIM], ones_rows], axis=0)
            if limit is not None:
                vth = jnp.where(key_pos < limit, vth, jnp.zeros_like(vth))
            blocks = [block_of(g, c) for c in range(tk // TK)]
            for c, extra in extras.items():
                blocks[c] = blocks[c] + extra(g)
            mx = blocks[0]
            for blk_s in blocks[1:]:
                mx = jnp.maximum(mx, blk_s)
            m_old = m_ref[st, g]
            m_new = jnp.maximum(m_old, jnp.max(mx, axis=0, keepdims=True))
            m_row = m_new[0:1]
            pt = jnp.concatenate([jnp.exp2(blk_s - m_row).astype(_BF16) for blk_s in blocks], axis=0)
            acc_ref[st, g] = jnp.exp2(m_old - m_new)[0:1] * acc_ref[st, g] + _dot(vth, pt)
            m_ref[st, g] = m_new

    def finish(c, st):
        outs = []
        for g in range(NSA_KV_GROUPS):
            den = acc_ref[st, g, HEAD_DIM:HEAD_DIM + 1]
            outs.append(acc_ref[st, g, :HEAD_DIM] * jnp.where(den > 0.0, 1.0 / den, 0.0))
        for h in range(hpg):
            osum_ref[h] += gated(c, h, outs)

    def near_bias(d):
        return lambda g: bd_ref[g, d]

    def short_path(q_src, k_ref, v_ref, onehot, st):
        def far(kt, carry):
            chunk(q_src, k_ref, v_ref, pl.multiple_of(kt * TK, TK), TK, onehot, {}, st)
            return carry

        lax.fori_loop(0, jnp.maximum(qi - 1, 0), far, 0)

        @pl.when(qi >= 1)
        def _():
            chunk(q_src, k_ref, v_ref, pl.multiple_of((qi - 1) * TK, TK), TK, onehot,
                  {0: near_bias(1)}, st)

        chunk(q_src, k_ref, v_ref, pl.multiple_of(qi * TK, TK), TK, onehot, {0: near_bias(0)}, st)

    reset(SEL)
    reset(WIN)
    big = 4 * TK
    n_win = WINDOW // TK
    main_from = max(big // TK - 1, n_win)
    win_mask = jnp.where(lax.broadcasted_iota(jnp.int32, (TK, rows), 0)
                         > (lax.broadcasted_iota(jnp.int32, (TK, rows), 1) & (tq - 1)), 0.0, NEG)

    @pl.when(qi < main_from)
    def _():
        short_path(qsel_ref, ks_ref, vs_ref, True, SEL)
        short_path(qz_ref, kw_ref, vw_ref, False, WIN)

    @pl.when(qi >= main_from)
    def _():
        last = pl.multiple_of((qi - (big // TK - 1)) * TK, TK)
        n_far = (qi - (big // TK - 1) + big // TK - 1) >> int(math.log2(big // TK))
        near = {big // TK - 2: near_bias(1), big // TK - 1: near_bias(0)}

        def start_of(k):
            return pl.multiple_of(jnp.where(k < n_far, k * big, last), TK)

        def put(k, slot):
            for g, s in enumerate(scores(qsel_ref, ks_ref, start_of(k), big, True)):
                s_ref[slot, g] = s

        def take(k, slot, extras, limit):
            consume(lambda g, c: s_ref[slot, g, c * TK:(c + 1) * TK, :], vs_ref,
                    start_of(k), big, extras, SEL, limit)

        win_start = pl.multiple_of((qi - n_win) * TK, TK)
        for g, s in enumerate(scores(qz_ref, kw_ref, win_start, WINDOW + TK, False)):
            ws_ref[g] = s
        put(0, 0)
        consume(lambda g, c: ws_ref[g, c * TK:(c + 1) * TK, :], vw_ref, win_start, WINDOW + TK,
                {0: lambda g: win_mask, n_win - 1: near_bias(1), n_win: near_bias(0)}, WIN)
        n_pair = n_far >> 1

        def far_pair(i, carry):
            k = 2 * i
            put(k + 1, 1)
            take(k, 0, {}, last)
            put(k + 2, 0)
            take(k + 1, 1, {}, last)
            return carry

        lax.fori_loop(0, n_pair, far_pair, 0)

        @pl.when(n_far == 2 * n_pair)
        def _():
            take(n_far, 0, near, None)

        @pl.when(n_far != 2 * n_pair)
        def _():
            put(n_far, 1)
            take(n_far - 1, 0, {}, last)
            take(n_far, 1, near, None)

    finish(1, SEL)
    finish(2, WIN)

    for h in range(hpg):
        o_ref[0, h] = (osum_ref[h].T * _silu(z_ref[0, h])).astype(o_ref.dtype)


def _nsa_attention(pb3, pf3, vt, kvc, kvc_t, bias_c, bias_d, ov):
    b, _, t, _ = pb3.shape
    n_cmp = kvc.shape[2]
    n_blk = N_HEADS_NSA // 2
    full = lambda shape: pl.BlockSpec(shape, lambda i, q: (0,) * len(shape))
    kv_spec = lambda col: pl.BlockSpec((1, 1, t, LANES), lambda i, q: (i, col, 0, 0))
    vt_spec = lambda blk: pl.BlockSpec((1, 1, t // TK, LANES, TK), lambda i, q: (i, blk, 0, 0, 0))
    return pl.pallas_call(
        _nsa_kernel,
        grid=(b, t // TQ_NSA),
        in_specs=[pl.BlockSpec((1, n_blk, TQ_NSA, LANES), lambda i, q: (i, PB_QB // n_blk, q, 0)),
                  pl.BlockSpec((1, 1, n_cmp, LANES), lambda i, q: (i, 0, 0, 0)),
                  pl.BlockSpec((1, 1, LANES, n_cmp), lambda i, q: (i, 1, 0, 0)),
                  kv_spec(PB_KS), vt_spec(VT_VS), kv_spec(PB_KW), vt_spec(VT_VW),
                  pl.BlockSpec((1, 1, TQ_NSA, LANES), lambda i, q: (i, PF_GB, q, 0)),
                  pl.BlockSpec((1, n_blk, TQ_NSA, LANES), lambda i, q: (i, PF_ZB // n_blk, q, 0)),
                  full(bias_c.shape),
                  full(bias_d.shape), full(ov.shape)],
        out_specs=pl.BlockSpec((1, n_blk, TQ_NSA, LANES), lambda i, q: (i, 0, q, 0)),
        out_shape=jax.ShapeDtypeStruct((b, n_blk, t, LANES), _BF16),
        scratch_shapes=[pltpu.VMEM((2, NSA_KV_GROUPS, SUBLANES, NSA_HPG * TQ_NSA), _F32),
                        pltpu.VMEM((2, NSA_KV_GROUPS, HEAD_DIM + SUBLANES, NSA_HPG * TQ_NSA), _F32),
                        pltpu.VMEM((n_blk, TQ_NSA, LANES), _F32),
                        pltpu.VMEM((NSA_KV_GROUPS, NSA_HPG * TQ_NSA, LANES), _BF16),
                        pltpu.VMEM((NSA_KV_GROUPS, NSA_HPG * TQ_NSA, LANES), _BF16),
                        pltpu.VMEM((LANES, TQ_NSA), _F32),
                        pltpu.VMEM((NSA_KV_GROUPS, n_cmp, TQ_NSA), _F32),
                        pltpu.VMEM((2, NSA_KV_GROUPS, 4 * TK, NSA_HPG * TQ_NSA), _F32),
                        pltpu.VMEM((NSA_KV_GROUPS, WINDOW + TK, NSA_HPG * TQ_NSA), _F32)],
        compiler_params=_cparams(("arbitrary", "arbitrary")),
        name="native_sparse_attention",
    )(pb3, kvc, kvc_t, pb3, vt, pb3, vt, pf3, pf3, bias_c, bias_d, ov)


def _out_kernel(x_ref, oa_ref, ob_ref, oc_ref, w_ref, g_ref, o_ref, *, final_norm):
    mixed = jnp.concatenate([o_ref_in[0, j] for o_ref_in in (oa_ref, ob_ref, oc_ref)
                             for j in range(o_ref_in.shape[1])], axis=1)
    x = x_ref[...] + _dot(mixed, w_ref[...])
    if final_norm:
        x = x * lax.rsqrt(jnp.mean(x * x, axis=-1, keepdims=True) + RMS_EPS) * g_ref[...]
    o_ref[...] = x


def _out_proj(x2, oa, ob, oc, w, g, final_norm):
    n = x2.shape[0]
    per_seq = oa.shape[2] // TM_PROJ
    row = lambda width: pl.BlockSpec((TM_PROJ, width), lambda i: (i, 0))
    blocks = lambda a: pl.BlockSpec((1, a.shape[1], TM_PROJ, LANES),
                                    lambda i: (i // per_seq, 0, i % per_seq, 0))
    return pl.pallas_call(
        functools.partial(_out_kernel, final_norm=final_norm),
        grid=(n // TM_PROJ,),
        in_specs=[row(D_MODEL), blocks(oa), blocks(ob), blocks(oc),
                  pl.BlockSpec((D_MODEL, D_MODEL), lambda i: (0, 0)),
                  pl.BlockSpec((1, D_MODEL), lambda i: (0, 0))],
        out_specs=row(D_MODEL),
        out_shape=jax.ShapeDtypeStruct((n, D_MODEL), _F32),
        compiler_params=_cparams(("arbitrary",)),
        name="out_proj_residual",
    )(x2, oa, ob, oc, w, g)


def _head_perm_cols(width_per_head, order):
    return np.concatenate([np.arange(h * width_per_head, (h + 1) * width_per_head) for h in order])


def _layout_w_in(w):
    widths = [256, 256, 256, 256, 512, 128, 128, 128, 128, 128, 128, 24, 512, 256, 256, 256, 4, 256]
    offs = np.concatenate([[0], np.cumsum(widths)])
    (qa, ka, va, za, qb, kc, vc, ks, vs, kw, vw, gb, zb, qc, kf, vf, fc, zc) = [
        w[:, offs[i]:offs[i + 1]] for i in range(len(widths))]
    scale = HEAD_DIM ** -0.5
    perm = _head_perm_cols(HEAD_DIM, NSA_HEAD_ORDER)
    pad = lambda a: jnp.pad(a, ((0, 0), (0, LANES - a.shape[1])))
    wb = jnp.concatenate([qb[:, perm] * (scale * LOG2E), qa * (-scale), ka, va, ks, kw,
                          qc * (scale * LOG2E), kf], axis=1)
    wf = jnp.concatenate([zb[:, perm], za, zc, pad(gb), pad(fc)], axis=1)
    wc = jnp.concatenate([kc, vc], axis=1)
    wvt = jnp.concatenate([vs, vw, vf], axis=1)
    return wb.astype(_BF16), wf.astype(_BF16), wc.astype(_BF16), wvt.astype(_BF16)


def _static_tables(t):
    tq = TQ_NSA
    n_cmp_pad = t // CMP_STRIDE
    j = np.arange(n_cmp_pad)
    rel = np.arange(2 * n_cmp_pad) - n_cmp_pad
    dist_c = np.arange(tq)[None, :] - (rel[:, None] * CMP_STRIDE + CMP_BLOCK - 1)
    i_, j_ = np.arange(tq)[:, None], np.arange(TK)[None, :]
    dist_d = np.concatenate([d * TK + i_ - j_ for d in range(3)], axis=0)
    n_slc = LANES // NSA_KV_GROUPS
    cmp_start = j * CMP_STRIDE
    cmp_end = cmp_start + CMP_BLOCK - 1
    slc_start = np.arange(n_slc) * SLC_BLOCK
    ov1 = np.clip(np.minimum(cmp_end[:, None], slc_start[None, :] + SLC_BLOCK - 1)
                  - np.maximum(cmp_start[:, None], slc_start[None, :]) + 1, 0, None) / CMP_BLOCK
    ov1[n_cmp_pad - 1:] = 0.0
    ov1[:, t // SLC_BLOCK:] = 0.0
    ov3 = np.concatenate([ov1.T, ov1.T, ov1.T], axis=1)
    return jnp.asarray(dist_c, jnp.int32), jnp.asarray(dist_d, jnp.int32), jnp.asarray(ov3, _BF16)


def kernel(x, norm_g, w_in, w_out, forget_b, cmp_w1, cmp_b1, cmp_w2, cmp_pe, rel_bias, final_g):
    b, t, d = x.shape
    depth = norm_g.shape[0]
    assert d == D_MODEL and t % TM_PROJ == 0 and t % (CMP_STRIDE * LANES) == 0
    assert t // SLC_BLOCK <= LANES // NSA_KV_GROUPS and TQ_NSA == TK
    n_cmp_pad = t // CMP_STRIDE

    dist_c, dist_d, ov3 = _static_tables(t)
    bias_c = _bias_table(rel_bias * LOG2E, dist_c, 32)
    bias_d = _bias_table(rel_bias * LOG2E, dist_d, 32).reshape(N_HEADS_NSA, 3, TQ_NSA, TK)
    bias_d = bias_d[:, :2] - bias_d[:, 2:3]
    bias_d = bias_d.reshape(NSA_KV_GROUPS, NSA_HPG, 2, TQ_NSA, TK).transpose(0, 2, 4, 1, 3).reshape(
        NSA_KV_GROUPS, 2, TK, NSA_HPG * TQ_NSA)

    perm_rows = _head_perm_cols(HEAD_DIM, NSA_HEAD_ORDER)
    x2 = x.reshape(b * t, d)
    out = None
    for l in range(depth):
        wb, wf, wc, wvt = _layout_w_in(w_in[l])
        pb, pf, pc, vt = _proj(x2, norm_g[l].reshape(1, d), wb, wf, wc, wvt, t)
        pb3, pf3 = pb, pf

        fb_row = jnp.pad(forget_b[l], (0, LANES - N_HEADS_FOX)).reshape(1, LANES)
        augq, augk = _fgate(pf3, fb_row)

        halves = pc.reshape(b, 2, NSA_KV_GROUPS, n_cmp_pad, CMP_STRIDE * HEAD_DIM)
        w2 = cmp_w2[l]
        zeros = jnp.zeros_like(w2)
        w2p = jnp.stack([jnp.concatenate([w2, zeros], axis=-1),
                         jnp.concatenate([zeros, w2], axis=-1)], axis=1).astype(_BF16)
        kvc, kvc_t = _compress(halves, cmp_w1[l].astype(_BF16),
                               cmp_pe[l].reshape(2, 1, CMP_BLOCK * HEAD_DIM).astype(_BF16),
                               cmp_b1[l].reshape(2, 1, CMP_HIDDEN), w2p)

        o_a = _sb_attention(pb3, pf3)
        o_b = _nsa_attention(pb3, pf3, vt, kvc, kvc_t, bias_c, bias_d, ov3)
        o_c = _fox_attention(pb3, pf3, vt, augq, augk)

        wo = w_out[l]
        wo = jnp.concatenate([wo[:N_HEADS_SB * HEAD_DIM],
                              wo[N_HEADS_SB * HEAD_DIM:][:N_HEADS_NSA * HEAD_DIM][perm_rows],
                              wo[(N_HEADS_SB + N_HEADS_NSA) * HEAD_DIM:]], axis=0).astype(_BF16)
        last = l == depth - 1
        x2 = _out_proj(x2, o_a, o_b, o_c, wo, final_g.reshape(1, d), last)
    return x2.reshape(b, t, d)
```

```python
import functools
import math

import jax
import jax.numpy as jnp
import numpy as np
from jax import lax
from jax.experimental import pallas as pl
from jax.experimental.pallas import tpu as pltpu

D_MODEL = 1024
HEAD_DIM = 64
N_HEADS_SB = 4
N_HEADS_FOX = 4
N_HEADS_NSA = 8
NSA_KV_GROUPS = 2
NSA_HPG = N_HEADS_NSA // NSA_KV_GROUPS
CMP_BLOCK = 32
CMP_STRIDE = 16
CMP_HIDDEN = 256
SLC_BLOCK = 64
SLC_TOP = 16
WINDOW = 512
REL_BUCKETS = 32
REL_MAX_DIST = 128
FORCE_SCORE = 1e4
RMS_EPS = 1e-6
NEG = -1e30
LOG2E = math.log2(math.e)

LANES = 128
SUBLANES = 8
VMEM_LIMIT = 56 * 1024 * 1024

TM_PROJ = 512
TQ_SB = 512
TQ_FOX = 512
FOX_PARTS = 2
NSA_PARTS = 2
TQ_NSA = 128
TK = 128

PB_QB, PB_QA, PB_KA, PB_VA = 0, 4, 6, 8
PB_KS, PB_KW = 10, 11
PB_QC, PB_KF = 12, 14
PB_BLOCKS = 16
N_CMP_SLABS = 2 * NSA_KV_GROUPS
VT_VS, VT_VW, VT_VF = 0, 1, 2
N_VT = 4
PF_ZB, PF_ZA, PF_ZC, PF_GB, PF_FC = 0, 4, 6, 8, 9
PF_BLOCKS = 10

NSA_HEAD_ORDER = [0, 4, 1, 5, 2, 6, 3, 7]

_F32 = jnp.float32
_BF16 = jnp.bfloat16


def _cparams(sem):
    return pltpu.CompilerParams(dimension_semantics=sem, vmem_limit_bytes=VMEM_LIMIT)


def _dot(a, b):
    return jnp.dot(a, b, preferred_element_type=_F32)


def _dot_nt(a, b):
    return lax.dot_general(a, b, (((1,), (1,)), ((), ())), preferred_element_type=_F32)


def _split3(x):
    h1 = x.astype(_BF16)
    r1 = x - h1.astype(_F32)
    h2 = r1.astype(_BF16)
    h3 = (r1 - h2.astype(_F32)).astype(_BF16)
    return jnp.concatenate([h1, h2, h3], axis=1)


def _sigmoid(x):
    return 1.0 / (1.0 + jnp.exp(-x))


def _silu(x):
    return x * _sigmoid(x)


def _rel_bucket_np(n):
    n = np.maximum(n, 0)
    max_exact = REL_BUCKETS // 2
    nf = np.maximum(n, 1).astype(np.float64)
    large = max_exact + (np.log(nf / max_exact) / math.log(REL_MAX_DIST / max_exact)
                         * (REL_BUCKETS - max_exact)).astype(np.int64)
    large = np.minimum(large, REL_BUCKETS - 1)
    return np.where(n < max_exact, n, large)


def _bucket_thresholds():
    n = np.arange(0, 4 * REL_MAX_DIST)
    bk = _rel_bucket_np(n)
    assert np.all(np.diff(bk) >= 0) and bk[-1] == REL_BUCKETS - 1
    return [int(np.argmax(bk >= b)) for b in range(REL_BUCKETS)]


_BUCKET_THR = _bucket_thresholds()


def _bias_kernel(tab_ref, dist_ref, o_ref):
    n = dist_ref[...]
    acc = [jnp.full(n.shape, tab_ref[0, h], _F32) for h in range(N_HEADS_NSA)]
    for b in range(1, REL_BUCKETS):
        ge = n >= _BUCKET_THR[b]
        for h in range(N_HEADS_NSA):
            acc[h] = jnp.where(ge, tab_ref[b, h], acc[h])
    valid = n >= 0
    for h in range(N_HEADS_NSA):
        o_ref[h] = jnp.where(valid, acc[h], NEG)


def _bias_table(rel_bias, dist, rows):
    n_rows, n_cols = dist.shape
    return pl.pallas_call(
        _bias_kernel,
        grid=(n_rows // rows,),
        in_specs=[pl.BlockSpec(memory_space=pltpu.SMEM),
                  pl.BlockSpec((rows, n_cols), lambda i: (i, 0))],
        out_specs=pl.BlockSpec((N_HEADS_NSA, rows, n_cols), lambda i: (0, i, 0)),
        out_shape=jax.ShapeDtypeStruct((N_HEADS_NSA, n_rows, n_cols), _F32),
        compiler_params=_cparams(("arbitrary",)),
        name="rel_bias_table",
    )(rel_bias, dist)


def _proj_kernel(x_ref, g_ref, wb_ref, wf_ref, wc_ref, wvt_ref, pb_ref, pf_ref, pc_ref, vt_ref):
    x = x_ref[...]
    y = x * lax.rsqrt(jnp.mean(x * x, axis=-1, keepdims=True) + RMS_EPS)
    h = (y * g_ref[...]).astype(_BF16)
    v_all = _dot(h, wvt_ref[...])
    for j in range(N_VT):
        v_t = v_all[:, j * LANES:(j + 1) * LANES].T.astype(_BF16)
        for c in range(TM_PROJ // TK):
            vt_ref[0, j, c] = v_t[:, c * TK:(c + 1) * TK]
    kv_cmp = _dot(h, wc_ref[...]).astype(_BF16)
    for s in range(N_CMP_SLABS):
        pc_ref[0, s] = kv_cmp[:, s * HEAD_DIM:(s + 1) * HEAD_DIM]
    chunk = 4 * LANES
    for c in range(0, PB_BLOCKS * LANES, chunk):
        w = min(chunk, PB_BLOCKS * LANES - c)
        res = _dot(h, wb_ref[:, c:c + w]).astype(_BF16)
        for j in range(w // LANES):
            pb_ref[0, c // LANES + j] = res[:, j * LANES:(j + 1) * LANES]
    for c in range(0, PF_BLOCKS * LANES, chunk):
        w = min(chunk, PF_BLOCKS * LANES - c)
        res = _dot(h, wf_ref[:, c:c + w])
        for j in range(w // LANES):
            pf_ref[0, c // LANES + j] = res[:, j * LANES:(j + 1) * LANES]


def _proj(x2, g, wb, wf, wc, wvt, t):
    n = x2.shape[0]
    per_seq = t // TM_PROJ
    return pl.pallas_call(
        _proj_kernel,
        grid=(n // TM_PROJ,),
        in_specs=[pl.BlockSpec((TM_PROJ, D_MODEL), lambda i: (i, 0)),
                  pl.BlockSpec((1, D_MODEL), lambda i: (0, 0)),
                  pl.BlockSpec((D_MODEL, PB_BLOCKS * LANES), lambda i: (0, 0)),
                  pl.BlockSpec((D_MODEL, PF_BLOCKS * LANES), lambda i: (0, 0)),
                  pl.BlockSpec((D_MODEL, N_CMP_SLABS * HEAD_DIM), lambda i: (0, 0)),
                  pl.BlockSpec((D_MODEL, N_VT * LANES), lambda i: (0, 0))],
        out_specs=[pl.BlockSpec((1, PB_BLOCKS, TM_PROJ, LANES), lambda i: (i // per_seq, 0, i % per_seq, 0)),
                   pl.BlockSpec((1, PF_BLOCKS, TM_PROJ, LANES), lambda i: (i // per_seq, 0, i % per_seq, 0)),
                   pl.BlockSpec((1, N_CMP_SLABS, TM_PROJ, HEAD_DIM),
                                lambda i: (i // per_seq, 0, i % per_seq, 0)),
                   pl.BlockSpec((1, N_VT, TM_PROJ // TK, LANES, TK),
                                lambda i: (i // per_seq, 0, i % per_seq, 0, 0))],
        out_shape=[jax.ShapeDtypeStruct((n // t, PB_BLOCKS, t, LANES), _BF16),
                   jax.ShapeDtypeStruct((n // t, PF_BLOCKS, t, LANES), _F32),
                   jax.ShapeDtypeStruct((n // t, N_CMP_SLABS, t, HEAD_DIM), _BF16),
                   jax.ShapeDtypeStruct((n // t, N_VT, t // TK, LANES, TK), _BF16)],
        compiler_params=_cparams(("arbitrary",)),
        name="rmsnorm_in_proj",
    )(x2, g, wb, wf, wc, wvt)


N_SPLIT = 3


def _fgate_tables():
    n_pairs = N_HEADS_FOX // 2
    pq = np.zeros((N_SPLIT * LANES, n_pairs * LANES), np.float32)
    pk = np.zeros_like(pq)
    ones_q = np.zeros((1, n_pairs * LANES), np.float32)
    ones_k = np.zeros_like(ones_q)
    for head in range(N_HEADS_FOX):
        pair, slot = divmod(head, 2)
        base = pair * LANES + (HEAD_DIM if slot == 0 else 0)
        for j in range(N_SPLIT):
            pq[j * LANES + head, base + j] = 1.0
            pk[j * LANES + head, base + N_SPLIT + j] = 1.0
        ones_q[0, base + N_SPLIT:base + 2 * N_SPLIT] = 1.0
        ones_k[0, base:base + N_SPLIT] = 1.0
    return (jnp.asarray(pq, _BF16), jnp.asarray(pk, _BF16),
            jnp.asarray(ones_q), jnp.asarray(ones_k))


def _fgate_kernel(fc_ref, fb_ref, pq_ref, pk_ref, oq_ref, ok_ref, augq_ref, augk_ref):
    t = fc_ref.shape[2]
    z = fc_ref[0, 0] + fb_ref[...]
    logf = jnp.minimum(z, 0.0) - jnp.log1p(jnp.exp(-jnp.abs(z)))
    row = lax.broadcasted_iota(jnp.int32, (t, LANES), 0)
    c = logf
    shift = 1
    while shift < t:
        c = c + jnp.where(row >= shift, pltpu.roll(c, shift, axis=0), 0.0)
        shift *= 2
    c3 = _split3(c * LOG2E)
    aq = _dot(c3, pq_ref[...]) + oq_ref[...]
    ak = ok_ref[...] - _dot(c3, pk_ref[...])
    for p in range(N_HEADS_FOX // 2):
        augq_ref[0, p] = aq[:, p * LANES:(p + 1) * LANES].astype(_BF16)
        augk_ref[0, p] = ak[:, p * LANES:(p + 1) * LANES].astype(_BF16)


def _fgate(pf3, fb_row):
    b, _, t, _ = pf3.shape
    n_pairs = N_HEADS_FOX // 2
    tables = _fgate_tables()
    full = lambda a: pl.BlockSpec(a.shape, lambda i: (0,) * a.ndim)
    aug = lambda: pl.BlockSpec((1, n_pairs, t, LANES), lambda i: (i, 0, 0, 0))
    return pl.pallas_call(
        _fgate_kernel,
        grid=(b,),
        in_specs=[pl.BlockSpec((1, 1, t, LANES), lambda i: (i, PF_FC, 0, 0)),
                  pl.BlockSpec((1, LANES), lambda i: (0, 0))] + [full(a) for a in tables],
        out_specs=[aug(), aug()],
        out_shape=[jax.ShapeDtypeStruct((b, n_pairs, t, LANES), _BF16),
                   jax.ShapeDtypeStruct((b, n_pairs, t, LANES), _BF16)],
        compiler_params=_cparams(("arbitrary",)),
        name="forget_gate_cumsum",
    )(pf3, fb_row, *tables)


def _compress_kernel(x_ref, w1_ref, pe_ref, b1_ref, w2_ref, o_ref, ot_ref):
    nc = x_ref.shape[3]
    half = CMP_STRIDE * HEAD_DIM
    w1 = w1_ref[0]
    c1 = _dot(jnp.broadcast_to(pe_ref[0], (SUBLANES, 2 * half)), w1)[0:1] + b1_ref[0]
    out = jnp.zeros((nc, LANES), _F32)
    for g in range(NSA_KV_GROUPS):
        xg = x_ref[0, 0, g]
        a = _dot(xg, w1[:half])
        bb = _dot(xg, w1[half:])
        h = a + pltpu.roll(bb, nc - 1, axis=0) + c1
        out = out + _dot(_silu(h).astype(_BF16), w2_ref[0, g])
    o_ref[0, 0] = out.astype(_BF16)
    ot_ref[0, 0] = out.T.astype(_BF16)


def _compress(halves, w1, pe, b1, w2p):
    b, _, g, nc, width = halves.shape
    return pl.pallas_call(
        _compress_kernel,
        grid=(b, 2),
        in_specs=[pl.BlockSpec((1, 1, g, nc, width), lambda i, k: (i, k, 0, 0, 0)),
                  pl.BlockSpec((1, 2 * width, CMP_HIDDEN), lambda i, k: (k, 0, 0)),
                  pl.BlockSpec((1, 1, 2 * width), lambda i, k: (k, 0, 0)),
                  pl.BlockSpec((1, 1, CMP_HIDDEN), lambda i, k: (k, 0, 0)),
                  pl.BlockSpec((1, g, CMP_HIDDEN, LANES), lambda i, k: (k, 0, 0, 0))],
        out_specs=[pl.BlockSpec((1, 1, nc, LANES), lambda i, k: (i, k, 0, 0)),
                   pl.BlockSpec((1, 1, LANES, nc), lambda i, k: (i, k, 0, 0))],
        out_shape=[jax.ShapeDtypeStruct((b, 2, nc, LANES), _BF16),
                   jax.ShapeDtypeStruct((b, 2, LANES, nc), _BF16)],
        compiler_params=_cparams(("arbitrary", "arbitrary")),
        name="nsa_compress",
    )(halves, w1, pe, b1, w2p)


def _sb_kernel(q_ref, k_ref, v_ref, z_ref, o_ref, qh_ref, c_ref, acc_ref):
    qi = pl.program_id(2)
    tq = q_ref.shape[2]
    lane = lax.broadcasted_iota(jnp.int32, (tq, LANES), 1)
    q2 = q_ref[0, 0]
    qh = [jnp.where(lane < HEAD_DIM, q2, jnp.zeros_like(q2)),
          jnp.where(lane >= HEAD_DIM, q2, jnp.zeros_like(q2))]
    r_i = lax.broadcasted_iota(jnp.int32, (2 * TK, 2 * TK), 0)
    c_i = lax.broadcasted_iota(jnp.int32, (2 * TK, 2 * TK), 1)
    uu = jnp.where(r_i >= c_i, 1.0, 0.0).astype(_BF16)
    c_ref[...] = jnp.zeros_like(c_ref)
    acc_ref[...] = jnp.zeros_like(acc_ref)
    for h in range(2):
        qh_ref[h] = qh[h]
    row = lax.broadcasted_iota(jnp.int32, (tq, LANES), 0)
    n_blocks = tq // TK

    def chunk(start, diagonal):
        k2 = k_ref[0, 0, pl.ds(start, tq), :]
        v2 = v_ref[0, 0, pl.ds(start, tq), :]
        for h in range(2):
            s = _dot_nt(qh_ref[h], k2)
            carry = c_ref[h]
            w_blocks = [None] * n_blocks
            for c in reversed(range(0, n_blocks, 2)):
                nz, l1m, mask = [], [], []
                for cc in (c, c + 1):
                    nzc = s[:, cc * TK:(cc + 1) * TK]
                    neg_abs = lax.bitcast_convert_type(
                        lax.bitcast_convert_type(nzc, jnp.uint32) | jnp.uint32(0x80000000), _F32)
                    lc = jnp.minimum(nzc, 0.0) - jnp.log(1.0 + jnp.exp(neg_abs))
                    if diagonal:
                        mask.append(lane + cc * TK < row)
                        lc = jnp.where(mask[-1], lc, 0.0)
                    nz.append(nzc)
                    l1m.append(lc.astype(_BF16))
                rc = _dot(jnp.concatenate(l1m, axis=1), uu)
                for i, cc in enumerate((c, c + 1)):
                    w = jnp.exp((rc[:, i * TK:(i + 1) * TK] + carry) - nz[i])
                    if diagonal:
                        w = jnp.where(mask[i], w, 0.0)
                    w_blocks[cc] = w.astype(_BF16)
                carry = carry + jnp.broadcast_to(rc[:, 0:1], carry.shape)
            acc_ref[h] += _dot(jnp.concatenate(w_blocks, axis=1), v2)
            c_ref[h] = carry

    chunk(pl.multiple_of(qi * tq, tq), True)

    def far_pair(it, carry):
        chunk(pl.multiple_of((qi - 1 - 2 * it) * tq, tq), False)
        chunk(pl.multiple_of((qi - 2 - 2 * it) * tq, tq), False)
        return carry

    lax.fori_loop(0, qi >> 1, far_pair, 0)

    @pl.when((qi & 1) == 1)
    def _():
        chunk(0, False)
    o = jnp.where(lane < HEAD_DIM, acc_ref[0], acc_ref[1])
    o_ref[0, 0] = (o * _silu(z_ref[0, 0])).astype(o_ref.dtype)


def _sb_attention(pb3, pf3):
    b, _, t, _ = pb3.shape
    n_pairs = N_HEADS_SB // 2
    return pl.pallas_call(
        _sb_kernel,
        grid=(b, n_pairs, t // TQ_SB),
        in_specs=[pl.BlockSpec((1, 1, TQ_SB, LANES), lambda i, p, q: (i, PB_QA + p, q, 0)),
                  pl.BlockSpec((1, 1, t, LANES), lambda i, p, q: (i, PB_KA + p, 0, 0)),
                  pl.BlockSpec((1, 1, t, LANES), lambda i, p, q: (i, PB_VA + p, 0, 0)),
                  pl.BlockSpec((1, 1, TQ_SB, LANES), lambda i, p, q: (i, PF_ZA + p, q, 0))],
        out_specs=pl.BlockSpec((1, 1, TQ_SB, LANES), lambda i, p, q: (i, p, q, 0)),
        out_shape=jax.ShapeDtypeStruct((b, n_pairs, t, LANES), _BF16),
        scratch_shapes=[pltpu.VMEM((2, TQ_SB, LANES), _BF16),
                        pltpu.VMEM((2, TQ_SB, LANES), _F32),
                        pltpu.VMEM((2, TQ_SB, LANES), _F32)],
        compiler_params=_cparams(("arbitrary", "arbitrary", "arbitrary")),
        name="stick_breaking_attention",
    )(pb3, pb3, pb3, pf3)


def _fox_kernel(q_ref, k_ref, vt_ref, augq_ref, augk_ref, z_ref, o_ref,
                qh_ref, m_ref, acc_ref, s_ref):
    qi = pl.program_id(2)
    tq = q_ref.shape[2]
    tk = tq
    lane = lax.broadcasted_iota(jnp.int32, (tq, LANES), 1)
    keep = [lane < HEAD_DIM, lane >= HEAD_DIM]
    q2 = q_ref[0, 0]
    aq = augq_ref[0, 0]
    for h in range(2):
        qh_ref[h] = jnp.where(keep[h], q2, aq)
    m_ref[...] = jnp.full_like(m_ref, NEG)
    acc_ref[...] = jnp.zeros_like(acc_ref)
    key_i = lax.broadcasted_iota(jnp.int32, (TK, tq), 0)
    qry_i = lax.broadcasted_iota(jnp.int32, (TK, tq), 1)

    def scores(c, slot):
        start = c * tk if isinstance(c, int) else pl.multiple_of(c * tk, tk)
        k2 = k_ref[0, 0, pl.ds(start, tk), :]
        ak = augk_ref[0, 0, pl.ds(start, tk), :]
        for h in range(2):
            s_ref[slot, h] = _dot_nt(jnp.where(keep[h], k2, ak), qh_ref[h])

    ones_rows = jnp.ones((SUBLANES, tk), _BF16)

    def consume(c, slot, diagonal):
        per = tk // TK
        vt = jnp.concatenate([vt_ref[0, 0, c * per + j] for j in range(per)], axis=1)
        for h in range(2):
            vth = jnp.concatenate([vt[h * HEAD_DIM:(h + 1) * HEAD_DIM], ones_rows], axis=0)
            for part in range(FOX_PARTS):
                js = range(part * per // FOX_PARTS, (part + 1) * per // FOX_PARTS)
                blocks = [s_ref[slot, h, j * TK:(j + 1) * TK, :] for j in js]
                if diagonal:
                    blocks = [jnp.where(key_i + j * TK <= qry_i, blk, NEG) for j, blk in zip(js, blocks)]
                mx = blocks[0]
                for blk in blocks[1:]:
                    mx = jnp.maximum(mx, blk)
                m_old = m_ref[h]
                m_new = jnp.maximum(m_old, jnp.max(mx, axis=0, keepdims=True))
                m_row = m_new[0:1]
                pt = jnp.concatenate([jnp.exp2(blk - m_row).astype(_BF16) for blk in blocks], axis=0)
                acc_ref[h] = (jnp.exp2(m_old - m_new)[0:1] * acc_ref[h]
                              + _dot(vth[:, js[0] * TK:(js[-1] + 1) * TK], pt))
                m_ref[h] = m_new

    scores(0, 0)
    n_pairs = qi >> 1

    def far_pair(i, carry):
        c = 2 * i
        scores(c + 1, 1)
        consume(c, 0, False)
        scores(c + 2, 0)
        consume(c + 1, 1, False)
        return carry

    lax.fori_loop(0, n_pairs, far_pair, 0)

    @pl.when(qi == 2 * n_pairs)
    def _():
        consume(qi, 0, True)

    @pl.when(qi != 2 * n_pairs)
    def _():
        scores(qi, 1)
        consume(qi - 1, 0, False)
        consume(qi, 1, True)
    o_t = jnp.concatenate([acc_ref[h, :HEAD_DIM] / acc_ref[h, HEAD_DIM:HEAD_DIM + 1]
                           for h in range(2)], axis=0)
    o_ref[0, 0] = (o_t.T * _silu(z_ref[0, 0])).astype(o_ref.dtype)


def _fox_attention(pb3, pf3, vt, augq, augk):
    b, _, t, _ = pb3.shape
    n_pairs = N_HEADS_FOX // 2
    return pl.pallas_call(
        _fox_kernel,
        grid=(b, n_pairs, t // TQ_FOX),
        in_specs=[pl.BlockSpec((1, 1, TQ_FOX, LANES), lambda i, p, q: (i, PB_QC + p, q, 0)),
                  pl.BlockSpec((1, 1, t, LANES), lambda i, p, q: (i, PB_KF + p, 0, 0)),
                  pl.BlockSpec((1, 1, t // TK, LANES, TK), lambda i, p, q: (i, VT_VF + p, 0, 0, 0)),
                  pl.BlockSpec((1, 1, TQ_FOX, LANES), lambda i, p, q: (i, p, q, 0)),
                  pl.BlockSpec((1, 1, t, LANES), lambda i, p, q: (i, p, 0, 0)),
                  pl.BlockSpec((1, 1, TQ_FOX, LANES), lambda i, p, q: (i, PF_ZC + p, q, 0))],
        out_specs=pl.BlockSpec((1, 1, TQ_FOX, LANES), lambda i, p, q: (i, p, q, 0)),
        out_shape=jax.ShapeDtypeStruct((b, n_pairs, t, LANES), _BF16),
        scratch_shapes=[pltpu.VMEM((2, TQ_FOX, LANES), _BF16),
                        pltpu.VMEM((2, SUBLANES, TQ_FOX), _F32),
                        pltpu.VMEM((2, HEAD_DIM + SUBLANES, TQ_FOX), _F32),
                        pltpu.VMEM((2, 2, TQ_FOX, TQ_FOX), _F32)],
        compiler_params=_cparams(("arbitrary", "arbitrary", "arbitrary")),
        name="forgetting_attention",
    )(pb3, pb3, vt, augq, augk, pf3)


def _nsa_kernel(q_ref, kc_ref, vct_ref, ks_ref, vs_ref, kw_ref, vw_ref, gl_ref, z_ref,
                bc_ref, bd_ref, ov_ref, o_ref,
                m_ref, acc_ref, osum_ref, qz_ref, qsel_ref, gates_ref, psum_ref, s_ref, ws_ref):
    qi = pl.program_id(1)
    tq = q_ref.shape[2]
    hpg = NSA_HPG
    rows = hpg * tq
    n_cmp = kc_ref.shape[2]
    lane_r = lax.broadcasted_iota(jnp.int32, (rows, LANES), 1)
    half = [lane_r < HEAD_DIM, lane_r >= HEAD_DIM]

    gates_ref[...] = _sigmoid(gl_ref[0, 0]).T

    def gated(c, h, per_group):
        parts = []
        for g, o in enumerate(per_group):
            r = 3 * (g * hpg + h) + c
            parts.append(gates_ref[r:r + 1, :] * o[:, h * tq:(h + 1) * tq])
        return jnp.concatenate(parts, axis=0)

    q4 = jnp.concatenate([q_ref[0, h] for h in range(hpg)], axis=0)
    for g in range(NSA_KV_GROUPS):
        qz_ref[g] = jnp.where(half[g], q4, jnp.zeros_like(q4))

    def scores(q_src, k_ref, start, tk, onehot):
        k2 = k_ref[0, 0, pl.ds(start, tk), :]
        lane_k = lax.broadcasted_iota(jnp.int32, (tk, LANES), 1)
        key_blk = (start + lax.broadcasted_iota(jnp.int32, (tk, LANES), 0)) >> int(math.log2(SLC_BLOCK))
        oh = jnp.where((lane_k & (SLC_BLOCK - 1)) == key_blk, 1.0, 0.0).astype(_BF16)
        out = []
        for g in range(NSA_KV_GROUPS):
            keep = (lane_k < HEAD_DIM) if g == 0 else (lane_k >= HEAD_DIM)
            out.append(_dot_nt(jnp.where(keep, k2, oh) if onehot else k2, q_src[g]))
        return out


    kc = kc_ref[0, 0]
    vct = vct_ref[0, 0]
    cmp_row0 = pl.multiple_of(n_cmp - qi * (tq // CMP_STRIDE), SUBLANES)
    o_cmp_t = []
    raw = [_dot_nt(kc, qz_ref[g]) for g in range(NSA_KV_GROUPS)]
    for g in range(NSA_KV_GROUPS):
        bias = jnp.concatenate([bc_ref[g * hpg + h, pl.ds(cmp_row0, n_cmp), :]
                                for h in range(hpg)], axis=1)
        sc = raw[g] + bias
        mx = jnp.max(sc, axis=0, keepdims=True)
        e = jnp.exp2(sc - mx)
        den = jnp.sum(e, axis=0, keepdims=True)
        pc = e * jnp.where(mx > 0.5 * NEG, 1.0 / den, 0.0)
        psum = pc[:, 0:tq]
        for h in range(1, hpg):
            psum = psum + pc[:, h * tq:(h + 1) * tq]
        psum_ref[g] = psum
        o_cmp_t.append(_dot(vct[g * HEAD_DIM:(g + 1) * HEAD_DIM], pc.astype(_BF16)))
    for h in range(hpg):
        osum_ref[h] = gated(0, h, o_cmp_t)

    rank_from = SLC_TOP * SLC_BLOCK // tq

    @pl.when(qi < rank_from)
    def _():
        for g in range(NSA_KV_GROUPS):
            qsel_ref[g] = qz_ref[g]

    @pl.when(qi >= rank_from)
    def _():
        n_sel = LANES // NSA_KV_GROUPS
        n_grp = n_sel // SUBLANES
        blk = lax.broadcasted_iota(jnp.int32, (n_sel, tq), 0)
        cur = (qi * tq + lax.broadcasted_iota(jnp.int32, (n_sel, tq), 1)) >> int(math.log2(SLC_BLOCK))
        forced = (blk == 0) | (blk == cur) | (blk == cur - 1)
        sub = lax.broadcasted_iota(jnp.int32, (SUBLANES, tq), 0)
        neg_t = []
        for g in range(NSA_KV_GROUPS):
            p = psum_ref[g]
            p1 = p.astype(_BF16)
            r1 = p - p1.astype(_F32)
            p2 = r1.astype(_BF16)
            p3 = (r1 - p2.astype(_F32)).astype(_BF16)
            a = _dot(ov_ref[...], jnp.concatenate([p1, p2, p3], axis=0))
            a = jnp.where(forced, FORCE_SCORE, a)
            a = jnp.where(blk > cur, -FORCE_SCORE, a)
            a_grp = [a[r * SUBLANES:(r + 1) * SUBLANES] for r in range(n_grp)]
            cnt = [jnp.zeros((SUBLANES, tq), _F32) for _ in range(n_grp)]
            for j in range(n_sel):
                rj = jnp.broadcast_to(a[j:j + 1], (SUBLANES, tq))
                jr, jo = divmod(j, SUBLANES)
                for r in range(n_grp):
                    if r > jr:
                        one = jnp.where(rj >= a_grp[r], 1.0, 0.0)
                    elif r < jr:
                        one = jnp.where(rj > a_grp[r], 1.0, 0.0)
                    else:
                        tie = jnp.where(sub > jo, jnp.where(rj == a_grp[r], 1.0, 0.0), 0.0)
                        one = jnp.where(rj > a_grp[r], 1.0, tie)
                    cnt[r] = cnt[r] + one
            neg_t.append(jnp.where(jnp.concatenate(cnt, axis=0) < float(SLC_TOP), 0.0, NEG))
        selneg = jnp.concatenate(neg_t[::-1], axis=0).T.astype(_BF16)
        selneg4 = jnp.concatenate([selneg] * hpg, axis=0)
        for g in range(NSA_KV_GROUPS):
            qsel_ref[g] = jnp.where(half[g], q4, selneg4)

    SEL, WIN = 0, 1

    def reset(st):
        m_ref[st] = jnp.full(m_ref.shape[1:], 2.0 * NEG, _F32)
        acc_ref[st] = jnp.zeros(acc_ref.shape[1:], _F32)

    def chunk(q_src, k_ref, vt_ref, start, tk, onehot, extras, st):
        s = scores(q_src, k_ref, start, tk, onehot)
        consume(lambda g, c: s[g][c * TK:(c + 1) * TK], vt_ref, start, tk, extras, st)

    def consume(block_of, vt_ref, start, tk, extras, st, limit=None):
        first = start // TK if isinstance(start, int) else start >> int(math.log2(TK))
        vt = jnp.concatenate([vt_ref[0, 0, first + j] for j in range(tk // TK)], axis=1)
        ones_rows = jnp.ones((SUBLANES, tk), _BF16)
        key_pos = start + lax.broadcasted_iota(jnp.int32, (HEAD_DIM + SUBLANES, tk), 1)
        for g in range(NSA_KV_GROUPS):
            vth = jnp.concatenate([vt[g * HEAD_DIM:(g + 1) * HEAD_DIM], ones_rows], axis=0)
            if limit is not None:
                vth = jnp.where(key_pos < limit, vth, jnp.zeros_like(vth))
            n_blk = tk // TK
            n_parts = NSA_PARTS if n_blk >= 2 * NSA_PARTS else 1
            for part in range(n_parts):
                cs = range(part * n_blk // n_parts, (part + 1) * n_blk // n_parts)
                blocks = [block_of(g, c) + extras[c](g) if c in extras else block_of(g, c) for c in cs]
                mx = blocks[0]
                for blk_s in blocks[1:]:
                    mx = jnp.maximum(mx, blk_s)
                m_old = m_ref[st, g]
                m_new = jnp.maximum(m_old, jnp.max(mx, axis=0, keepdims=True))
                m_row = m_new[0:1]
                pt = jnp.concatenate([jnp.exp2(blk_s - m_row).astype(_BF16) for blk_s in blocks], axis=0)
                acc_ref[st, g] = (jnp.exp2(m_old - m_new)[0:1] * acc_ref[st, g]
                                  + _dot(vth[:, cs[0] * TK:(cs[-1] + 1) * TK], pt))
                m_ref[st, g] = m_new

    def finish(c, st):
        outs = []
        for g in range(NSA_KV_GROUPS):
            den = acc_ref[st, g, HEAD_DIM:HEAD_DIM + 1]
            outs.append(acc_ref[st, g, :HEAD_DIM] * jnp.where(den > 0.0, 1.0 / den, 0.0))
        for h in range(hpg):
            osum_ref[h] += gated(c, h, outs)

    def near_bias(d):
        return lambda g: bd_ref[g, d]

    def short_path(q_src, k_ref, v_ref, onehot, st):
        def far(kt, carry):
            chunk(q_src, k_ref, v_ref, pl.multiple_of(kt * TK, TK), TK, onehot, {}, st)
            return carry

        lax.fori_loop(0, jnp.maximum(qi - 1, 0), far, 0)

        @pl.when(qi >= 1)
        def _():
            chunk(q_src, k_ref, v_ref, pl.multiple_of((qi - 1) * TK, TK), TK, onehot,
                  {0: near_bias(1)}, st)

        chunk(q_src, k_ref, v_ref, pl.multiple_of(qi * TK, TK), TK, onehot, {0: near_bias(0)}, st)

    reset(SEL)
    reset(WIN)
    big = 4 * TK
    n_win = WINDOW // TK
    main_from = max(big // TK - 1, n_win)
    win_mask = jnp.where(lax.broadcasted_iota(jnp.int32, (TK, rows), 0)
                         > (lax.broadcasted_iota(jnp.int32, (TK, rows), 1) & (tq - 1)), 0.0, NEG)

    @pl.when(qi < main_from)
    def _():
        short_path(qsel_ref, ks_ref, vs_ref, True, SEL)
        short_path(qz_ref, kw_ref, vw_ref, False, WIN)

    @pl.when(qi >= main_from)
    def _():
        last = pl.multiple_of((qi - (big // TK - 1)) * TK, TK)
        n_far = (qi - (big // TK - 1) + big // TK - 1) >> int(math.log2(big // TK))
        near = {big // TK - 2: near_bias(1), big // TK - 1: near_bias(0)}

        def start_of(k):
            return pl.multiple_of(jnp.where(k < n_far, k * big, last), TK)

        def put(k, slot):
            for g, s in enumerate(scores(qsel_ref, ks_ref, start_of(k), big, True)):
                s_ref[slot, g] = s

        def take(k, slot, extras, limit):
            consume(lambda g, c: s_ref[slot, g, c * TK:(c + 1) * TK, :], vs_ref,
                    start_of(k), big, extras, SEL, limit)

        win_start = pl.multiple_of((qi - n_win) * TK, TK)
        for g, s in enumerate(scores(qz_ref, kw_ref, win_start, WINDOW + TK, False)):
            ws_ref[g] = s
        put(0, 0)
        consume(lambda g, c: ws_ref[g, c * TK:(c + 1) * TK, :], vw_ref, win_start, WINDOW + TK,
                {0: lambda g: win_mask, n_win - 1: near_bias(1), n_win: near_bias(0)}, WIN)
        n_pair = n_far >> 1

        def far_pair(i, carry):
            k = 2 * i
            put(k + 1, 1)
            take(k, 0, {}, last)
            put(k + 2, 0)
            take(k + 1, 1, {}, last)
            return carry

        lax.fori_loop(0, n_pair, far_pair, 0)

        @pl.when(n_far == 2 * n_pair)
        def _():
            take(n_far, 0, near, None)

        @pl.when(n_far != 2 * n_pair)
        def _():
            put(n_far, 1)
            take(n_far - 1, 0, {}, last)
            take(n_far, 1, near, None)

    finish(1, SEL)
    finish(2, WIN)

    for h in range(hpg):
        o_ref[0, h] = (osum_ref[h].T * _silu(z_ref[0, h])).astype(o_ref.dtype)


def _nsa_attention(pb3, pf3, vt, kvc, kvc_t, bias_c, bias_d, ov):
    b, _, t, _ = pb3.shape
    n_cmp = kvc.shape[2]
    n_blk = N_HEADS_NSA // 2
    full = lambda shape: pl.BlockSpec(shape, lambda i, q: (0,) * len(shape))
    kv_spec = lambda col: pl.BlockSpec((1, 1, t, LANES), lambda i, q: (i, col, 0, 0))
    vt_spec = lambda blk: pl.BlockSpec((1, 1, t // TK, LANES, TK), lambda i, q: (i, blk, 0, 0, 0))
    return pl.pallas_call(
        _nsa_kernel,
        grid=(b, t // TQ_NSA),
        in_specs=[pl.BlockSpec((1, n_blk, TQ_NSA, LANES), lambda i, q: (i, PB_QB // n_blk, q, 0)),
                  pl.BlockSpec((1, 1, n_cmp, LANES), lambda i, q: (i, 0, 0, 0)),
                  pl.BlockSpec((1, 1, LANES, n_cmp), lambda i, q: (i, 1, 0, 0)),
                  kv_spec(PB_KS), vt_spec(VT_VS), kv_spec(PB_KW), vt_spec(VT_VW),
                  pl.BlockSpec((1, 1, TQ_NSA, LANES), lambda i, q: (i, PF_GB, q, 0)),
                  pl.BlockSpec((1, n_blk, TQ_NSA, LANES), lambda i, q: (i, PF_ZB // n_blk, q, 0)),
                  full(bias_c.shape),
                  full(bias_d.shape), full(ov.shape)],
        out_specs=pl.BlockSpec((1, n_blk, TQ_NSA, LANES), lambda i, q: (i, 0, q, 0)),
        out_shape=jax.ShapeDtypeStruct((b, n_blk, t, LANES), _BF16),
        scratch_shapes=[pltpu.VMEM((2, NSA_KV_GROUPS, SUBLANES, NSA_HPG * TQ_NSA), _F32),
                        pltpu.VMEM((2, NSA_KV_GROUPS, HEAD_DIM + SUBLANES, NSA_HPG * TQ_NSA), _F32),
                        pltpu.VMEM((n_blk, TQ_NSA, LANES), _F32),
                        pltpu.VMEM((NSA_KV_GROUPS, NSA_HPG * TQ_NSA, LANES), _BF16),
                        pltpu.VMEM((NSA_KV_GROUPS, NSA_HPG * TQ_NSA, LANES), _BF16),
                        pltpu.VMEM((LANES, TQ_NSA), _F32),
                        pltpu.VMEM((NSA_KV_GROUPS, n_cmp, TQ_NSA), _F32),
                        pltpu.VMEM((2, NSA_KV_GROUPS, 4 * TK, NSA_HPG * TQ_NSA), _F32),
                        pltpu.VMEM((NSA_KV_GROUPS, WINDOW + TK, NSA_HPG * TQ_NSA), _F32)],
        compiler_params=_cparams(("arbitrary", "arbitrary")),
        name="native_sparse_attention",
    )(pb3, kvc, kvc_t, pb3, vt, pb3, vt, pf3, pf3, bias_c, bias_d, ov)


def _out_kernel(x_ref, oa_ref, ob_ref, oc_ref, w_ref, g_ref, o_ref, *, final_norm):
    mixed = jnp.concatenate([o_ref_in[0, j] for o_ref_in in (oa_ref, ob_ref, oc_ref)
                             for j in range(o_ref_in.shape[1])], axis=1)
    x = x_ref[...] + _dot(mixed, w_ref[...])
    if final_norm:
        x = x * lax.rsqrt(jnp.mean(x * x, axis=-1, keepdims=True) + RMS_EPS) * g_ref[...]
    o_ref[...] = x


def _out_proj(x2, oa, ob, oc, w, g, final_norm):
    n = x2.shape[0]
    per_seq = oa.shape[2] // TM_PROJ
    row = lambda width: pl.BlockSpec((TM_PROJ, width), lambda i: (i, 0))
    blocks = lambda a: pl.BlockSpec((1, a.shape[1], TM_PROJ, LANES),
                                    lambda i: (i // per_seq, 0, i % per_seq, 0))
    return pl.pallas_call(
        functools.partial(_out_kernel, final_norm=final_norm),
        grid=(n // TM_PROJ,),
        in_specs=[row(D_MODEL), blocks(oa), blocks(ob), blocks(oc),
                  pl.BlockSpec((D_MODEL, D_MODEL), lambda i: (0, 0)),
                  pl.BlockSpec((1, D_MODEL), lambda i: (0, 0))],
        out_specs=row(D_MODEL),
        out_shape=jax.ShapeDtypeStruct((n, D_MODEL), _F32),
        compiler_params=_cparams(("arbitrary",)),
        name="out_proj_residual",
    )(x2, oa, ob, oc, w, g)


def _head_perm_cols(width_per_head, order):
    return np.concatenate([np.arange(h * width_per_head, (h + 1) * width_per_head) for h in order])


def _layout_w_in(w):
    widths = [256, 256, 256, 256, 512, 128, 128, 128, 128, 128, 128, 24, 512, 256, 256, 256, 4, 256]
    offs = np.concatenate([[0], np.cumsum(widths)])
    (qa, ka, va, za, qb, kc, vc, ks, vs, kw, vw, gb, zb, qc, kf, vf, fc, zc) = [
        w[:, offs[i]:offs[i + 1]] for i in range(len(widths))]
    scale = HEAD_DIM ** -0.5
    perm = _head_perm_cols(HEAD_DIM, NSA_HEAD_ORDER)
    pad = lambda a: jnp.pad(a, ((0, 0), (0, LANES - a.shape[1])))
    wb = jnp.concatenate([qb[:, perm] * (scale * LOG2E), qa * (-scale), ka, va, ks, kw,
                          qc * (scale * LOG2E), kf], axis=1)
    wf = jnp.concatenate([zb[:, perm], za, zc, pad(gb), pad(fc)], axis=1)
    wc = jnp.concatenate([kc, vc], axis=1)
    wvt = jnp.concatenate([vs, vw, vf], axis=1)
    return wb.astype(_BF16), wf.astype(_BF16), wc.astype(_BF16), wvt.astype(_BF16)


def _static_tables(t):
    tq = TQ_NSA
    n_cmp_pad = t // CMP_STRIDE
    j = np.arange(n_cmp_pad)
    rel = np.arange(2 * n_cmp_pad) - n_cmp_pad
    dist_c = np.arange(tq)[None, :] - (rel[:, None] * CMP_STRIDE + CMP_BLOCK - 1)
    i_, j_ = np.arange(tq)[:, None], np.arange(TK)[None, :]
    dist_d = np.concatenate([d * TK + i_ - j_ for d in range(3)], axis=0)
    n_slc = LANES // NSA_KV_GROUPS
    cmp_start = j * CMP_STRIDE
    cmp_end = cmp_start + CMP_BLOCK - 1
    slc_start = np.arange(n_slc) * SLC_BLOCK
    ov1 = np.clip(np.minimum(cmp_end[:, None], slc_start[None, :] + SLC_BLOCK - 1)
                  - np.maximum(cmp_start[:, None], slc_start[None, :]) + 1, 0, None) / CMP_BLOCK
    ov1[n_cmp_pad - 1:] = 0.0
    ov1[:, t // SLC_BLOCK:] = 0.0
    ov3 = np.concatenate([ov1.T, ov1.T, ov1.T], axis=1)
    return jnp.asarray(dist_c, jnp.int32), jnp.asarray(dist_d, jnp.int32), jnp.asarray(ov3, _BF16)


def kernel(x, norm_g, w_in, w_out, forget_b, cmp_w1, cmp_b1, cmp_w2, cmp_pe, rel_bias, final_g):
    b, t, d = x.shape
    depth = norm_g.shape[0]
    assert d == D_MODEL and t % TM_PROJ == 0 and t % (CMP_STRIDE * LANES) == 0
    assert t // SLC_BLOCK <= LANES // NSA_KV_GROUPS and TQ_NSA == TK
    n_cmp_pad = t // CMP_STRIDE

    dist_c, dist_d, ov3 = _static_tables(t)
    bias_c = _bias_table(rel_bias * LOG2E, dist_c, 32)
    bias_d = _bias_table(rel_bias * LOG2E, dist_d, 32).reshape(N_HEADS_NSA, 3, TQ_NSA, TK)
    bias_d = bias_d[:, :2] - bias_d[:, 2:3]
    bias_d = bias_d.reshape(NSA_KV_GROUPS, NSA_HPG, 2, TQ_NSA, TK).transpose(0, 2, 4, 1, 3).reshape(
        NSA_KV_GROUPS, 2, TK, NSA_HPG * TQ_NSA)

    perm_rows = _head_perm_cols(HEAD_DIM, NSA_HEAD_ORDER)
    x2 = x.reshape(b * t, d)
    for l in range(depth):
        wb, wf, wc, wvt = _layout_w_in(w_in[l])
        pb, pf, pc, vt = _proj(x2, norm_g[l].reshape(1, d), wb, wf, wc, wvt, t)
        pb3, pf3 = pb, pf

        fb_row = jnp.pad(forget_b[l], (0, LANES - N_HEADS_FOX)).reshape(1, LANES)
        augq, augk = _fgate(pf3, fb_row)

        halves = pc.reshape(b, 2, NSA_KV_GROUPS, n_cmp_pad, CMP_STRIDE * HEAD_DIM)
        w2 = cmp_w2[l]
        zeros = jnp.zeros_like(w2)
        w2p = jnp.stack([jnp.concatenate([w2, zeros], axis=-1),
                         jnp.concatenate([zeros, w2], axis=-1)], axis=1).astype(_BF16)
        kvc, kvc_t = _compress(halves, cmp_w1[l].astype(_BF16),
                               cmp_pe[l].reshape(2, 1, CMP_BLOCK * HEAD_DIM).astype(_BF16),
                               cmp_b1[l].reshape(2, 1, CMP_HIDDEN), w2p)

        o_a = _sb_attention(pb3, pf3)
        o_b = _nsa_attention(pb3, pf3, vt, kvc, kvc_t, bias_c, bias_d, ov3)
        o_c = _fox_attention(pb3, pf3, vt, augq, augk)

        wo = w_out[l]
        wo = jnp.concatenate([wo[:N_HEADS_SB * HEAD_DIM],
                              wo[N_HEADS_SB * HEAD_DIM:][:N_HEADS_NSA * HEAD_DIM][perm_rows],
                              wo[(N_HEADS_SB + N_HEADS_NSA) * HEAD_DIM:]], axis=0).astype(_BF16)
        last = l == depth - 1
        x2 = _out_proj(x2, o_a, o_b, o_c, wo, final_g.reshape(1, d), last)
    return x2.reshape(b, t, d)
```

```python
import functools
import math

import jax
import jax.numpy as jnp
import numpy as np
from jax import lax
from jax.experimental import pallas as pl
from jax.experimental.pallas import tpu as pltpu

D_MODEL = 1024
HEAD_DIM = 64
N_HEADS_SB = 4
N_HEADS_FOX = 4
N_HEADS_NSA = 8
NSA_KV_GROUPS = 2
NSA_HPG = N_HEADS_NSA // NSA_KV_GROUPS
CMP_BLOCK = 32
CMP_STRIDE = 16
CMP_HIDDEN = 256
SLC_BLOCK = 64
SLC_TOP = 16
WINDOW = 512
REL_BUCKETS = 32
REL_MAX_DIST = 128
FORCE_SCORE = 1e4
RMS_EPS = 1e-6
NEG = -1e30
LOG2E = math.log2(math.e)

LANES = 128
SUBLANES = 8
VMEM_LIMIT = 56 * 1024 * 1024

TM_PROJ = 512
TQ_SB = 512
TQ_FOX = 512
FOX_PARTS = 2
NSA_PARTS = 2
TQ_NSA = 128
TK = 128

PB_QB, PB_QA, PB_KA, PB_VA = 0, 4, 6, 8
PB_KS, PB_KW = 10, 11
PB_QC, PB_KF = 12, 14
PB_BLOCKS = 16
N_CMP_SLABS = 2 * NSA_KV_GROUPS
VT_VS, VT_VW, VT_VF = 0, 1, 2
N_VT = 4
PF_ZB, PF_ZA, PF_ZC, PF_GB, PF_FC = 0, 4, 6, 8, 9
PF_BLOCKS = 10

NSA_HEAD_ORDER = [0, 4, 1, 5, 2, 6, 3, 7]

_F32 = jnp.float32
_BF16 = jnp.bfloat16


def _cparams(sem):
    return pltpu.CompilerParams(dimension_semantics=sem, vmem_limit_bytes=VMEM_LIMIT)


def _dot(a, b):
    return jnp.dot(a, b, preferred_element_type=_F32)


def _dot_nt(a, b):
    return lax.dot_general(a, b, (((1,), (1,)), ((), ())), preferred_element_type=_F32)


def _split3(x):
    h1 = x.astype(_BF16)
    r1 = x - h1.astype(_F32)
    h2 = r1.astype(_BF16)
    h3 = (r1 - h2.astype(_F32)).astype(_BF16)
    return jnp.concatenate([h1, h2, h3], axis=1)


def _sigmoid(x):
    return 1.0 / (1.0 + jnp.exp(-x))


def _silu(x):
    return x * _sigmoid(x)


def _rel_bucket_np(n):
    n = np.maximum(n, 0)
    max_exact = REL_BUCKETS // 2
    nf = np.maximum(n, 1).astype(np.float64)
    large = max_exact + (np.log(nf / max_exact) / math.log(REL_MAX_DIST / max_exact)
                         * (REL_BUCKETS - max_exact)).astype(np.int64)
    large = np.minimum(large, REL_BUCKETS - 1)
    return np.where(n < max_exact, n, large)


def _bucket_thresholds():
    n = np.arange(0, 4 * REL_MAX_DIST)
    bk = _rel_bucket_np(n)
    assert np.all(np.diff(bk) >= 0) and bk[-1] == REL_BUCKETS - 1
    return [int(np.argmax(bk >= b)) for b in range(REL_BUCKETS)]


_BUCKET_THR = _bucket_thresholds()


def _bias_kernel(tab_ref, dist_ref, o_ref):
    n = dist_ref[...]
    acc = [jnp.full(n.shape, tab_ref[0, h], _F32) for h in range(N_HEADS_NSA)]
    for b in range(1, REL_BUCKETS):
        ge = n >= _BUCKET_THR[b]
        for h in range(N_HEADS_NSA):
            acc[h] = jnp.where(ge, tab_ref[b, h], acc[h])
    valid = n >= 0
    for h in range(N_HEADS_NSA):
        o_ref[h] = jnp.where(valid, acc[h], NEG)


def _bias_table(rel_bias, dist, rows):
    n_rows, n_cols = dist.shape
    return pl.pallas_call(
        _bias_kernel,
        grid=(n_rows // rows,),
        in_specs=[pl.BlockSpec(memory_space=pltpu.SMEM),
                  pl.BlockSpec((rows, n_cols), lambda i: (i, 0))],
        out_specs=pl.BlockSpec((N_HEADS_NSA, rows, n_cols), lambda i: (0, i, 0)),
        out_shape=jax.ShapeDtypeStruct((N_HEADS_NSA, n_rows, n_cols), _F32),
        compiler_params=_cparams(("arbitrary",)),
        name="rel_bias_table",
    )(rel_bias, dist)


def _proj_kernel(x_ref, g_ref, wb_ref, wf_ref, wc_ref, wvt_ref, pb_ref, pf_ref, pc_ref, vt_ref):
    x = x_ref[...]
    y = x * lax.rsqrt(jnp.mean(x * x, axis=-1, keepdims=True) + RMS_EPS)
    h = (y * g_ref[...]).astype(_BF16)
    v_all = _dot(h, wvt_ref[...])
    for j in range(N_VT):
        v_t = v_all[:, j * LANES:(j + 1) * LANES].T.astype(_BF16)
        for c in range(TM_PROJ // TK):
            vt_ref[0, j, c] = v_t[:, c * TK:(c + 1) * TK]
    kv_cmp = _dot(h, wc_ref[...]).astype(_BF16)
    for s in range(N_CMP_SLABS):
        pc_ref[0, s] = kv_cmp[:, s * HEAD_DIM:(s + 1) * HEAD_DIM]
    chunk = 4 * LANES
    for c in range(0, PB_BLOCKS * LANES, chunk):
        w = min(chunk, PB_BLOCKS * LANES - c)
        res = _dot(h, wb_ref[:, c:c + w]).astype(_BF16)
        for j in range(w // LANES):
            pb_ref[0, c // LANES + j] = res[:, j * LANES:(j + 1) * LANES]
    for c in range(0, PF_BLOCKS * LANES, chunk):
        w = min(chunk, PF_BLOCKS * LANES - c)
        res = _dot(h, wf_ref[:, c:c + w])
        for j in range(w // LANES):
            pf_ref[0, c // LANES + j] = res[:, j * LANES:(j + 1) * LANES]


def _proj(x2, g, wb, wf, wc, wvt, t):
    n = x2.shape[0]
    per_seq = t // TM_PROJ
    return pl.pallas_call(
        _proj_kernel,
        grid=(n // TM_PROJ,),
        in_specs=[pl.BlockSpec((TM_PROJ, D_MODEL), lambda i: (i, 0)),
                  pl.BlockSpec((1, D_MODEL), lambda i: (0, 0)),
                  pl.BlockSpec((D_MODEL, PB_BLOCKS * LANES), lambda i: (0, 0)),
                  pl.BlockSpec((D_MODEL, PF_BLOCKS * LANES), lambda i: (0, 0)),
                  pl.BlockSpec((D_MODEL, N_CMP_SLABS * HEAD_DIM), lambda i: (0, 0)),
                  pl.BlockSpec((D_MODEL, N_VT * LANES), lambda i: (0, 0))],
        out_specs=[pl.BlockSpec((1, PB_BLOCKS, TM_PROJ, LANES), lambda i: (i // per_seq, 0, i % per_seq, 0)),
                   pl.BlockSpec((1, PF_BLOCKS, TM_PROJ, LANES), lambda i: (i // per_seq, 0, i % per_seq, 0)),
                   pl.BlockSpec((1, N_CMP_SLABS, TM_PROJ, HEAD_DIM),
                                lambda i: (i // per_seq, 0, i % per_seq, 0)),
                   pl.BlockSpec((1, N_VT, TM_PROJ // TK, LANES, TK),
                                lambda i: (i // per_seq, 0, i % per_seq, 0, 0))],
        out_shape=[jax.ShapeDtypeStruct((n // t, PB_BLOCKS, t, LANES), _BF16),
                   jax.ShapeDtypeStruct((n // t, PF_BLOCKS, t, LANES), _F32),
                   jax.ShapeDtypeStruct((n // t, N_CMP_SLABS, t, HEAD_DIM), _BF16),
                   jax.ShapeDtypeStruct((n // t, N_VT, t // TK, LANES, TK), _BF16)],
        compiler_params=_cparams(("arbitrary",)),
        name="rmsnorm_in_proj",
    )(x2, g, wb, wf, wc, wvt)


N_SPLIT = 3


def _fgate_tables():
    n_pairs = N_HEADS_FOX // 2
    pq = np.zeros((N_SPLIT * LANES, n_pairs * LANES), np.float32)
    pk = np.zeros_like(pq)
    ones_q = np.zeros((1, n_pairs * LANES), np.float32)
    ones_k = np.zeros_like(ones_q)
    for head in range(N_HEADS_FOX):
        pair, slot = divmod(head, 2)
        base = pair * LANES + (HEAD_DIM if slot == 0 else 0)
        for j in range(N_SPLIT):
            pq[j * LANES + head, base + j] = 1.0
            pk[j * LANES + head, base + N_SPLIT + j] = 1.0
        ones_q[0, base + N_SPLIT:base + 2 * N_SPLIT] = 1.0
        ones_k[0, base:base + N_SPLIT] = 1.0
    return (jnp.asarray(pq, _BF16), jnp.asarray(pk, _BF16),
            jnp.asarray(ones_q), jnp.asarray(ones_k))


def _fgate_kernel(fc_ref, fb_ref, pq_ref, pk_ref, oq_ref, ok_ref, augq_ref, augk_ref):
    t = fc_ref.shape[2]
    z = fc_ref[0, 0] + fb_ref[...]
    logf = jnp.minimum(z, 0.0) - jnp.log1p(jnp.exp(-jnp.abs(z)))
    row = lax.broadcasted_iota(jnp.int32, (t, LANES), 0)
    c = logf
    shift = 1
    while shift < t:
        c = c + jnp.where(row >= shift, pltpu.roll(c, shift, axis=0), 0.0)
        shift *= 2
    c3 = _split3(c * LOG2E)
    aq = _dot(c3, pq_ref[...]) + oq_ref[...]
    ak = ok_ref[...] - _dot(c3, pk_ref[...])
    for p in range(N_HEADS_FOX // 2):
        augq_ref[0, p] = aq[:, p * LANES:(p + 1) * LANES].astype(_BF16)
        augk_ref[0, p] = ak[:, p * LANES:(p + 1) * LANES].astype(_BF16)


def _fgate(pf3, fb_row):
    b, _, t, _ = pf3.shape
    n_pairs = N_HEADS_FOX // 2
    tables = _fgate_tables()
    full = lambda a: pl.BlockSpec(a.shape, lambda i: (0,) * a.ndim)
    aug = lambda: pl.BlockSpec((1, n_pairs, t, LANES), lambda i: (i, 0, 0, 0))
    return pl.pallas_call(
        _fgate_kernel,
        grid=(b,),
        in_specs=[pl.BlockSpec((1, 1, t, LANES), lambda i: (i, PF_FC, 0, 0)),
                  pl.BlockSpec((1, LANES), lambda i: (0, 0))] + [full(a) for a in tables],
        out_specs=[aug(), aug()],
        out_shape=[jax.ShapeDtypeStruct((b, n_pairs, t, LANES), _BF16),
                   jax.ShapeDtypeStruct((b, n_pairs, t, LANES), _BF16)],
        compiler_params=_cparams(("arbitrary",)),
        name="forget_gate_cumsum",
    )(pf3, fb_row, *tables)


def _compress_kernel(x_ref, w1_ref, pe_ref, b1_ref, w2_ref, o_ref, ot_ref):
    nc = x_ref.shape[3]
    half = CMP_STRIDE * HEAD_DIM
    w1 = w1_ref[0]
    c1 = _dot(jnp.broadcast_to(pe_ref[0], (SUBLANES, 2 * half)), w1)[0:1] + b1_ref[0]
    out = jnp.zeros((nc, LANES), _F32)
    for g in range(NSA_KV_GROUPS):
        xg = x_ref[0, 0, g]
        a = _dot(xg, w1[:half])
        bb = _dot(xg, w1[half:])
        h = a + pltpu.roll(bb, nc - 1, axis=0) + c1
        out = out + _dot(_silu(h).astype(_BF16), w2_ref[0, g])
    o_ref[0, 0] = out.astype(_BF16)
    ot_ref[0, 0] = out.T.astype(_BF16)


def _compress(halves, w1, pe, b1, w2p):
    b, _, g, nc, width = halves.shape
    return pl.pallas_call(
        _compress_kernel,
        grid=(b, 2),
        in_specs=[pl.BlockSpec((1, 1, g, nc, width), lambda i, k: (i, k, 0, 0, 0)),
                  pl.BlockSpec((1, 2 * width, CMP_HIDDEN), lambda i, k: (k, 0, 0)),
                  pl.BlockSpec((1, 1, 2 * width), lambda i, k: (k, 0, 0)),
                  pl.BlockSpec((1, 1, CMP_HIDDEN), lambda i, k: (k, 0, 0)),
                  pl.BlockSpec((1, g, CMP_HIDDEN, LANES), lambda i, k: (k, 0, 0, 0))],
        out_specs=[pl.BlockSpec((1, 1, nc, LANES), lambda i, k: (i, k, 0, 0)),
                   pl.BlockSpec((1, 1, LANES, nc), lambda i, k: (i, k, 0, 0))],
        out_shape=[jax.ShapeDtypeStruct((b, 2, nc, LANES), _BF16),
                   jax.ShapeDtypeStruct((b, 2, LANES, nc), _BF16)],
        compiler_params=_cparams(("arbitrary", "arbitrary")),
        name="nsa_compress",
    )(halves, w1, pe, b1, w2p)


def _sb_kernel(q_ref, k_ref, v_ref, z_ref, o_ref, qh_ref, c_ref, acc_ref):
    qi = pl.program_id(1)
    tq = q_ref.shape[2]
    n_pairs = q_ref.shape[1]
    lane = lax.broadcasted_iota(jnp.int32, (tq, LANES), 1)
    r_i = lax.broadcasted_iota(jnp.int32, (2 * TK, 2 * TK), 0)
    c_i = lax.broadcasted_iota(jnp.int32, (2 * TK, 2 * TK), 1)
    uu = jnp.where(r_i >= c_i, 1.0, 0.0).astype(_BF16)
    c_ref[...] = jnp.zeros_like(c_ref)
    acc_ref[...] = jnp.zeros_like(acc_ref)
    for p in range(n_pairs):
        q2 = q_ref[0, p]
        qh_ref[2 * p] = jnp.where(lane < HEAD_DIM, q2, jnp.zeros_like(q2))
        qh_ref[2 * p + 1] = jnp.where(lane >= HEAD_DIM, q2, jnp.zeros_like(q2))
    row = lax.broadcasted_iota(jnp.int32, (tq, LANES), 0)
    n_blocks = tq // TK

    def chunk(start, diagonal):
        for p in range(n_pairs):
            head_pair(p, k_ref[0, p, pl.ds(start, tq), :], v_ref[0, p, pl.ds(start, tq), :], diagonal)

    def head_pair(p, k2, v2, diagonal):
        for h in range(2 * p, 2 * p + 2):
            s = _dot_nt(qh_ref[h], k2)
            carry = c_ref[h]
            w_blocks = [None] * n_blocks
            for c in reversed(range(0, n_blocks, 2)):
                nz, l1m, mask = [], [], []
                for cc in (c, c + 1):
                    nzc = s[:, cc * TK:(cc + 1) * TK]
                    neg_abs = lax.bitcast_convert_type(
                        lax.bitcast_convert_type(nzc, jnp.uint32) | jnp.uint32(0x80000000), _F32)
                    lc = jnp.minimum(nzc, 0.0) - jnp.log(1.0 + jnp.exp(neg_abs))
                    if diagonal:
                        mask.append(lane + cc * TK < row)
                        lc = jnp.where(mask[-1], lc, 0.0)
                    nz.append(nzc)
                    l1m.append(lc.astype(_BF16))
                rc = _dot(jnp.concatenate(l1m, axis=1), uu)
                for i, cc in enumerate((c, c + 1)):
                    w = jnp.exp((rc[:, i * TK:(i + 1) * TK] + carry) - nz[i])
                    if diagonal:
                        w = jnp.where(mask[i], w, 0.0)
                    w_blocks[cc] = w.astype(_BF16)
                carry = carry + jnp.broadcast_to(rc[:, 0:1], carry.shape)
            acc_ref[h] += _dot(jnp.concatenate(w_blocks, axis=1), v2)
            c_ref[h] = carry

    chunk(pl.multiple_of(qi * tq, tq), True)

    def far_pair(it, carry):
        chunk(pl.multiple_of((qi - 1 - 2 * it) * tq, tq), False)
        chunk(pl.multiple_of((qi - 2 - 2 * it) * tq, tq), False)
        return carry

    lax.fori_loop(0, qi >> 1, far_pair, 0)

    @pl.when((qi & 1) == 1)
    def _():
        chunk(0, False)
    for p in range(n_pairs):
        o = jnp.where(lane < HEAD_DIM, acc_ref[2 * p], acc_ref[2 * p + 1])
        o_ref[0, p] = (o * _silu(z_ref[0, p])).astype(o_ref.dtype)


def _sb_attention(pb3, pf3):
    b, _, t, _ = pb3.shape
    n_pairs = N_HEADS_SB // 2
    assert PB_QA % n_pairs == 0 and PB_KA % n_pairs == 0 and PB_VA % n_pairs == 0 and PF_ZA % n_pairs == 0
    return pl.pallas_call(
        _sb_kernel,
        grid=(b, t // TQ_SB),
        in_specs=[pl.BlockSpec((1, n_pairs, TQ_SB, LANES), lambda i, q: (i, PB_QA // n_pairs, q, 0)),
                  pl.BlockSpec((1, n_pairs, t, LANES), lambda i, q: (i, PB_KA // n_pairs, 0, 0)),
                  pl.BlockSpec((1, n_pairs, t, LANES), lambda i, q: (i, PB_VA // n_pairs, 0, 0)),
                  pl.BlockSpec((1, n_pairs, TQ_SB, LANES), lambda i, q: (i, PF_ZA // n_pairs, q, 0))],
        out_specs=pl.BlockSpec((1, n_pairs, TQ_SB, LANES), lambda i, q: (i, 0, q, 0)),
        out_shape=jax.ShapeDtypeStruct((b, n_pairs, t, LANES), _BF16),
        scratch_shapes=[pltpu.VMEM((N_HEADS_SB, TQ_SB, LANES), _BF16),
                        pltpu.VMEM((N_HEADS_SB, TQ_SB, LANES), _F32),
                        pltpu.VMEM((N_HEADS_SB, TQ_SB, LANES), _F32)],
        compiler_params=_cparams(("arbitrary", "arbitrary")),
        name="stick_breaking_attention",
    )(pb3, pb3, pb3, pf3)


def _fox_kernel(q_ref, k_ref, vt_ref, augq_ref, augk_ref, z_ref, o_ref,
                qh_ref, m_ref, acc_ref, s_ref):
    qi = pl.program_id(2)
    tq = q_ref.shape[2]
    tk = tq
    lane = lax.broadcasted_iota(jnp.int32, (tq, LANES), 1)
    keep = [lane < HEAD_DIM, lane >= HEAD_DIM]
    q2 = q_ref[0, 0]
    aq = augq_ref[0, 0]
    for h in range(2):
        qh_ref[h] = jnp.where(keep[h], q2, aq)
    m_ref[...] = jnp.full_like(m_ref, NEG)
    acc_ref[...] = jnp.zeros_like(acc_ref)
    key_i = lax.broadcasted_iota(jnp.int32, (TK, tq), 0)
    qry_i = lax.broadcasted_iota(jnp.int32, (TK, tq), 1)

    def scores(c, slot):
        start = c * tk if isinstance(c, int) else pl.multiple_of(c * tk, tk)
        k2 = k_ref[0, 0, pl.ds(start, tk), :]
        ak = augk_ref[0, 0, pl.ds(start, tk), :]
        for h in range(2):
            s_ref[slot, h] = _dot_nt(jnp.where(keep[h], k2, ak), qh_ref[h])

    ones_rows = jnp.ones((SUBLANES, tk), _BF16)

    def consume(c, slot, diagonal):
        per = tk // TK
        vt = jnp.concatenate([vt_ref[0, 0, c * per + j] for j in range(per)], axis=1)
        for h in range(2):
            vth = jnp.concatenate([vt[h * HEAD_DIM:(h + 1) * HEAD_DIM], ones_rows], axis=0)
            for part in range(FOX_PARTS):
                js = range(part * per // FOX_PARTS, (part + 1) * per // FOX_PARTS)
                blocks = [s_ref[slot, h, j * TK:(j + 1) * TK, :] for j in js]
                if diagonal:
                    blocks = [jnp.where(key_i + j * TK <= qry_i, blk, NEG) for j, blk in zip(js, blocks)]
                mx = blocks[0]
                for blk in blocks[1:]:
                    mx = jnp.maximum(mx, blk)
                m_old = m_ref[h]
                m_new = jnp.maximum(m_old, jnp.max(mx, axis=0, keepdims=True))
                m_row = m_new[0:1]
                pt = jnp.concatenate([jnp.exp2(blk - m_row).astype(_BF16) for blk in blocks], axis=0)
                acc_ref[h] = (jnp.exp2(m_old - m_new)[0:1] * acc_ref[h]
                              + _dot(vth[:, js[0] * TK:(js[-1] + 1) * TK], pt))
                m_ref[h] = m_new

    scores(0, 0)
    n_pairs = qi >> 1

    def far_pair(i, carry):
        c = 2 * i
        scores(c + 1, 1)
        consume(c, 0, False)
        scores(c + 2, 0)
        consume(c + 1, 1, False)
        return carry

    lax.fori_loop(0, n_pairs, far_pair, 0)

    @pl.when(qi == 2 * n_pairs)
    def _():
        consume(qi, 0, True)

    @pl.when(qi != 2 * n_pairs)
    def _():
        scores(qi, 1)
        consume(qi - 1, 0, False)
        consume(qi, 1, True)
    o_t = jnp.concatenate([acc_ref[h, :HEAD_DIM] / acc_ref[h, HEAD_DIM:HEAD_DIM + 1]
                           for h in range(2)], axis=0)
    o_ref[0, 0] = (o_t.T * _silu(z_ref[0, 0])).astype(o_ref.dtype)


def _fox_attention(pb3, pf3, vt, augq, augk):
    b, _, t, _ = pb3.shape
    n_pairs = N_HEADS_FOX // 2
    return pl.pallas_call(
        _fox_kernel,
        grid=(b, n_pairs, t // TQ_FOX),
        in_specs=[pl.BlockSpec((1, 1, TQ_FOX, LANES), lambda i, p, q: (i, PB_QC + p, q, 0)),
                  pl.BlockSpec((1, 1, t, LANES), lambda i, p, q: (i, PB_KF + p, 0, 0)),
                  pl.BlockSpec((1, 1, t // TK, LANES, TK), lambda i, p, q: (i, VT_VF + p, 0, 0, 0)),
                  pl.BlockSpec((1, 1, TQ_FOX, LANES), lambda i, p, q: (i, p, q, 0)),
                  pl.BlockSpec((1, 1, t, LANES), lambda i, p, q: (i, p, 0, 0)),
                  pl.BlockSpec((1, 1, TQ_FOX, LANES), lambda i, p, q: (i, PF_ZC + p, q, 0))],
        out_specs=pl.BlockSpec((1, 1, TQ_FOX, LANES), lambda i, p, q: (i, p, q, 0)),
        out_shape=jax.ShapeDtypeStruct((b, n_pairs, t, LANES), _BF16),
        scratch_shapes=[pltpu.VMEM((2, TQ_FOX, LANES), _BF16),
                        pltpu.VMEM((2, SUBLANES, TQ_FOX), _F32),
                        pltpu.VMEM((2, HEAD_DIM + SUBLANES, TQ_FOX), _F32),
                        pltpu.VMEM((2, 2, TQ_FOX, TQ_FOX), _F32)],
        compiler_params=_cparams(("arbitrary", "arbitrary", "arbitrary")),
        name="forgetting_attention",
    )(pb3, pb3, vt, augq, augk, pf3)


def _nsa_kernel(q_ref, kc_ref, vct_ref, ks_ref, vs_ref, kw_ref, vw_ref, gl_ref, z_ref,
                bc_ref, bd_ref, ov_ref, o_ref,
                m_ref, acc_ref, osum_ref, qz_ref, qsel_ref, gates_ref, psum_ref, s_ref, ws_ref):
    qi = pl.program_id(1)
    tq = q_ref.shape[2]
    hpg = NSA_HPG
    rows = hpg * tq
    n_cmp = kc_ref.shape[2]
    lane_r = lax.broadcasted_iota(jnp.int32, (rows, LANES), 1)
    half = [lane_r < HEAD_DIM, lane_r >= HEAD_DIM]

    gates_ref[...] = _sigmoid(gl_ref[0, 0]).T

    def gated(c, h, per_group):
        parts = []
        for g, o in enumerate(per_group):
            r = 3 * (g * hpg + h) + c
            parts.append(gates_ref[r:r + 1, :] * o[:, h * tq:(h + 1) * tq])
        return jnp.concatenate(parts, axis=0)

    q4 = jnp.concatenate([q_ref[0, h] for h in range(hpg)], axis=0)
    for g in range(NSA_KV_GROUPS):
        qz_ref[g] = jnp.where(half[g], q4, jnp.zeros_like(q4))

    def scores(q_src, k_ref, start, tk, onehot):
        k2 = k_ref[0, 0, pl.ds(start, tk), :]
        lane_k = lax.broadcasted_iota(jnp.int32, (tk, LANES), 1)
        key_blk = (start + lax.broadcasted_iota(jnp.int32, (tk, LANES), 0)) >> int(math.log2(SLC_BLOCK))
        oh = jnp.where((lane_k & (SLC_BLOCK - 1)) == key_blk, 1.0, 0.0).astype(_BF16)
        out = []
        for g in range(NSA_KV_GROUPS):
            keep = (lane_k < HEAD_DIM) if g == 0 else (lane_k >= HEAD_DIM)
            out.append(_dot_nt(jnp.where(keep, k2, oh) if onehot else k2, q_src[g]))
        return out


    kc = kc_ref[0, 0]
    vct = vct_ref[0, 0]
    cmp_row0 = pl.multiple_of(n_cmp - qi * (tq // CMP_STRIDE), SUBLANES)
    o_cmp_t = []
    raw = [_dot_nt(kc, qz_ref[g]) for g in range(NSA_KV_GROUPS)]
    for g in range(NSA_KV_GROUPS):
        bias = jnp.concatenate([bc_ref[g * hpg + h, pl.ds(cmp_row0, n_cmp), :]
                                for h in range(hpg)], axis=1)
        sc = raw[g] + bias
        mx = jnp.max(sc, axis=0, keepdims=True)
        e = jnp.exp2(sc - mx)
        den = jnp.sum(e, axis=0, keepdims=True)
        pc = e * jnp.where(mx > 0.5 * NEG, 1.0 / den, 0.0)
        psum = pc[:, 0:tq]
        for h in range(1, hpg):
            psum = psum + pc[:, h * tq:(h + 1) * tq]
        psum_ref[g] = psum
        o_cmp_t.append(_dot(vct[g * HEAD_DIM:(g + 1) * HEAD_DIM], pc.astype(_BF16)))
    for h in range(hpg):
        osum_ref[h] = gated(0, h, o_cmp_t)

    rank_from = SLC_TOP * SLC_BLOCK // tq

    @pl.when(qi < rank_from)
    def _():
        for g in range(NSA_KV_GROUPS):
            qsel_ref[g] = qz_ref[g]

    @pl.when(qi >= rank_from)
    def _():
        n_sel = LANES // NSA_KV_GROUPS
        n_grp = n_sel // SUBLANES
        blk = lax.broadcasted_iota(jnp.int32, (n_sel, tq), 0)
        cur = (qi * tq + lax.broadcasted_iota(jnp.int32, (n_sel, tq), 1)) >> int(math.log2(SLC_BLOCK))
        forced = (blk == 0) | (blk == cur) | (blk == cur - 1)
        sub = lax.broadcasted_iota(jnp.int32, (SUBLANES, tq), 0)
        neg_t = []
        for g in range(NSA_KV_GROUPS):
            p = psum_ref[g]
            p1 = p.astype(_BF16)
            r1 = p - p1.astype(_F32)
            p2 = r1.astype(_BF16)
            p3 = (r1 - p2.astype(_F32)).astype(_BF16)
            a = _dot(ov_ref[...], jnp.concatenate([p1, p2, p3], axis=0))
            a = jnp.where(forced, FORCE_SCORE, a)
            a = jnp.where(blk > cur, -FORCE_SCORE, a)
            a_grp = [a[r * SUBLANES:(r + 1) * SUBLANES] for r in range(n_grp)]
            cnt = [jnp.zeros((SUBLANES, tq), _F32) for _ in range(n_grp)]
            for j in range(n_sel):
                rj = jnp.broadcast_to(a[j:j + 1], (SUBLANES, tq))
                jr, jo = divmod(j, SUBLANES)
                for r in range(n_grp):
                    if r > jr:
                        one = jnp.where(rj >= a_grp[r], 1.0, 0.0)
                    elif r < jr:
                        one = jnp.where(rj > a_grp[r], 1.0, 0.0)
                    else:
                        tie = jnp.where(sub > jo, jnp.where(rj == a_grp[r], 1.0, 0.0), 0.0)
                        one = jnp.where(rj > a_grp[r], 1.0, tie)
                    cnt[r] = cnt[r] + one
            neg_t.append(jnp.where(jnp.concatenate(cnt, axis=0) < float(SLC_TOP), 0.0, NEG))
        selneg = jnp.concatenate(neg_t[::-1], axis=0).T.astype(_BF16)
        selneg4 = jnp.concatenate([selneg] * hpg, axis=0)
        for g in range(NSA_KV_GROUPS):
            qsel_ref[g] = jnp.where(half[g], q4, selneg4)

    SEL, WIN = 0, 1

    def reset(st):
        m_ref[st] = jnp.full(m_ref.shape[1:], 2.0 * NEG, _F32)
        acc_ref[st] = jnp.zeros(acc_ref.shape[1:], _F32)

    def chunk(q_src, k_ref, vt_ref, start, tk, onehot, extras, st):
        s = scores(q_src, k_ref, start, tk, onehot)
        consume(lambda g, c: s[g][c * TK:(c + 1) * TK], vt_ref, start, tk, extras, st)

    def consume(block_of, vt_ref, start, tk, extras, st, limit=None):
        first = start // TK if isinstance(start, int) else start >> int(math.log2(TK))
        vt = jnp.concatenate([vt_ref[0, 0, first + j] for j in range(tk // TK)], axis=1)
        ones_rows = jnp.ones((SUBLANES, tk), _BF16)
        key_pos = start + lax.broadcasted_iota(jnp.int32, (HEAD_DIM + SUBLANES, tk), 1)
        for g in range(NSA_KV_GROUPS):
            vth = jnp.concatenate([vt[g * HEAD_DIM:(g + 1) * HEAD_DIM], ones_rows], axis=0)
            if limit is not None:
                vth = jnp.where(key_pos < limit, vth, jnp.zeros_like(vth))
            n_blk = tk // TK
            n_parts = NSA_PARTS if n_blk >= 2 * NSA_PARTS else 1
            for part in range(n_parts):
                cs = range(part * n_blk // n_parts, (part + 1) * n_blk // n_parts)
                blocks = [block_of(g, c) + extras[c](g) if c in extras else block_of(g, c) for c in cs]
                mx = blocks[0]
                for blk_s in blocks[1:]:
                    mx = jnp.maximum(mx, blk_s)
                m_old = m_ref[st, g]
                m_new = jnp.maximum(m_old, jnp.max(mx, axis=0, keepdims=True))
                m_row = m_new[0:1]
                pt = jnp.concatenate([jnp.exp2(blk_s - m_row).astype(_BF16) for blk_s in blocks], axis=0)
                acc_ref[st, g] = (jnp.exp2(m_old - m_new)[0:1] * acc_ref[st, g]
                                  + _dot(vth[:, cs[0] * TK:(cs[-1] + 1) * TK], pt))
                m_ref[st, g] = m_new

    def finish(c, st):
        outs = []
        for g in range(NSA_KV_GROUPS):
            den = acc_ref[st, g, HEAD_DIM:HEAD_DIM + 1]
            outs.append(acc_ref[st, g, :HEAD_DIM] * jnp.where(den > 0.0, 1.0 / den, 0.0))
        for h in range(hpg):
            osum_ref[h] += gated(c, h, outs)

    def near_bias(d):
        return lambda g: bd_ref[g, d]

    def short_path(q_src, k_ref, v_ref, onehot, st):
        def far(kt, carry):
            chunk(q_src, k_ref, v_ref, pl.multiple_of(kt * TK, TK), TK, onehot, {}, st)
            return carry

        lax.fori_loop(0, jnp.maximum(qi - 1, 0), far, 0)

        @pl.when(qi >= 1)
        def _():
            chunk(q_src, k_ref, v_ref, pl.multiple_of((qi - 1) * TK, TK), TK, onehot,
                  {0: near_bias(1)}, st)

        chunk(q_src, k_ref, v_ref, pl.multiple_of(qi * TK, TK), TK, onehot, {0: near_bias(0)}, st)

    reset(SEL)
    reset(WIN)
    big = 4 * TK
    n_win = WINDOW // TK
    main_from = max(big // TK - 1, n_win)
    win_mask = jnp.where(lax.broadcasted_iota(jnp.int32, (TK, rows), 0)
                         > (lax.broadcasted_iota(jnp.int32, (TK, rows), 1) & (tq - 1)), 0.0, NEG)

    @pl.when(qi < main_from)
    def _():
        short_path(qsel_ref, ks_ref, vs_ref, True, SEL)
        short_path(qz_ref, kw_ref, vw_ref, False, WIN)

    @pl.when(qi >= main_from)
    def _():
        last = pl.multiple_of((qi - (big // TK - 1)) * TK, TK)
        n_far = (qi - (big // TK - 1) + big // TK - 1) >> int(math.log2(big // TK))
        near = {big // TK - 2: near_bias(1), big // TK - 1: near_bias(0)}

        def start_of(k):
            return pl.multiple_of(jnp.where(k < n_far, k * big, last), TK)

        def put(k, slot):
            for g, s in enumerate(scores(qsel_ref, ks_ref, start_of(k), big, True)):
                s_ref[slot, g] = s

        def take(k, slot, extras, limit):
            consume(lambda g, c: s_ref[slot, g, c * TK:(c + 1) * TK, :], vs_ref,
                    start_of(k), big, extras, SEL, limit)

        win_start = pl.multiple_of((qi - n_win) * TK, TK)
        for g, s in enumerate(scores(qz_ref, kw_ref, win_start, WINDOW + TK, False)):
            ws_ref[g] = s
        put(0, 0)
        consume(lambda g, c: ws_ref[g, c * TK:(c + 1) * TK, :], vw_ref, win_start, WINDOW + TK,
                {0: lambda g: win_mask, n_win - 1: near_bias(1), n_win: near_bias(0)}, WIN)
        n_pair = n_far >> 1

        def far_pair(i, carry):
            k = 2 * i
            put(k + 1, 1)
            take(k, 0, {}, last)
            put(k + 2, 0)
            take(k + 1, 1, {}, last)
            return carry

        lax.fori_loop(0, n_pair, far_pair, 0)

        @pl.when(n_far == 2 * n_pair)
        def _():
            take(n_far, 0, near, None)

        @pl.when(n_far != 2 * n_pair)
        def _():
            put(n_far, 1)
            take(n_far - 1, 0, {}, last)
            take(n_far, 1, near, None)

    finish(1, SEL)
    finish(2, WIN)

    for h in range(hpg):
        o_ref[0, h] = (osum_ref[h].T * _silu(z_ref[0, h])).astype(o_ref.dtype)


def _nsa_attention(pb3, pf3, vt, kvc, kvc_t, bias_c, bias_d, ov):
    b, _, t, _ = pb3.shape
    n_cmp = kvc.shape[2]
    n_blk = N_HEADS_NSA // 2
    full = lambda shape: pl.BlockSpec(shape, lambda i, q: (0,) * len(shape))
    kv_spec = lambda col: pl.BlockSpec((1, 1, t, LANES), lambda i, q: (i, col, 0, 0))
    vt_spec = lambda blk: pl.BlockSpec((1, 1, t // TK, LANES, TK), lambda i, q: (i, blk, 0, 0, 0))
    return pl.pallas_call(
        _nsa_kernel,
        grid=(b, t // TQ_NSA),
        in_specs=[pl.BlockSpec((1, n_blk, TQ_NSA, LANES), lambda i, q: (i, PB_QB // n_blk, q, 0)),
                  pl.BlockSpec((1, 1, n_cmp, LANES), lambda i, q: (i, 0, 0, 0)),
                  pl.BlockSpec((1, 1, LANES, n_cmp), lambda i, q: (i, 1, 0, 0)),
                  kv_spec(PB_KS), vt_spec(VT_VS), kv_spec(PB_KW), vt_spec(VT_VW),
                  pl.BlockSpec((1, 1, TQ_NSA, LANES), lambda i, q: (i, PF_GB, q, 0)),
                  pl.BlockSpec((1, n_blk, TQ_NSA, LANES), lambda i, q: (i, PF_ZB // n_blk, q, 0)),
                  full(bias_c.shape),
                  full(bias_d.shape), full(ov.shape)],
        out_specs=pl.BlockSpec((1, n_blk, TQ_NSA, LANES), lambda i, q: (i, 0, q, 0)),
        out_shape=jax.ShapeDtypeStruct((b, n_blk, t, LANES), _BF16),
        scratch_shapes=[pltpu.VMEM((2, NSA_KV_GROUPS, SUBLANES, NSA_HPG * TQ_NSA), _F32),
                        pltpu.VMEM((2, NSA_KV_GROUPS, HEAD_DIM + SUBLANES, NSA_HPG * TQ_NSA), _F32),
                        pltpu.VMEM((n_blk, TQ_NSA, LANES), _F32),
                        pltpu.VMEM((NSA_KV_GROUPS, NSA_HPG * TQ_NSA, LANES), _BF16),
                        pltpu.VMEM((NSA_KV_GROUPS, NSA_HPG * TQ_NSA, LANES), _BF16),
                        pltpu.VMEM((LANES, TQ_NSA), _F32),
                        pltpu.VMEM((NSA_KV_GROUPS, n_cmp, TQ_NSA), _F32),
                        pltpu.VMEM((2, NSA_KV_GROUPS, 4 * TK, NSA_HPG * TQ_NSA), _F32),
                        pltpu.VMEM((NSA_KV_GROUPS, WINDOW + TK, NSA_HPG * TQ_NSA), _F32)],
        compiler_params=_cparams(("arbitrary", "arbitrary")),
        name="native_sparse_attention",
    )(pb3, kvc, kvc_t, pb3, vt, pb3, vt, pf3, pf3, bias_c, bias_d, ov)


def _out_kernel(x_ref, oa_ref, ob_ref, oc_ref, w_ref, g_ref, o_ref, *, final_norm):
    mixed = jnp.concatenate([o_ref_in[0, j] for o_ref_in in (oa_ref, ob_ref, oc_ref)
                             for j in range(o_ref_in.shape[1])], axis=1)
    x = x_ref[...] + _dot(mixed, w_ref[...])
    if final_norm:
        x = x * lax.rsqrt(jnp.mean(x * x, axis=-1, keepdims=True) + RMS_EPS) * g_ref[...]
    o_ref[...] = x


def _out_proj(x2, oa, ob, oc, w, g, final_norm):
    n = x2.shape[0]
    per_seq = oa.shape[2] // TM_PROJ
    row = lambda width: pl.BlockSpec((TM_PROJ, width), lambda i: (i, 0))
    blocks = lambda a: pl.BlockSpec((1, a.shape[1], TM_PROJ, LANES),
                                    lambda i: (i // per_seq, 0, i % per_seq, 0))
    return pl.pallas_call(
        functools.partial(_out_kernel, final_norm=final_norm),
        grid=(n // TM_PROJ,),
        in_specs=[row(D_MODEL), blocks(oa), blocks(ob), blocks(oc),
                  pl.BlockSpec((D_MODEL, D_MODEL), lambda i: (0, 0)),
                  pl.BlockSpec((1, D_MODEL), lambda i: (0, 0))],
        out_specs=row(D_MODEL),
        out_shape=jax.ShapeDtypeStruct((n, D_MODEL), _F32),
        compiler_params=_cparams(("arbitrary",)),
        name="out_proj_residual",
    )(x2, oa, ob, oc, w, g)


def _head_perm_cols(width_per_head, order):
    return np.concatenate([np.arange(h * width_per_head, (h + 1) * width_per_head) for h in order])


def _layout_w_in(w):
    widths = [256, 256, 256, 256, 512, 128, 128, 128, 128, 128, 128, 24, 512, 256, 256, 256, 4, 256]
    offs = np.concatenate([[0], np.cumsum(widths)])
    (qa, ka, va, za, qb, kc, vc, ks, vs, kw, vw, gb, zb, qc, kf, vf, fc, zc) = [
        w[:, offs[i]:offs[i + 1]] for i in range(len(widths))]
    scale = HEAD_DIM ** -0.5
    perm = _head_perm_cols(HEAD_DIM, NSA_HEAD_ORDER)
    pad = lambda a: jnp.pad(a, ((0, 0), (0, LANES - a.shape[1])))
    wb = jnp.concatenate([qb[:, perm] * (scale * LOG2E), qa * (-scale), ka, va, ks, kw,
                          qc * (scale * LOG2E), kf], axis=1)
    wf = jnp.concatenate([zb[:, perm], za, zc, pad(gb), pad(fc)], axis=1)
    wc = jnp.concatenate([kc, vc], axis=1)
    wvt = jnp.concatenate([vs, vw, vf], axis=1)
    return wb.astype(_BF16), wf.astype(_BF16), wc.astype(_BF16), wvt.astype(_BF16)


def _static_tables(t):
    tq = TQ_NSA
    n_cmp_pad = t // CMP_STRIDE
    j = np.arange(n_cmp_pad)
    rel = np.arange(2 * n_cmp_pad) - n_cmp_pad
    dist_c = np.arange(tq)[None, :] - (rel[:, None] * CMP_STRIDE + CMP_BLOCK - 1)
    i_, j_ = np.arange(tq)[:, None], np.arange(TK)[None, :]
    dist_d = np.concatenate([d * TK + i_ - j_ for d in range(3)], axis=0)
    n_slc = LANES // NSA_KV_GROUPS
    cmp_start = j * CMP_STRIDE
    cmp_end = cmp_start + CMP_BLOCK - 1
    slc_start = np.arange(n_slc) * SLC_BLOCK
    ov1 = np.clip(np.minimum(cmp_end[:, None], slc_start[None, :] + SLC_BLOCK - 1)
                  - np.maximum(cmp_start[:, None], slc_start[None, :]) + 1, 0, None) / CMP_BLOCK
    ov1[n_cmp_pad - 1:] = 0.0
    ov1[:, t // SLC_BLOCK:] = 0.0
    ov3 = np.concatenate([ov1.T, ov1.T, ov1.T], axis=1)
    return jnp.asarray(dist_c, jnp.int32), jnp.asarray(dist_d, jnp.int32), jnp.asarray(ov3, _BF16)


def kernel(x, norm_g, w_in, w_out, forget_b, cmp_w1, cmp_b1, cmp_w2, cmp_pe, rel_bias, final_g):
    b, t, d = x.shape
    depth = norm_g.shape[0]
    assert d == D_MODEL and t % TM_PROJ == 0 and t % (CMP_STRIDE * LANES) == 0
    assert t // SLC_BLOCK <= LANES // NSA_KV_GROUPS and TQ_NSA == TK
    n_cmp_pad = t // CMP_STRIDE

    dist_c, dist_d, ov3 = _static_tables(t)
    bias_c = _bias_table(rel_bias * LOG2E, dist_c, 32)
    bias_d = _bias_table(rel_bias * LOG2E, dist_d, 32).reshape(N_HEADS_NSA, 3, TQ_NSA, TK)
    bias_d = bias_d[:, :2] - bias_d[:, 2:3]
    bias_d = bias_d.reshape(NSA_KV_GROUPS, NSA_HPG, 2, TQ_NSA, TK).transpose(0, 2, 4, 1, 3).reshape(
        NSA_KV_GROUPS, 2, TK, NSA_HPG * TQ_NSA)

    perm_rows = _head_perm_cols(HEAD_DIM, NSA_HEAD_ORDER)
    x2 = x.reshape(b * t, d)
    for l in range(depth):
        wb, wf, wc, wvt = _layout_w_in(w_in[l])
        pb, pf, pc, vt = _proj(x2, norm_g[l].reshape(1, d), wb, wf, wc, wvt, t)
        pb3, pf3 = pb, pf

        fb_row = jnp.pad(forget_b[l], (0, LANES - N_HEADS_FOX)).reshape(1, LANES)
        augq, augk = _fgate(pf3, fb_row)

        halves = pc.reshape(b, 2, NSA_KV_GROUPS, n_cmp_pad, CMP_STRIDE * HEAD_DIM)
        w2 = cmp_w2[l]
        zeros = jnp.zeros_like(w2)
        w2p = jnp.stack([jnp.concatenate([w2, zeros], axis=-1),
                         jnp.concatenate([zeros, w2], axis=-1)], axis=1).astype(_BF16)
        kvc, kvc_t = _compress(halves, cmp_w1[l].astype(_BF16),
                               cmp_pe[l].reshape(2, 1, CMP_BLOCK * HEAD_DIM).astype(_BF16),
                               cmp_b1[l].reshape(2, 1, CMP_HIDDEN), w2p)

        o_a = _sb_attention(pb3, pf3)
        o_b = _nsa_attention(pb3, pf3, vt, kvc, kvc_t, bias_c, bias_d, ov3)
        o_c = _fox_attention(pb3, pf3, vt, augq, augk)

        wo = w_out[l]
        wo = jnp.concatenate([wo[:N_HEADS_SB * HEAD_DIM],
                              wo[N_HEADS_SB * HEAD_DIM:][:N_HEADS_NSA * HEAD_DIM][perm_rows],
                              wo[(N_HEADS_SB + N_HEADS_NSA) * HEAD_DIM:]], axis=0).astype(_BF16)
        last = l == depth - 1
        x2 = _out_proj(x2, o_a, o_b, o_c, wo, final_g.reshape(1, d), last)
    return x2.reshape(b, t, d)
```

```python
import functools
import math

import jax
import jax.numpy as jnp
import numpy as np
from jax import lax
from jax.experimental import pallas as pl
from jax.experimental.pallas import tpu as pltpu

D_MODEL = 1024
HEAD_DIM = 64
N_HEADS_SB = 4
N_HEADS_FOX = 4
N_HEADS_NSA = 8
NSA_KV_GROUPS = 2
NSA_HPG = N_HEADS_NSA // NSA_KV_GROUPS
CMP_BLOCK = 32
CMP_STRIDE = 16
CMP_HIDDEN = 256
SLC_BLOCK = 64
SLC_TOP = 16
WINDOW = 512
REL_BUCKETS = 32
REL_MAX_DIST = 128
FORCE_SCORE = 1e4
RMS_EPS = 1e-6
NEG = -1e30
LOG2E = math.log2(math.e)

LANES = 128
SUBLANES = 8
VMEM_LIMIT = 56 * 1024 * 1024

TM_PROJ = 512
TQ_SB = 512
TQ_FOX = 512
FOX_PARTS = 2
NSA_PARTS = 2
TQ_NSA = 128
TK = 128

PB_QB, PB_QA, PB_KA, PB_VA = 0, 4, 6, 8
PB_KS, PB_KW = 10, 11
PB_QC, PB_KF = 12, 14
PB_BLOCKS = 16
N_CMP_SLABS = 2 * NSA_KV_GROUPS
VT_VS, VT_VW, VT_VF = 0, 1, 2
N_VT = 4
PF_ZB, PF_ZA, PF_ZC, PF_GB, PF_FC = 0, 4, 6, 8, 9
PF_BLOCKS = 10

NSA_HEAD_ORDER = [0, 4, 1, 5, 2, 6, 3, 7]

_F32 = jnp.float32
_BF16 = jnp.bfloat16


def _cparams(sem):
    return pltpu.CompilerParams(dimension_semantics=sem, vmem_limit_bytes=VMEM_LIMIT)


def _dot(a, b):
    return jnp.dot(a, b, preferred_element_type=_F32)


def _dot_nt(a, b):
    return lax.dot_general(a, b, (((1,), (1,)), ((), ())), preferred_element_type=_F32)


def _split3(x):
    h1 = x.astype(_BF16)
    r1 = x - h1.astype(_F32)
    h2 = r1.astype(_BF16)
    h3 = (r1 - h2.astype(_F32)).astype(_BF16)
    return jnp.concatenate([h1, h2, h3], axis=1)


def _sigmoid(x):
    return 1.0 / (1.0 + jnp.exp(-x))


def _silu(x):
    return x * _sigmoid(x)


def _rel_bucket_np(n):
    n = np.maximum(n, 0)
    max_exact = REL_BUCKETS // 2
    nf = np.maximum(n, 1).astype(np.float64)
    large = max_exact + (np.log(nf / max_exact) / math.log(REL_MAX_DIST / max_exact)
                         * (REL_BUCKETS - max_exact)).astype(np.int64)
    large = np.minimum(large, REL_BUCKETS - 1)
    return np.where(n < max_exact, n, large)


def _bucket_thresholds():
    n = np.arange(0, 4 * REL_MAX_DIST)
    bk = _rel_bucket_np(n)
    assert np.all(np.diff(bk) >= 0) and bk[-1] == REL_BUCKETS - 1
    return [int(np.argmax(bk >= b)) for b in range(REL_BUCKETS)]


_BUCKET_THR = _bucket_thresholds()


def _bias_kernel(tab_ref, dist_ref, o_ref):
    n = dist_ref[...]
    acc = [jnp.full(n.shape, tab_ref[0, h], _F32) for h in range(N_HEADS_NSA)]
    for b in range(1, REL_BUCKETS):
        ge = n >= _BUCKET_THR[b]
        for h in range(N_HEADS_NSA):
            acc[h] = jnp.where(ge, tab_ref[b, h], acc[h])
    valid = n >= 0
    for h in range(N_HEADS_NSA):
        o_ref[h] = jnp.where(valid, acc[h], NEG)


def _bias_table(rel_bias, dist, rows):
    n_rows, n_cols = dist.shape
    return pl.pallas_call(
        _bias_kernel,
        grid=(n_rows // rows,),
        in_specs=[pl.BlockSpec(memory_space=pltpu.SMEM),
                  pl.BlockSpec((rows, n_cols), lambda i: (i, 0))],
        out_specs=pl.BlockSpec((N_HEADS_NSA, rows, n_cols), lambda i: (0, i, 0)),
        out_shape=jax.ShapeDtypeStruct((N_HEADS_NSA, n_rows, n_cols), _F32),
        compiler_params=_cparams(("arbitrary",)),
        name="rel_bias_table",
    )(rel_bias, dist)


def _proj_kernel(x_ref, g_ref, wb_ref, wf_ref, wc_ref, wvt_ref, pb_ref, pf_ref, pc_ref, vt_ref):
    x = x_ref[...]
    y = x * lax.rsqrt(jnp.mean(x * x, axis=-1, keepdims=True) + RMS_EPS)
    h = (y * g_ref[...]).astype(_BF16)
    v_all = _dot(h, wvt_ref[...])
    for j in range(N_VT):
        v_t = v_all[:, j * LANES:(j + 1) * LANES].T.astype(_BF16)
        for c in range(TM_PROJ // TK):
            vt_ref[0, j, c] = v_t[:, c * TK:(c + 1) * TK]
    kv_cmp = _dot(h, wc_ref[...]).astype(_BF16)
    for s in range(N_CMP_SLABS):
        pc_ref[0, s] = kv_cmp[:, s * HEAD_DIM:(s + 1) * HEAD_DIM]
    chunk = 4 * LANES
    for c in range(0, PB_BLOCKS * LANES, chunk):
        w = min(chunk, PB_BLOCKS * LANES - c)
        res = _dot(h, wb_ref[:, c:c + w]).astype(_BF16)
        for j in range(w // LANES):
            pb_ref[0, c // LANES + j] = res[:, j * LANES:(j + 1) * LANES]
    for c in range(0, PF_BLOCKS * LANES, chunk):
        w = min(chunk, PF_BLOCKS * LANES - c)
        res = _dot(h, wf_ref[:, c:c + w])
        for j in range(w // LANES):
            pf_ref[0, c // LANES + j] = res[:, j * LANES:(j + 1) * LANES]


def _proj(x2, g, wb, wf, wc, wvt, t):
    n = x2.shape[0]
    per_seq = t // TM_PROJ
    return pl.pallas_call(
        _proj_kernel,
        grid=(n // TM_PROJ,),
        in_specs=[pl.BlockSpec((TM_PROJ, D_MODEL), lambda i: (i, 0)),
                  pl.BlockSpec((1, D_MODEL), lambda i: (0, 0)),
                  pl.BlockSpec((D_MODEL, PB_BLOCKS * LANES), lambda i: (0, 0)),
                  pl.BlockSpec((D_MODEL, PF_BLOCKS * LANES), lambda i: (0, 0)),
                  pl.BlockSpec((D_MODEL, N_CMP_SLABS * HEAD_DIM), lambda i: (0, 0)),
                  pl.BlockSpec((D_MODEL, N_VT * LANES), lambda i: (0, 0))],
        out_specs=[pl.BlockSpec((1, PB_BLOCKS, TM_PROJ, LANES), lambda i: (i // per_seq, 0, i % per_seq, 0)),
                   pl.BlockSpec((1, PF_BLOCKS, TM_PROJ, LANES), lambda i: (i // per_seq, 0, i % per_seq, 0)),
                   pl.BlockSpec((1, N_CMP_SLABS, TM_PROJ, HEAD_DIM),
                                lambda i: (i // per_seq, 0, i % per_seq, 0)),
                   pl.BlockSpec((1, N_VT, TM_PROJ // TK, LANES, TK),
                                lambda i: (i // per_seq, 0, i % per_seq, 0, 0))],
        out_shape=[jax.ShapeDtypeStruct((n // t, PB_BLOCKS, t, LANES), _BF16),
                   jax.ShapeDtypeStruct((n // t, PF_BLOCKS, t, LANES), _F32),
                   jax.ShapeDtypeStruct((n // t, N_CMP_SLABS, t, HEAD_DIM), _BF16),
                   jax.ShapeDtypeStruct((n // t, N_VT, t // TK, LANES, TK), _BF16)],
        compiler_params=_cparams(("arbitrary",)),
        name="rmsnorm_in_proj",
    )(x2, g, wb, wf, wc, wvt)


N_SPLIT = 3


def _fgate_tables():
    n_pairs = N_HEADS_FOX // 2
    pq = np.zeros((N_SPLIT * LANES, n_pairs * LANES), np.float32)
    pk = np.zeros_like(pq)
    ones_q = np.zeros((1, n_pairs * LANES), np.float32)
    ones_k = np.zeros_like(ones_q)
    for head in range(N_HEADS_FOX):
        pair, slot = divmod(head, 2)
        base = pair * LANES + (HEAD_DIM if slot == 0 else 0)
        for j in range(N_SPLIT):
            pq[j * LANES + head, base + j] = 1.0
            pk[j * LANES + head, base + N_SPLIT + j] = 1.0
        ones_q[0, base + N_SPLIT:base + 2 * N_SPLIT] = 1.0
        ones_k[0, base:base + N_SPLIT] = 1.0
    return (jnp.asarray(pq, _BF16), jnp.asarray(pk, _BF16),
            jnp.asarray(ones_q), jnp.asarray(ones_k))


def _fgate_kernel(fc_ref, fb_ref, pq_ref, pk_ref, oq_ref, ok_ref, augq_ref, augk_ref):
    t = fc_ref.shape[2]
    z = fc_ref[0, 0] + fb_ref[...]
    logf = jnp.minimum(z, 0.0) - jnp.log1p(jnp.exp(-jnp.abs(z)))
    row = lax.broadcasted_iota(jnp.int32, (t, LANES), 0)
    c = logf
    shift = 1
    while shift < t:
        c = c + jnp.where(row >= shift, pltpu.roll(c, shift, axis=0), 0.0)
        shift *= 2
    c3 = _split3(c * LOG2E)
    aq = _dot(c3, pq_ref[...]) + oq_ref[...]
    ak = ok_ref[...] - _dot(c3, pk_ref[...])
    for p in range(N_HEADS_FOX // 2):
        augq_ref[0, p] = aq[:, p * LANES:(p + 1) * LANES].astype(_BF16)
        augk_ref[0, p] = ak[:, p * LANES:(p + 1) * LANES].astype(_BF16)


def _fgate(pf3, fb_row):
    b, _, t, _ = pf3.shape
    n_pairs = N_HEADS_FOX // 2
    tables = _fgate_tables()
    full = lambda a: pl.BlockSpec(a.shape, lambda i: (0,) * a.ndim)
    aug = lambda: pl.BlockSpec((1, n_pairs, t, LANES), lambda i: (i, 0, 0, 0))
    return pl.pallas_call(
        _fgate_kernel,
        grid=(b,),
        in_specs=[pl.BlockSpec((1, 1, t, LANES), lambda i: (i, PF_FC, 0, 0)),
                  pl.BlockSpec((1, LANES), lambda i: (0, 0))] + [full(a) for a in tables],
        out_specs=[aug(), aug()],
        out_shape=[jax.ShapeDtypeStruct((b, n_pairs, t, LANES), _BF16),
                   jax.ShapeDtypeStruct((b, n_pairs, t, LANES), _BF16)],
        compiler_params=_cparams(("arbitrary",)),
        name="forget_gate_cumsum",
    )(pf3, fb_row, *tables)


def _compress_kernel(x_ref, w1_ref, pe_ref, b1_ref, w2_ref, o_ref, ot_ref):
    nc = x_ref.shape[3]
    half = CMP_STRIDE * HEAD_DIM
    w1 = w1_ref[0]
    c1 = _dot(jnp.broadcast_to(pe_ref[0], (SUBLANES, 2 * half)), w1)[0:1] + b1_ref[0]
    out = jnp.zeros((nc, LANES), _F32)
    for g in range(NSA_KV_GROUPS):
        xg = x_ref[0, 0, g]
        a = _dot(xg, w1[:half])
        bb = _dot(xg, w1[half:])
        h = a + pltpu.roll(bb, nc - 1, axis=0) + c1
        out = out + _dot(_silu(h).astype(_BF16), w2_ref[0, g])
    o_ref[0, 0] = out.astype(_BF16)
    ot_ref[0, 0] = out.T.astype(_BF16)


def _compress(halves, w1, pe, b1, w2p):
    b, _, g, nc, width = halves.shape
    return pl.pallas_call(
        _compress_kernel,
        grid=(b, 2),
        in_specs=[pl.BlockSpec((1, 1, g, nc, width), lambda i, k: (i, k, 0, 0, 0)),
                  pl.BlockSpec((1, 2 * width, CMP_HIDDEN), lambda i, k: (k, 0, 0)),
                  pl.BlockSpec((1, 1, 2 * width), lambda i, k: (k, 0, 0)),
                  pl.BlockSpec((1, 1, CMP_HIDDEN), lambda i, k: (k, 0, 0)),
                  pl.BlockSpec((1, g, CMP_HIDDEN, LANES), lambda i, k: (k, 0, 0, 0))],
        out_specs=[pl.BlockSpec((1, 1, nc, LANES), lambda i, k: (i, k, 0, 0)),
                   pl.BlockSpec((1, 1, LANES, nc), lambda i, k: (i, k, 0, 0))],
        out_shape=[jax.ShapeDtypeStruct((b, 2, nc, LANES), _BF16),
                   jax.ShapeDtypeStruct((b, 2, LANES, nc), _BF16)],
        compiler_params=_cparams(("arbitrary", "arbitrary")),
        name="nsa_compress",
    )(halves, w1, pe, b1, w2p)


def _sb_kernel(q_ref, k_ref, v_ref, z_ref, o_ref, qh_ref, c_ref, acc_ref):
    qi = pl.program_id(1)
    tq = q_ref.shape[2]
    n_pairs = q_ref.shape[1]
    lane = lax.broadcasted_iota(jnp.int32, (tq, LANES), 1)
    r_i = lax.broadcasted_iota(jnp.int32, (2 * TK, 2 * TK), 0)
    c_i = lax.broadcasted_iota(jnp.int32, (2 * TK, 2 * TK), 1)
    uu = jnp.where(r_i >= c_i, 1.0, 0.0).astype(_BF16)
    c_ref[...] = jnp.zeros_like(c_ref)
    acc_ref[...] = jnp.zeros_like(acc_ref)
    for p in range(n_pairs):
        q2 = q_ref[0, p]
        qh_ref[2 * p] = jnp.where(lane < HEAD_DIM, q2, jnp.zeros_like(q2))
        qh_ref[2 * p + 1] = jnp.where(lane >= HEAD_DIM, q2, jnp.zeros_like(q2))
    row = lax.broadcasted_iota(jnp.int32, (tq, LANES), 0)
    n_blocks = tq // TK

    def chunk(start, diagonal):
        for p in range(n_pairs):
            head_pair(p, k_ref[0, p, pl.ds(start, tq), :], v_ref[0, p, pl.ds(start, tq), :], diagonal)

    def head_pair(p, k2, v2, diagonal):
        for h in range(2 * p, 2 * p + 2):
            s = _dot_nt(qh_ref[h], k2)
            carry = c_ref[h]
            w_blocks = [None] * n_blocks
            for c in reversed(range(0, n_blocks, 2)):
                nz, l1m, mask = [], [], []
                for cc in (c, c + 1):
                    nzc = s[:, cc * TK:(cc + 1) * TK]
                    neg_abs = lax.bitcast_convert_type(
                        lax.bitcast_convert_type(nzc, jnp.uint32) | jnp.uint32(0x80000000), _F32)
                    lc = jnp.minimum(nzc, 0.0) - jnp.log(1.0 + jnp.exp(neg_abs))
                    if diagonal:
                        mask.append(lane + cc * TK < row)
                        lc = jnp.where(mask[-1], lc, 0.0)
                    nz.append(nzc)
                    l1m.append(lc.astype(_BF16))
                rc = _dot(jnp.concatenate(l1m, axis=1), uu)
                for i, cc in enumerate((c, c + 1)):
                    w = jnp.exp((rc[:, i * TK:(i + 1) * TK] + carry) - nz[i])
                    if diagonal:
                        w = jnp.where(mask[i], w, 0.0)
                    w_blocks[cc] = w.astype(_BF16)
                carry = carry + jnp.broadcast_to(rc[:, 0:1], carry.shape)
            acc_ref[h] += _dot(jnp.concatenate(w_blocks, axis=1), v2)
            c_ref[h] = carry

    chunk(pl.multiple_of(qi * tq, tq), True)

    def far_pair(it, carry):
        chunk(pl.multiple_of((qi - 1 - 2 * it) * tq, tq), False)
        chunk(pl.multiple_of((qi - 2 - 2 * it) * tq, tq), False)
        return carry

    lax.fori_loop(0, qi >> 1, far_pair, 0)

    @pl.when((qi & 1) == 1)
    def _():
        chunk(0, False)
    for p in range(n_pairs):
        o = jnp.where(lane < HEAD_DIM, acc_ref[2 * p], acc_ref[2 * p + 1])
        o_ref[0, p] = (o * _silu(z_ref[0, p])).astype(o_ref.dtype)


def _sb_attention(pb3, pf3):
    b, _, t, _ = pb3.shape
    n_pairs = N_HEADS_SB // 2
    assert PB_QA % n_pairs == 0 and PB_KA % n_pairs == 0 and PB_VA % n_pairs == 0 and PF_ZA % n_pairs == 0
    return pl.pallas_call(
        _sb_kernel,
        grid=(b, t // TQ_SB),
        in_specs=[pl.BlockSpec((1, n_pairs, TQ_SB, LANES), lambda i, q: (i, PB_QA // n_pairs, q, 0)),
                  pl.BlockSpec((1, n_pairs, t, LANES), lambda i, q: (i, PB_KA // n_pairs, 0, 0)),
                  pl.BlockSpec((1, n_pairs, t, LANES), lambda i, q: (i, PB_VA // n_pairs, 0, 0)),
                  pl.BlockSpec((1, n_pairs, TQ_SB, LANES), lambda i, q: (i, PF_ZA // n_pairs, q, 0))],
        out_specs=pl.BlockSpec((1, n_pairs, TQ_SB, LANES), lambda i, q: (i, 0, q, 0)),
        out_shape=jax.ShapeDtypeStruct((b, n_pairs, t, LANES), _BF16),
        scratch_shapes=[pltpu.VMEM((N_HEADS_SB, TQ_SB, LANES), _BF16),
                        pltpu.VMEM((N_HEADS_SB, TQ_SB, LANES), _F32),
                        pltpu.VMEM((N_HEADS_SB, TQ_SB, LANES), _F32)],
        compiler_params=_cparams(("arbitrary", "arbitrary")),
        name="stick_breaking_attention",
    )(pb3, pb3, pb3, pf3)


def _fox_kernel(q_ref, k_ref, vt_ref, augq_ref, augk_ref, z_ref, o_ref,
                qh_ref, m_ref, acc_ref, s_ref):
    qi = pl.program_id(1)
    tq = q_ref.shape[2]
    tk = tq
    n_hp = q_ref.shape[1]
    lane = lax.broadcasted_iota(jnp.int32, (tq, LANES), 1)
    keep = [lane < HEAD_DIM, lane >= HEAD_DIM]
    for p in range(n_hp):
        for h in range(2):
            qh_ref[2 * p + h] = jnp.where(keep[h], q_ref[0, p], augq_ref[0, p])
    m_ref[...] = jnp.full_like(m_ref, NEG)
    acc_ref[...] = jnp.zeros_like(acc_ref)
    key_i = lax.broadcasted_iota(jnp.int32, (TK, tq), 0)
    qry_i = lax.broadcasted_iota(jnp.int32, (TK, tq), 1)

    def scores(c, slot):
        start = c * tk if isinstance(c, int) else pl.multiple_of(c * tk, tk)
        for p in range(n_hp):
            k2 = k_ref[0, p, pl.ds(start, tk), :]
            ak = augk_ref[0, p, pl.ds(start, tk), :]
            for h in range(2):
                s_ref[slot, 2 * p + h] = _dot_nt(jnp.where(keep[h], k2, ak), qh_ref[2 * p + h])

    ones_rows = jnp.ones((SUBLANES, tk), _BF16)

    def consume(c, slot, diagonal):
        per = tk // TK
        for p in range(n_hp):
            vt = jnp.concatenate([vt_ref[0, p, c * per + j] for j in range(per)], axis=1)
            for hh in range(2):
                consume_head(2 * p + hh, vt[hh * HEAD_DIM:(hh + 1) * HEAD_DIM], slot, diagonal)

    def consume_head(h, vt_h, slot, diagonal):
        per = tk // TK
        vth = jnp.concatenate([vt_h, ones_rows], axis=0)
        for part in range(FOX_PARTS):
            js = range(part * per // FOX_PARTS, (part + 1) * per // FOX_PARTS)
            blocks = [s_ref[slot, h, j * TK:(j + 1) * TK, :] for j in js]
            if diagonal:
                blocks = [jnp.where(key_i + j * TK <= qry_i, blk, NEG) for j, blk in zip(js, blocks)]
            mx = blocks[0]
            for blk in blocks[1:]:
                mx = jnp.maximum(mx, blk)
            m_old = m_ref[h]
            m_new = jnp.maximum(m_old, jnp.max(mx, axis=0, keepdims=True))
            m_row = m_new[0:1]
            pt = jnp.concatenate([jnp.exp2(blk - m_row).astype(_BF16) for blk in blocks], axis=0)
            acc_ref[h] = (jnp.exp2(m_old - m_new)[0:1] * acc_ref[h]
                          + _dot(vth[:, js[0] * TK:(js[-1] + 1) * TK], pt))
            m_ref[h] = m_new

    scores(0, 0)
    n_pairs = qi >> 1

    def far_pair(i, carry):
        c = 2 * i
        scores(c + 1, 1)
        consume(c, 0, False)
        scores(c + 2, 0)
        consume(c + 1, 1, False)
        return carry

    lax.fori_loop(0, n_pairs, far_pair, 0)

    @pl.when(qi == 2 * n_pairs)
    def _():
        consume(qi, 0, True)

    @pl.when(qi != 2 * n_pairs)
    def _():
        scores(qi, 1)
        consume(qi - 1, 0, False)
        consume(qi, 1, True)
    for p in range(n_hp):
        o_t = jnp.concatenate([acc_ref[2 * p + h, :HEAD_DIM] / acc_ref[2 * p + h, HEAD_DIM:HEAD_DIM + 1]
                               for h in range(2)], axis=0)
        o_ref[0, p] = (o_t.T * _silu(z_ref[0, p])).astype(o_ref.dtype)


def _fox_attention(pb3, pf3, vt, augq, augk):
    b, _, t, _ = pb3.shape
    n_pairs = N_HEADS_FOX // 2
    assert PB_QC % n_pairs == 0 and PB_KF % n_pairs == 0 and VT_VF % n_pairs == 0 and PF_ZC % n_pairs == 0
    return pl.pallas_call(
        _fox_kernel,
        grid=(b, t // TQ_FOX),
        in_specs=[pl.BlockSpec((1, n_pairs, TQ_FOX, LANES), lambda i, q: (i, PB_QC // n_pairs, q, 0)),
                  pl.BlockSpec((1, n_pairs, t, LANES), lambda i, q: (i, PB_KF // n_pairs, 0, 0)),
                  pl.BlockSpec((1, n_pairs, t // TK, LANES, TK), lambda i, q: (i, VT_VF // n_pairs, 0, 0, 0)),
                  pl.BlockSpec((1, n_pairs, TQ_FOX, LANES), lambda i, q: (i, 0, q, 0)),
                  pl.BlockSpec((1, n_pairs, t, LANES), lambda i, q: (i, 0, 0, 0)),
                  pl.BlockSpec((1, n_pairs, TQ_FOX, LANES), lambda i, q: (i, PF_ZC // n_pairs, q, 0))],
        out_specs=pl.BlockSpec((1, n_pairs, TQ_FOX, LANES), lambda i, q: (i, 0, q, 0)),
        out_shape=jax.ShapeDtypeStruct((b, n_pairs, t, LANES), _BF16),
        scratch_shapes=[pltpu.VMEM((N_HEADS_FOX, TQ_FOX, LANES), _BF16),
                        pltpu.VMEM((N_HEADS_FOX, SUBLANES, TQ_FOX), _F32),
                        pltpu.VMEM((N_HEADS_FOX, HEAD_DIM + SUBLANES, TQ_FOX), _F32),
                        pltpu.VMEM((2, N_HEADS_FOX, TQ_FOX, TQ_FOX), _F32)],
        compiler_params=_cparams(("arbitrary", "arbitrary")),
        name="forgetting_attention",
    )(pb3, pb3, vt, augq, augk, pf3)


def _nsa_kernel(q_ref, kc_ref, vct_ref, ks_ref, vs_ref, kw_ref, vw_ref, gl_ref, z_ref,
                bc_ref, bd_ref, ov_ref, o_ref,
                m_ref, acc_ref, osum_ref, qz_ref, qsel_ref, gates_ref, psum_ref, s_ref, ws_ref):
    qi = pl.program_id(1)
    tq = q_ref.shape[2]
    hpg = NSA_HPG
    rows = hpg * tq
    n_cmp = kc_ref.shape[2]
    lane_r = lax.broadcasted_iota(jnp.int32, (rows, LANES), 1)
    half = [lane_r < HEAD_DIM, lane_r >= HEAD_DIM]

    gates_ref[...] = _sigmoid(gl_ref[0, 0]).T

    def gated(c, h, per_group):
        parts = []
        for g, o in enumerate(per_group):
            r = 3 * (g * hpg + h) + c
            parts.append(gates_ref[r:r + 1, :] * o[:, h * tq:(h + 1) * tq])
        return jnp.concatenate(parts, axis=0)

    q4 = jnp.concatenate([q_ref[0, h] for h in range(hpg)], axis=0)
    for g in range(NSA_KV_GROUPS):
        qz_ref[g] = jnp.where(half[g], q4, jnp.zeros_like(q4))

    def scores(q_src, k_ref, start, tk, onehot):
        k2 = k_ref[0, 0, pl.ds(start, tk), :]
        lane_k = lax.broadcasted_iota(jnp.int32, (tk, LANES), 1)
        key_blk = (start + lax.broadcasted_iota(jnp.int32, (tk, LANES), 0)) >> int(math.log2(SLC_BLOCK))
        oh = jnp.where((lane_k & (SLC_BLOCK - 1)) == key_blk, 1.0, 0.0).astype(_BF16)
        out = []
        for g in range(NSA_KV_GROUPS):
            keep = (lane_k < HEAD_DIM) if g == 0 else (lane_k >= HEAD_DIM)
            out.append(_dot_nt(jnp.where(keep, k2, oh) if onehot else k2, q_src[g]))
        return out


    kc = kc_ref[0, 0]
    vct = vct_ref[0, 0]
    cmp_row0 = pl.multiple_of(n_cmp - qi * (tq // CMP_STRIDE), SUBLANES)
    o_cmp_t = []
    raw = [_dot_nt(kc, qz_ref[g]) for g in range(NSA_KV_GROUPS)]
    for g in range(NSA_KV_GROUPS):
        bias = jnp.concatenate([bc_ref[g * hpg + h, pl.ds(cmp_row0, n_cmp), :]
                                for h in range(hpg)], axis=1)
        sc = raw[g] + bias
        mx = jnp.max(sc, axis=0, keepdims=True)
        e = jnp.exp2(sc - mx)
        den = jnp.sum(e, axis=0, keepdims=True)
        pc = e * jnp.where(mx > 0.5 * NEG, 1.0 / den, 0.0)
        psum = pc[:, 0:tq]
        for h in range(1, hpg):
            psum = psum + pc[:, h * tq:(h + 1) * tq]
        psum_ref[g] = psum
        o_cmp_t.append(_dot(vct[g * HEAD_DIM:(g + 1) * HEAD_DIM], pc.astype(_BF16)))
    for h in range(hpg):
        osum_ref[h] = gated(0, h, o_cmp_t)

    rank_from = SLC_TOP * SLC_BLOCK // tq

    @pl.when(qi < rank_from)
    def _():
        for g in range(NSA_KV_GROUPS):
            qsel_ref[g] = qz_ref[g]

    @pl.when(qi >= rank_from)
    def _():
        n_sel = LANES // NSA_KV_GROUPS
        n_grp = n_sel // SUBLANES
        blk = lax.broadcasted_iota(jnp.int32, (n_sel, tq), 0)
        cur = (qi * tq + lax.broadcasted_iota(jnp.int32, (n_sel, tq), 1)) >> int(math.log2(SLC_BLOCK))
        forced = (blk == 0) | (blk == cur) | (blk == cur - 1)
        sub = lax.broadcasted_iota(jnp.int32, (SUBLANES, tq), 0)
        neg_t = []
        for g in range(NSA_KV_GROUPS):
            p = psum_ref[g]
            p1 = p.astype(_BF16)
            r1 = p - p1.astype(_F32)
            p2 = r1.astype(_BF16)
            p3 = (r1 - p2.astype(_F32)).astype(_BF16)
            a = _dot(ov_ref[...], jnp.concatenate([p1, p2, p3], axis=0))
            a = jnp.where(forced, FORCE_SCORE, a)
            a = jnp.where(blk > cur, -FORCE_SCORE, a)
            a_grp = [a[r * SUBLANES:(r + 1) * SUBLANES] for r in range(n_grp)]
            cnt = [jnp.zeros((SUBLANES, tq), _F32) for _ in range(n_grp)]
            for j in range(n_sel):
                rj = jnp.broadcast_to(a[j:j + 1], (SUBLANES, tq))
                jr, jo = divmod(j, SUBLANES)
                for r in range(n_grp):
                    if r > jr:
                        one = jnp.where(rj >= a_grp[r], 1.0, 0.0)
                    elif r < jr:
                        one = jnp.where(rj > a_grp[r], 1.0, 0.0)
                    else:
                        tie = jnp.where(sub > jo, jnp.where(rj == a_grp[r], 1.0, 0.0), 0.0)
                        one = jnp.where(rj > a_grp[r], 1.0, tie)
                    cnt[r] = cnt[r] + one
            neg_t.append(jnp.where(jnp.concatenate(cnt, axis=0) < float(SLC_TOP), 0.0, NEG))
        selneg = jnp.concatenate(neg_t[::-1], axis=0).T.astype(_BF16)
        selneg4 = jnp.concatenate([selneg] * hpg, axis=0)
        for g in range(NSA_KV_GROUPS):
            qsel_ref[g] = jnp.where(half[g], q4, selneg4)

    SEL, WIN = 0, 1

    def reset(st):
        m_ref[st] = jnp.full(m_ref.shape[1:], 2.0 * NEG, _F32)
        acc_ref[st] = jnp.zeros(acc_ref.shape[1:], _F32)

    def chunk(q_src, k_ref, vt_ref, start, tk, onehot, extras, st):
        s = scores(q_src, k_ref, start, tk, onehot)
        consume(lambda g, c: s[g][c * TK:(c + 1) * TK], vt_ref, start, tk, extras, st)

    def consume(block_of, vt_ref, start, tk, extras, st, limit=None):
        first = start // TK if isinstance(start, int) else start >> int(math.log2(TK))
        vt = jnp.concatenate([vt_ref[0, 0, first + j] for j in range(tk // TK)], axis=1)
        ones_rows = jnp.ones((SUBLANES, tk), _BF16)
        key_pos = start + lax.broadcasted_iota(jnp.int32, (HEAD_DIM + SUBLANES, tk), 1)
        for g in range(NSA_KV_GROUPS):
            vth = jnp.concatenate([vt[g * HEAD_DIM:(g + 1) * HEAD_DIM], ones_rows], axis=0)
            if limit is not None:
                vth = jnp.where(key_pos < limit, vth, jnp.zeros_like(vth))
            n_blk = tk // TK
            n_parts = NSA_PARTS if n_blk >= 2 * NSA_PARTS else 1
            for part in range(n_parts):
                cs = range(part * n_blk // n_parts, (part + 1) * n_blk // n_parts)
                blocks = [block_of(g, c) + extras[c](g) if c in extras else block_of(g, c) for c in cs]
                mx = blocks[0]
                for blk_s in blocks[1:]:
                    mx = jnp.maximum(mx, blk_s)
                m_old = m_ref[st, g]
                m_new = jnp.maximum(m_old, jnp.max(mx, axis=0, keepdims=True))
                m_row = m_new[0:1]
                pt = jnp.concatenate([jnp.exp2(blk_s - m_row).astype(_BF16) for blk_s in blocks], axis=0)
                acc_ref[st, g] = (jnp.exp2(m_old - m_new)[0:1] * acc_ref[st, g]
                                  + _dot(vth[:, cs[0] * TK:(cs[-1] + 1) * TK], pt))
                m_ref[st, g] = m_new

    def finish(c, st):
        outs = []
        for g in range(NSA_KV_GROUPS):
            den = acc_ref[st, g, HEAD_DIM:HEAD_DIM + 1]
            outs.append(acc_ref[st, g, :HEAD_DIM] * jnp.where(den > 0.0, 1.0 / den, 0.0))
        for h in range(hpg):
            osum_ref[h] += gated(c, h, outs)

    def near_bias(d):
        return lambda g: bd_ref[g, d]

    def short_path(q_src, k_ref, v_ref, onehot, st):
        def far(kt, carry):
            chunk(q_src, k_ref, v_ref, pl.multiple_of(kt * TK, TK), TK, onehot, {}, st)
            return carry

        lax.fori_loop(0, jnp.maximum(qi - 1, 0), far, 0)

        @pl.when(qi >= 1)
        def _():
            chunk(q_src, k_ref, v_ref, pl.multiple_of((qi - 1) * TK, TK), TK, onehot,
                  {0: near_bias(1)}, st)

        chunk(q_src, k_ref, v_ref, pl.multiple_of(qi * TK, TK), TK, onehot, {0: near_bias(0)}, st)

    reset(SEL)
    reset(WIN)
    big = 4 * TK
    n_win = WINDOW // TK
    main_from = max(big // TK - 1, n_win)
    win_mask = jnp.where(lax.broadcasted_iota(jnp.int32, (TK, rows), 0)
                         > (lax.broadcasted_iota(jnp.int32, (TK, rows), 1) & (tq - 1)), 0.0, NEG)

    @pl.when(qi < main_from)
    def _():
        short_path(qsel_ref, ks_ref, vs_ref, True, SEL)
        short_path(qz_ref, kw_ref, vw_ref, False, WIN)

    @pl.when(qi >= main_from)
    def _():
        last = pl.multiple_of((qi - (big // TK - 1)) * TK, TK)
        n_far = (qi - (big // TK - 1) + big // TK - 1) >> int(math.log2(big // TK))
        near = {big // TK - 2: near_bias(1), big // TK - 1: near_bias(0)}

        def start_of(k):
            return pl.multiple_of(jnp.where(k < n_far, k * big, last), TK)

        def put(k, slot):
            for g, s in enumerate(scores(qsel_ref, ks_ref, start_of(k), big, True)):
                s_ref[slot, g] = s

        def take(k, slot, extras, limit):
            consume(lambda g, c: s_ref[slot, g, c * TK:(c + 1) * TK, :], vs_ref,
                    start_of(k), big, extras, SEL, limit)

        win_start = pl.multiple_of((qi - n_win) * TK, TK)
        for g, s in enumerate(scores(qz_ref, kw_ref, win_start, WINDOW + TK, False)):
            ws_ref[g] = s
        put(0, 0)
        consume(lambda g, c: ws_ref[g, c * TK:(c + 1) * TK, :], vw_ref, win_start, WINDOW + TK,
                {0: lambda g: win_mask, n_win - 1: near_bias(1), n_win: near_bias(0)}, WIN)
        n_pair = n_far >> 1

        def far_pair(i, carry):
            k = 2 * i
            put(k + 1, 1)
            take(k, 0, {}, last)
            put(k + 2, 0)
            take(k + 1, 1, {}, last)
            return carry

        lax.fori_loop(0, n_pair, far_pair, 0)

        @pl.when(n_far == 2 * n_pair)
        def _():
            take(n_far, 0, near, None)

        @pl.when(n_far != 2 * n_pair)
        def _():
            put(n_far, 1)
            take(n_far - 1, 0, {}, last)
            take(n_far, 1, near, None)

    finish(1, SEL)
    finish(2, WIN)

    for h in range(hpg):
        o_ref[0, h] = (osum_ref[h].T * _silu(z_ref[0, h])).astype(o_ref.dtype)


def _nsa_attention(pb3, pf3, vt, kvc, kvc_t, bias_c, bias_d, ov):
    b, _, t, _ = pb3.shape
    n_cmp = kvc.shape[2]
    n_blk = N_HEADS_NSA // 2
    full = lambda shape: pl.BlockSpec(shape, lambda i, q: (0,) * len(shape))
    kv_spec = lambda col: pl.BlockSpec((1, 1, t, LANES), lambda i, q: (i, col, 0, 0))
    vt_spec = lambda blk: pl.BlockSpec((1, 1, t // TK, LANES, TK), lambda i, q: (i, blk, 0, 0, 0))
    return pl.pallas_call(
        _nsa_kernel,
        grid=(b, t // TQ_NSA),
        in_specs=[pl.BlockSpec((1, n_blk, TQ_NSA, LANES), lambda i, q: (i, PB_QB // n_blk, q, 0)),
                  pl.BlockSpec((1, 1, n_cmp, LANES), lambda i, q: (i, 0, 0, 0)),
                  pl.BlockSpec((1, 1, LANES, n_cmp), lambda i, q: (i, 1, 0, 0)),
                  kv_spec(PB_KS), vt_spec(VT_VS), kv_spec(PB_KW), vt_spec(VT_VW),
                  pl.BlockSpec((1, 1, TQ_NSA, LANES), lambda i, q: (i, PF_GB, q, 0)),
                  pl.BlockSpec((1, n_blk, TQ_NSA, LANES), lambda i, q: (i, PF_ZB // n_blk, q, 0)),
                  full(bias_c.shape),
                  full(bias_d.shape), full(ov.shape)],
        out_specs=pl.BlockSpec((1, n_blk, TQ_NSA, LANES), lambda i, q: (i, 0, q, 0)),
        out_shape=jax.ShapeDtypeStruct((b, n_blk, t, LANES), _BF16),
        scratch_shapes=[pltpu.VMEM((2, NSA_KV_GROUPS, SUBLANES, NSA_HPG * TQ_NSA), _F32),
                        pltpu.VMEM((2, NSA_KV_GROUPS, HEAD_DIM + SUBLANES, NSA_HPG * TQ_NSA), _F32),
                        pltpu.VMEM((n_blk, TQ_NSA, LANES), _F32),
                        pltpu.VMEM((NSA_KV_GROUPS, NSA_HPG * TQ_NSA, LANES), _BF16),
                        pltpu.VMEM((NSA_KV_GROUPS, NSA_HPG * TQ_NSA, LANES), _BF16),
                        pltpu.VMEM((LANES, TQ_NSA), _F32),
                        pltpu.VMEM((NSA_KV_GROUPS, n_cmp, TQ_NSA), _F32),
                        pltpu.VMEM((2, NSA_KV_GROUPS, 4 * TK, NSA_HPG * TQ_NSA), _F32),
                        pltpu.VMEM((NSA_KV_GROUPS, WINDOW + TK, NSA_HPG * TQ_NSA), _F32)],
        compiler_params=_cparams(("arbitrary", "arbitrary")),
        name="native_sparse_attention",
    )(pb3, kvc, kvc_t, pb3, vt, pb3, vt, pf3, pf3, bias_c, bias_d, ov)


def _out_kernel(x_ref, oa_ref, ob_ref, oc_ref, w_ref, g_ref, o_ref, *, final_norm):
    mixed = jnp.concatenate([o_ref_in[0, j] for o_ref_in in (oa_ref, ob_ref, oc_ref)
                             for j in range(o_ref_in.shape[1])], axis=1)
    x = x_ref[...] + _dot(mixed, w_ref[...])
    if final_norm:
        x = x * lax.rsqrt(jnp.mean(x * x, axis=-1, keepdims=True) + RMS_EPS) * g_ref[...]
    o_ref[...] = x


def _out_proj(x2, oa, ob, oc, w, g, final_norm):
    n = x2.shape[0]
    per_seq = oa.shape[2] // TM_PROJ
    row = lambda width: pl.BlockSpec((TM_PROJ, width), lambda i: (i, 0))
    blocks = lambda a: pl.BlockSpec((1, a.shape[1], TM_PROJ, LANES),
                                    lambda i: (i // per_seq, 0, i % per_seq, 0))
    return pl.pallas_call(
        functools.partial(_out_kernel, final_norm=final_norm),
        grid=(n // TM_PROJ,),
        in_specs=[row(D_MODEL), blocks(oa), blocks(ob), blocks(oc),
                  pl.BlockSpec((D_MODEL, D_MODEL), lambda i: (0, 0)),
                  pl.BlockSpec((1, D_MODEL), lambda i: (0, 0))],
        out_specs=row(D_MODEL),
        out_shape=jax.ShapeDtypeStruct((n, D_MODEL), _F32),
        compiler_params=_cparams(("arbitrary",)),
        name="out_proj_residual",
    )(x2, oa, ob, oc, w, g)


def _head_perm_cols(width_per_head, order):
    return np.concatenate([np.arange(h * width_per_head, (h + 1) * width_per_head) for h in order])


def _layout_w_in(w):
    widths = [256, 256, 256, 256, 512, 128, 128, 128, 128, 128, 128, 24, 512, 256, 256, 256, 4, 256]
    offs = np.concatenate([[0], np.cumsum(widths)])
    (qa, ka, va, za, qb, kc, vc, ks, vs, kw, vw, gb, zb, qc, kf, vf, fc, zc) = [
        w[:, offs[i]:offs[i + 1]] for i in range(len(widths))]
    scale = HEAD_DIM ** -0.5
    perm = _head_perm_cols(HEAD_DIM, NSA_HEAD_ORDER)
    pad = lambda a: jnp.pad(a, ((0, 0), (0, LANES - a.shape[1])))
    wb = jnp.concatenate([qb[:, perm] * (scale * LOG2E), qa * (-scale), ka, va, ks, kw,
                          qc * (scale * LOG2E), kf], axis=1)
    wf = jnp.concatenate([zb[:, perm], za, zc, pad(gb), pad(fc)], axis=1)
    wc = jnp.concatenate([kc, vc], axis=1)
    wvt = jnp.concatenate([vs, vw, vf], axis=1)
    return wb.astype(_BF16), wf.astype(_BF16), wc.astype(_BF16), wvt.astype(_BF16)


def _static_tables(t):
    tq = TQ_NSA
    n_cmp_pad = t // CMP_STRIDE
    j = np.arange(n_cmp_pad)
    rel = np.arange(2 * n_cmp_pad) - n_cmp_pad
    dist_c = np.arange(tq)[None, :] - (rel[:, None] * CMP_STRIDE + CMP_BLOCK - 1)
    i_, j_ = np.arange(tq)[:, None], np.arange(TK)[None, :]
    dist_d = np.concatenate([d * TK + i_ - j_ for d in range(3)], axis=0)
    n_slc = LANES // NSA_KV_GROUPS
    cmp_start = j * CMP_STRIDE
    cmp_end = cmp_start + CMP_BLOCK - 1
    slc_start = np.arange(n_slc) * SLC_BLOCK
    ov1 = np.clip(np.minimum(cmp_end[:, None], slc_start[None, :] + SLC_BLOCK - 1)
                  - np.maximum(cmp_start[:, None], slc_start[None, :]) + 1, 0, None) / CMP_BLOCK
    ov1[n_cmp_pad - 1:] = 0.0
    ov1[:, t // SLC_BLOCK:] = 0.0
    ov3 = np.concatenate([ov1.T, ov1.T, ov1.T], axis=1)
    return jnp.asarray(dist_c, jnp.int32), jnp.asarray(dist_d, jnp.int32), jnp.asarray(ov3, _BF16)


def kernel(x, norm_g, w_in, w_out, forget_b, cmp_w1, cmp_b1, cmp_w2, cmp_pe, rel_bias, final_g):
    b, t, d = x.shape
    depth = norm_g.shape[0]
    assert d == D_MODEL and t % TM_PROJ == 0 and t % (CMP_STRIDE * LANES) == 0
    assert t // SLC_BLOCK <= LANES // NSA_KV_GROUPS and TQ_NSA == TK
    n_cmp_pad = t // CMP_STRIDE

    dist_c, dist_d, ov3 = _static_tables(t)
    bias_c = _bias_table(rel_bias * LOG2E, dist_c, 32)
    bias_d = _bias_table(rel_bias * LOG2E, dist_d, 32).reshape(N_HEADS_NSA, 3, TQ_NSA, TK)
    bias_d = bias_d[:, :2] - bias_d[:, 2:3]
    bias_d = bias_d.reshape(NSA_KV_GROUPS, NSA_HPG, 2, TQ_NSA, TK).transpose(0, 2, 4, 1, 3).reshape(
        NSA_KV_GROUPS, 2, TK, NSA_HPG * TQ_NSA)

    perm_rows = _head_perm_cols(HEAD_DIM, NSA_HEAD_ORDER)
    x2 = x.reshape(b * t, d)
    for l in range(depth):
        wb, wf, wc, wvt = _layout_w_in(w_in[l])
        pb, pf, pc, vt = _proj(x2, norm_g[l].reshape(1, d), wb, wf, wc, wvt, t)
        pb3, pf3 = pb, pf

        fb_row = jnp.pad(forget_b[l], (0, LANES - N_HEADS_FOX)).reshape(1, LANES)
        augq, augk = _fgate(pf3, fb_row)

        halves = pc.reshape(b, 2, NSA_KV_GROUPS, n_cmp_pad, CMP_STRIDE * HEAD_DIM)
        w2 = cmp_w2[l]
        zeros = jnp.zeros_like(w2)
        w2p = jnp.stack([jnp.concatenate([w2, zeros], axis=-1),
                         jnp.concatenate([zeros, w2], axis=-1)], axis=1).astype(_BF16)
        kvc, kvc_t = _compress(halves, cmp_w1[l].astype(_BF16),
                               cmp_pe[l].reshape(2, 1, CMP_BLOCK * HEAD_DIM).astype(_BF16),
                               cmp_b1[l].reshape(2, 1, CMP_HIDDEN), w2p)

        o_a = _sb_attention(pb3, pf3)
        o_b = _nsa_attention(pb3, pf3, vt, kvc, kvc_t, bias_c, bias_d, ov3)
        o_c = _fox_attention(pb3, pf3, vt, augq, augk)

        wo = w_out[l]
        wo = jnp.concatenate([wo[:N_HEADS_SB * HEAD_DIM],
                              wo[N_HEADS_SB * HEAD_DIM:][:N_HEADS_NSA * HEAD_DIM][perm_rows],
                              wo[(N_HEADS_SB + N_HEADS_NSA) * HEAD_DIM:]], axis=0).astype(_BF16)
        last = l == depth - 1
        x2 = _out_proj(x2, o_a, o_b, o_c, wo, final_g.reshape(1, d), last)
    return x2.reshape(b, t, d)
```

```python
import functools
import math

import jax
import jax.numpy as jnp
import numpy as np
from jax import lax
from jax.experimental import pallas as pl
from jax.experimental.pallas import tpu as pltpu

D_MODEL = 1024
HEAD_DIM = 64
N_HEADS_SB = 4
N_HEADS_FOX = 4
N_HEADS_NSA = 8
NSA_KV_GROUPS = 2
NSA_HPG = N_HEADS_NSA // NSA_KV_GROUPS
CMP_BLOCK = 32
CMP_STRIDE = 16
CMP_HIDDEN = 256
SLC_BLOCK = 64
SLC_TOP = 16
WINDOW = 512
REL_BUCKETS = 32
REL_MAX_DIST = 128
FORCE_SCORE = 1e4
RMS_EPS = 1e-6
NEG = -1e30
LOG2E = math.log2(math.e)

LANES = 128
SUBLANES = 8
VMEM_LIMIT = 56 * 1024 * 1024

TM_PROJ = 512
TQ_SB = 512
TQ_FOX = 512
FOX_PARTS = 2
NSA_PARTS = 2
TQ_NSA = 128
TK = 128

PB_QB, PB_QA, PB_KA, PB_VA = 0, 4, 6, 8
PB_KS, PB_KW = 10, 11
PB_QC, PB_KF = 12, 14
PB_BLOCKS = 16
N_CMP_SLABS = 2 * NSA_KV_GROUPS
VT_VS, VT_VW, VT_VF = 0, 1, 2
N_VT = 4
PF_ZB, PF_ZA, PF_ZC, PF_GB, PF_FC = 0, 4, 6, 8, 9
PF_BLOCKS = 10

NSA_HEAD_ORDER = [0, 4, 1, 5, 2, 6, 3, 7]

_F32 = jnp.float32
_BF16 = jnp.bfloat16


def _cparams(sem):
    return pltpu.CompilerParams(dimension_semantics=sem, vmem_limit_bytes=VMEM_LIMIT)


def _dot(a, b):
    return jnp.dot(a, b, preferred_element_type=_F32)


def _dot_nt(a, b):
    return lax.dot_general(a, b, (((1,), (1,)), ((), ())), preferred_element_type=_F32)


def _split3(x):
    h1 = x.astype(_BF16)
    r1 = x - h1.astype(_F32)
    h2 = r1.astype(_BF16)
    h3 = (r1 - h2.astype(_F32)).astype(_BF16)
    return jnp.concatenate([h1, h2, h3], axis=1)


def _sigmoid(x):
    return 1.0 / (1.0 + jnp.exp(-x))


def _silu(x):
    return x * _sigmoid(x)


def _rel_bucket_np(n):
    n = np.maximum(n, 0)
    max_exact = REL_BUCKETS // 2
    nf = np.maximum(n, 1).astype(np.float64)
    large = max_exact + (np.log(nf / max_exact) / math.log(REL_MAX_DIST / max_exact)
                         * (REL_BUCKETS - max_exact)).astype(np.int64)
    large = np.minimum(large, REL_BUCKETS - 1)
    return np.where(n < max_exact, n, large)


def _bucket_thresholds():
    n = np.arange(0, 4 * REL_MAX_DIST)
    bk = _rel_bucket_np(n)
    assert np.all(np.diff(bk) >= 0) and bk[-1] == REL_BUCKETS - 1
    return [int(np.argmax(bk >= b)) for b in range(REL_BUCKETS)]


_BUCKET_THR = _bucket_thresholds()


def _bias_kernel(tab_ref, dist_ref, o_ref):
    n = dist_ref[...]
    acc = [jnp.full(n.shape, tab_ref[0, h], _F32) for h in range(N_HEADS_NSA)]
    for b in range(1, REL_BUCKETS):
        ge = n >= _BUCKET_THR[b]
        for h in range(N_HEADS_NSA):
            acc[h] = jnp.where(ge, tab_ref[b, h], acc[h])
    valid = n >= 0
    for h in range(N_HEADS_NSA):
        o_ref[h] = jnp.where(valid, acc[h], NEG)


def _bias_table(rel_bias, dist, rows):
    n_rows, n_cols = dist.shape
    return pl.pallas_call(
        _bias_kernel,
        grid=(n_rows // rows,),
        in_specs=[pl.BlockSpec(memory_space=pltpu.SMEM),
                  pl.BlockSpec((rows, n_cols), lambda i: (i, 0))],
        out_specs=pl.BlockSpec((N_HEADS_NSA, rows, n_cols), lambda i: (0, i, 0)),
        out_shape=jax.ShapeDtypeStruct((N_HEADS_NSA, n_rows, n_cols), _F32),
        compiler_params=_cparams(("arbitrary",)),
        name="rel_bias_table",
    )(rel_bias, dist)


def _proj_kernel(x_ref, g_ref, wb_ref, wf_ref, wc_ref, wvt_ref, pb_ref, pf_ref, pc_ref, vt_ref):
    x = x_ref[...]
    y = x * lax.rsqrt(jnp.mean(x * x, axis=-1, keepdims=True) + RMS_EPS)
    h = (y * g_ref[...]).astype(_BF16)
    v_all = _dot(h, wvt_ref[...])
    for j in range(N_VT):
        v_t = v_all[:, j * LANES:(j + 1) * LANES].T.astype(_BF16)
        for c in range(TM_PROJ // TK):
            vt_ref[0, j, c] = v_t[:, c * TK:(c + 1) * TK]
    kv_cmp = _dot(h, wc_ref[...]).astype(_BF16)
    for s in range(N_CMP_SLABS):
        pc_ref[0, s] = kv_cmp[:, s * HEAD_DIM:(s + 1) * HEAD_DIM]
    chunk = 4 * LANES
    for c in range(0, PB_BLOCKS * LANES, chunk):
        w = min(chunk, PB_BLOCKS * LANES - c)
        res = _dot(h, wb_ref[:, c:c + w]).astype(_BF16)
        for j in range(w // LANES):
            pb_ref[0, c // LANES + j] = res[:, j * LANES:(j + 1) * LANES]
    for c in range(0, PF_BLOCKS * LANES, chunk):
        w = min(chunk, PF_BLOCKS * LANES - c)
        res = _dot(h, wf_ref[:, c:c + w])
        for j in range(w // LANES):
            pf_ref[0, c // LANES + j] = res[:, j * LANES:(j + 1) * LANES]


def _proj(x2, g, wb, wf, wc, wvt, t):
    n = x2.shape[0]
    per_seq = t // TM_PROJ
    return pl.pallas_call(
        _proj_kernel,
        grid=(n // TM_PROJ,),
        in_specs=[pl.BlockSpec((TM_PROJ, D_MODEL), lambda i: (i, 0)),
                  pl.BlockSpec((1, D_MODEL), lambda i: (0, 0)),
                  pl.BlockSpec((D_MODEL, PB_BLOCKS * LANES), lambda i: (0, 0)),
                  pl.BlockSpec((D_MODEL, PF_BLOCKS * LANES), lambda i: (0, 0)),
                  pl.BlockSpec((D_MODEL, N_CMP_SLABS * HEAD_DIM), lambda i: (0, 0)),
                  pl.BlockSpec((D_MODEL, N_VT * LANES), lambda i: (0, 0))],
        out_specs=[pl.BlockSpec((1, PB_BLOCKS, TM_PROJ, LANES), lambda i: (i // per_seq, 0, i % per_seq, 0)),
                   pl.BlockSpec((1, PF_BLOCKS, TM_PROJ, LANES), lambda i: (i // per_seq, 0, i % per_seq, 0)),
                   pl.BlockSpec((1, N_CMP_SLABS, TM_PROJ, HEAD_DIM),
                                lambda i: (i // per_seq, 0, i % per_seq, 0)),
                   pl.BlockSpec((1, N_VT, TM_PROJ // TK, LANES, TK),
                                lambda i: (i // per_seq, 0, i % per_seq, 0, 0))],
        out_shape=[jax.ShapeDtypeStruct((n // t, PB_BLOCKS, t, LANES), _BF16),
                   jax.ShapeDtypeStruct((n // t, PF_BLOCKS, t, LANES), _F32),
                   jax.ShapeDtypeStruct((n // t, N_CMP_SLABS, t, HEAD_DIM), _BF16),
                   jax.ShapeDtypeStruct((n // t, N_VT, t // TK, LANES, TK), _BF16)],
        compiler_params=_cparams(("arbitrary",)),
        name="rmsnorm_in_proj",
    )(x2, g, wb, wf, wc, wvt)


N_SPLIT = 3


def _fgate_tables():
    n_pairs = N_HEADS_FOX // 2
    pq = np.zeros((N_SPLIT * LANES, n_pairs * LANES), np.float32)
    pk = np.zeros_like(pq)
    ones_q = np.zeros((1, n_pairs * LANES), np.float32)
    ones_k = np.zeros_like(ones_q)
    for head in range(N_HEADS_FOX):
        pair, slot = divmod(head, 2)
        base = pair * LANES + (HEAD_DIM if slot == 0 else 0)
        for j in range(N_SPLIT):
            pq[j * LANES + head, base + j] = 1.0
            pk[j * LANES + head, base + N_SPLIT + j] = 1.0
        ones_q[0, base + N_SPLIT:base + 2 * N_SPLIT] = 1.0
        ones_k[0, base:base + N_SPLIT] = 1.0
    return (jnp.asarray(pq, _BF16), jnp.asarray(pk, _BF16),
            jnp.asarray(ones_q), jnp.asarray(ones_k))


def _fgate_kernel(fc_ref, fb_ref, pq_ref, pk_ref, oq_ref, ok_ref, augq_ref, augk_ref):
    t = fc_ref.shape[2]
    z = fc_ref[0, 0] + fb_ref[...]
    logf = jnp.minimum(z, 0.0) - jnp.log1p(jnp.exp(-jnp.abs(z)))
    row = lax.broadcasted_iota(jnp.int32, (t, LANES), 0)
    c = logf
    shift = 1
    while shift < t:
        c = c + jnp.where(row >= shift, pltpu.roll(c, shift, axis=0), 0.0)
        shift *= 2
    c3 = _split3(c * LOG2E)
    aq = _dot(c3, pq_ref[...]) + oq_ref[...]
    ak = ok_ref[...] - _dot(c3, pk_ref[...])
    for p in range(N_HEADS_FOX // 2):
        augq_ref[0, p] = aq[:, p * LANES:(p + 1) * LANES].astype(_BF16)
        augk_ref[0, p] = ak[:, p * LANES:(p + 1) * LANES].astype(_BF16)


def _fgate(pf3, fb_row):
    b, _, t, _ = pf3.shape
    n_pairs = N_HEADS_FOX // 2
    tables = _fgate_tables()
    full = lambda a: pl.BlockSpec(a.shape, lambda i: (0,) * a.ndim)
    aug = lambda: pl.BlockSpec((1, n_pairs, t, LANES), lambda i: (i, 0, 0, 0))
    return pl.pallas_call(
        _fgate_kernel,
        grid=(b,),
        in_specs=[pl.BlockSpec((1, 1, t, LANES), lambda i: (i, PF_FC, 0, 0)),
                  pl.BlockSpec((1, LANES), lambda i: (0, 0))] + [full(a) for a in tables],
        out_specs=[aug(), aug()],
        out_shape=[jax.ShapeDtypeStruct((b, n_pairs, t, LANES), _BF16),
                   jax.ShapeDtypeStruct((b, n_pairs, t, LANES), _BF16)],
        compiler_params=_cparams(("arbitrary",)),
        name="forget_gate_cumsum",
    )(pf3, fb_row, *tables)


def _compress_kernel(x_ref, w1_ref, pe_ref, b1_ref, w2_ref, o_ref, ot_ref):
    nc = x_ref.shape[3]
    half = CMP_STRIDE * HEAD_DIM
    w1 = w1_ref[0]
    c1 = _dot(jnp.broadcast_to(pe_ref[0], (SUBLANES, 2 * half)), w1)[0:1] + b1_ref[0]
    out = jnp.zeros((nc, LANES), _F32)
    for g in range(NSA_KV_GROUPS):
        xg = x_ref[0, 0, g]
        a = _dot(xg, w1[:half])
        bb = _dot(xg, w1[half:])
        h = a + pltpu.roll(bb, nc - 1, axis=0) + c1
        out = out + _dot(_silu(h).astype(_BF16), w2_ref[0, g])
    o_ref[0, 0] = out.astype(_BF16)
    ot_ref[0, 0] = out.T.astype(_BF16)


def _compress(halves, w1, pe, b1, w2p):
    b, _, g, nc, width = halves.shape
    return pl.pallas_call(
        _compress_kernel,
        grid=(b, 2),
        in_specs=[pl.BlockSpec((1, 1, g, nc, width), lambda i, k: (i, k, 0, 0, 0)),
                  pl.BlockSpec((1, 2 * width, CMP_HIDDEN), lambda i, k: (k, 0, 0)),
                  pl.BlockSpec((1, 1, 2 * width), lambda i, k: (k, 0, 0)),
                  pl.BlockSpec((1, 1, CMP_HIDDEN), lambda i, k: (k, 0, 0)),
                  pl.BlockSpec((1, g, CMP_HIDDEN, LANES), lambda i, k: (k, 0, 0, 0))],
        out_specs=[pl.BlockSpec((1, 1, nc, LANES), lambda i, k: (i, k, 0, 0)),
                   pl.BlockSpec((1, 1, LANES, nc), lambda i, k: (i, k, 0, 0))],
        out_shape=[jax.ShapeDtypeStruct((b, 2, nc, LANES), _BF16),
                   jax.ShapeDtypeStruct((b, 2, LANES, nc), _BF16)],
        compiler_params=_cparams(("arbitrary", "arbitrary")),
        name="nsa_compress",
    )(halves, w1, pe, b1, w2p)


def _sb_kernel(q_ref, k_ref, v_ref, z_ref, o_ref, qh_ref, c_ref, acc_ref):
    qi = pl.program_id(1)
    tq = q_ref.shape[2]
    n_pairs = q_ref.shape[1]
    lane = lax.broadcasted_iota(jnp.int32, (tq, LANES), 1)
    r_i = lax.broadcasted_iota(jnp.int32, (2 * TK, 2 * TK), 0)
    c_i = lax.broadcasted_iota(jnp.int32, (2 * TK, 2 * TK), 1)
    uu = jnp.where(r_i >= c_i, 1.0, 0.0).astype(_BF16)
    c_ref[...] = jnp.zeros_like(c_ref)
    acc_ref[...] = jnp.zeros_like(acc_ref)
    for p in range(n_pairs):
        q2 = q_ref[0, p]
        qh_ref[2 * p] = jnp.where(lane < HEAD_DIM, q2, jnp.zeros_like(q2))
        qh_ref[2 * p + 1] = jnp.where(lane >= HEAD_DIM, q2, jnp.zeros_like(q2))
    row = lax.broadcasted_iota(jnp.int32, (tq, LANES), 0)
    n_blocks = tq // TK

    def chunk(start, diagonal):
        for p in range(n_pairs):
            head_pair(p, k_ref[0, p, pl.ds(start, tq), :], v_ref[0, p, pl.ds(start, tq), :], diagonal)

    def head_pair(p, k2, v2, diagonal):
        for h in range(2 * p, 2 * p + 2):
            s = _dot_nt(qh_ref[h], k2)
            carry = c_ref[h]
            w_blocks = [None] * n_blocks
            for c in reversed(range(0, n_blocks, 2)):
                nz, l1m, mask = [], [], []
                for cc in (c, c + 1):
                    nzc = s[:, cc * TK:(cc + 1) * TK]
                    neg_abs = lax.bitcast_convert_type(
                        lax.bitcast_convert_type(nzc, jnp.uint32) | jnp.uint32(0x80000000), _F32)
                    lc = jnp.minimum(nzc, 0.0) - jnp.log(1.0 + jnp.exp(neg_abs))
                    if diagonal:
                        mask.append(lane + cc * TK < row)
                        lc = jnp.where(mask[-1], lc, 0.0)
                    nz.append(nzc)
                    l1m.append(lc.astype(_BF16))
                rc = _dot(jnp.concatenate(l1m, axis=1), uu)
                for i, cc in enumerate((c, c + 1)):
                    w = jnp.exp((rc[:, i * TK:(i + 1) * TK] + carry) - nz[i])
                    if diagonal:
                        w = jnp.where(mask[i], w, 0.0)
                    w_blocks[cc] = w.astype(_BF16)
                carry = carry + jnp.broadcast_to(rc[:, 0:1], carry.shape)
            acc_ref[h] += _dot(jnp.concatenate(w_blocks, axis=1), v2)
            c_ref[h] = carry

    chunk(pl.multiple_of(qi * tq, tq), True)

    def far_pair(it, carry):
        chunk(pl.multiple_of((qi - 1 - 2 * it) * tq, tq), False)
        chunk(pl.multiple_of((qi - 2 - 2 * it) * tq, tq), False)
        return carry

    lax.fori_loop(0, qi >> 1, far_pair, 0)

    @pl.when((qi & 1) == 1)
    def _():
        chunk(0, False)
    for p in range(n_pairs):
        o = jnp.where(lane < HEAD_DIM, acc_ref[2 * p], acc_ref[2 * p + 1])
        o_ref[0, p] = (o * _silu(z_ref[0, p])).astype(o_ref.dtype)


def _sb_attention(pb3, pf3):
    b, _, t, _ = pb3.shape
    n_pairs = N_HEADS_SB // 2
    assert PB_QA % n_pairs == 0 and PB_KA % n_pairs == 0 and PB_VA % n_pairs == 0 and PF_ZA % n_pairs == 0
    return pl.pallas_call(
        _sb_kernel,
        grid=(b, t // TQ_SB),
        in_specs=[pl.BlockSpec((1, n_pairs, TQ_SB, LANES), lambda i, q: (i, PB_QA // n_pairs, q, 0)),
                  pl.BlockSpec((1, n_pairs, t, LANES), lambda i, q: (i, PB_KA // n_pairs, 0, 0)),
                  pl.BlockSpec((1, n_pairs, t, LANES), lambda i, q: (i, PB_VA // n_pairs, 0, 0)),
                  pl.BlockSpec((1, n_pairs, TQ_SB, LANES), lambda i, q: (i, PF_ZA // n_pairs, q, 0))],
        out_specs=pl.BlockSpec((1, n_pairs, TQ_SB, LANES), lambda i, q: (i, 0, q, 0)),
        out_shape=jax.ShapeDtypeStruct((b, n_pairs, t, LANES), _BF16),
        scratch_shapes=[pltpu.VMEM((N_HEADS_SB, TQ_SB, LANES), _BF16),
                        pltpu.VMEM((N_HEADS_SB, TQ_SB, LANES), _F32),
                        pltpu.VMEM((N_HEADS_SB, TQ_SB, LANES), _F32)],
        compiler_params=_cparams(("arbitrary", "arbitrary")),
        name="stick_breaking_attention",
    )(pb3, pb3, pb3, pf3)


def _fox_kernel(q_ref, k_ref, vt_ref, augq_ref, augk_ref, z_ref, o_ref,
                qh_ref, m_ref, acc_ref, s_ref):
    qi = pl.program_id(1)
    tq = q_ref.shape[2]
    tk = tq
    n_hp = q_ref.shape[1]
    lane = lax.broadcasted_iota(jnp.int32, (tq, LANES), 1)
    keep = [lane < HEAD_DIM, lane >= HEAD_DIM]
    for p in range(n_hp):
        for h in range(2):
            qh_ref[2 * p + h] = jnp.where(keep[h], q_ref[0, p], augq_ref[0, p])
    m_ref[...] = jnp.full_like(m_ref, NEG)
    acc_ref[...] = jnp.zeros_like(acc_ref)
    key_i = lax.broadcasted_iota(jnp.int32, (TK, tq), 0)
    qry_i = lax.broadcasted_iota(jnp.int32, (TK, tq), 1)

    def scores(c, slot):
        start = c * tk if isinstance(c, int) else pl.multiple_of(c * tk, tk)
        for p in range(n_hp):
            k2 = k_ref[0, p, pl.ds(start, tk), :]
            ak = augk_ref[0, p, pl.ds(start, tk), :]
            for h in range(2):
                s_ref[slot, 2 * p + h] = _dot_nt(jnp.where(keep[h], k2, ak), qh_ref[2 * p + h])

    ones_rows = jnp.ones((SUBLANES, tk), _BF16)

    def consume(c, slot, diagonal):
        per = tk // TK
        for p in range(n_hp):
            vt = jnp.concatenate([vt_ref[0, p, c * per + j] for j in range(per)], axis=1)
            for hh in range(2):
                consume_head(2 * p + hh, vt[hh * HEAD_DIM:(hh + 1) * HEAD_DIM], slot, diagonal)

    def consume_head(h, vt_h, slot, diagonal):
        per = tk // TK
        vth = jnp.concatenate([vt_h, ones_rows], axis=0)
        for part in range(FOX_PARTS):
            js = range(part * per // FOX_PARTS, (part + 1) * per // FOX_PARTS)
            blocks = [s_ref[slot, h, j * TK:(j + 1) * TK, :] for j in js]
            if diagonal:
                blocks = [jnp.where(key_i + j * TK <= qry_i, blk, NEG) for j, blk in zip(js, blocks)]
            mx = blocks[0]
            for blk in blocks[1:]:
                mx = jnp.maximum(mx, blk)
            m_old = m_ref[h]
            m_new = jnp.maximum(m_old, jnp.max(mx, axis=0, keepdims=True))
            m_row = m_new[0:1]
            pt = jnp.concatenate([jnp.exp2(blk - m_row).astype(_BF16) for blk in blocks], axis=0)
            acc_ref[h] = (jnp.exp2(m_old - m_new)[0:1] * acc_ref[h]
                          + _dot(vth[:, js[0] * TK:(js[-1] + 1) * TK], pt))
            m_ref[h] = m_new

    scores(0, 0)
    n_pairs = qi >> 1

    def far_pair(i, carry):
        c = 2 * i
        scores(c + 1, 1)
        consume(c, 0, False)
        scores(c + 2, 0)
        consume(c + 1, 1, False)
        return carry

    lax.fori_loop(0, n_pairs, far_pair, 0)

    @pl.when(qi == 2 * n_pairs)
    def _():
        consume(qi, 0, True)

    @pl.when(qi != 2 * n_pairs)
    def _():
        scores(qi, 1)
        consume(qi - 1, 0, False)
        consume(qi, 1, True)
    for p in range(n_hp):
        o_t = jnp.concatenate([acc_ref[2 * p + h, :HEAD_DIM] / acc_ref[2 * p + h, HEAD_DIM:HEAD_DIM + 1]
                               for h in range(2)], axis=0)
        o_ref[0, p] = (o_t.T * _silu(z_ref[0, p])).astype(o_ref.dtype)


def _fox_attention(pb3, pf3, vt, augq, augk):
    b, _, t, _ = pb3.shape
    n_pairs = N_HEADS_FOX // 2
    assert PB_QC % n_pairs == 0 and PB_KF % n_pairs == 0 and VT_VF % n_pairs == 0 and PF_ZC % n_pairs == 0
    return pl.pallas_call(
        _fox_kernel,
        grid=(b, t // TQ_FOX),
        in_specs=[pl.BlockSpec((1, n_pairs, TQ_FOX, LANES), lambda i, q: (i, PB_QC // n_pairs, q, 0)),
                  pl.BlockSpec((1, n_pairs, t, LANES), lambda i, q: (i, PB_KF // n_pairs, 0, 0)),
                  pl.BlockSpec((1, n_pairs, t // TK, LANES, TK), lambda i, q: (i, VT_VF // n_pairs, 0, 0, 0)),
                  pl.BlockSpec((1, n_pairs, TQ_FOX, LANES), lambda i, q: (i, 0, q, 0)),
                  pl.BlockSpec((1, n_pairs, t, LANES), lambda i, q: (i, 0, 0, 0)),
                  pl.BlockSpec((1, n_pairs, TQ_FOX, LANES), lambda i, q: (i, PF_ZC // n_pairs, q, 0))],
        out_specs=pl.BlockSpec((1, n_pairs, TQ_FOX, LANES), lambda i, q: (i, 0, q, 0)),
        out_shape=jax.ShapeDtypeStruct((b, n_pairs, t, LANES), _BF16),
        scratch_shapes=[pltpu.VMEM((N_HEADS_FOX, TQ_FOX, LANES), _BF16),
                        pltpu.VMEM((N_HEADS_FOX, SUBLANES, TQ_FOX), _F32),
                        pltpu.VMEM((N_HEADS_FOX, HEAD_DIM + SUBLANES, TQ_FOX), _F32),
                        pltpu.VMEM((2, N_HEADS_FOX, TQ_FOX, TQ_FOX), _F32)],
        compiler_params=_cparams(("arbitrary", "arbitrary")),
        name="forgetting_attention",
    )(pb3, pb3, vt, augq, augk, pf3)


def _nsa_kernel(q_ref, kc_ref, vct_ref, ks_ref, vts_ref, kw_ref, vtw_ref, gl_ref, z_ref,
                bc_ref, bd_ref, ov_ref, o_ref,
                m_ref, acc_ref, osum_ref, qz_ref, qsel_ref, gates_ref, psum_ref, s_ref, ws_ref):
    qi = pl.program_id(1)
    tq = q_ref.shape[2]
    hpg = NSA_HPG
    rows = hpg * tq
    n_cmp = kc_ref.shape[2]
    lane_r = lax.broadcasted_iota(jnp.int32, (rows, LANES), 1)
    half = [lane_r < HEAD_DIM, lane_r >= HEAD_DIM]

    gates_ref[...] = _sigmoid(gl_ref[0, 0]).T

    def gated(c, h, per_group):
        parts = []
        for g, o in enumerate(per_group):
            r = 3 * (g * hpg + h) + c
            parts.append(gates_ref[r:r + 1, :] * o[:, h * tq:(h + 1) * tq])
        return jnp.concatenate(parts, axis=0)

    q4 = jnp.concatenate([q_ref[0, h] for h in range(hpg)], axis=0)
    for g in range(NSA_KV_GROUPS):
        qz_ref[g] = jnp.where(half[g], q4, jnp.zeros_like(q4))

    def scores(q_src, k_ref, start, tk, onehot):
        k2 = k_ref[0, 0, pl.ds(start, tk), :]
        lane_k = lax.broadcasted_iota(jnp.int32, (tk, LANES), 1)
        key_blk = (start + lax.broadcasted_iota(jnp.int32, (tk, LANES), 0)) >> int(math.log2(SLC_BLOCK))
        oh = jnp.where((lane_k & (SLC_BLOCK - 1)) == key_blk, 1.0, 0.0).astype(_BF16)
        out = []
        for g in range(NSA_KV_GROUPS):
            keep = (lane_k < HEAD_DIM) if g == 0 else (lane_k >= HEAD_DIM)
            out.append(_dot_nt(jnp.where(keep, k2, oh) if onehot else k2, q_src[g]))
        return out


    kc = kc_ref[0, 0]
    vct = vct_ref[0, 0]
    cmp_row0 = pl.multiple_of(n_cmp - qi * (tq // CMP_STRIDE), SUBLANES)
    o_cmp_t = []
    raw = [_dot_nt(kc, qz_ref[g]) for g in range(NSA_KV_GROUPS)]
    for g in range(NSA_KV_GROUPS):
        bias = jnp.concatenate([bc_ref[g * hpg + h, pl.ds(cmp_row0, n_cmp), :]
                                for h in range(hpg)], axis=1)
        sc = raw[g] + bias
        mx = jnp.max(sc, axis=0, keepdims=True)
        e = jnp.exp2(sc - mx)
        den = jnp.sum(e, axis=0, keepdims=True)
        pc = e * jnp.where(mx > 0.5 * NEG, 1.0 / den, 0.0)
        psum = pc[:, 0:tq]
        for h in range(1, hpg):
            psum = psum + pc[:, h * tq:(h + 1) * tq]
        psum_ref[g] = psum
        o_cmp_t.append(_dot(vct[g * HEAD_DIM:(g + 1) * HEAD_DIM], pc.astype(_BF16)))
    for h in range(hpg):
        osum_ref[h] = gated(0, h, o_cmp_t)

    rank_from = SLC_TOP * SLC_BLOCK // tq

    @pl.when(qi < rank_from)
    def _():
        for g in range(NSA_KV_GROUPS):
            qsel_ref[g] = qz_ref[g]

    @pl.when(qi >= rank_from)
    def _():
        n_sel = LANES // NSA_KV_GROUPS
        n_grp = n_sel // SUBLANES
        blk = lax.broadcasted_iota(jnp.int32, (n_sel, tq), 0)
        cur = (qi * tq + lax.broadcasted_iota(jnp.int32, (n_sel, tq), 1)) >> int(math.log2(SLC_BLOCK))
        forced = (blk == 0) | (blk == cur) | (blk == cur - 1)
        sub = lax.broadcasted_iota(jnp.int32, (SUBLANES, tq), 0)
        neg_t = []
        for g in range(NSA_KV_GROUPS):
            p = psum_ref[g]
            p1 = p.astype(_BF16)
            r1 = p - p1.astype(_F32)
            p2 = r1.astype(_BF16)
            p3 = (r1 - p2.astype(_F32)).astype(_BF16)
            a = _dot(ov_ref[...], jnp.concatenate([p1, p2, p3], axis=0))
            a = jnp.where(forced, FORCE_SCORE, a)
            a = jnp.where(blk > cur, -FORCE_SCORE, a)
            a_grp = [a[r * SUBLANES:(r + 1) * SUBLANES] for r in range(n_grp)]
            cnt = [jnp.zeros((SUBLANES, tq), _F32) for _ in range(n_grp)]
            for j in range(n_sel):
                rj = jnp.broadcast_to(a[j:j + 1], (SUBLANES, tq))
                jr, jo = divmod(j, SUBLANES)
                for r in range(n_grp):
                    if r > jr:
                        one = jnp.where(rj >= a_grp[r], 1.0, 0.0)
                    elif r < jr:
                        one = jnp.where(rj > a_grp[r], 1.0, 0.0)
                    else:
                        tie = jnp.where(sub > jo, jnp.where(rj == a_grp[r], 1.0, 0.0), 0.0)
                        one = jnp.where(rj > a_grp[r], 1.0, tie)
                    cnt[r] = cnt[r] + one
            neg_t.append(jnp.where(jnp.concatenate(cnt, axis=0) < float(SLC_TOP), 0.0, NEG))
        selneg = jnp.concatenate(neg_t[::-1], axis=0).T.astype(_BF16)
        selneg4 = jnp.concatenate([selneg] * hpg, axis=0)
        for g in range(NSA_KV_GROUPS):
            qsel_ref[g] = jnp.where(half[g], q4, selneg4)

    SEL, WIN = 0, 1

    def reset(st):
        m_ref[st] = jnp.full(m_ref.shape[1:], 2.0 * NEG, _F32)
        acc_ref[st] = jnp.zeros(acc_ref.shape[1:], _F32)

    def chunk(q_src, k_ref, vt_ref, start, tk, onehot, extras, st):
        s = scores(q_src, k_ref, start, tk, onehot)
        consume(lambda g, c: s[g][c * TK:(c + 1) * TK], vt_ref, start, tk, extras, st)

    def consume(block_of, vt_ref, start, tk, extras, st, limit=None):
        first = start // TK if isinstance(start, int) else start >> int(math.log2(TK))
        vt = jnp.concatenate([vt_ref[0, 0, first + j] for j in range(tk // TK)], axis=1)
        ones_rows = jnp.ones((SUBLANES, tk), _BF16)
        key_pos = start + lax.broadcasted_iota(jnp.int32, (HEAD_DIM + SUBLANES, tk), 1)
        for g in range(NSA_KV_GROUPS):
            vth = jnp.concatenate([vt[g * HEAD_DIM:(g + 1) * HEAD_DIM], ones_rows], axis=0)
            if limit is not None:
                vth = jnp.where(key_pos < limit, vth, jnp.zeros_like(vth))
            n_blk = tk // TK
            n_parts = NSA_PARTS if n_blk >= 2 * NSA_PARTS else 1
            for part in range(n_parts):
                cs = range(part * n_blk // n_parts, (part + 1) * n_blk // n_parts)
                blocks = [block_of(g, c) + extras[c](g) if c in extras else block_of(g, c) for c in cs]
                mx = blocks[0]
                for blk_s in blocks[1:]:
                    mx = jnp.maximum(mx, blk_s)
                m_old = m_ref[st, g]
                m_new = jnp.maximum(m_old, jnp.max(mx, axis=0, keepdims=True))
                m_row = m_new[0:1]
                pt = jnp.concatenate([jnp.exp2(blk_s - m_row).astype(_BF16) for blk_s in blocks], axis=0)
                acc_ref[st, g] = (jnp.exp2(m_old - m_new)[0:1] * acc_ref[st, g]
                                  + _dot(vth[:, cs[0] * TK:(cs[-1] + 1) * TK], pt))
                m_ref[st, g] = m_new

    def finish(c, st):
        outs = []
        for g in range(NSA_KV_GROUPS):
            den = acc_ref[st, g, HEAD_DIM:HEAD_DIM + 1]
            outs.append(acc_ref[st, g, :HEAD_DIM] * jnp.where(den > 0.0, 1.0 / den, 0.0))
        for h in range(hpg):
            osum_ref[h] += gated(c, h, outs)

    def near_bias(d):
        return lambda g: bd_ref[g, d]

    def short_path(q_src, k_ref, v_ref, onehot, st):
        def far(kt, carry):
            chunk(q_src, k_ref, v_ref, pl.multiple_of(kt * TK, TK), TK, onehot, {}, st)
            return carry

        lax.fori_loop(0, jnp.maximum(qi - 1, 0), far, 0)

        @pl.when(qi >= 1)
        def _():
            chunk(q_src, k_ref, v_ref, pl.multiple_of((qi - 1) * TK, TK), TK, onehot,
                  {0: near_bias(1)}, st)

        chunk(q_src, k_ref, v_ref, pl.multiple_of(qi * TK, TK), TK, onehot, {0: near_bias(0)}, st)

    reset(SEL)
    reset(WIN)
    big = 4 * TK
    n_win = WINDOW // TK
    main_from = max(big // TK - 1, n_win)
    win_mask = jnp.where(lax.broadcasted_iota(jnp.int32, (TK, rows), 0)
                         > (lax.broadcasted_iota(jnp.int32, (TK, rows), 1) & (tq - 1)), 0.0, NEG)

    @pl.when(qi < main_from)
    def _():
        short_path(qsel_ref, ks_ref, vts_ref, True, SEL)
        short_path(qz_ref, kw_ref, vtw_ref, False, WIN)

    @pl.when(qi >= main_from)
    def _():
        last = pl.multiple_of((qi - (big // TK - 1)) * TK, TK)
        n_far = (qi - (big // TK - 1) + big // TK - 1) >> int(math.log2(big // TK))
        near = {big // TK - 2: near_bias(1), big // TK - 1: near_bias(0)}

        def start_of(k):
            return pl.multiple_of(jnp.where(k < n_far, k * big, last), TK)

        def put(k, slot):
            for g, s in enumerate(scores(qsel_ref, ks_ref, start_of(k), big, True)):
                s_ref[slot, g] = s

        def take(k, slot, extras, limit):
            consume(lambda g, c: s_ref[slot, g, c * TK:(c + 1) * TK, :], vts_ref,
                    start_of(k), big, extras, SEL, limit)

        win_start = pl.multiple_of((qi - n_win) * TK, TK)
        for g, s in enumerate(scores(qz_ref, kw_ref, win_start, WINDOW + TK, False)):
            ws_ref[g] = s
        put(0, 0)
        consume(lambda g, c: ws_ref[g, c * TK:(c + 1) * TK, :], vtw_ref, win_start, WINDOW + TK,
                {0: lambda g: win_mask, n_win - 1: near_bias(1), n_win: near_bias(0)}, WIN)
        n_pair = n_far >> 1

        def far_pair(i, carry):
            k = 2 * i
            put(k + 1, 1)
            take(k, 0, {}, last)
            put(k + 2, 0)
            take(k + 1, 1, {}, last)
            return carry

        lax.fori_loop(0, n_pair, far_pair, 0)

        @pl.when(n_far == 2 * n_pair)
        def _():
            take(n_far, 0, near, None)

        @pl.when(n_far != 2 * n_pair)
        def _():
            put(n_far, 1)
            take(n_far - 1, 0, {}, last)
            take(n_far, 1, near, None)

    finish(1, SEL)
    finish(2, WIN)

    for h in range(hpg):
        o_ref[0, h] = (osum_ref[h].T * _silu(z_ref[0, h])).astype(o_ref.dtype)


def _nsa_attention(pb3, pf3, vt, kvc, kvc_t, bias_c, bias_d, ov):
    b, _, t, _ = pb3.shape
    n_cmp = kvc.shape[2]
    n_blk = N_HEADS_NSA // 2
    full = lambda shape: pl.BlockSpec(shape, lambda i, q: (0,) * len(shape))
    kv_spec = lambda col: pl.BlockSpec((1, 1, t, LANES), lambda i, q: (i, col, 0, 0))
    vt_spec = lambda blk: pl.BlockSpec((1, 1, t // TK, LANES, TK), lambda i, q: (i, blk, 0, 0, 0))
    return pl.pallas_call(
        _nsa_kernel,
        grid=(b, t // TQ_NSA),
        in_specs=[pl.BlockSpec((1, n_blk, TQ_NSA, LANES), lambda i, q: (i, PB_QB // n_blk, q, 0)),
                  pl.BlockSpec((1, 1, n_cmp, LANES), lambda i, q: (i, 0, 0, 0)),
                  pl.BlockSpec((1, 1, LANES, n_cmp), lambda i, q: (i, 1, 0, 0)),
                  kv_spec(PB_KS), vt_spec(VT_VS), kv_spec(PB_KW), vt_spec(VT_VW),
                  pl.BlockSpec((1, 1, TQ_NSA, LANES), lambda i, q: (i, PF_GB, q, 0)),
                  pl.BlockSpec((1, n_blk, TQ_NSA, LANES), lambda i, q: (i, PF_ZB // n_blk, q, 0)),
                  full(bias_c.shape),
                  full(bias_d.shape), full(ov.shape)],
        out_specs=pl.BlockSpec((1, n_blk, TQ_NSA, LANES), lambda i, q: (i, 0, q, 0)),
        out_shape=jax.ShapeDtypeStruct((b, n_blk, t, LANES), _BF16),
        scratch_shapes=[pltpu.VMEM((2, NSA_KV_GROUPS, SUBLANES, NSA_HPG * TQ_NSA), _F32),
                        pltpu.VMEM((2, NSA_KV_GROUPS, HEAD_DIM + SUBLANES, NSA_HPG * TQ_NSA), _F32),
                        pltpu.VMEM((n_blk, TQ_NSA, LANES), _F32),
                        pltpu.VMEM((NSA_KV_GROUPS, NSA_HPG * TQ_NSA, LANES), _BF16),
                        pltpu.VMEM((NSA_KV_GROUPS, NSA_HPG * TQ_NSA, LANES), _BF16),
                        pltpu.VMEM((LANES, TQ_NSA), _F32),
                        pltpu.VMEM((NSA_KV_GROUPS, n_cmp, TQ_NSA), _F32),
                        pltpu.VMEM((2, NSA_KV_GROUPS, 4 * TK, NSA_HPG * TQ_NSA), _F32),
                        pltpu.VMEM((NSA_KV_GROUPS, WINDOW + TK, NSA_HPG * TQ_NSA), _F32)],
        compiler_params=_cparams(("arbitrary", "arbitrary")),
        name="native_sparse_attention",
    )(pb3, kvc, kvc_t, pb3, vt, pb3, vt, pf3, pf3, bias_c, bias_d, ov)


def _out_kernel(x_ref, oa_ref, ob_ref, oc_ref, w_ref, g_ref, o_ref, *, final_norm):
    mixed = jnp.concatenate([o_ref_in[0, j] for o_ref_in in (oa_ref, ob_ref, oc_ref)
                             for j in range(o_ref_in.shape[1])], axis=1)
    x = x_ref[...] + _dot(mixed, w_ref[...])
    if final_norm:
        x = x * lax.rsqrt(jnp.mean(x * x, axis=-1, keepdims=True) + RMS_EPS) * g_ref[...]
    o_ref[...] = x


def _out_proj(x2, oa, ob, oc, w, g, final_norm):
    n = x2.shape[0]
    per_seq = oa.shape[2] // TM_PROJ
    row = lambda width: pl.BlockSpec((TM_PROJ, width), lambda i: (i, 0))
    blocks = lambda a: pl.BlockSpec((1, a.shape[1], TM_PROJ, LANES),
                                    lambda i: (i // per_seq, 0, i % per_seq, 0))
    return pl.pallas_call(
        functools.partial(_out_kernel, final_norm=final_norm),
        grid=(n // TM_PROJ,),
        in_specs=[row(D_MODEL), blocks(oa), blocks(ob), blocks(oc),
                  pl.BlockSpec((D_MODEL, D_MODEL), lambda i: (0, 0)),
                  pl.BlockSpec((1, D_MODEL), lambda i: (0, 0))],
        out_specs=row(D_MODEL),
        out_shape=jax.ShapeDtypeStruct((n, D_MODEL), _F32),
        compiler_params=_cparams(("arbitrary",)),
        name="out_proj_residual",
    )(x2, oa, ob, oc, w, g)


def _head_perm_cols(width_per_head, order):
    return np.concatenate([np.arange(h * width_per_head, (h + 1) * width_per_head) for h in order])


def _layout_w_in(w):
    widths = [256, 256, 256, 256, 512, 128, 128, 128, 128, 128, 128, 24, 512, 256, 256, 256, 4, 256]
    offs = np.concatenate([[0], np.cumsum(widths)])
    (qa, ka, va, za, qb, kc, vc, ks, vs, kw, vw, gb, zb, qc, kf, vf, fc, zc) = [
        w[:, offs[i]:offs[i + 1]] for i in range(len(widths))]
    scale = HEAD_DIM ** -0.5
    perm = _head_perm_cols(HEAD_DIM, NSA_HEAD_ORDER)
    pad = lambda a: jnp.pad(a, ((0, 0), (0, LANES - a.shape[1])))
    wb = jnp.concatenate([qb[:, perm] * (scale * LOG2E), qa * (-scale), ka, va, ks, kw,
                          qc * (scale * LOG2E), kf], axis=1)
    wf = jnp.concatenate([zb[:, perm], za, zc, pad(gb), pad(fc)], axis=1)
    wc = jnp.concatenate([kc, vc], axis=1)
    wvt = jnp.concatenate([vs, vw, vf], axis=1)
    return wb.astype(_BF16), wf.astype(_BF16), wc.astype(_BF16), wvt.astype(_BF16)


def _static_tables(t):
    tq = TQ_NSA
    n_cmp_pad = t // CMP_STRIDE
    j = np.arange(n_cmp_pad)
    rel = np.arange(2 * n_cmp_pad) - n_cmp_pad
    dist_c = np.arange(tq)[None, :] - (rel[:, None] * CMP_STRIDE + CMP_BLOCK - 1)
    i_, j_ = np.arange(tq)[:, None], np.arange(TK)[None, :]
    dist_d = np.concatenate([d * TK + i_ - j_ for d in range(3)], axis=0)
    n_slc = LANES // NSA_KV_GROUPS
    cmp_start = j * CMP_STRIDE
    cmp_end = cmp_start + CMP_BLOCK - 1
    slc_start = np.arange(n_slc) * SLC_BLOCK
    ov1 = np.clip(np.minimum(cmp_end[:, None], slc_start[None, :] + SLC_BLOCK - 1)
                  - np.maximum(cmp_start[:, None], slc_start[None, :]) + 1, 0, None) / CMP_BLOCK
    ov1[n_cmp_pad - 1:] = 0.0
    ov1[:, t // SLC_BLOCK:] = 0.0
    ov3 = np.concatenate([ov1.T, ov1.T, ov1.T], axis=1)
    return jnp.asarray(dist_c, jnp.int32), jnp.asarray(dist_d, jnp.int32), jnp.asarray(ov3, _BF16)


def kernel(x, norm_g, w_in, w_out, forget_b, cmp_w1, cmp_b1, cmp_w2, cmp_pe, rel_bias, final_g):
    b, t, d = x.shape
    depth = norm_g.shape[0]
    assert d == D_MODEL and t % TM_PROJ == 0 and t % (CMP_STRIDE * LANES) == 0
    assert t // SLC_BLOCK <= LANES // NSA_KV_GROUPS and TQ_NSA == TK
    n_cmp_pad = t // CMP_STRIDE

    dist_c, dist_d, ov3 = _static_tables(t)
    bias_c = _bias_table(rel_bias * LOG2E, dist_c, 32)
    bias_d = _bias_table(rel_bias * LOG2E, dist_d, 32).reshape(N_HEADS_NSA, 3, TQ_NSA, TK)
    bias_d = bias_d[:, :2] - bias_d[:, 2:3]
    bias_d = bias_d.reshape(NSA_KV_GROUPS, NSA_HPG, 2, TQ_NSA, TK).transpose(0, 2, 4, 1, 3).reshape(
        NSA_KV_GROUPS, 2, TK, NSA_HPG * TQ_NSA)

    perm_rows = _head_perm_cols(HEAD_DIM, NSA_HEAD_ORDER)
    x2 = x.reshape(b * t, d)
    for l in range(depth):
        wb, wf, wc, wvt = _layout_w_in(w_in[l])
        pb, pf, pc, vt = _proj(x2, norm_g[l].reshape(1, d), wb, wf, wc, wvt, t)
        pb3, pf3 = pb, pf

        fb_row = jnp.pad(forget_b[l], (0, LANES - N_HEADS_FOX)).reshape(1, LANES)
        augq, augk = _fgate(pf3, fb_row)

        halves = pc.reshape(b, 2, NSA_KV_GROUPS, n_cmp_pad, CMP_STRIDE * HEAD_DIM)
        w2 = cmp_w2[l]
        zeros = jnp.zeros_like(w2)
        w2p = jnp.stack([jnp.concatenate([w2, zeros], axis=-1),
                         jnp.concatenate([zeros, w2], axis=-1)], axis=1).astype(_BF16)
        kvc, kvc_t = _compress(halves, cmp_w1[l].astype(_BF16),
                               cmp_pe[l].reshape(2, 1, CMP_BLOCK * HEAD_DIM).astype(_BF16),
                               cmp_b1[l].reshape(2, 1, CMP_HIDDEN), w2p)

        o_a = _sb_attention(pb3, pf3)
        o_b = _nsa_attention(pb3, pf3, vt, kvc, kvc_t, bias_c, bias_d, ov3)
        o_c = _fox_attention(pb3, pf3, vt, augq, augk)

        wo = w_out[l]
        wo = jnp.concatenate([wo[:N_HEADS_SB * HEAD_DIM],
                              wo[N_HEADS_SB * HEAD_DIM:][:N_HEADS_NSA * HEAD_DIM][perm_rows],
                              wo[(N_HEADS_SB + N_HEADS_NSA) * HEAD_DIM:]], axis=0).astype(_BF16)
        last = l == depth - 1
        x2 = _out_proj(x2, o_a, o_b, o_c, wo, final_g.reshape(1, d), last)
    return x2.reshape(b, t, d)
```

```python
import functools
import math

import jax
import jax.numpy as jnp
import numpy as np
from jax import lax
from jax.experimental import pallas as pl
from jax.experimental.pallas import tpu as pltpu

D_MODEL = 1024
HEAD_DIM = 64
N_HEADS_SB = 4
N_HEADS_FOX = 4
N_HEADS_NSA = 8
NSA_KV_GROUPS = 2
NSA_HPG = N_HEADS_NSA // NSA_KV_GROUPS
CMP_BLOCK = 32
CMP_STRIDE = 16
CMP_HIDDEN = 256
SLC_BLOCK = 64
SLC_TOP = 16
WINDOW = 512
REL_BUCKETS = 32
REL_MAX_DIST = 128
FORCE_SCORE = 1e4
RMS_EPS = 1e-6
NEG = -1e30
LOG2E = math.log2(math.e)

LANES = 128
SUBLANES = 8
VMEM_LIMIT = 56 * 1024 * 1024

TM_PROJ = 512
TQ_SB = 512
TQ_FOX = 512
FOX_PARTS = 2
NSA_PARTS = 2
TQ_NSA = 128
TK = 128

PB_QB, PB_QA, PB_KA, PB_VA = 0, 4, 6, 8
PB_KS, PB_KW = 10, 11
PB_QC, PB_KF = 12, 14
PB_BLOCKS = 16
N_CMP_SLABS = 2 * NSA_KV_GROUPS
VT_VS, VT_VW, VT_VF = 0, 1, 2
N_VT = 4
PF_ZB, PF_ZA, PF_ZC, PF_GB, PF_FC = 0, 4, 6, 8, 8
PF_BLOCKS = 9
FC_LANE0 = 3 * N_HEADS_NSA

NSA_HEAD_ORDER = [0, 4, 1, 5, 2, 6, 3, 7]

_F32 = jnp.float32
_BF16 = jnp.bfloat16


def _cparams(sem):
    return pltpu.CompilerParams(dimension_semantics=sem, vmem_limit_bytes=VMEM_LIMIT)


def _dot(a, b):
    return jnp.dot(a, b, preferred_element_type=_F32)


def _dot_nt(a, b):
    return lax.dot_general(a, b, (((1,), (1,)), ((), ())), preferred_element_type=_F32)


def _split3(x):
    h1 = x.astype(_BF16)
    r1 = x - h1.astype(_F32)
    h2 = r1.astype(_BF16)
    h3 = (r1 - h2.astype(_F32)).astype(_BF16)
    return jnp.concatenate([h1, h2, h3], axis=1)


def _sigmoid(x):
    return 1.0 / (1.0 + jnp.exp(-x))


def _silu(x):
    return x * _sigmoid(x)


def _rel_bucket_np(n):
    n = np.maximum(n, 0)
    max_exact = REL_BUCKETS // 2
    nf = np.maximum(n, 1).astype(np.float64)
    large = max_exact + (np.log(nf / max_exact) / math.log(REL_MAX_DIST / max_exact)
                         * (REL_BUCKETS - max_exact)).astype(np.int64)
    large = np.minimum(large, REL_BUCKETS - 1)
    return np.where(n < max_exact, n, large)


def _bucket_thresholds():
    n = np.arange(0, 4 * REL_MAX_DIST)
    bk = _rel_bucket_np(n)
    assert np.all(np.diff(bk) >= 0) and bk[-1] == REL_BUCKETS - 1
    return [int(np.argmax(bk >= b)) for b in range(REL_BUCKETS)]


_BUCKET_THR = _bucket_thresholds()


def _bias_kernel(tab_ref, dist_ref, o_ref):
    n = dist_ref[...]
    acc = [jnp.full(n.shape, tab_ref[0, h], _F32) for h in range(N_HEADS_NSA)]
    for b in range(1, REL_BUCKETS):
        ge = n >= _BUCKET_THR[b]
        for h in range(N_HEADS_NSA):
            acc[h] = jnp.where(ge, tab_ref[b, h], acc[h])
    valid = n >= 0
    for h in range(N_HEADS_NSA):
        o_ref[h] = jnp.where(valid, acc[h], NEG)


def _bias_table(rel_bias, dist, rows):
    n_rows, n_cols = dist.shape
    return pl.pallas_call(
        _bias_kernel,
        grid=(n_rows // rows,),
        in_specs=[pl.BlockSpec(memory_space=pltpu.SMEM),
                  pl.BlockSpec((rows, n_cols), lambda i: (i, 0))],
        out_specs=pl.BlockSpec((N_HEADS_NSA, rows, n_cols), lambda i: (0, i, 0)),
        out_shape=jax.ShapeDtypeStruct((N_HEADS_NSA, n_rows, n_cols), _F32),
        compiler_params=_cparams(("arbitrary",)),
        name="rel_bias_table",
    )(rel_bias, dist)


def _proj_kernel(x_ref, g_ref, wb_ref, wf_ref, wc_ref, wvt_ref, pb_ref, pf_ref, pc_ref, vt_ref):
    x = x_ref[...]
    y = x * lax.rsqrt(jnp.mean(x * x, axis=-1, keepdims=True) + RMS_EPS)
    h = (y * g_ref[...]).astype(_BF16)
    v_all = _dot(h, wvt_ref[...])
    for j in range(N_VT):
        v_t = v_all[:, j * LANES:(j + 1) * LANES].T.astype(_BF16)
        for c in range(TM_PROJ // TK):
            vt_ref[0, j, c] = v_t[:, c * TK:(c + 1) * TK]
    kv_cmp = _dot(h, wc_ref[...]).astype(_BF16)
    for s in range(N_CMP_SLABS):
        pc_ref[0, s] = kv_cmp[:, s * HEAD_DIM:(s + 1) * HEAD_DIM]
    chunk = 4 * LANES
    for c in range(0, PB_BLOCKS * LANES, chunk):
        w = min(chunk, PB_BLOCKS * LANES - c)
        res = _dot(h, wb_ref[:, c:c + w]).astype(_BF16)
        for j in range(w // LANES):
            pb_ref[0, c // LANES + j] = res[:, j * LANES:(j + 1) * LANES]
    for c in range(0, PF_BLOCKS * LANES, chunk):
        w = min(chunk, PF_BLOCKS * LANES - c)
        res = _dot(h, wf_ref[:, c:c + w])
        for j in range(w // LANES):
            pf_ref[0, c // LANES + j] = res[:, j * LANES:(j + 1) * LANES]


def _proj(x2, g, wb, wf, wc, wvt, t):
    n = x2.shape[0]
    per_seq = t // TM_PROJ
    return pl.pallas_call(
        _proj_kernel,
        grid=(n // TM_PROJ,),
        in_specs=[pl.BlockSpec((TM_PROJ, D_MODEL), lambda i: (i, 0)),
                  pl.BlockSpec((1, D_MODEL), lambda i: (0, 0)),
                  pl.BlockSpec((D_MODEL, PB_BLOCKS * LANES), lambda i: (0, 0)),
                  pl.BlockSpec((D_MODEL, PF_BLOCKS * LANES), lambda i: (0, 0)),
                  pl.BlockSpec((D_MODEL, N_CMP_SLABS * HEAD_DIM), lambda i: (0, 0)),
                  pl.BlockSpec((D_MODEL, N_VT * LANES), lambda i: (0, 0))],
        out_specs=[pl.BlockSpec((1, PB_BLOCKS, TM_PROJ, LANES), lambda i: (i // per_seq, 0, i % per_seq, 0)),
                   pl.BlockSpec((1, PF_BLOCKS, TM_PROJ, LANES), lambda i: (i // per_seq, 0, i % per_seq, 0)),
                   pl.BlockSpec((1, N_CMP_SLABS, TM_PROJ, HEAD_DIM),
                                lambda i: (i // per_seq, 0, i % per_seq, 0)),
                   pl.BlockSpec((1, N_VT, TM_PROJ // TK, LANES, TK),
                                lambda i: (i // per_seq, 0, i % per_seq, 0, 0))],
        out_shape=[jax.ShapeDtypeStruct((n // t, PB_BLOCKS, t, LANES), _BF16),
                   jax.ShapeDtypeStruct((n // t, PF_BLOCKS, t, LANES), _F32),
                   jax.ShapeDtypeStruct((n // t, N_CMP_SLABS, t, HEAD_DIM), _BF16),
                   jax.ShapeDtypeStruct((n // t, N_VT, t // TK, LANES, TK), _BF16)],
        compiler_params=_cparams(("arbitrary",)),
        name="rmsnorm_in_proj",
    )(x2, g, wb, wf, wc, wvt)


N_SPLIT = 3


def _fgate_tables():
    n_pairs = N_HEADS_FOX // 2
    pq = np.zeros((N_SPLIT * LANES, n_pairs * LANES), np.float32)
    pk = np.zeros_like(pq)
    ones_q = np.zeros((1, n_pairs * LANES), np.float32)
    ones_k = np.zeros_like(ones_q)
    for head in range(N_HEADS_FOX):
        pair, slot = divmod(head, 2)
        base = pair * LANES + (HEAD_DIM if slot == 0 else 0)
        for j in range(N_SPLIT):
            pq[j * LANES + FC_LANE0 + head, base + j] = 1.0
            pk[j * LANES + FC_LANE0 + head, base + N_SPLIT + j] = 1.0
        ones_q[0, base + N_SPLIT:base + 2 * N_SPLIT] = 1.0
        ones_k[0, base:base + N_SPLIT] = 1.0
    return (jnp.asarray(pq, _BF16), jnp.asarray(pk, _BF16),
            jnp.asarray(ones_q), jnp.asarray(ones_k))


def _fgate_kernel(fc_ref, fb_ref, pq_ref, pk_ref, oq_ref, ok_ref, augq_ref, augk_ref):
    t = fc_ref.shape[2]
    z = fc_ref[0, 0] + fb_ref[...]
    logf = jnp.minimum(z, 0.0) - jnp.log1p(jnp.exp(-jnp.abs(z)))
    row = lax.broadcasted_iota(jnp.int32, (t, LANES), 0)
    c = logf
    shift = 1
    while shift < t:
        c = c + jnp.where(row >= shift, pltpu.roll(c, shift, axis=0), 0.0)
        shift *= 2
    c3 = _split3(c * LOG2E)
    aq = _dot(c3, pq_ref[...]) + oq_ref[...]
    ak = ok_ref[...] - _dot(c3, pk_ref[...])
    for p in range(N_HEADS_FOX // 2):
        augq_ref[0, p] = aq[:, p * LANES:(p + 1) * LANES].astype(_BF16)
        augk_ref[0, p] = ak[:, p * LANES:(p + 1) * LANES].astype(_BF16)


def _fgate(pf3, fb_row):
    b, _, t, _ = pf3.shape
    n_pairs = N_HEADS_FOX // 2
    tables = _fgate_tables()
    full = lambda a: pl.BlockSpec(a.shape, lambda i: (0,) * a.ndim)
    aug = lambda: pl.BlockSpec((1, n_pairs, t, LANES), lambda i: (i, 0, 0, 0))
    return pl.pallas_call(
        _fgate_kernel,
        grid=(b,),
        in_specs=[pl.BlockSpec((1, 1, t, LANES), lambda i: (i, PF_FC, 0, 0)),
                  pl.BlockSpec((1, LANES), lambda i: (0, 0))] + [full(a) for a in tables],
        out_specs=[aug(), aug()],
        out_shape=[jax.ShapeDtypeStruct((b, n_pairs, t, LANES), _BF16),
                   jax.ShapeDtypeStruct((b, n_pairs, t, LANES), _BF16)],
        compiler_params=_cparams(("arbitrary",)),
        name="forget_gate_cumsum",
    )(pf3, fb_row, *tables)


def _compress_kernel(x_ref, w1_ref, pe_ref, b1_ref, w2_ref, o_ref, ot_ref):
    nc = x_ref.shape[3]
    half = CMP_STRIDE * HEAD_DIM
    w1 = w1_ref[0]
    c1 = _dot(jnp.broadcast_to(pe_ref[0], (SUBLANES, 2 * half)), w1)[0:1] + b1_ref[0]
    out = jnp.zeros((nc, LANES), _F32)
    for g in range(NSA_KV_GROUPS):
        xg = x_ref[0, 0, g]
        a = _dot(xg, w1[:half])
        bb = _dot(xg, w1[half:])
        h = a + pltpu.roll(bb, nc - 1, axis=0) + c1
        out = out + _dot(_silu(h).astype(_BF16), w2_ref[0, g])
    o_ref[0, 0] = out.astype(_BF16)
    ot_ref[0, 0] = out.T.astype(_BF16)


def _compress(halves, w1, pe, b1, w2p):
    b, _, g, nc, width = halves.shape
    return pl.pallas_call(
        _compress_kernel,
        grid=(b, 2),
        in_specs=[pl.BlockSpec((1, 1, g, nc, width), lambda i, k: (i, k, 0, 0, 0)),
                  pl.BlockSpec((1, 2 * width, CMP_HIDDEN), lambda i, k: (k, 0, 0)),
                  pl.BlockSpec((1, 1, 2 * width), lambda i, k: (k, 0, 0)),
                  pl.BlockSpec((1, 1, CMP_HIDDEN), lambda i, k: (k, 0, 0)),
                  pl.BlockSpec((1, g, CMP_HIDDEN, LANES), lambda i, k: (k, 0, 0, 0))],
        out_specs=[pl.BlockSpec((1, 1, nc, LANES), lambda i, k: (i, k, 0, 0)),
                   pl.BlockSpec((1, 1, LANES, nc), lambda i, k: (i, k, 0, 0))],
        out_shape=[jax.ShapeDtypeStruct((b, 2, nc, LANES), _BF16),
                   jax.ShapeDtypeStruct((b, 2, LANES, nc), _BF16)],
        compiler_params=_cparams(("arbitrary", "arbitrary")),
        name="nsa_compress",
    )(halves, w1, pe, b1, w2p)


def _sb_kernel(q_ref, k_ref, v_ref, z_ref, o_ref, qh_ref, c_ref, acc_ref):
    qi = pl.program_id(1)
    tq = q_ref.shape[2]
    n_pairs = q_ref.shape[1]
    lane = lax.broadcasted_iota(jnp.int32, (tq, LANES), 1)
    r_i = lax.broadcasted_iota(jnp.int32, (2 * TK, 2 * TK), 0)
    c_i = lax.broadcasted_iota(jnp.int32, (2 * TK, 2 * TK), 1)
    uu = jnp.where(r_i >= c_i, 1.0, 0.0).astype(_BF16)
    c_ref[...] = jnp.zeros_like(c_ref)
    acc_ref[...] = jnp.zeros_like(acc_ref)
    for p in range(n_pairs):
        q2 = q_ref[0, p]
        qh_ref[2 * p] = jnp.where(lane < HEAD_DIM, q2, jnp.zeros_like(q2))
        qh_ref[2 * p + 1] = jnp.where(lane >= HEAD_DIM, q2, jnp.zeros_like(q2))
    row = lax.broadcasted_iota(jnp.int32, (tq, LANES), 0)
    n_blocks = tq // TK

    def chunk(start, diagonal):
        for p in range(n_pairs):
            head_pair(p, k_ref[0, p, pl.ds(start, tq), :], v_ref[0, p, pl.ds(start, tq), :], diagonal)

    def head_pair(p, k2, v2, diagonal):
        for h in range(2 * p, 2 * p + 2):
            s = _dot_nt(qh_ref[h], k2)
            carry = c_ref[h]
            w_blocks = [None] * n_blocks
            for c in reversed(range(0, n_blocks, 2)):
                nz, l1m, mask = [], [], []
                for cc in (c, c + 1):
                    nzc = s[:, cc * TK:(cc + 1) * TK]
                    neg_abs = lax.bitcast_convert_type(
                        lax.bitcast_convert_type(nzc, jnp.uint32) | jnp.uint32(0x80000000), _F32)
                    lc = jnp.minimum(nzc, 0.0) - jnp.log(1.0 + jnp.exp(neg_abs))
                    if diagonal:
                        mask.append(lane + cc * TK < row)
                        lc = jnp.where(mask[-1], lc, 0.0)
                    nz.append(nzc)
                    l1m.append(lc.astype(_BF16))
                rc = _dot(jnp.concatenate(l1m, axis=1), uu)
                for i, cc in enumerate((c, c + 1)):
                    w = jnp.exp((rc[:, i * TK:(i + 1) * TK] + carry) - nz[i])
                    if diagonal:
                        w = jnp.where(mask[i], w, 0.0)
                    w_blocks[cc] = w.astype(_BF16)
                carry = carry + jnp.broadcast_to(rc[:, 0:1], carry.shape)
            acc_ref[h] += _dot(jnp.concatenate(w_blocks, axis=1), v2)
            c_ref[h] = carry

    chunk(pl.multiple_of(qi * tq, tq), True)

    def far_pair(it, carry):
        chunk(pl.multiple_of((qi - 1 - 2 * it) * tq, tq), False)
        chunk(pl.multiple_of((qi - 2 - 2 * it) * tq, tq), False)
        return carry

    lax.fori_loop(0, qi >> 1, far_pair, 0)

    @pl.when((qi & 1) == 1)
    def _():
        chunk(0, False)
    for p in range(n_pairs):
        o = jnp.where(lane < HEAD_DIM, acc_ref[2 * p], acc_ref[2 * p + 1])
        o_ref[0, p] = (o * _silu(z_ref[0, p])).astype(o_ref.dtype)


def _sb_attention(pb3, pf3):
    b, _, t, _ = pb3.shape
    n_pairs = N_HEADS_SB // 2
    assert PB_QA % n_pairs == 0 and PB_KA % n_pairs == 0 and PB_VA % n_pairs == 0 and PF_ZA % n_pairs == 0
    return pl.pallas_call(
        _sb_kernel,
        grid=(b, t // TQ_SB),
        in_specs=[pl.BlockSpec((1, n_pairs, TQ_SB, LANES), lambda i, q: (i, PB_QA // n_pairs, q, 0)),
                  pl.BlockSpec((1, n_pairs, t, LANES), lambda i, q: (i, PB_KA // n_pairs, 0, 0)),
                  pl.BlockSpec((1, n_pairs, t, LANES), lambda i, q: (i, PB_VA // n_pairs, 0, 0)),
                  pl.BlockSpec((1, n_pairs, TQ_SB, LANES), lambda i, q: (i, PF_ZA // n_pairs, q, 0))],
        out_specs=pl.BlockSpec((1, n_pairs, TQ_SB, LANES), lambda i, q: (i, 0, q, 0)),
        out_shape=jax.ShapeDtypeStruct((b, n_pairs, t, LANES), _BF16),
        scratch_shapes=[pltpu.VMEM((N_HEADS_SB, TQ_SB, LANES), _BF16),
                        pltpu.VMEM((N_HEADS_SB, TQ_SB, LANES), _F32),
                        pltpu.VMEM((N_HEADS_SB, TQ_SB, LANES), _F32)],
        compiler_params=_cparams(("arbitrary", "arbitrary")),
        name="stick_breaking_attention",
    )(pb3, pb3, pb3, pf3)


def _fox_kernel(q_ref, k_ref, vt_ref, augq_ref, augk_ref, z_ref, o_ref,
                qh_ref, m_ref, acc_ref, s_ref):
    qi = pl.program_id(1)
    tq = q_ref.shape[2]
    tk = tq
    n_hp = q_ref.shape[1]
    lane = lax.broadcasted_iota(jnp.int32, (tq, LANES), 1)
    keep = [lane < HEAD_DIM, lane >= HEAD_DIM]
    for p in range(n_hp):
        for h in range(2):
            qh_ref[2 * p + h] = jnp.where(keep[h], q_ref[0, p], augq_ref[0, p])
    m_ref[...] = jnp.full_like(m_ref, NEG)
    acc_ref[...] = jnp.zeros_like(acc_ref)
    key_i = lax.broadcasted_iota(jnp.int32, (TK, tq), 0)
    qry_i = lax.broadcasted_iota(jnp.int32, (TK, tq), 1)

    def scores(c, slot):
        start = c * tk if isinstance(c, int) else pl.multiple_of(c * tk, tk)
        for p in range(n_hp):
            k2 = k_ref[0, p, pl.ds(start, tk), :]
            ak = augk_ref[0, p, pl.ds(start, tk), :]
            for h in range(2):
                s_ref[slot, 2 * p + h] = _dot_nt(jnp.where(keep[h], k2, ak), qh_ref[2 * p + h])

    ones_rows = jnp.ones((SUBLANES, tk), _BF16)

    def consume(c, slot, diagonal):
        per = tk // TK
        for p in range(n_hp):
            vt = jnp.concatenate([vt_ref[0, p, c * per + j] for j in range(per)], axis=1)
            for hh in range(2):
                consume_head(2 * p + hh, vt[hh * HEAD_DIM:(hh + 1) * HEAD_DIM], slot, diagonal)

    def consume_head(h, vt_h, slot, diagonal):
        per = tk // TK
        vth = jnp.concatenate([vt_h, ones_rows], axis=0)
        for part in range(FOX_PARTS):
            js = range(part * per // FOX_PARTS, (part + 1) * per // FOX_PARTS)
            blocks = [s_ref[slot, h, j * TK:(j + 1) * TK, :] for j in js]
            if diagonal:
                blocks = [jnp.where(key_i + j * TK <= qry_i, blk, NEG) for j, blk in zip(js, blocks)]
            mx = blocks[0]
            for blk in blocks[1:]:
                mx = jnp.maximum(mx, blk)
            m_old = m_ref[h]
            m_new = jnp.maximum(m_old, jnp.max(mx, axis=0, keepdims=True))
            m_row = m_new[0:1]
            pt = jnp.concatenate([jnp.exp2(blk - m_row).astype(_BF16) for blk in blocks], axis=0)
            acc_ref[h] = (jnp.exp2(m_old - m_new)[0:1] * acc_ref[h]
                          + _dot(vth[:, js[0] * TK:(js[-1] + 1) * TK], pt))
            m_ref[h] = m_new

    scores(0, 0)
    n_pairs = qi >> 1

    def far_pair(i, carry):
        c = 2 * i
        scores(c + 1, 1)
        consume(c, 0, False)
        scores(c + 2, 0)
        consume(c + 1, 1, False)
        return carry

    lax.fori_loop(0, n_pairs, far_pair, 0)

    @pl.when(qi == 2 * n_pairs)
    def _():
        consume(qi, 0, True)

    @pl.when(qi != 2 * n_pairs)
    def _():
        scores(qi, 1)
        consume(qi - 1, 0, False)
        consume(qi, 1, True)
    for p in range(n_hp):
        o_t = jnp.concatenate([acc_ref[2 * p + h, :HEAD_DIM] / acc_ref[2 * p + h, HEAD_DIM:HEAD_DIM + 1]
                               for h in range(2)], axis=0)
        o_ref[0, p] = (o_t.T * _silu(z_ref[0, p])).astype(o_ref.dtype)


def _fox_attention(pb3, pf3, vt, augq, augk):
    b, _, t, _ = pb3.shape
    n_pairs = N_HEADS_FOX // 2
    assert PB_QC % n_pairs == 0 and PB_KF % n_pairs == 0 and VT_VF % n_pairs == 0 and PF_ZC % n_pairs == 0
    return pl.pallas_call(
        _fox_kernel,
        grid=(b, t // TQ_FOX),
        in_specs=[pl.BlockSpec((1, n_pairs, TQ_FOX, LANES), lambda i, q: (i, PB_QC // n_pairs, q, 0)),
                  pl.BlockSpec((1, n_pairs, t, LANES), lambda i, q: (i, PB_KF // n_pairs, 0, 0)),
                  pl.BlockSpec((1, n_pairs, t // TK, LANES, TK), lambda i, q: (i, VT_VF // n_pairs, 0, 0, 0)),
                  pl.BlockSpec((1, n_pairs, TQ_FOX, LANES), lambda i, q: (i, 0, q, 0)),
                  pl.BlockSpec((1, n_pairs, t, LANES), lambda i, q: (i, 0, 0, 0)),
                  pl.BlockSpec((1, n_pairs, TQ_FOX, LANES), lambda i, q: (i, PF_ZC // n_pairs, q, 0))],
        out_specs=pl.BlockSpec((1, n_pairs, TQ_FOX, LANES), lambda i, q: (i, 0, q, 0)),
        out_shape=jax.ShapeDtypeStruct((b, n_pairs, t, LANES), _BF16),
        scratch_shapes=[pltpu.VMEM((N_HEADS_FOX, TQ_FOX, LANES), _BF16),
                        pltpu.VMEM((N_HEADS_FOX, SUBLANES, TQ_FOX), _F32),
                        pltpu.VMEM((N_HEADS_FOX, HEAD_DIM + SUBLANES, TQ_FOX), _F32),
                        pltpu.VMEM((2, N_HEADS_FOX, TQ_FOX, TQ_FOX), _F32)],
        compiler_params=_cparams(("arbitrary", "arbitrary")),
        name="forgetting_attention",
    )(pb3, pb3, vt, augq, augk, pf3)


def _nsa_kernel(q_ref, kc_ref, vct_ref, ks_ref, vts_ref, kw_ref, vtw_ref, gl_ref, z_ref,
                bc_ref, bd_ref, ov_ref, o_ref,
                m_ref, acc_ref, osum_ref, qz_ref, qsel_ref, gates_ref, psum_ref, s_ref, ws_ref):
    qi = pl.program_id(1)
    tq = q_ref.shape[2]
    hpg = NSA_HPG
    rows = hpg * tq
    n_cmp = kc_ref.shape[2]
    lane_r = lax.broadcasted_iota(jnp.int32, (rows, LANES), 1)
    half = [lane_r < HEAD_DIM, lane_r >= HEAD_DIM]

    gates_ref[...] = _sigmoid(gl_ref[0, 0]).T

    def gated(c, h, per_group):
        parts = []
        for g, o in enumerate(per_group):
            r = 3 * (g * hpg + h) + c
            parts.append(gates_ref[r:r + 1, :] * o[:, h * tq:(h + 1) * tq])
        return jnp.concatenate(parts, axis=0)

    q4 = jnp.concatenate([q_ref[0, h] for h in range(hpg)], axis=0)
    for g in range(NSA_KV_GROUPS):
        qz_ref[g] = jnp.where(half[g], q4, jnp.zeros_like(q4))

    def scores(q_src, k_ref, start, tk, onehot):
        k2 = k_ref[0, 0, pl.ds(start, tk), :]
        lane_k = lax.broadcasted_iota(jnp.int32, (tk, LANES), 1)
        key_blk = (start + lax.broadcasted_iota(jnp.int32, (tk, LANES), 0)) >> int(math.log2(SLC_BLOCK))
        oh = jnp.where((lane_k & (SLC_BLOCK - 1)) == key_blk, 1.0, 0.0).astype(_BF16)
        out = []
        for g in range(NSA_KV_GROUPS):
            keep = (lane_k < HEAD_DIM) if g == 0 else (lane_k >= HEAD_DIM)
            out.append(_dot_nt(jnp.where(keep, k2, oh) if onehot else k2, q_src[g]))
        return out


    kc = kc_ref[0, 0]
    vct = vct_ref[0, 0]
    cmp_row0 = pl.multiple_of(n_cmp - qi * (tq // CMP_STRIDE), SUBLANES)
    o_cmp_t = []
    raw = [_dot_nt(kc, qz_ref[g]) for g in range(NSA_KV_GROUPS)]
    for g in range(NSA_KV_GROUPS):
        bias = jnp.concatenate([bc_ref[g * hpg + h, pl.ds(cmp_row0, n_cmp), :]
                                for h in range(hpg)], axis=1)
        sc = raw[g] + bias
        mx = jnp.max(sc, axis=0, keepdims=True)
        e = jnp.exp2(sc - mx)
        den = jnp.sum(e, axis=0, keepdims=True)
        pc = e * jnp.where(mx > 0.5 * NEG, 1.0 / den, 0.0)
        psum = pc[:, 0:tq]
        for h in range(1, hpg):
            psum = psum + pc[:, h * tq:(h + 1) * tq]
        psum_ref[g] = psum
        o_cmp_t.append(_dot(vct[g * HEAD_DIM:(g + 1) * HEAD_DIM], pc.astype(_BF16)))
    for h in range(hpg):
        osum_ref[h] = gated(0, h, o_cmp_t)

    rank_from = SLC_TOP * SLC_BLOCK // tq

    @pl.when(qi < rank_from)
    def _():
        for g in range(NSA_KV_GROUPS):
            qsel_ref[g] = qz_ref[g]

    @pl.when(qi >= rank_from)
    def _():
        n_sel = LANES // NSA_KV_GROUPS
        n_grp = n_sel // SUBLANES
        blk = lax.broadcasted_iota(jnp.int32, (n_sel, tq), 0)
        cur = (qi * tq + lax.broadcasted_iota(jnp.int32, (n_sel, tq), 1)) >> int(math.log2(SLC_BLOCK))
        forced = (blk == 0) | (blk == cur) | (blk == cur - 1)
        sub = lax.broadcasted_iota(jnp.int32, (SUBLANES, tq), 0)
        neg_t = []
        for g in range(NSA_KV_GROUPS):
            p = psum_ref[g]
            p1 = p.astype(_BF16)
            r1 = p - p1.astype(_F32)
            p2 = r1.astype(_BF16)
            p3 = (r1 - p2.astype(_F32)).astype(_BF16)
            a = _dot(ov_ref[...], jnp.concatenate([p1, p2, p3], axis=0))
            a = jnp.where(forced, FORCE_SCORE, a)
            a = jnp.where(blk > cur, -FORCE_SCORE, a)
            a_grp = [a[r * SUBLANES:(r + 1) * SUBLANES] for r in range(n_grp)]
            cnt = [jnp.zeros((SUBLANES, tq), _F32) for _ in range(n_grp)]
            for j in range(n_sel):
                rj = jnp.broadcast_to(a[j:j + 1], (SUBLANES, tq))
                jr, jo = divmod(j, SUBLANES)
                for r in range(n_grp):
                    if r > jr:
                        one = jnp.where(rj >= a_grp[r], 1.0, 0.0)
                    elif r < jr:
                        one = jnp.where(rj > a_grp[r], 1.0, 0.0)
                    else:
                        tie = jnp.where(sub > jo, jnp.where(rj == a_grp[r], 1.0, 0.0), 0.0)
                        one = jnp.where(rj > a_grp[r], 1.0, tie)
                    cnt[r] = cnt[r] + one
            neg_t.append(jnp.where(jnp.concatenate(cnt, axis=0) < float(SLC_TOP), 0.0, NEG))
        selneg = jnp.concatenate(neg_t[::-1], axis=0).T.astype(_BF16)
        selneg4 = jnp.concatenate([selneg] * hpg, axis=0)
        for g in range(NSA_KV_GROUPS):
            qsel_ref[g] = jnp.where(half[g], q4, selneg4)

    SEL, WIN = 0, 1

    def reset(st):
        m_ref[st] = jnp.full(m_ref.shape[1:], 2.0 * NEG, _F32)
        acc_ref[st] = jnp.zeros(acc_ref.shape[1:], _F32)

    def chunk(q_src, k_ref, vt_ref, start, tk, onehot, extras, st):
        s = scores(q_src, k_ref, start, tk, onehot)
        consume(lambda g, c: s[g][c * TK:(c + 1) * TK], vt_ref, start, tk, extras, st)

    def consume(block_of, vt_ref, start, tk, extras, st, limit=None):
        first = start // TK if isinstance(start, int) else start >> int(math.log2(TK))
        vt = jnp.concatenate([vt_ref[0, 0, first + j] for j in range(tk // TK)], axis=1)
        ones_rows = jnp.ones((SUBLANES, tk), _BF16)
        key_pos = start + lax.broadcasted_iota(jnp.int32, (HEAD_DIM + SUBLANES, tk), 1)
        for g in range(NSA_KV_GROUPS):
            vth = jnp.concatenate([vt[g * HEAD_DIM:(g + 1) * HEAD_DIM], ones_rows], axis=0)
            if limit is not None:
                vth = jnp.where(key_pos < limit, vth, jnp.zeros_like(vth))
            n_blk = tk // TK
            n_parts = NSA_PARTS if n_blk >= 2 * NSA_PARTS else 1
            for part in range(n_parts):
                cs = range(part * n_blk // n_parts, (part + 1) * n_blk // n_parts)
                blocks = [block_of(g, c) + extras[c](g) if c in extras else block_of(g, c) for c in cs]
                mx = blocks[0]
                for blk_s in blocks[1:]:
                    mx = jnp.maximum(mx, blk_s)
                m_old = m_ref[st, g]
                m_new = jnp.maximum(m_old, jnp.max(mx, axis=0, keepdims=True))
                m_row = m_new[0:1]
                pt = jnp.concatenate([jnp.exp2(blk_s - m_row).astype(_BF16) for blk_s in blocks], axis=0)
                acc_ref[st, g] = (jnp.exp2(m_old - m_new)[0:1] * acc_ref[st, g]
                                  + _dot(vth[:, cs[0] * TK:(cs[-1] + 1) * TK], pt))
                m_ref[st, g] = m_new

    def finish(c, st):
        outs = []
        for g in range(NSA_KV_GROUPS):
            den = acc_ref[st, g, HEAD_DIM:HEAD_DIM + 1]
            outs.append(acc_ref[st, g, :HEAD_DIM] * jnp.where(den > 0.0, 1.0 / den, 0.0))
        for h in range(hpg):
            osum_ref[h] += gated(c, h, outs)

    def near_bias(d):
        return lambda g: bd_ref[g, d]

    def short_path(q_src, k_ref, v_ref, onehot, st):
        def far(kt, carry):
            chunk(q_src, k_ref, v_ref, pl.multiple_of(kt * TK, TK), TK, onehot, {}, st)
            return carry

        lax.fori_loop(0, jnp.maximum(qi - 1, 0), far, 0)

        @pl.when(qi >= 1)
        def _():
            chunk(q_src, k_ref, v_ref, pl.multiple_of((qi - 1) * TK, TK), TK, onehot,
                  {0: near_bias(1)}, st)

        chunk(q_src, k_ref, v_ref, pl.multiple_of(qi * TK, TK), TK, onehot, {0: near_bias(0)}, st)

    reset(SEL)
    reset(WIN)
    big = 4 * TK
    n_win = WINDOW // TK
    main_from = max(big // TK - 1, n_win)
    win_mask = jnp.where(lax.broadcasted_iota(jnp.int32, (TK, rows), 0)
                         > (lax.broadcasted_iota(jnp.int32, (TK, rows), 1) & (tq - 1)), 0.0, NEG)

    @pl.when(qi < main_from)
    def _():
        short_path(qsel_ref, ks_ref, vts_ref, True, SEL)
        short_path(qz_ref, kw_ref, vtw_ref, False, WIN)

    @pl.when(qi >= main_from)
    def _():
        last = pl.multiple_of((qi - (big // TK - 1)) * TK, TK)
        n_far = (qi - (big // TK - 1) + big // TK - 1) >> int(math.log2(big // TK))
        near = {big // TK - 2: near_bias(1), big // TK - 1: near_bias(0)}

        def start_of(k):
            return pl.multiple_of(jnp.where(k < n_far, k * big, last), TK)

        def put(k, slot):
            for g, s in enumerate(scores(qsel_ref, ks_ref, start_of(k), big, True)):
                s_ref[slot, g] = s

        def take(k, slot, extras, limit):
            consume(lambda g, c: s_ref[slot, g, c * TK:(c + 1) * TK, :], vts_ref,
                    start_of(k), big, extras, SEL, limit)

        win_start = pl.multiple_of((qi - n_win) * TK, TK)
        for g, s in enumerate(scores(qz_ref, kw_ref, win_start, WINDOW + TK, False)):
            ws_ref[g] = s
        put(0, 0)
        consume(lambda g, c: ws_ref[g, c * TK:(c + 1) * TK, :], vtw_ref, win_start, WINDOW + TK,
                {0: lambda g: win_mask, n_win - 1: near_bias(1), n_win: near_bias(0)}, WIN)
        n_pair = n_far >> 1

        def far_pair(i, carry):
            k = 2 * i
            put(k + 1, 1)
            take(k, 0, {}, last)
            put(k + 2, 0)
            take(k + 1, 1, {}, last)
            return carry

        lax.fori_loop(0, n_pair, far_pair, 0)

        @pl.when(n_far == 2 * n_pair)
        def _():
            take(n_far, 0, near, None)

        @pl.when(n_far != 2 * n_pair)
        def _():
            put(n_far, 1)
            take(n_far - 1, 0, {}, last)
            take(n_far, 1, near, None)

    finish(1, SEL)
    finish(2, WIN)

    for h in range(hpg):
        o_ref[0, h] = (osum_ref[h].T * _silu(z_ref[0, h])).astype(o_ref.dtype)


def _nsa_attention(pb3, pf3, vt, kvc, kvc_t, bias_c, bias_d, ov):
    b, _, t, _ = pb3.shape
    n_cmp = kvc.shape[2]
    n_blk = N_HEADS_NSA // 2
    full = lambda shape: pl.BlockSpec(shape, lambda i, q: (0,) * len(shape))
    kv_spec = lambda col: pl.BlockSpec((1, 1, t, LANES), lambda i, q: (i, col, 0, 0))
    vt_spec = lambda blk: pl.BlockSpec((1, 1, t // TK, LANES, TK), lambda i, q: (i, blk, 0, 0, 0))
    return pl.pallas_call(
        _nsa_kernel,
        grid=(b, t // TQ_NSA),
        in_specs=[pl.BlockSpec((1, n_blk, TQ_NSA, LANES), lambda i, q: (i, PB_QB // n_blk, q, 0)),
                  pl.BlockSpec((1, 1, n_cmp, LANES), lambda i, q: (i, 0, 0, 0)),
                  pl.BlockSpec((1, 1, LANES, n_cmp), lambda i, q: (i, 1, 0, 0)),
                  kv_spec(PB_KS), vt_spec(VT_VS), kv_spec(PB_KW), vt_spec(VT_VW),
                  pl.BlockSpec((1, 1, TQ_NSA, LANES), lambda i, q: (i, PF_GB, q, 0)),
                  pl.BlockSpec((1, n_blk, TQ_NSA, LANES), lambda i, q: (i, PF_ZB // n_blk, q, 0)),
                  full(bias_c.shape),
                  full(bias_d.shape), full(ov.shape)],
        out_specs=pl.BlockSpec((1, n_blk, TQ_NSA, LANES), lambda i, q: (i, 0, q, 0)),
        out_shape=jax.ShapeDtypeStruct((b, n_blk, t, LANES), _BF16),
        scratch_shapes=[pltpu.VMEM((2, NSA_KV_GROUPS, SUBLANES, NSA_HPG * TQ_NSA), _F32),
                        pltpu.VMEM((2, NSA_KV_GROUPS, HEAD_DIM + SUBLANES, NSA_HPG * TQ_NSA), _F32),
                        pltpu.VMEM((n_blk, TQ_NSA, LANES), _F32),
                        pltpu.VMEM((NSA_KV_GROUPS, NSA_HPG * TQ_NSA, LANES), _BF16),
                        pltpu.VMEM((NSA_KV_GROUPS, NSA_HPG * TQ_NSA, LANES), _BF16),
                        pltpu.VMEM((LANES, TQ_NSA), _F32),
                        pltpu.VMEM((NSA_KV_GROUPS, n_cmp, TQ_NSA), _F32),
                        pltpu.VMEM((2, NSA_KV_GROUPS, 4 * TK, NSA_HPG * TQ_NSA), _F32),
                        pltpu.VMEM((NSA_KV_GROUPS, WINDOW + TK, NSA_HPG * TQ_NSA), _F32)],
        compiler_params=_cparams(("arbitrary", "arbitrary")),
        name="native_sparse_attention",
    )(pb3, kvc, kvc_t, pb3, vt, pb3, vt, pf3, pf3, bias_c, bias_d, ov)


def _out_kernel(x_ref, oa_ref, ob_ref, oc_ref, w_ref, g_ref, o_ref, *, final_norm):
    mixed = jnp.concatenate([o_ref_in[0, j] for o_ref_in in (oa_ref, ob_ref, oc_ref)
                             for j in range(o_ref_in.shape[1])], axis=1)
    x = x_ref[...] + _dot(mixed, w_ref[...])
    if final_norm:
        x = x * lax.rsqrt(jnp.mean(x * x, axis=-1, keepdims=True) + RMS_EPS) * g_ref[...]
    o_ref[...] = x


def _out_proj(x2, oa, ob, oc, w, g, final_norm):
    n = x2.shape[0]
    per_seq = oa.shape[2] // TM_PROJ
    row = lambda width: pl.BlockSpec((TM_PROJ, width), lambda i: (i, 0))
    blocks = lambda a: pl.BlockSpec((1, a.shape[1], TM_PROJ, LANES),
                                    lambda i: (i // per_seq, 0, i % per_seq, 0))
    return pl.pallas_call(
        functools.partial(_out_kernel, final_norm=final_norm),
        grid=(n // TM_PROJ,),
        in_specs=[row(D_MODEL), blocks(oa), blocks(ob), blocks(oc),
                  pl.BlockSpec((D_MODEL, D_MODEL), lambda i: (0, 0)),
                  pl.BlockSpec((1, D_MODEL), lambda i: (0, 0))],
        out_specs=row(D_MODEL),
        out_shape=jax.ShapeDtypeStruct((n, D_MODEL), _F32),
        compiler_params=_cparams(("arbitrary",)),
        name="out_proj_residual",
    )(x2, oa, ob, oc, w, g)


def _head_perm_cols(width_per_head, order):
    return np.concatenate([np.arange(h * width_per_head, (h + 1) * width_per_head) for h in order])


def _layout_w_in(w):
    widths = [256, 256, 256, 256, 512, 128, 128, 128, 128, 128, 128, 24, 512, 256, 256, 256, 4, 256]
    offs = np.concatenate([[0], np.cumsum(widths)])
    (qa, ka, va, za, qb, kc, vc, ks, vs, kw, vw, gb, zb, qc, kf, vf, fc, zc) = [
        w[:, offs[i]:offs[i + 1]] for i in range(len(widths))]
    scale = HEAD_DIM ** -0.5
    perm = _head_perm_cols(HEAD_DIM, NSA_HEAD_ORDER)
    pad = lambda a: jnp.pad(a, ((0, 0), (0, LANES - a.shape[1])))
    wb = jnp.concatenate([qb[:, perm] * (scale * LOG2E), qa * (-scale), ka, va, ks, kw,
                          qc * (scale * LOG2E), kf], axis=1)
    wf = jnp.concatenate([zb[:, perm], za, zc, pad(jnp.concatenate([gb, fc], axis=1))], axis=1)
    wc = jnp.concatenate([kc, vc], axis=1)
    wvt = jnp.concatenate([vs, vw, vf], axis=1)
    return wb.astype(_BF16), wf.astype(_BF16), wc.astype(_BF16), wvt.astype(_BF16)


def _static_tables(t):
    tq = TQ_NSA
    n_cmp_pad = t // CMP_STRIDE
    j = np.arange(n_cmp_pad)
    rel = np.arange(2 * n_cmp_pad) - n_cmp_pad
    dist_c = np.arange(tq)[None, :] - (rel[:, None] * CMP_STRIDE + CMP_BLOCK - 1)
    i_, j_ = np.arange(tq)[:, None], np.arange(TK)[None, :]
    dist_d = np.concatenate([d * TK + i_ - j_ for d in range(3)], axis=0)
    n_slc = LANES // NSA_KV_GROUPS
    cmp_start = j * CMP_STRIDE
    cmp_end = cmp_start + CMP_BLOCK - 1
    slc_start = np.arange(n_slc) * SLC_BLOCK
    ov1 = np.clip(np.minimum(cmp_end[:, None], slc_start[None, :] + SLC_BLOCK - 1)
                  - np.maximum(cmp_start[:, None], slc_start[None, :]) + 1, 0, None) / CMP_BLOCK
    ov1[n_cmp_pad - 1:] = 0.0
    ov1[:, t // SLC_BLOCK:] = 0.0
    ov3 = np.concatenate([ov1.T, ov1.T, ov1.T], axis=1)
    return jnp.asarray(dist_c, jnp.int32), jnp.asarray(dist_d, jnp.int32), jnp.asarray(ov3, _BF16)


def kernel(x, norm_g, w_in, w_out, forget_b, cmp_w1, cmp_b1, cmp_w2, cmp_pe, rel_bias, final_g):
    b, t, d = x.shape
    depth = norm_g.shape[0]
    assert d == D_MODEL and t % TM_PROJ == 0 and t % (CMP_STRIDE * LANES) == 0
    assert t // SLC_BLOCK <= LANES // NSA_KV_GROUPS and TQ_NSA == TK
    n_cmp_pad = t // CMP_STRIDE

    dist_c, dist_d, ov3 = _static_tables(t)
    bias_c = _bias_table(rel_bias * LOG2E, dist_c, 32)
    bias_d = _bias_table(rel_bias * LOG2E, dist_d, 32).reshape(N_HEADS_NSA, 3, TQ_NSA, TK)
    bias_d = bias_d[:, :2] - bias_d[:, 2:3]
    bias_d = bias_d.reshape(NSA_KV_GROUPS, NSA_HPG, 2, TQ_NSA, TK).transpose(0, 2, 4, 1, 3).reshape(
        NSA_KV_GROUPS, 2, TK, NSA_HPG * TQ_NSA)

    perm_rows = _head_perm_cols(HEAD_DIM, NSA_HEAD_ORDER)
    x2 = x.reshape(b * t, d)
    for l in range(depth):
        wb, wf, wc, wvt = _layout_w_in(w_in[l])
        pb, pf, pc, vt = _proj(x2, norm_g[l].reshape(1, d), wb, wf, wc, wvt, t)
        pb3, pf3 = pb, pf

        fb_row = jnp.pad(forget_b[l], (FC_LANE0, LANES - FC_LANE0 - N_HEADS_FOX)).reshape(1, LANES)
        augq, augk = _fgate(pf3, fb_row)

        halves = pc.reshape(b, 2, NSA_KV_GROUPS, n_cmp_pad, CMP_STRIDE * HEAD_DIM)
        w2 = cmp_w2[l]
        zeros = jnp.zeros_like(w2)
        w2p = jnp.stack([jnp.concatenate([w2, zeros], axis=-1),
                         jnp.concatenate([zeros, w2], axis=-1)], axis=1).astype(_BF16)
        kvc, kvc_t = _compress(halves, cmp_w1[l].astype(_BF16),
                               cmp_pe[l].reshape(2, 1, CMP_BLOCK * HEAD_DIM).astype(_BF16),
                               cmp_b1[l].reshape(2, 1, CMP_HIDDEN), w2p)

        o_a = _sb_attention(pb3, pf3)
        o_b = _nsa_attention(pb3, pf3, vt, kvc, kvc_t, bias_c, bias_d, ov3)
        o_c = _fox_attention(pb3, pf3, vt, augq, augk)

        wo = w_out[l]
        wo = jnp.concatenate([wo[:N_HEADS_SB * HEAD_DIM],
                              wo[N_HEADS_SB * HEAD_DIM:][:N_HEADS_NSA * HEAD_DIM][perm_rows],
                              wo[(N_HEADS_SB + N_HEADS_NSA) * HEAD_DIM:]], axis=0).astype(_BF16)
        last = l == depth - 1
        x2 = _out_proj(x2, o_a, o_b, o_c, wo, final_g.reshape(1, d), last)
    return x2.reshape(b, t, d)
```

```python
import functools
import math

import jax
import jax.numpy as jnp
import numpy as np
from jax import lax
from jax.experimental import pallas as pl
from jax.experimental.pallas import tpu as pltpu

D_MODEL = 1024
HEAD_DIM = 64
N_HEADS_SB = 4
N_HEADS_FOX = 4
N_HEADS_NSA = 8
NSA_KV_GROUPS = 2
NSA_HPG = N_HEADS_NSA // NSA_KV_GROUPS
CMP_BLOCK = 32
CMP_STRIDE = 16
CMP_HIDDEN = 256
SLC_BLOCK = 64
SLC_TOP = 16
WINDOW = 512
REL_BUCKETS = 32
REL_MAX_DIST = 128
FORCE_SCORE = 1e4
RMS_EPS = 1e-6
NEG = -1e30
LOG2E = math.log2(math.e)

LANES = 128
SUBLANES = 8
VMEM_LIMIT = 56 * 1024 * 1024

TM_PROJ = 1024
TQ_SB = 512
TQ_FOX = 512
FOX_PARTS = 2
NSA_PARTS = 2
TQ_NSA = 128
TK = 128

PB_QB, PB_QA, PB_KA, PB_VA = 0, 4, 6, 8
PB_KS, PB_KW = 10, 11
PB_QC, PB_KF = 12, 14
PB_BLOCKS = 16
N_CMP_SLABS = 2 * NSA_KV_GROUPS
VT_VS, VT_VW, VT_VF = 0, 1, 2
N_VT = 4
PF_ZB, PF_ZA, PF_ZC, PF_GB, PF_FC = 0, 4, 6, 8, 8
PF_BLOCKS = 9
FC_LANE0 = 3 * N_HEADS_NSA

NSA_HEAD_ORDER = [0, 4, 1, 5, 2, 6, 3, 7]

_F32 = jnp.float32
_BF16 = jnp.bfloat16


def _cparams(sem):
    return pltpu.CompilerParams(dimension_semantics=sem, vmem_limit_bytes=VMEM_LIMIT)


def _dot(a, b):
    return jnp.dot(a, b, preferred_element_type=_F32)


def _dot_nt(a, b):
    return lax.dot_general(a, b, (((1,), (1,)), ((), ())), preferred_element_type=_F32)


def _split3(x):
    h1 = x.astype(_BF16)
    r1 = x - h1.astype(_F32)
    h2 = r1.astype(_BF16)
    h3 = (r1 - h2.astype(_F32)).astype(_BF16)
    return jnp.concatenate([h1, h2, h3], axis=1)


def _sigmoid(x):
    return 1.0 / (1.0 + jnp.exp(-x))


def _silu(x):
    return x * _sigmoid(x)


def _rel_bucket_np(n):
    n = np.maximum(n, 0)
    max_exact = REL_BUCKETS // 2
    nf = np.maximum(n, 1).astype(np.float64)
    large = max_exact + (np.log(nf / max_exact) / math.log(REL_MAX_DIST / max_exact)
                         * (REL_BUCKETS - max_exact)).astype(np.int64)
    large = np.minimum(large, REL_BUCKETS - 1)
    return np.where(n < max_exact, n, large)


def _bucket_thresholds():
    n = np.arange(0, 4 * REL_MAX_DIST)
    bk = _rel_bucket_np(n)
    assert np.all(np.diff(bk) >= 0) and bk[-1] == REL_BUCKETS - 1
    return [int(np.argmax(bk >= b)) for b in range(REL_BUCKETS)]


_BUCKET_THR = _bucket_thresholds()


def _bias_kernel(tab_ref, dist_ref, o_ref):
    n = dist_ref[...]
    acc = [jnp.full(n.shape, tab_ref[0, h], _F32) for h in range(N_HEADS_NSA)]
    for b in range(1, REL_BUCKETS):
        ge = n >= _BUCKET_THR[b]
        for h in range(N_HEADS_NSA):
            acc[h] = jnp.where(ge, tab_ref[b, h], acc[h])
    valid = n >= 0
    for h in range(N_HEADS_NSA):
        o_ref[h] = jnp.where(valid, acc[h], NEG)


def _bias_table(rel_bias, dist, rows):
    n_rows, n_cols = dist.shape
    return pl.pallas_call(
        _bias_kernel,
        grid=(n_rows // rows,),
        in_specs=[pl.BlockSpec(memory_space=pltpu.SMEM),
                  pl.BlockSpec((rows, n_cols), lambda i: (i, 0))],
        out_specs=pl.BlockSpec((N_HEADS_NSA, rows, n_cols), lambda i: (0, i, 0)),
        out_shape=jax.ShapeDtypeStruct((N_HEADS_NSA, n_rows, n_cols), _F32),
        compiler_params=_cparams(("arbitrary",)),
        name="rel_bias_table",
    )(rel_bias, dist)


def _proj_kernel(x_ref, g_ref, wb_ref, wf_ref, wc_ref, wvt_ref, pb_ref, pf_ref, pc_ref, vt_ref):
    x = x_ref[...]
    y = x * lax.rsqrt(jnp.mean(x * x, axis=-1, keepdims=True) + RMS_EPS)
    h = (y * g_ref[...]).astype(_BF16)
    v_all = _dot(h, wvt_ref[...])
    for j in range(N_VT):
        v_t = v_all[:, j * LANES:(j + 1) * LANES].T.astype(_BF16)
        for c in range(TM_PROJ // TK):
            vt_ref[0, j, c] = v_t[:, c * TK:(c + 1) * TK]
    kv_cmp = _dot(h, wc_ref[...]).astype(_BF16)
    for s in range(N_CMP_SLABS):
        pc_ref[0, s] = kv_cmp[:, s * HEAD_DIM:(s + 1) * HEAD_DIM]
    chunk = 4 * LANES
    for c in range(0, PB_BLOCKS * LANES, chunk):
        w = min(chunk, PB_BLOCKS * LANES - c)
        res = _dot(h, wb_ref[:, c:c + w]).astype(_BF16)
        for j in range(w // LANES):
            pb_ref[0, c // LANES + j] = res[:, j * LANES:(j + 1) * LANES]
    for c in range(0, PF_BLOCKS * LANES, chunk):
        w = min(chunk, PF_BLOCKS * LANES - c)
        res = _dot(h, wf_ref[:, c:c + w])
        for j in range(w // LANES):
            pf_ref[0, c // LANES + j] = res[:, j * LANES:(j + 1) * LANES]


def _proj(x2, g, wb, wf, wc, wvt, t):
    n = x2.shape[0]
    per_seq = t // TM_PROJ
    return pl.pallas_call(
        _proj_kernel,
        grid=(n // TM_PROJ,),
        in_specs=[pl.BlockSpec((TM_PROJ, D_MODEL), lambda i: (i, 0)),
                  pl.BlockSpec((1, D_MODEL), lambda i: (0, 0)),
                  pl.BlockSpec((D_MODEL, PB_BLOCKS * LANES), lambda i: (0, 0), pipeline_mode=pl.Buffered(1)),
                  pl.BlockSpec((D_MODEL, PF_BLOCKS * LANES), lambda i: (0, 0), pipeline_mode=pl.Buffered(1)),
                  pl.BlockSpec((D_MODEL, N_CMP_SLABS * HEAD_DIM), lambda i: (0, 0), pipeline_mode=pl.Buffered(1)),
                  pl.BlockSpec((D_MODEL, N_VT * LANES), lambda i: (0, 0), pipeline_mode=pl.Buffered(1))],
        out_specs=[pl.BlockSpec((1, PB_BLOCKS, TM_PROJ, LANES), lambda i: (i // per_seq, 0, i % per_seq, 0)),
                   pl.BlockSpec((1, PF_BLOCKS, TM_PROJ, LANES), lambda i: (i // per_seq, 0, i % per_seq, 0)),
                   pl.BlockSpec((1, N_CMP_SLABS, TM_PROJ, HEAD_DIM),
                                lambda i: (i // per_seq, 0, i % per_seq, 0)),
                   pl.BlockSpec((1, N_VT, TM_PROJ // TK, LANES, TK),
                                lambda i: (i // per_seq, 0, i % per_seq, 0, 0))],
        out_shape=[jax.ShapeDtypeStruct((n // t, PB_BLOCKS, t, LANES), _BF16),
                   jax.ShapeDtypeStruct((n // t, PF_BLOCKS, t, LANES), _F32),
                   jax.ShapeDtypeStruct((n // t, N_CMP_SLABS, t, HEAD_DIM), _BF16),
                   jax.ShapeDtypeStruct((n // t, N_VT, t // TK, LANES, TK), _BF16)],
        compiler_params=_cparams(("arbitrary",)),
        name="rmsnorm_in_proj",
    )(x2, g, wb, wf, wc, wvt)


N_SPLIT = 3


def _fgate_tables():
    n_pairs = N_HEADS_FOX // 2
    pq = np.zeros((N_SPLIT * LANES, n_pairs * LANES), np.float32)
    pk = np.zeros_like(pq)
    ones_q = np.zeros((1, n_pairs * LANES), np.float32)
    ones_k = np.zeros_like(ones_q)
    for head in range(N_HEADS_FOX):
        pair, slot = divmod(head, 2)
        base = pair * LANES + (HEAD_DIM if slot == 0 else 0)
        for j in range(N_SPLIT):
            pq[j * LANES + FC_LANE0 + head, base + j] = 1.0
            pk[j * LANES + FC_LANE0 + head, base + N_SPLIT + j] = 1.0
        ones_q[0, base + N_SPLIT:base + 2 * N_SPLIT] = 1.0
        ones_k[0, base:base + N_SPLIT] = 1.0
    return (jnp.asarray(pq, _BF16), jnp.asarray(pk, _BF16),
            jnp.asarray(ones_q), jnp.asarray(ones_k))


def _fgate_kernel(fc_ref, fb_ref, pq_ref, pk_ref, oq_ref, ok_ref, augq_ref, augk_ref):
    t = fc_ref.shape[2]
    z = fc_ref[0, 0] + fb_ref[...]
    logf = jnp.minimum(z, 0.0) - jnp.log1p(jnp.exp(-jnp.abs(z)))
    row = lax.broadcasted_iota(jnp.int32, (t, LANES), 0)
    c = logf
    shift = 1
    while shift < t:
        c = c + jnp.where(row >= shift, pltpu.roll(c, shift, axis=0), 0.0)
        shift *= 2
    c3 = _split3(c * LOG2E)
    aq = _dot(c3, pq_ref[...]) + oq_ref[...]
    ak = ok_ref[...] - _dot(c3, pk_ref[...])
    for p in range(N_HEADS_FOX // 2):
        augq_ref[0, p] = aq[:, p * LANES:(p + 1) * LANES].astype(_BF16)
        augk_ref[0, p] = ak[:, p * LANES:(p + 1) * LANES].astype(_BF16)


def _fgate(pf3, fb_row):
    b, _, t, _ = pf3.shape
    n_pairs = N_HEADS_FOX // 2
    tables = _fgate_tables()
    full = lambda a: pl.BlockSpec(a.shape, lambda i: (0,) * a.ndim)
    aug = lambda: pl.BlockSpec((1, n_pairs, t, LANES), lambda i: (i, 0, 0, 0))
    return pl.pallas_call(
        _fgate_kernel,
        grid=(b,),
        in_specs=[pl.BlockSpec((1, 1, t, LANES), lambda i: (i, PF_FC, 0, 0)),
                  pl.BlockSpec((1, LANES), lambda i: (0, 0))] + [full(a) for a in tables],
        out_specs=[aug(), aug()],
        out_shape=[jax.ShapeDtypeStruct((b, n_pairs, t, LANES), _BF16),
                   jax.ShapeDtypeStruct((b, n_pairs, t, LANES), _BF16)],
        compiler_params=_cparams(("arbitrary",)),
        name="forget_gate_cumsum",
    )(pf3, fb_row, *tables)


def _compress_kernel(x_ref, w1_ref, pe_ref, b1_ref, w2_ref, o_ref, ot_ref):
    nc = x_ref.shape[3]
    half = CMP_STRIDE * HEAD_DIM
    w1 = w1_ref[0]
    c1 = _dot(jnp.broadcast_to(pe_ref[0], (SUBLANES, 2 * half)), w1)[0:1] + b1_ref[0]
    out = jnp.zeros((nc, LANES), _F32)
    for g in range(NSA_KV_GROUPS):
        xg = x_ref[0, 0, g]
        a = _dot(xg, w1[:half])
        bb = _dot(xg, w1[half:])
        h = a + pltpu.roll(bb, nc - 1, axis=0) + c1
        out = out + _dot(_silu(h).astype(_BF16), w2_ref[0, g])
    o_ref[0, 0] = out.astype(_BF16)
    ot_ref[0, 0] = out.T.astype(_BF16)


def _compress(halves, w1, pe, b1, w2p):
    b, _, g, nc, width = halves.shape
    return pl.pallas_call(
        _compress_kernel,
        grid=(b, 2),
        in_specs=[pl.BlockSpec((1, 1, g, nc, width), lambda i, k: (i, k, 0, 0, 0)),
                  pl.BlockSpec((1, 2 * width, CMP_HIDDEN), lambda i, k: (k, 0, 0)),
                  pl.BlockSpec((1, 1, 2 * width), lambda i, k: (k, 0, 0)),
                  pl.BlockSpec((1, 1, CMP_HIDDEN), lambda i, k: (k, 0, 0)),
                  pl.BlockSpec((1, g, CMP_HIDDEN, LANES), lambda i, k: (k, 0, 0, 0))],
        out_specs=[pl.BlockSpec((1, 1, nc, LANES), lambda i, k: (i, k, 0, 0)),
                   pl.BlockSpec((1, 1, LANES, nc), lambda i, k: (i, k, 0, 0))],
        out_shape=[jax.ShapeDtypeStruct((b, 2, nc, LANES), _BF16),
                   jax.ShapeDtypeStruct((b, 2, LANES, nc), _BF16)],
        compiler_params=_cparams(("arbitrary", "arbitrary")),
        name="nsa_compress",
    )(halves, w1, pe, b1, w2p)


def _sb_kernel(q_ref, k_ref, v_ref, z_ref, o_ref, qh_ref, c_ref, acc_ref):
    qi = pl.program_id(1)
    tq = q_ref.shape[2]
    n_pairs = q_ref.shape[1]
    lane = lax.broadcasted_iota(jnp.int32, (tq, LANES), 1)
    r_i = lax.broadcasted_iota(jnp.int32, (2 * TK, 2 * TK), 0)
    c_i = lax.broadcasted_iota(jnp.int32, (2 * TK, 2 * TK), 1)
    uu = jnp.where(r_i >= c_i, 1.0, 0.0).astype(_BF16)
    c_ref[...] = jnp.zeros_like(c_ref)
    acc_ref[...] = jnp.zeros_like(acc_ref)
    for p in range(n_pairs):
        q2 = q_ref[0, p]
        qh_ref[2 * p] = jnp.where(lane < HEAD_DIM, q2, jnp.zeros_like(q2))
        qh_ref[2 * p + 1] = jnp.where(lane >= HEAD_DIM, q2, jnp.zeros_like(q2))
    row = lax.broadcasted_iota(jnp.int32, (tq, LANES), 0)
    n_blocks = tq // TK

    def chunk(start, diagonal):
        for p in range(n_pairs):
            head_pair(p, k_ref[0, p, pl.ds(start, tq), :], v_ref[0, p, pl.ds(start, tq), :], diagonal)

    def head_pair(p, k2, v2, diagonal):
        for h in range(2 * p, 2 * p + 2):
            s = _dot_nt(qh_ref[h], k2)
            carry = c_ref[h]
            w_blocks = [None] * n_blocks
            for c in reversed(range(0, n_blocks, 2)):
                nz, l1m, mask = [], [], []
                for cc in (c, c + 1):
                    nzc = s[:, cc * TK:(cc + 1) * TK]
                    neg_abs = lax.bitcast_convert_type(
                        lax.bitcast_convert_type(nzc, jnp.uint32) | jnp.uint32(0x80000000), _F32)
                    lc = jnp.minimum(nzc, 0.0) - jnp.log(1.0 + jnp.exp(neg_abs))
                    if diagonal:
                        mask.append(lane + cc * TK < row)
                        lc = jnp.where(mask[-1], lc, 0.0)
                    nz.append(nzc)
                    l1m.append(lc.astype(_BF16))
                rc = _dot(jnp.concatenate(l1m, axis=1), uu)
                for i, cc in enumerate((c, c + 1)):
                    w = jnp.exp((rc[:, i * TK:(i + 1) * TK] + carry) - nz[i])
                    if diagonal:
                        w = jnp.where(mask[i], w, 0.0)
                    w_blocks[cc] = w.astype(_BF16)
                carry = carry + jnp.broadcast_to(rc[:, 0:1], carry.shape)
            acc_ref[h] += _dot(jnp.concatenate(w_blocks, axis=1), v2)
            c_ref[h] = carry

    chunk(pl.multiple_of(qi * tq, tq), True)

    def far_pair(it, carry):
        chunk(pl.multiple_of((qi - 1 - 2 * it) * tq, tq), False)
        chunk(pl.multiple_of((qi - 2 - 2 * it) * tq, tq), False)
        return carry

    lax.fori_loop(0, qi >> 1, far_pair, 0)

    @pl.when((qi & 1) == 1)
    def _():
        chunk(0, False)
    for p in range(n_pairs):
        o = jnp.where(lane < HEAD_DIM, acc_ref[2 * p], acc_ref[2 * p + 1])
        o_ref[0, p] = (o * _silu(z_ref[0, p])).astype(o_ref.dtype)


def _sb_attention(pb3, pf3):
    b, _, t, _ = pb3.shape
    n_pairs = N_HEADS_SB // 2
    assert PB_QA % n_pairs == 0 and PB_KA % n_pairs == 0 and PB_VA % n_pairs == 0 and PF_ZA % n_pairs == 0
    return pl.pallas_call(
        _sb_kernel,
        grid=(b, t // TQ_SB),
        in_specs=[pl.BlockSpec((1, n_pairs, TQ_SB, LANES), lambda i, q: (i, PB_QA // n_pairs, q, 0)),
                  pl.BlockSpec((1, n_pairs, t, LANES), lambda i, q: (i, PB_KA // n_pairs, 0, 0)),
                  pl.BlockSpec((1, n_pairs, t, LANES), lambda i, q: (i, PB_VA // n_pairs, 0, 0)),
                  pl.BlockSpec((1, n_pairs, TQ_SB, LANES), lambda i, q: (i, PF_ZA // n_pairs, q, 0))],
        out_specs=pl.BlockSpec((1, n_pairs, TQ_SB, LANES), lambda i, q: (i, 0, q, 0)),
        out_shape=jax.ShapeDtypeStruct((b, n_pairs, t, LANES), _BF16),
        scratch_shapes=[pltpu.VMEM((N_HEADS_SB, TQ_SB, LANES), _BF16),
                        pltpu.VMEM((N_HEADS_SB, TQ_SB, LANES), _F32),
                        pltpu.VMEM((N_HEADS_SB, TQ_SB, LANES), _F32)],
        compiler_params=_cparams(("arbitrary", "arbitrary")),
        name="stick_breaking_attention",
    )(pb3, pb3, pb3, pf3)


def _fox_kernel(q_ref, k_ref, vt_ref, augq_ref, augk_ref, z_ref, o_ref,
                qh_ref, m_ref, acc_ref, s_ref):
    qi = pl.program_id(1)
    tq = q_ref.shape[2]
    tk = tq
    n_hp = q_ref.shape[1]
    lane = lax.broadcasted_iota(jnp.int32, (tq, LANES), 1)
    keep = [lane < HEAD_DIM, lane >= HEAD_DIM]
    for p in range(n_hp):
        for h in range(2):
            qh_ref[2 * p + h] = jnp.where(keep[h], q_ref[0, p], augq_ref[0, p])
    m_ref[...] = jnp.full_like(m_ref, NEG)
    acc_ref[...] = jnp.zeros_like(acc_ref)
    key_i = lax.broadcasted_iota(jnp.int32, (TK, tq), 0)
    qry_i = lax.broadcasted_iota(jnp.int32, (TK, tq), 1)

    def scores(c, slot):
        start = c * tk if isinstance(c, int) else pl.multiple_of(c * tk, tk)
        for p in range(n_hp):
            k2 = k_ref[0, p, pl.ds(start, tk), :]
            ak = augk_ref[0, p, pl.ds(start, tk), :]
            for h in range(2):
                s_ref[slot, 2 * p + h] = _dot_nt(jnp.where(keep[h], k2, ak), qh_ref[2 * p + h])

    ones_rows = jnp.ones((SUBLANES, tk), _BF16)

    def consume(c, slot, diagonal):
        per = tk // TK
        for p in range(n_hp):
            vt = jnp.concatenate([vt_ref[0, p, c * per + j] for j in range(per)], axis=1)
            for hh in range(2):
                consume_head(2 * p + hh, vt[hh * HEAD_DIM:(hh + 1) * HEAD_DIM], slot, diagonal)

    def consume_head(h, vt_h, slot, diagonal):
        per = tk // TK
        vth = jnp.concatenate([vt_h, ones_rows], axis=0)
        for part in range(FOX_PARTS):
            js = range(part * per // FOX_PARTS, (part + 1) * per // FOX_PARTS)
            blocks = [s_ref[slot, h, j * TK:(j + 1) * TK, :] for j in js]
            if diagonal:
                blocks = [jnp.where(key_i + j * TK <= qry_i, blk, NEG) for j, blk in zip(js, blocks)]
            mx = blocks[0]
            for blk in blocks[1:]:
                mx = jnp.maximum(mx, blk)
            m_old = m_ref[h]
            m_new = jnp.maximum(m_old, jnp.max(mx, axis=0, keepdims=True))
            m_row = m_new[0:1]
            pt = jnp.concatenate([jnp.exp2(blk - m_row).astype(_BF16) for blk in blocks], axis=0)
            acc_ref[h] = (jnp.exp2(m_old - m_new)[0:1] * acc_ref[h]
                          + _dot(vth[:, js[0] * TK:(js[-1] + 1) * TK], pt))
            m_ref[h] = m_new

    scores(0, 0)
    n_pairs = qi >> 1

    def far_pair(i, carry):
        c = 2 * i
        scores(c + 1, 1)
        consume(c, 0, False)
        scores(c + 2, 0)
        consume(c + 1, 1, False)
        return carry

    lax.fori_loop(0, n_pairs, far_pair, 0)

    @pl.when(qi == 2 * n_pairs)
    def _():
        consume(qi, 0, True)

    @pl.when(qi != 2 * n_pairs)
    def _():
        scores(qi, 1)
        consume(qi - 1, 0, False)
        consume(qi, 1, True)
    for p in range(n_hp):
        o_t = jnp.concatenate([acc_ref[2 * p + h, :HEAD_DIM] / acc_ref[2 * p + h, HEAD_DIM:HEAD_DIM + 1]
                               for h in range(2)], axis=0)
        o_ref[0, p] = (o_t.T * _silu(z_ref[0, p])).astype(o_ref.dtype)


def _fox_attention(pb3, pf3, vt, augq, augk):
    b, _, t, _ = pb3.shape
    n_pairs = N_HEADS_FOX // 2
    assert PB_QC % n_pairs == 0 and PB_KF % n_pairs == 0 and VT_VF % n_pairs == 0 and PF_ZC % n_pairs == 0
    return pl.pallas_call(
        _fox_kernel,
        grid=(b, t // TQ_FOX),
        in_specs=[pl.BlockSpec((1, n_pairs, TQ_FOX, LANES), lambda i, q: (i, PB_QC // n_pairs, q, 0)),
                  pl.BlockSpec((1, n_pairs, t, LANES), lambda i, q: (i, PB_KF // n_pairs, 0, 0)),
                  pl.BlockSpec((1, n_pairs, t // TK, LANES, TK), lambda i, q: (i, VT_VF // n_pairs, 0, 0, 0)),
                  pl.BlockSpec((1, n_pairs, TQ_FOX, LANES), lambda i, q: (i, 0, q, 0)),
                  pl.BlockSpec((1, n_pairs, t, LANES), lambda i, q: (i, 0, 0, 0)),
                  pl.BlockSpec((1, n_pairs, TQ_FOX, LANES), lambda i, q: (i, PF_ZC // n_pairs, q, 0))],
        out_specs=pl.BlockSpec((1, n_pairs, TQ_FOX, LANES), lambda i, q: (i, 0, q, 0)),
        out_shape=jax.ShapeDtypeStruct((b, n_pairs, t, LANES), _BF16),
        scratch_shapes=[pltpu.VMEM((N_HEADS_FOX, TQ_FOX, LANES), _BF16),
                        pltpu.VMEM((N_HEADS_FOX, SUBLANES, TQ_FOX), _F32),
                        pltpu.VMEM((N_HEADS_FOX, HEAD_DIM + SUBLANES, TQ_FOX), _F32),
                        pltpu.VMEM((2, N_HEADS_FOX, TQ_FOX, TQ_FOX), _F32)],
        compiler_params=_cparams(("arbitrary", "arbitrary")),
        name="forgetting_attention",
    )(pb3, pb3, vt, augq, augk, pf3)


def _nsa_kernel(q_ref, kc_ref, vct_ref, ks_ref, vts_ref, kw_ref, vtw_ref, gl_ref, z_ref,
                bc_ref, bd_ref, ov_ref, o_ref,
                m_ref, acc_ref, osum_ref, qz_ref, qsel_ref, gates_ref, psum_ref, s_ref, ws_ref):
    qi = pl.program_id(1)
    tq = q_ref.shape[2]
    hpg = NSA_HPG
    rows = hpg * tq
    n_cmp = kc_ref.shape[2]
    lane_r = lax.broadcasted_iota(jnp.int32, (rows, LANES), 1)
    half = [lane_r < HEAD_DIM, lane_r >= HEAD_DIM]

    gates_ref[...] = _sigmoid(gl_ref[0, 0]).T

    def gated(c, h, per_group):
        parts = []
        for g, o in enumerate(per_group):
            r = 3 * (g * hpg + h) + c
            parts.append(gates_ref[r:r + 1, :] * o[:, h * tq:(h + 1) * tq])
        return jnp.concatenate(parts, axis=0)

    q4 = jnp.concatenate([q_ref[0, h] for h in range(hpg)], axis=0)
    for g in range(NSA_KV_GROUPS):
        qz_ref[g] = jnp.where(half[g], q4, jnp.zeros_like(q4))

    def scores(q_src, k_ref, start, tk, onehot):
        k2 = k_ref[0, 0, pl.ds(start, tk), :]
        lane_k = lax.broadcasted_iota(jnp.int32, (tk, LANES), 1)
        key_blk = (start + lax.broadcasted_iota(jnp.int32, (tk, LANES), 0)) >> int(math.log2(SLC_BLOCK))
        oh = jnp.where((lane_k & (SLC_BLOCK - 1)) == key_blk, 1.0, 0.0).astype(_BF16)
        out = []
        for g in range(NSA_KV_GROUPS):
            keep = (lane_k < HEAD_DIM) if g == 0 else (lane_k >= HEAD_DIM)
            out.append(_dot_nt(jnp.where(keep, k2, oh) if onehot else k2, q_src[g]))
        return out


    kc = kc_ref[0, 0]
    vct = vct_ref[0, 0]
    cmp_row0 = pl.multiple_of(n_cmp - qi * (tq // CMP_STRIDE), SUBLANES)
    o_cmp_t = []
    raw = [_dot_nt(kc, qz_ref[g]) for g in range(NSA_KV_GROUPS)]
    for g in range(NSA_KV_GROUPS):
        bias = jnp.concatenate([bc_ref[g * hpg + h, pl.ds(cmp_row0, n_cmp), :]
                                for h in range(hpg)], axis=1)
        sc = raw[g] + bias
        mx = jnp.max(sc, axis=0, keepdims=True)
        e = jnp.exp2(sc - mx)
        den = jnp.sum(e, axis=0, keepdims=True)
        pc = e * jnp.where(mx > 0.5 * NEG, 1.0 / den, 0.0)
        psum = pc[:, 0:tq]
        for h in range(1, hpg):
            psum = psum + pc[:, h * tq:(h + 1) * tq]
        psum_ref[g] = psum
        o_cmp_t.append(_dot(vct[g * HEAD_DIM:(g + 1) * HEAD_DIM], pc.astype(_BF16)))
    for h in range(hpg):
        osum_ref[h] = gated(0, h, o_cmp_t)

    rank_from = SLC_TOP * SLC_BLOCK // tq

    @pl.when(qi < rank_from)
    def _():
        for g in range(NSA_KV_GROUPS):
            qsel_ref[g] = qz_ref[g]

    @pl.when(qi >= rank_from)
    def _():
        n_sel = LANES // NSA_KV_GROUPS
        n_grp = n_sel // SUBLANES
        blk = lax.broadcasted_iota(jnp.int32, (n_sel, tq), 0)
        cur = (qi * tq + lax.broadcasted_iota(jnp.int32, (n_sel, tq), 1)) >> int(math.log2(SLC_BLOCK))
        forced = (blk == 0) | (blk == cur) | (blk == cur - 1)
        sub = lax.broadcasted_iota(jnp.int32, (SUBLANES, tq), 0)
        neg_t = []
        for g in range(NSA_KV_GROUPS):
            p = psum_ref[g]
            p1 = p.astype(_BF16)
            r1 = p - p1.astype(_F32)
            p2 = r1.astype(_BF16)
            p3 = (r1 - p2.astype(_F32)).astype(_BF16)
            a = _dot(ov_ref[...], jnp.concatenate([p1, p2, p3], axis=0))
            a = jnp.where(forced, FORCE_SCORE, a)
            a = jnp.where(blk > cur, -FORCE_SCORE, a)
            a_grp = [a[r * SUBLANES:(r + 1) * SUBLANES] for r in range(n_grp)]
            cnt = [jnp.zeros((SUBLANES, tq), _F32) for _ in range(n_grp)]
            for j in range(n_sel):
                rj = jnp.broadcast_to(a[j:j + 1], (SUBLANES, tq))
                jr, jo = divmod(j, SUBLANES)
                for r in range(n_grp):
                    if r > jr:
                        one = jnp.where(rj >= a_grp[r], 1.0, 0.0)
                    elif r < jr:
                        one = jnp.where(rj > a_grp[r], 1.0, 0.0)
                    else:
                        tie = jnp.where(sub > jo, jnp.where(rj == a_grp[r], 1.0, 0.0), 0.0)
                        one = jnp.where(rj > a_grp[r], 1.0, tie)
                    cnt[r] = cnt[r] + one
            neg_t.append(jnp.where(jnp.concatenate(cnt, axis=0) < float(SLC_TOP), 0.0, NEG))
        selneg = jnp.concatenate(neg_t[::-1], axis=0).T.astype(_BF16)
        selneg4 = jnp.concatenate([selneg] * hpg, axis=0)
        for g in range(NSA_KV_GROUPS):
            qsel_ref[g] = jnp.where(half[g], q4, selneg4)

    SEL, WIN = 0, 1

    def reset(st):
        m_ref[st] = jnp.full(m_ref.shape[1:], 2.0 * NEG, _F32)
        acc_ref[st] = jnp.zeros(acc_ref.shape[1:], _F32)

    def chunk(q_src, k_ref, vt_ref, start, tk, onehot, extras, st):
        s = scores(q_src, k_ref, start, tk, onehot)
        consume(lambda g, c: s[g][c * TK:(c + 1) * TK], vt_ref, start, tk, extras, st)

    def consume(block_of, vt_ref, start, tk, extras, st, limit=None):
        first = start // TK if isinstance(start, int) else start >> int(math.log2(TK))
        vt = jnp.concatenate([vt_ref[0, 0, first + j] for j in range(tk // TK)], axis=1)
        ones_rows = jnp.ones((SUBLANES, tk), _BF16)
        key_pos = start + lax.broadcasted_iota(jnp.int32, (HEAD_DIM + SUBLANES, tk), 1)
        for g in range(NSA_KV_GROUPS):
            vth = jnp.concatenate([vt[g * HEAD_DIM:(g + 1) * HEAD_DIM], ones_rows], axis=0)
            if limit is not None:
                vth = jnp.where(key_pos < limit, vth, jnp.zeros_like(vth))
            n_blk = tk // TK
            n_parts = NSA_PARTS if n_blk >= 2 * NSA_PARTS else 1
            for part in range(n_parts):
                cs = range(part * n_blk // n_parts, (part + 1) * n_blk // n_parts)
                blocks = [block_of(g, c) + extras[c](g) if c in extras else block_of(g, c) for c in cs]
                mx = blocks[0]
                for blk_s in blocks[1:]:
                    mx = jnp.maximum(mx, blk_s)
                m_old = m_ref[st, g]
                m_new = jnp.maximum(m_old, jnp.max(mx, axis=0, keepdims=True))
                m_row = m_new[0:1]
                pt = jnp.concatenate([jnp.exp2(blk_s - m_row).astype(_BF16) for blk_s in blocks], axis=0)
                acc_ref[st, g] = (jnp.exp2(m_old - m_new)[0:1] * acc_ref[st, g]
                                  + _dot(vth[:, cs[0] * TK:(cs[-1] + 1) * TK], pt))
                m_ref[st, g] = m_new

    def finish(c, st):
        outs = []
        for g in range(NSA_KV_GROUPS):
            den = acc_ref[st, g, HEAD_DIM:HEAD_DIM + 1]
            outs.append(acc_ref[st, g, :HEAD_DIM] * jnp.where(den > 0.0, 1.0 / den, 0.0))
        for h in range(hpg):
            osum_ref[h] += gated(c, h, outs)

    def near_bias(d):
        return lambda g: bd_ref[g, d]

    def short_path(q_src, k_ref, v_ref, onehot, st):
        def far(kt, carry):
            chunk(q_src, k_ref, v_ref, pl.multiple_of(kt * TK, TK), TK, onehot, {}, st)
            return carry

        lax.fori_loop(0, jnp.maximum(qi - 1, 0), far, 0)

        @pl.when(qi >= 1)
        def _():
            chunk(q_src, k_ref, v_ref, pl.multiple_of((qi - 1) * TK, TK), TK, onehot,
                  {0: near_bias(1)}, st)

        chunk(q_src, k_ref, v_ref, pl.multiple_of(qi * TK, TK), TK, onehot, {0: near_bias(0)}, st)

    reset(SEL)
    reset(WIN)
    big = 4 * TK
    n_win = WINDOW // TK
    main_from = max(big // TK - 1, n_win)
    win_mask = jnp.where(lax.broadcasted_iota(jnp.int32, (TK, rows), 0)
                         > (lax.broadcasted_iota(jnp.int32, (TK, rows), 1) & (tq - 1)), 0.0, NEG)

    @pl.when(qi < main_from)
    def _():
        short_path(qsel_ref, ks_ref, vts_ref, True, SEL)
        short_path(qz_ref, kw_ref, vtw_ref, False, WIN)

    @pl.when(qi >= main_from)
    def _():
        last = pl.multiple_of((qi - (big // TK - 1)) * TK, TK)
        n_far = (qi - (big // TK - 1) + big // TK - 1) >> int(math.log2(big // TK))
        near = {big // TK - 2: near_bias(1), big // TK - 1: near_bias(0)}

        def start_of(k):
            return pl.multiple_of(jnp.where(k < n_far, k * big, last), TK)

        def put(k, slot):
            for g, s in enumerate(scores(qsel_ref, ks_ref, start_of(k), big, True)):
                s_ref[slot, g] = s

        def take(k, slot, extras, limit):
            consume(lambda g, c: s_ref[slot, g, c * TK:(c + 1) * TK, :], vts_ref,
                    start_of(k), big, extras, SEL, limit)

        win_start = pl.multiple_of((qi - n_win) * TK, TK)
        for g, s in enumerate(scores(qz_ref, kw_ref, win_start, WINDOW + TK, False)):
            ws_ref[g] = s
        put(0, 0)
        consume(lambda g, c: ws_ref[g, c * TK:(c + 1) * TK, :], vtw_ref, win_start, WINDOW + TK,
                {0: lambda g: win_mask, n_win - 1: near_bias(1), n_win: near_bias(0)}, WIN)
        n_pair = n_far >> 1

        def far_pair(i, carry):
            k = 2 * i
            put(k + 1, 1)
            take(k, 0, {}, last)
            put(k + 2, 0)
            take(k + 1, 1, {}, last)
            return carry

        lax.fori_loop(0, n_pair, far_pair, 0)

        @pl.when(n_far == 2 * n_pair)
        def _():
            take(n_far, 0, near, None)

        @pl.when(n_far != 2 * n_pair)
        def _():
            put(n_far, 1)
            take(n_far - 1, 0, {}, last)
            take(n_far, 1, near, None)

    finish(1, SEL)
    finish(2, WIN)

    for h in range(hpg):
        o_ref[0, h] = (osum_ref[h].T * _silu(z_ref[0, h])).astype(o_ref.dtype)


def _nsa_attention(pb3, pf3, vt, kvc, kvc_t, bias_c, bias_d, ov):
    b, _, t, _ = pb3.shape
    n_cmp = kvc.shape[2]
    n_blk = N_HEADS_NSA // 2
    full = lambda shape: pl.BlockSpec(shape, lambda i, q: (0,) * len(shape))
    kv_spec = lambda col: pl.BlockSpec((1, 1, t, LANES), lambda i, q: (i, col, 0, 0))
    vt_spec = lambda blk: pl.BlockSpec((1, 1, t // TK, LANES, TK), lambda i, q: (i, blk, 0, 0, 0))
    return pl.pallas_call(
        _nsa_kernel,
        grid=(b, t // TQ_NSA),
        in_specs=[pl.BlockSpec((1, n_blk, TQ_NSA, LANES), lambda i, q: (i, PB_QB // n_blk, q, 0)),
                  pl.BlockSpec((1, 1, n_cmp, LANES), lambda i, q: (i, 0, 0, 0)),
                  pl.BlockSpec((1, 1, LANES, n_cmp), lambda i, q: (i, 1, 0, 0)),
                  kv_spec(PB_KS), vt_spec(VT_VS), kv_spec(PB_KW), vt_spec(VT_VW),
                  pl.BlockSpec((1, 1, TQ_NSA, LANES), lambda i, q: (i, PF_GB, q, 0)),
                  pl.BlockSpec((1, n_blk, TQ_NSA, LANES), lambda i, q: (i, PF_ZB // n_blk, q, 0)),
                  full(bias_c.shape),
                  full(bias_d.shape), full(ov.shape)],
        out_specs=pl.BlockSpec((1, n_blk, TQ_NSA, LANES), lambda i, q: (i, 0, q, 0)),
        out_shape=jax.ShapeDtypeStruct((b, n_blk, t, LANES), _BF16),
        scratch_shapes=[pltpu.VMEM((2, NSA_KV_GROUPS, SUBLANES, NSA_HPG * TQ_NSA), _F32),
                        pltpu.VMEM((2, NSA_KV_GROUPS, HEAD_DIM + SUBLANES, NSA_HPG * TQ_NSA), _F32),
                        pltpu.VMEM((n_blk, TQ_NSA, LANES), _F32),
                        pltpu.VMEM((NSA_KV_GROUPS, NSA_HPG * TQ_NSA, LANES), _BF16),
                        pltpu.VMEM((NSA_KV_GROUPS, NSA_HPG * TQ_NSA, LANES), _BF16),
                        pltpu.VMEM((LANES, TQ_NSA), _F32),
                        pltpu.VMEM((NSA_KV_GROUPS, n_cmp, TQ_NSA), _F32),
                        pltpu.VMEM((2, NSA_KV_GROUPS, 4 * TK, NSA_HPG * TQ_NSA), _F32),
                        pltpu.VMEM((NSA_KV_GROUPS, WINDOW + TK, NSA_HPG * TQ_NSA), _F32)],
        compiler_params=_cparams(("arbitrary", "arbitrary")),
        name="native_sparse_attention",
    )(pb3, kvc, kvc_t, pb3, vt, pb3, vt, pf3, pf3, bias_c, bias_d, ov)


def _out_kernel(x_ref, oa_ref, ob_ref, oc_ref, w_ref, g_ref, o_ref, *, final_norm):
    mixed = jnp.concatenate([o_ref_in[0, j] for o_ref_in in (oa_ref, ob_ref, oc_ref)
                             for j in range(o_ref_in.shape[1])], axis=1)
    x = x_ref[...] + _dot(mixed, w_ref[...])
    if final_norm:
        x = x * lax.rsqrt(jnp.mean(x * x, axis=-1, keepdims=True) + RMS_EPS) * g_ref[...]
    o_ref[...] = x


def _out_proj(x2, oa, ob, oc, w, g, final_norm):
    n = x2.shape[0]
    per_seq = oa.shape[2] // TM_PROJ
    row = lambda width: pl.BlockSpec((TM_PROJ, width), lambda i: (i, 0))
    blocks = lambda a: pl.BlockSpec((1, a.shape[1], TM_PROJ, LANES),
                                    lambda i: (i // per_seq, 0, i % per_seq, 0))
    return pl.pallas_call(
        functools.partial(_out_kernel, final_norm=final_norm),
        grid=(n // TM_PROJ,),
        in_specs=[row(D_MODEL), blocks(oa), blocks(ob), blocks(oc),
                  pl.BlockSpec((D_MODEL, D_MODEL), lambda i: (0, 0), pipeline_mode=pl.Buffered(1)),
                  pl.BlockSpec((1, D_MODEL), lambda i: (0, 0))],
        out_specs=row(D_MODEL),
        out_shape=jax.ShapeDtypeStruct((n, D_MODEL), _F32),
        compiler_params=_cparams(("arbitrary",)),
        name="out_proj_residual",
    )(x2, oa, ob, oc, w, g)


def _head_perm_cols(width_per_head, order):
    return np.concatenate([np.arange(h * width_per_head, (h + 1) * width_per_head) for h in order])


def _layout_w_in(w):
    widths = [256, 256, 256, 256, 512, 128, 128, 128, 128, 128, 128, 24, 512, 256, 256, 256, 4, 256]
    offs = np.concatenate([[0], np.cumsum(widths)])
    (qa, ka, va, za, qb, kc, vc, ks, vs, kw, vw, gb, zb, qc, kf, vf, fc, zc) = [
        w[:, offs[i]:offs[i + 1]] for i in range(len(widths))]
    scale = HEAD_DIM ** -0.5
    perm = _head_perm_cols(HEAD_DIM, NSA_HEAD_ORDER)
    pad = lambda a: jnp.pad(a, ((0, 0), (0, LANES - a.shape[1])))
    wb = jnp.concatenate([qb[:, perm] * (scale * LOG2E), qa * (-scale), ka, va, ks, kw,
                          qc * (scale * LOG2E), kf], axis=1)
    wf = jnp.concatenate([zb[:, perm], za, zc, pad(jnp.concatenate([gb, fc], axis=1))], axis=1)
    wc = jnp.concatenate([kc, vc], axis=1)
    wvt = jnp.concatenate([vs, vw, vf], axis=1)
    return wb.astype(_BF16), wf.astype(_BF16), wc.astype(_BF16), wvt.astype(_BF16)


def _static_tables(t):
    tq = TQ_NSA
    n_cmp_pad = t // CMP_STRIDE
    j = np.arange(n_cmp_pad)
    rel = np.arange(2 * n_cmp_pad) - n_cmp_pad
    dist_c = np.arange(tq)[None, :] - (rel[:, None] * CMP_STRIDE + CMP_BLOCK - 1)
    i_, j_ = np.arange(tq)[:, None], np.arange(TK)[None, :]
    dist_d = np.concatenate([d * TK + i_ - j_ for d in range(3)], axis=0)
    n_slc = LANES // NSA_KV_GROUPS
    cmp_start = j * CMP_STRIDE
    cmp_end = cmp_start + CMP_BLOCK - 1
    slc_start = np.arange(n_slc) * SLC_BLOCK
    ov1 = np.clip(np.minimum(cmp_end[:, None], slc_start[None, :] + SLC_BLOCK - 1)
                  - np.maximum(cmp_start[:, None], slc_start[None, :]) + 1, 0, None) / CMP_BLOCK
    ov1[n_cmp_pad - 1:] = 0.0
    ov1[:, t // SLC_BLOCK:] = 0.0
    ov3 = np.concatenate([ov1.T, ov1.T, ov1.T], axis=1)
    return jnp.asarray(dist_c, jnp.int32), jnp.asarray(dist_d, jnp.int32), jnp.asarray(ov3, _BF16)


def kernel(x, norm_g, w_in, w_out, forget_b, cmp_w1, cmp_b1, cmp_w2, cmp_pe, rel_bias, final_g):
    b, t, d = x.shape
    depth = norm_g.shape[0]
    assert d == D_MODEL and t % TM_PROJ == 0 and t % (CMP_STRIDE * LANES) == 0
    assert t // SLC_BLOCK <= LANES // NSA_KV_GROUPS and TQ_NSA == TK
    n_cmp_pad = t // CMP_STRIDE

    dist_c, dist_d, ov3 = _static_tables(t)
    bias_c = _bias_table(rel_bias * LOG2E, dist_c, 32)
    bias_d = _bias_table(rel_bias * LOG2E, dist_d, 32).reshape(N_HEADS_NSA, 3, TQ_NSA, TK)
    bias_d = bias_d[:, :2] - bias_d[:, 2:3]
    bias_d = bias_d.reshape(NSA_KV_GROUPS, NSA_HPG, 2, TQ_NSA, TK).transpose(0, 2, 4, 1, 3).reshape(
        NSA_KV_GROUPS, 2, TK, NSA_HPG * TQ_NSA)

    perm_rows = _head_perm_cols(HEAD_DIM, NSA_HEAD_ORDER)
    x2 = x.reshape(b * t, d)
    for l in range(depth):
        wb, wf, wc, wvt = _layout_w_in(w_in[l])
        pb, pf, pc, vt = _proj(x2, norm_g[l].reshape(1, d), wb, wf, wc, wvt, t)
        pb3, pf3 = pb, pf

        fb_row = jnp.pad(forget_b[l], (FC_LANE0, LANES - FC_LANE0 - N_HEADS_FOX)).reshape(1, LANES)
        augq, augk = _fgate(pf3, fb_row)

        halves = pc.reshape(b, 2, NSA_KV_GROUPS, n_cmp_pad, CMP_STRIDE * HEAD_DIM)
        w2 = cmp_w2[l]
        zeros = jnp.zeros_like(w2)
        w2p = jnp.stack([jnp.concatenate([w2, zeros], axis=-1),
                         jnp.concatenate([zeros, w2], axis=-1)], axis=1).astype(_BF16)
        kvc, kvc_t = _compress(halves, cmp_w1[l].astype(_BF16),
                               cmp_pe[l].reshape(2, 1, CMP_BLOCK * HEAD_DIM).astype(_BF16),
                               cmp_b1[l].reshape(2, 1, CMP_HIDDEN), w2p)

        o_a = _sb_attention(pb3, pf3)
        o_b = _nsa_attention(pb3, pf3, vt, kvc, kvc_t, bias_c, bias_d, ov3)
        o_c = _fox_attention(pb3, pf3, vt, augq, augk)

        wo = w_out[l]
        wo = jnp.concatenate([wo[:N_HEADS_SB * HEAD_DIM],
                              wo[N_HEADS_SB * HEAD_DIM:][:N_HEADS_NSA * HEAD_DIM][perm_rows],
                              wo[(N_HEADS_SB + N_HEADS_NSA) * HEAD_DIM:]], axis=0).astype(_BF16)
        last = l == depth - 1
        x2 = _out_proj(x2, o_a, o_b, o_c, wo, final_g.reshape(1, d), last)
    return x2.reshape(b, t, d)
```
